```python
import jax, jax.numpy as jnp
from jax import lax
import numpy as np

D_MODEL = 1024
BATCH = 4
SEQ = 4096
DEPTH = 4

GRID_W = 64
CTX_LEN = 256
N_MIXERS = 3
RETENTION, FOURIER, NEIGHBOURHOOD = 0, 1, 2
N_MOD = 6
EPS = 1e-6
NEG_INF = -1e30

RET_HEADS = 4
RET_DK = D_MODEL // RET_HEADS
RET_DV = 2 * RET_DK
RET_CHUNK = 128
RET_BASE_SHIFT = 5
ROPE_BASE = 10000.0

FN_GROUPS = 4

NA_HEADS = 16
NA_HD = D_MODEL // NA_HEADS
NA_KH = 8
NA_KW = 16

N_EXPERTS = 16
N_GROUPS = 4
EXPERTS_PER_GROUP = N_EXPERTS // N_GROUPS
TOP_K = 2
D_EXPERT = D_MODEL
MOE_BLOCK = 128

kernel_name = 'hybrid_retention_fnet_natten_moe_dit'


def _rmsnorm(x, g):
    xf = x.astype(jnp.float32)
    y = xf * lax.rsqrt(jnp.mean(xf * xf, axis=-1, keepdims=True) + EPS)
    return (y * g.astype(jnp.float32)).astype(x.dtype)


def _modulate(h, shift, scale):
    return h * (1.0 + scale) + shift


def _axial_rope(x):
    L, dh = x.shape[1], x.shape[-1]
    half = dh // 2
    nf = half // 2
    t = jnp.arange(L)
    row = (t // GRID_W).astype(jnp.float32)
    col = (t % GRID_W).astype(jnp.float32)
    inv = ROPE_BASE ** (-jnp.arange(nf, dtype=jnp.float32) / nf)

    def rot(seg, pos):
        ang = pos[:, None] * inv[None, :]
        cos = jnp.cos(ang)[None, :, None, :]
        sin = jnp.sin(ang)[None, :, None, :]
        a, b = seg[..., :nf], seg[..., nf:]
        return jnp.concatenate([a * cos - b * sin, a * sin + b * cos], axis=-1)

    xf = x.astype(jnp.float32)
    return jnp.concatenate([rot(xf[..., :half], row), rot(xf[..., half:], col)], axis=-1).astype(x.dtype)


def _to_bhld(a):
    return jnp.transpose(a, (0, 2, 1, 3)).astype(jnp.float32)


def _retention_scan(q, k, v, log_g, s0):
    B, H, L, _ = q.shape
    dv = v.shape[-1]
    C = RET_CHUNK
    n = L // C
    pos = jnp.arange(C, dtype=jnp.float32)
    diff = pos[:, None] - pos[None, :]
    dmat = jnp.where(diff >= 0, jnp.exp(jnp.maximum(diff, 0.0)[None] * log_g[:, None, None]), 0.0)
    q_dec = jnp.exp((pos + 1.0)[None, :] * log_g[:, None])
    k_dec = jnp.exp((C - 1.0 - pos)[None, :] * log_g[:, None])
    c_dec = jnp.exp(C * log_g)

    def chunks(a):
        return jnp.moveaxis(a.reshape(B, H, n, C, a.shape[-1]), 2, 0)

    def step(s, inp):
        qi, ki, vi = inp
        att = jnp.einsum('bhik,bhjk->bhij', qi, ki) * dmat
        o = (jnp.einsum('bhij,bhjv->bhiv', att, vi)
             + jnp.einsum('bhik,bhkv->bhiv', qi * q_dec[None, :, :, None], s))
        s = s * c_dec[None, :, None, None] + jnp.einsum('bhjk,bhjv->bhkv', ki * k_dec[None, :, :, None], vi)
        return s, o

    s_fin, o = lax.scan(step, s0, (chunks(q), chunks(k), chunks(v)))
    return jnp.moveaxis(o, 0, 2).reshape(B, H, L, dv), s_fin


def _retention_bidir(q, k, v, log_g, s0_f, s0_b):
    o_f, s_f = _retention_scan(q, k, v, log_g[0], s0_f)
    o_b, s_b = _retention_scan(q[:, :, ::-1], k[:, :, ::-1], v[:, :, ::-1], log_g[1], s0_b)
    return o_f + o_b[:, :, ::-1], s_f, s_b


def _retention_final_states(k, v, log_g):
    L = k.shape[2]
    t = jnp.arange(L, dtype=jnp.float32)
    w_f = jnp.exp((L - 1.0 - t)[None, :] * log_g[0][:, None])
    w_b = jnp.exp(t[None, :] * log_g[1][:, None])
    s_f = jnp.einsum('bhtk,ht,bhtv->bhkv', k, w_f, v)
    s_b = jnp.einsum('bhtk,ht,bhtv->bhkv', k, w_b, v)
    return s_f, s_b


def _retention_out(o, g, gn_w, w_out):
    B, H, L, dv = o.shape
    o = jnp.transpose(o, (0, 2, 1, 3))
    mu = jnp.mean(o, axis=-1, keepdims=True)
    var = jnp.mean(jnp.square(o - mu), axis=-1, keepdims=True)
    o = ((o - mu) * lax.rsqrt(var + EPS)).reshape(B, L, H * dv) * gn_w.astype(jnp.float32)
    return (o.astype(g.dtype) * jax.nn.silu(g)) @ w_out


def _retention_mixer(hx, hc, w_in, decay_logit, gn_w, w_out, need_ctx):
    D = D_MODEL
    B, T, _ = hx.shape
    Lc = hc.shape[1]
    log_g = jax.nn.log_sigmoid(decay_logit.astype(jnp.float32))
    k_scale = RET_DK ** -0.5
    qx, kx, vx, gx = jnp.split(hx @ w_in, [D, 2 * D, 4 * D], axis=-1)
    qx = _to_bhld(_axial_rope(qx.reshape(B, T, RET_HEADS, RET_DK)))
    kx = _to_bhld(_axial_rope(kx.reshape(B, T, RET_HEADS, RET_DK))) * k_scale
    vx = _to_bhld(vx.reshape(B, T, RET_HEADS, RET_DV))
    if need_ctx:
        qc, kc, vc, gc = jnp.split(hc @ w_in, [D, 2 * D, 4 * D], axis=-1)
        qc = _to_bhld(qc.reshape(B, Lc, RET_HEADS, RET_DK))
        kc = _to_bhld(kc.reshape(B, Lc, RET_HEADS, RET_DK)) * k_scale
        vc = _to_bhld(vc.reshape(B, Lc, RET_HEADS, RET_DV))
        zeros = jnp.zeros((B, RET_HEADS, RET_DK, RET_DV), jnp.float32)
        oc, s_f, s_b = _retention_bidir(qc, kc, vc, log_g, zeros, zeros)
        yc = _retention_out(oc, gc, gn_w, w_out)
    else:
        kc, vc = jnp.split(hc @ w_in[:, D:4 * D], [D], axis=-1)
        kc = _to_bhld(kc.reshape(B, Lc, RET_HEADS, RET_DK)) * k_scale
        vc = _to_bhld(vc.reshape(B, Lc, RET_HEADS, RET_DV))
        s_f, s_b = _retention_final_states(kc, vc, log_g)
        yc = None
    ox, _, _ = _retention_bidir(qx, kx, vx, log_g, s_f, s_b)
    return _retention_out(ox, gx, gn_w, w_out), yc


def _fourier_mix(h, w_out):
    B, L, D = h.shape
    hg = h.astype(jnp.float32).reshape(B, L, FN_GROUPS, D // FN_GROUPS)
    f = jnp.fft.fft2(hg, axes=(1, 3), norm='ortho').real
    return f.reshape(B, L, D).astype(h.dtype) @ w_out


def _na_mixer(hx, hc, w_qkv, rpb, w_out, need_ctx):
    B, T, D = hx.shape
    Lc = hc.shape[1]
    rows = T // GRID_W
    KH = min(NA_KH, rows)
    KW = NA_KW
    QB = KW
    NCB = GRID_W // QB
    BW = 2 * KW
    NK = KH * BW
    scale = NA_HD ** -0.5

    def heads(a):
        return a.reshape(a.shape[0], a.shape[1], NA_HEADS, NA_HD)

    qx, kx, vx = [heads(a) for a in jnp.split(hx @ w_qkv, 3, axis=-1)]
    if need_ctx:
        qc, kc, vc = [heads(a) for a in jnp.split(hc @ w_qkv, 3, axis=-1)]
    else:
        kc, vc = [heads(a) for a in jnp.split(hc @ w_qkv[:, D:], 2, axis=-1)]

    cb = np.clip(np.arange(NCB) * QB - KW // 2, 0, GRID_W - BW)
    qcol = np.arange(NCB)[:, None] * QB + np.arange(QB)[None, :]
    cstart = np.clip(qcol - KW // 2, 0, GRID_W - KW)
    kcol = np.tile(cb[:, None] + np.arange(BW)[None, :], (1, KH))
    krow = np.repeat(np.arange(KH), BW)
    valid = (kcol[:, None, :] >= cstart[:, :, None]) & (kcol[:, None, :] < cstart[:, :, None] + KW)
    dc_idx = np.clip(kcol[:, None, :] - qcol[:, :, None], -(KW - 1), KW - 1) + (KW - 1)

    def row_block(r):
        rs = jnp.clip(r - KH // 2, 0, rows - KH)
        idx = ((rs + krow)[None, :] * GRID_W + kcol).reshape(-1)
        kb = jnp.take(kx, idx, axis=1).reshape(B, NCB, NK, NA_HEADS, NA_HD)
        vb = jnp.take(vx, idx, axis=1).reshape(B, NCB, NK, NA_HEADS, NA_HD)
        qb = lax.dynamic_slice_in_dim(qx, r * GRID_W, GRID_W, axis=1).reshape(B, NCB, QB, NA_HEADS, NA_HD)
        dr_idx = rs + krow - r + (NA_KH - 1)
        bias = rpb[:, dr_idx[None, None, :], dc_idx].astype(jnp.float32)
        s_win = jnp.einsum('bjqhd,bjkhd->bhjqk', qb, kb).astype(jnp.float32) * scale + bias
        s_win = jnp.where(valid, s_win, NEG_INF)
        s_ctx = jnp.einsum('bjqhd,bchd->bhjqc', qb, kc).astype(jnp.float32) * scale
        p = jax.nn.softmax(jnp.concatenate([s_win, s_ctx], axis=-1), axis=-1).astype(vx.dtype)
        o = (jnp.einsum('bhjqk,bjkhd->bjqhd', p[..., :NK], vb)
             + jnp.einsum('bhjqc,bchd->bjqhd', p[..., NK:], vc))
        return o.reshape(B, GRID_W, D)

    o = lax.map(row_block, jnp.arange(rows))
    yx = jnp.moveaxis(o, 0, 1).reshape(B, T, D) @ w_out
    if need_ctx:
        s = jnp.einsum('bqhd,bkhd->bhqk', qc, kc).astype(jnp.float32) * scale
        p = jax.nn.softmax(s, axis=-1).astype(vc.dtype)
        yc = jnp.einsum('bhqk,bkhd->bqhd', p, vc).reshape(B, Lc, D) @ w_out
    else:
        yc = None
    return yx, yc


def _moe(tokens, router_w, router_bias, w_gate, w_up, w_down):
    N, D = tokens.shape
    rows = jnp.arange(N)
    scores = jax.nn.sigmoid((tokens @ router_w).astype(jnp.float32))
    sel = scores + router_bias.astype(jnp.float32)
    grp = sel.reshape(N, N_GROUPS, EXPERTS_PER_GROUP)
    grp_score = jnp.sum(lax.top_k(grp, 2)[0], axis=-1)
    g_idx = jnp.argmax(grp_score, axis=-1)
    _, local = lax.top_k(grp[rows, g_idx], TOP_K)
    experts = g_idx[:, None] * EXPERTS_PER_GROUP + local
    w = scores[rows[:, None], experts]
    w = w / jnp.sum(w, axis=-1, keepdims=True)

    S = N * TOP_K
    flat_e = experts.reshape(-1)
    flat_tok = jnp.repeat(rows, TOP_K)
    flat_w = w.reshape(-1)
    order = jnp.argsort(flat_e, stable=True)
    se, stok, sw = flat_e[order], flat_tok[order], flat_w[order]
    counts = jnp.bincount(flat_e, length=N_EXPERTS)
    starts = jnp.cumsum(counts) - counts
    padded = (counts + MOE_BLOCK - 1) // MOE_BLOCK * MOE_BLOCK
    pend = jnp.cumsum(padded)
    pstart = pend - padded
    dest = pstart[se] + jnp.arange(S) - starts[se]
    n_blocks = (S + N_EXPERTS * (MOE_BLOCK - 1) + MOE_BLOCK - 1) // MOE_BLOCK
    P = n_blocks * MOE_BLOCK
    row_tok = jnp.full((P,), N, jnp.int32).at[dest].set(stok.astype(jnp.int32))
    row_w = jnp.zeros((P,), jnp.float32).at[dest].set(sw)
    block_e = jnp.minimum(jnp.searchsorted(pend, jnp.arange(n_blocks) * MOE_BLOCK, side='right'), N_EXPERTS - 1)
    xpad = jnp.concatenate([tokens, jnp.zeros((1, D), tokens.dtype)], axis=0)
    xb = xpad[row_tok].reshape(n_blocks, MOE_BLOCK, D)

    def expert_block(args):
        xblk, e = args
        h = jax.nn.silu(xblk @ w_gate[e]) * (xblk @ w_up[e])
        return h @ w_down[e]

    yb = lax.map(expert_block, (xb, block_e)).reshape(P, D)
    y = yb * row_w[:, None].astype(yb.dtype)
    return jax.ops.segment_sum(y, row_tok, num_segments=N + 1)[:N]


def setup_inputs(seed: int = 0) -> dict:
    key = jax.random.key(seed)
    ks = jax.random.split(key, 22)
    f32 = jnp.float32
    D = D_MODEL
    n_ret = len(range(RETENTION, DEPTH, N_MIXERS))
    n_fn = len(range(FOURIER, DEPTH, N_MIXERS))
    n_na = len(range(NEIGHBOURHOOD, DEPTH, N_MIXERS))

    def nrm(k, shape, s):
        return jax.random.normal(k, shape, f32) * s

    base_logit = jnp.log(2.0 ** (RET_BASE_SHIFT + jnp.arange(RET_HEADS, dtype=f32)) - 1.0)
    return {
        'x': nrm(ks[0], (BATCH, SEQ, D), 1.0),
        'c': nrm(ks[1], (BATCH, D), 1.0),
        'ctx': nrm(ks[2], (BATCH, CTX_LEN, D), 1.0),
        'c_ctx': nrm(ks[3], (D,), 1.0),
        'w_mod': nrm(ks[4], (DEPTH, D, N_MOD * D), 0.5 * D ** -0.5),
        'b_mod': nrm(ks[5], (DEPTH, N_MOD * D), 0.02),
        'norm_mix': 1.0 + nrm(ks[6], (DEPTH, D), 0.02),
        'norm_ffn': 1.0 + nrm(ks[7], (DEPTH, D), 0.02),
        'norm_final': 1.0 + nrm(ks[8], (D,), 0.02),
        'ret_w_in': nrm(ks[9], (n_ret, D, 6 * D), D ** -0.5),
        'ret_decay': base_logit + nrm(ks[10], (n_ret, 2, RET_HEADS), 0.1),
        'ret_gn': 1.0 + nrm(ks[11], (n_ret, RET_HEADS * RET_DV), 0.02),
        'ret_w_out': nrm(ks[12], (n_ret, RET_HEADS * RET_DV, D), (RET_HEADS * RET_DV) ** -0.5),
        'fn_w_out': nrm(ks[13], (n_fn, D, D), D ** -0.5),
        'na_w_qkv': nrm(ks[14], (n_na, D, 3 * D), D ** -0.5),
        'na_rpb': nrm(ks[15], (n_na, NA_HEADS, 2 * NA_KH - 1, 2 * NA_KW - 1), 0.1),
        'na_w_out': nrm(ks[16], (n_na, D, D), D ** -0.5),
        'router_w': nrm(ks[17], (D, N_EXPERTS), D ** -0.5),
        'router_bias': nrm(ks[18], (N_EXPERTS,), 0.01),
        'moe_w_gate': nrm(ks[19], (DEPTH, N_EXPERTS, D, D_EXPERT), D ** -0.5),
        'moe_w_up': nrm(ks[20], (DEPTH, N_EXPERTS, D, D_EXPERT), D ** -0.5),
        'moe_w_down': nrm(ks[21], (DEPTH, N_EXPERTS, D_EXPERT, D), D_EXPERT ** -0.5),
    }


def reference(x, c, ctx, c_ctx, w_mod, b_mod, norm_mix, norm_ffn, norm_final,
              ret_w_in, ret_decay, ret_gn, ret_w_out, fn_w_out,
              na_w_qkv, na_rpb, na_w_out,
              router_w, router_bias, moe_w_gate, moe_w_up, moe_w_down):
    B, T, D = x.shape
    Lc = ctx.shape[1]
    c_act = jax.nn.silu(c)
    cc_act = jax.nn.silu(c_ctx)
    for i in range(DEPTH):
        kind, j = i % N_MIXERS, i // N_MIXERS
        need_ctx = i < DEPTH - 1
        mx = jnp.split(c_act @ w_mod[i] + b_mod[i], N_MOD, axis=-1)
        mc = jnp.split(cc_act @ w_mod[i] + b_mod[i], N_MOD, axis=-1)
        hx = _modulate(_rmsnorm(x, norm_mix[i]), mx[0][:, None], mx[1][:, None])
        if kind == FOURIER:
            yx = _fourier_mix(hx, fn_w_out[j])
            if need_ctx:
                hc = _modulate(_rmsnorm(ctx, norm_mix[i]), mc[0], mc[1])
                yc = _fourier_mix(hc, fn_w_out[j])
        else:
            hc = _modulate(_rmsnorm(ctx, norm_mix[i]), mc[0], mc[1])
            if kind == RETENTION:
                yx, yc = _retention_mixer(hx, hc, ret_w_in[j], ret_decay[j], ret_gn[j], ret_w_out[j], need_ctx)
            else:
                yx, yc = _na_mixer(hx, hc, na_w_qkv[j], na_rpb[j], na_w_out[j], need_ctx)
        x = x + mx[2][:, None] * yx
        fx = _modulate(_rmsnorm(x, norm_ffn[i]), mx[3][:, None], mx[4][:, None]).reshape(B * T, D)
        if need_ctx:
            ctx = ctx + mc[2] * yc
            fc = _modulate(_rmsnorm(ctx, norm_ffn[i]), mc[3], mc[4]).reshape(B * Lc, D)
            f = _moe(jnp.concatenate([fx, fc], axis=0), router_w, router_bias,
                     moe_w_gate[i], moe_w_up[i], moe_w_down[i])
            x = x + mx[5][:, None] * f[:B * T].reshape(B, T, D)
            ctx = ctx + mc[5] * f[B * T:].reshape(B, Lc, D)
        else:
            f = _moe(fx, router_w, router_bias, moe_w_gate[i], moe_w_up[i], moe_w_down[i])
            x = x + mx[5][:, None] * f.reshape(B, T, D)
    return _rmsnorm(x, norm_final)
```

```python
import functools

import numpy as np
import jax
import jax.numpy as jnp
from jax import lax
from jax.experimental import pallas as pl
from jax.experimental.pallas import tpu as pltpu

F32 = jnp.float32
BF16 = jnp.bfloat16
HIGHEST = lax.Precision.HIGHEST

D = 1024
BATCH = 4
SEQ = 4096
CTX = 256
TT = SEQ + CTX
NTOK = BATCH * TT
DEPTH = 4
GRID_W = 64
EPS = 1e-6
NEG_INF = -1e30
N_MOD = 6

RET_HEADS = 4
RET_DK = D // RET_HEADS
RET_DV = 2 * RET_DK
RET_CHUNK = 128
ROPE_BASE = 10000.0

FN_GROUPS = 4
FN_GC = D // FN_GROUPS

NA_HEADS = 16
NA_HD = D // NA_HEADS
NA_KH = 8
NA_KW = 16

N_EXPERTS = 16
N_GROUPS = 4
EPG = N_EXPERTS // N_GROUPS

ROW_TILE = 256
TPB = TT // ROW_TILE
LAT_TILES = SEQ // ROW_TILE
MM_TM = TT // 4
MOE_BM = 256
MOE_S = NTOK * 2
MOE_NB = (MOE_S + N_EXPERTS * (MOE_BM - 1) + MOE_BM - 1) // MOE_BM
MOE_P = MOE_NB * MOE_BM
NA_QT = GRID_W
VMEM_LIMIT = 56 * 1024 * 1024


def _cp(n_axes, vmem=VMEM_LIMIT):
    return pltpu.CompilerParams(dimension_semantics=("arbitrary",) * n_axes, vmem_limit_bytes=vmem)


def _silu(x):
    return x * jax.nn.sigmoid(x)


def _normmod(x, g, shift, scale):
    y = x * lax.rsqrt(jnp.mean(x * x, axis=-1, keepdims=True) + EPS)
    return (y * g) * (1.0 + scale) + shift


def _mod_idx(t):
    return jnp.where(t < LAT_TILES, 1, 0)


def _mod_kernel(c_ref, w_ref, b_ref, o_ref):
    a = _silu(c_ref[...])
    o_ref[...] = jnp.dot(a, w_ref[...], precision=HIGHEST, preferred_element_type=F32) + b_ref[...]


def _modulation(cvec, w_mod, b_mod):
    return pl.pallas_call(
        _mod_kernel,
        grid=(DEPTH, N_MOD),
        in_specs=[
            pl.BlockSpec((8, D), lambda i, j: (0, 0)),
            pl.BlockSpec((None, D, D), lambda i, j: (i, 0, j)),
            pl.BlockSpec((None, 1, D), lambda i, j: (i, 0, j)),
        ],
        out_specs=pl.BlockSpec((None, 8, D), lambda i, j: (i, 0, j)),
        out_shape=jax.ShapeDtypeStruct((DEPTH, 8, N_MOD * D), F32),
        compiler_params=_cp(2),
        name="modulation",
    )(cvec, w_mod, b_mod.reshape(DEPTH, 1, N_MOD * D))


def _embed_kernel(x_ref, ctx_ref, mod_ref, g_ref, xs_ref, h_ref):
    t = pl.program_id(1)

    def emit(src):
        xs_ref[...] = src
        h_ref[...] = _normmod(src, g_ref[...], mod_ref[0:1, :], mod_ref[1:2, :]).astype(BF16)

    @pl.when(t < LAT_TILES)
    def _():
        emit(x_ref[...])

    @pl.when(t >= LAT_TILES)
    def _():
        emit(ctx_ref[...])


def _embed(x, ctx, mod, g):
    return pl.pallas_call(
        _embed_kernel,
        grid=(BATCH, TPB),
        in_specs=[
            pl.BlockSpec((None, ROW_TILE, D), lambda b, t: (b, jnp.minimum(t, LAT_TILES - 1), 0)),
            pl.BlockSpec((None, CTX, D), lambda b, t: (b, 0, 0)),
            pl.BlockSpec((None, None, 8, D), lambda b, t: (b, _mod_idx(t), 0, 0)),
            pl.BlockSpec((1, D), lambda b, t: (0, 0)),
        ],
        out_specs=[
            pl.BlockSpec((None, ROW_TILE, D), lambda b, t: (b, t, 0)),
            pl.BlockSpec((None, ROW_TILE, D), lambda b, t: (b, t, 0)),
        ],
        out_shape=[
            jax.ShapeDtypeStruct((BATCH, TT, D), F32),
            jax.ShapeDtypeStruct((BATCH, TT, D), BF16),
        ],
        compiler_params=_cp(2),
        name="embed",
    )(x, ctx, mod, g)


def _rope_store(acc, cos_ref, sin_ref, o_ref):
    for c in range(acc.shape[1] // 128):
        xc = acc[:, c * 128:(c + 1) * 128]
        tsl = slice((c % 2) * 128, (c % 2 + 1) * 128)
        rot = xc * cos_ref[:, tsl] + pltpu.roll(xc, 64, 1) * sin_ref[:, tsl]
        o_ref[:, c * 128:(c + 1) * 128] = rot.astype(o_ref.dtype)


def _proj_kernel(a_ref, w_ref, *rest, n_rope):
    if n_rope:
        cos_ref, sin_ref, o_ref, wb_ref = rest
    else:
        o_ref, wb_ref = rest
    j = pl.program_id(0)

    @pl.when(pl.program_id(1) == 0)
    def _():
        wb_ref[...] = w_ref[...].astype(BF16)

    acc = jnp.dot(a_ref[...], wb_ref[...], preferred_element_type=F32)
    if n_rope:
        @pl.when(j < n_rope)
        def _():
            _rope_store(acc, cos_ref, sin_ref, o_ref)

        @pl.when(j >= n_rope)
        def _():
            o_ref[...] = acc.astype(o_ref.dtype)
    else:
        o_ref[...] = acc.astype(o_ref.dtype)


def _project(a, w, rope=None):
    k, n = w.shape
    tn = D
    n_rope = 0 if rope is None else rope[0].shape[0]
    in_specs = [
        pl.BlockSpec((MM_TM, k), lambda j, i: (i, 0)),
        pl.BlockSpec((k, tn), lambda j, i: (0, j)),
    ]
    args = [a, w]
    if n_rope:
        tspec = pl.BlockSpec((None, MM_TM, 256), lambda j, i: (jnp.minimum(j, n_rope - 1), i % (TT // MM_TM), 0))
        in_specs += [tspec, tspec]
        args += list(rope)
    return pl.pallas_call(
        functools.partial(_proj_kernel, n_rope=n_rope),
        grid=(n // tn, NTOK // MM_TM),
        in_specs=in_specs,
        out_specs=pl.BlockSpec((MM_TM, tn), lambda j, i: (i, j)),
        out_shape=jax.ShapeDtypeStruct((NTOK, n), BF16),
        scratch_shapes=[pltpu.VMEM((k, tn), BF16)],
        compiler_params=_cp(2),
        name="project",
    )(*args)


N_CHUNKS = TT // RET_CHUNK
LAT_CHUNKS = SEQ // RET_CHUNK
CTX_CHUNKS = CTX // RET_CHUNK


def _fwd_chunk(n):
    return jnp.where(n < CTX_CHUNKS, LAT_CHUNKS + n, n - CTX_CHUNKS)


def _bwd_chunk(n):
    return N_CHUNKS - 1 - n


def _ret_kernel(cdec_ref, qf, kf, vf, qb, kb, vb, dmat_ref, qdec_ref, kdec_ref, of_ref, ob_ref, s_ref):
    @pl.when(pl.program_id(1) == 0)
    def _():
        s_ref[...] = jnp.zeros_like(s_ref)

    for d, (q_r, k_r, v_r, o_r) in enumerate(((qf, kf, vf, of_ref), (qb, kb, vb, ob_ref))):
        for h in range(RET_HEADS):
            q = q_r[:, h * RET_DK:(h + 1) * RET_DK]
            k = k_r[:, h * RET_DK:(h + 1) * RET_DK]
            v = v_r[:, h * RET_DV:(h + 1) * RET_DV]
            att = lax.dot_general(q, k, (((1,), (1,)), ((), ())), preferred_element_type=F32) * dmat_ref[d, h]
            s = s_ref[d, h]
            qd = (q.astype(F32) * qdec_ref[d, h]).astype(BF16)
            o = (jnp.dot(att.astype(BF16), v, preferred_element_type=F32)
                 + jnp.dot(qd, s.astype(BF16), preferred_element_type=F32))
            kd = (k.astype(F32) * kdec_ref[d, h]).astype(BF16)
            s_ref[d, h] = s * cdec_ref[d * RET_HEADS + h] + lax.dot_general(
                kd, v, (((0,), (0,)), ((), ())), preferred_element_type=F32)
            o_r[:, h * RET_DV:(h + 1) * RET_DV] = o


def _retention(qkvg, dmat, qdec, kdec, cdec):
    C = RET_CHUNK

    def spec(width, col, chunk_fn):
        return pl.BlockSpec((None, C, width), lambda b, n, cd: (b, chunk_fn(n), col))

    ospec_f = pl.BlockSpec((None, C, 2 * D), lambda b, n, cd: (b, _fwd_chunk(n), 0))
    ospec_b = pl.BlockSpec((None, C, 2 * D), lambda b, n, cd: (b, _bwd_chunk(n), 0))
    full = lambda shape: pl.BlockSpec(shape, lambda b, n, cd: (0,) * len(shape))
    grid_spec = pltpu.PrefetchScalarGridSpec(
        num_scalar_prefetch=1,
        grid=(BATCH, N_CHUNKS),
        in_specs=[
            spec(D, 0, _fwd_chunk), spec(D, 1, _fwd_chunk), spec(2 * D, 1, _fwd_chunk),
            spec(D, 0, _bwd_chunk), spec(D, 1, _bwd_chunk), spec(2 * D, 1, _bwd_chunk),
            full((2, RET_HEADS, C, C)), full((2, RET_HEADS, C, 1)), full((2, RET_HEADS, C, 1)),
        ],
        out_specs=[ospec_f, ospec_b],
        scratch_shapes=[pltpu.VMEM((2, RET_HEADS, RET_DK, RET_DV), F32)],
    )
    return pl.pallas_call(
        _ret_kernel,
        grid_spec=grid_spec,
        out_shape=[jax.ShapeDtypeStruct((BATCH, TT, 2 * D), F32)] * 2,
        compiler_params=_cp(2),
        name="retention",
    )(cdec, qkvg, qkvg, qkvg, qkvg, qkvg, qkvg, dmat, qdec, kdec)


def _mixer_out_kernel(*refs, kind):
    if kind == "ret":
        of_ref, ob_ref, g_ref, gn_ref, w_ref, x_ref, mod_ref, nf_ref, xo_ref, h_ref, wb_ref = refs
    else:
        a_ref, w_ref, x_ref, mod_ref, nf_ref, xo_ref, h_ref, wb_ref = refs

    @pl.when((pl.program_id(0) == 0) & (pl.program_id(1) == 0))
    def _():
        wb_ref[...] = w_ref[...].astype(BF16)

    if kind == "ret":
        o = of_ref[...] + ob_ref[...]
        parts = []
        for h in range(RET_HEADS):
            oh = o[:, h * RET_DV:(h + 1) * RET_DV]
            mu = jnp.mean(oh, axis=-1, keepdims=True)
            ctr = oh - mu
            var = jnp.mean(ctr * ctr, axis=-1, keepdims=True)
            parts.append(ctr * lax.rsqrt(var + EPS))
        on = jnp.concatenate(parts, axis=-1) * gn_ref[...]
        a = (on * _silu(g_ref[...].astype(F32))).astype(BF16)
    else:
        a = a_ref[...]
    y = jnp.dot(a, wb_ref[...], preferred_element_type=F32)
    xn = x_ref[...] + mod_ref[2:3, :] * y
    xo_ref[...] = xn
    h_ref[...] = _normmod(xn, nf_ref[...], mod_ref[3:4, :], mod_ref[4:5, :])


def _mixer_out(kind, srcs, w, xs, mod, norm_ffn):
    k = w.shape[0]
    row = lambda width, col=0: pl.BlockSpec((None, ROW_TILE, width), lambda b, t: (b, t, col))
    const = lambda shape: pl.BlockSpec(shape, lambda b, t: (0,) * len(shape))
    if kind == "ret":
        src_specs = [row(2 * D), row(2 * D), row(2 * D, 2), const((1, 2 * D))]
    else:
        src_specs = [row(k)]
    return pl.pallas_call(
        functools.partial(_mixer_out_kernel, kind=kind),
        grid=(BATCH, TPB),
        in_specs=src_specs + [
            const((k, D)),
            row(D),
            pl.BlockSpec((None, None, 8, D), lambda b, t: (b, _mod_idx(t), 0, 0)),
            const((1, D)),
        ],
        out_specs=[row(D), row(D)],
        out_shape=[jax.ShapeDtypeStruct((BATCH, TT, D), F32)] * 2,
        scratch_shapes=[pltpu.VMEM((k, D), BF16)],
        compiler_params=_cp(2),
        name="mixer_out_" + kind,
    )(*srcs, w, xs, mod, norm_ffn)


def _fn_channel_kernel(h_ref, cs_ref, o_ref):
    for g in range(FN_GROUPS):
        r = jnp.dot(h_ref[:, g * FN_GC:(g + 1) * FN_GC], cs_ref[...], preferred_element_type=F32)
        o_ref[0, :, g * FN_GC:(g + 1) * FN_GC] = r[:, :FN_GC].astype(BF16)
        o_ref[1, :, g * FN_GC:(g + 1) * FN_GC] = r[:, FN_GC:].astype(BF16)


def _fn_channel(h, cs):
    return pl.pallas_call(
        _fn_channel_kernel,
        grid=(BATCH, TPB),
        in_specs=[
            pl.BlockSpec((None, ROW_TILE, D), lambda b, t: (b, t, 0)),
            pl.BlockSpec((FN_GC, 2 * FN_GC), lambda b, t: (0, 0)),
        ],
        out_specs=pl.BlockSpec((None, 2, ROW_TILE, D), lambda b, t: (b, 0, t, 0)),
        out_shape=jax.ShapeDtypeStruct((BATCH, 2, TT, D), BF16),
        compiler_params=_cp(2),
        name="fourier_channels",
    )(h, cs)


FN_TK = 2 * TT // 4


def _fn_token_kernel(m_ref, ab_ref, o_ref, acc_ref):
    kk = pl.program_id(2)

    @pl.when(kk == 0)
    def _():
        acc_ref[...] = jnp.zeros_like(acc_ref)

    acc_ref[...] += jnp.dot(m_ref[...], ab_ref[...], preferred_element_type=F32)

    @pl.when(kk == pl.num_programs(2) - 1)
    def _():
        o_ref[...] = acc_ref[...].astype(BF16)


def _fn_token(m, ab):
    return pl.pallas_call(
        _fn_token_kernel,
        grid=(BATCH, TT // MM_TM, 2 * TT // FN_TK),
        in_specs=[
            pl.BlockSpec((MM_TM, FN_TK), lambda b, i, kk: (i, kk)),
            pl.BlockSpec((None, FN_TK, D), lambda b, i, kk: (b, kk, 0)),
        ],
        out_specs=pl.BlockSpec((None, MM_TM, D), lambda b, i, kk: (b, i, 0)),
        out_shape=jax.ShapeDtypeStruct((BATCH, TT, D), BF16),
        scratch_shapes=[pltpu.VMEM((MM_TM, D), F32)],
        compiler_params=_cp(3),
        name="fourier_tokens",
    )(m, ab)


def _dft_tables():
    c = np.arange(FN_GC)
    ang = 2.0 * np.pi * ((c[:, None] * c[None, :]) % FN_GC) / FN_GC
    cs = np.concatenate([np.cos(ang), np.sin(ang)], axis=1) / np.sqrt(FN_GC)
    m = np.zeros((TT, 2 * TT), np.float32)
    for start, n in ((0, SEQ), (SEQ, CTX)):
        t = np.arange(n, dtype=np.int64)
        a = 2.0 * np.pi * ((t[:, None] * t[None, :]) % n) / n
        m[start:start + n, start:start + n] = np.cos(a) / np.sqrt(n)
        m[start:start + n, TT + start:TT + start + n] = -np.sin(a) / np.sqrt(n)
    return jnp.asarray(cs, BF16), jnp.asarray(m, BF16)


NA_ROWS = SEQ // GRID_W
NA_STEPS = TT // NA_QT
NA_NK = NA_KH * GRID_W
NA_D0 = NA_KH


def _na_row_start(step):
    r = jnp.minimum(step, NA_ROWS - 1)
    return r, jnp.clip(r - NA_KH // 2, 0, NA_ROWS - NA_KH)


def _na_bias_idx(step):
    r, rs = _na_row_start(step)
    return rs - r + (NA_KH - 1)


def _na_kernel(q_ref, k_ref, v_ref, bias_ref, o_ref):
    step = pl.program_id(1)
    lane = lax.broadcasted_iota(jnp.int32, (NA_QT, 128), 1)
    nt = (((1,), (1,)), ((), ()))
    scale = NA_HD ** -0.5

    def attend(start):
        for p in range(NA_HEADS // 2):
            sl = slice(p * 128, (p + 1) * 128)
            qp = q_ref[:, sl] * scale
            kc, vc = k_ref[SEQ:TT, sl], v_ref[SEQ:TT, sl]
            if start is not None:
                kw, vw = k_ref[pl.ds(start, NA_NK), sl], v_ref[pl.ds(start, NA_NK), sl]
            outs = []
            for hh in range(2):
                keep = (lane < NA_HD) if hh == 0 else (lane >= NA_HD)
                qm = jnp.where(keep, qp, jnp.zeros_like(qp))
                s_c = lax.dot_general(qm, kc, nt, preferred_element_type=F32)
                m = jnp.max(s_c, axis=-1, keepdims=True)
                if start is not None:
                    s_w = lax.dot_general(qm, kw, nt, preferred_element_type=F32) + bias_ref[2 * p + hh]
                    m = jnp.maximum(m, jnp.max(s_w, axis=-1, keepdims=True))
                p_c = jnp.exp(s_c - m)
                l = jnp.sum(p_c, axis=-1, keepdims=True)
                acc = jnp.dot(p_c.astype(BF16), vc, preferred_element_type=F32)
                if start is not None:
                    p_w = jnp.exp(s_w - m)
                    l = l + jnp.sum(p_w, axis=-1, keepdims=True)
                    acc = acc + jnp.dot(p_w.astype(BF16), vw, preferred_element_type=F32)
                outs.append(acc / l)
            o_ref[:, sl] = jnp.where(lane < NA_HD, outs[0], outs[1]).astype(BF16)

    @pl.when(step < NA_ROWS)
    def _():
        _, rs = _na_row_start(step)
        attend(pl.multiple_of(rs * GRID_W, GRID_W))

    @pl.when(step >= NA_ROWS)
    def _():
        attend(None)


def _na_attention(qkv, bias):
    return pl.pallas_call(
        _na_kernel,
        grid=(BATCH, NA_STEPS),
        in_specs=[
            pl.BlockSpec((None, NA_QT, D), lambda b, s: (b, s, 0)),
            pl.BlockSpec((None, TT, D), lambda b, s: (b, 0, 1), pipeline_mode=pl.Buffered(1)),
            pl.BlockSpec((None, TT, D), lambda b, s: (b, 0, 2), pipeline_mode=pl.Buffered(1)),
            pl.BlockSpec((None, NA_HEADS, NA_QT, NA_NK), lambda b, s: (_na_bias_idx(s), 0, 0, 0)),
        ],
        out_specs=pl.BlockSpec((None, NA_QT, D), lambda b, s: (b, s, 0)),
        out_shape=jax.ShapeDtypeStruct((BATCH, TT, D), BF16),
        compiler_params=_cp(2),
        name="na_attention",
    )(qkv, qkv, qkv, bias)


def _na_bias_table(rpb):
    qc = np.arange(GRID_W)
    kcol = np.arange(GRID_W)
    cstart = np.clip(qc - NA_KW // 2, 0, GRID_W - NA_KW)
    valid = (kcol[None, :] >= cstart[:, None]) & (kcol[None, :] < cstart[:, None] + NA_KW)
    dc = np.clip(kcol[None, :] - qc[:, None], -(NA_KW - 1), NA_KW - 1) + (NA_KW - 1)
    dr = np.arange(NA_D0)[:, None] + np.arange(NA_KH)[None, :]
    tab = rpb.astype(F32)[:, dr[:, :, None, None], dc[None, None, :, :]]
    tab = jnp.where(valid[None, None, None], tab, NEG_INF)
    tab = jnp.transpose(tab, (1, 0, 3, 2, 4))
    return tab.reshape(NA_D0, NA_HEADS, GRID_W, NA_NK)


def _router_kernel(h_ref, rw_ref, rb_ref, e_ref, w_ref):
    logits = lax.dot_general(rw_ref[...], h_ref[...], (((1,), (1,)), ((), ())),
                             precision=HIGHEST, preferred_element_type=F32)
    sc = jax.nn.sigmoid(logits)
    sel = sc + rb_ref[...]
    sc_r = [sc[e:e + 1, :] for e in range(N_EXPERTS)]
    sel_r = [sel[e:e + 1, :] for e in range(N_EXPERTS)]
    zero_i = jnp.zeros_like(sel_r[0]).astype(jnp.int32)

    best = gi = None
    picks = []
    for g in range(N_GROUPS):
        a = sel_r[g * EPG:(g + 1) * EPG]
        u = sc_r[g * EPG:(g + 1) * EPG]
        gs = None
        for i in range(EPG):
            for j in range(i + 1, EPG):
                pair = a[i] + a[j]
                gs = pair if gs is None else jnp.maximum(gs, pair)
        m1, i1, s1 = a[0], zero_i, u[0]
        for j in range(1, EPG):
            upd = a[j] > m1
            m1 = jnp.where(upd, a[j], m1)
            i1 = jnp.where(upd, j, i1)
            s1 = jnp.where(upd, u[j], s1)
        m2 = i2 = s2 = None
        for j in range(EPG):
            cand = jnp.where(i1 == j, -jnp.inf, a[j])
            if m2 is None:
                m2, i2, s2 = cand, zero_i, u[0]
            else:
                upd = cand > m2
                m2 = jnp.where(upd, cand, m2)
                i2 = jnp.where(upd, j, i2)
                s2 = jnp.where(upd, u[j], s2)
        picks.append((i1 + g * EPG, i2 + g * EPG, s1, s2))
        if best is None:
            best, gi = gs, zero_i
        else:
            upd = gs > best
            best = jnp.where(upd, gs, best)
            gi = jnp.where(upd, g, gi)

    e1, e2, s1, s2 = picks[0]
    for g in range(1, N_GROUPS):
        on = gi == g
        e1 = jnp.where(on, picks[g][0], e1)
        e2 = jnp.where(on, picks[g][1], e2)
        s1 = jnp.where(on, picks[g][2], s1)
        s2 = jnp.where(on, picks[g][3], s2)
    tot = s1 + s2
    e_ref[0:1, :] = e1
    e_ref[1:2, :] = e2
    w_ref[0:1, :] = s1 / tot
    w_ref[1:2, :] = s2 / tot


def _router(h, rw_t, rb):
    return pl.pallas_call(
        _router_kernel,
        grid=(NTOK // ROW_TILE,),
        in_specs=[
            pl.BlockSpec((ROW_TILE, D), lambda i: (i, 0)),
            pl.BlockSpec((N_EXPERTS, D), lambda i: (0, 0)),
            pl.BlockSpec((N_EXPERTS, 1), lambda i: (0, 0)),
        ],
        out_specs=[pl.BlockSpec((2, ROW_TILE), lambda i: (0, i))] * 2,
        out_shape=[jax.ShapeDtypeStruct((2, NTOK), jnp.int32), jax.ShapeDtypeStruct((2, NTOK), F32)],
        compiler_params=_cp(1),
        name="router",
    )(h, rw_t, rb)


def _row_copy(src_ref, src_row, dst_ref, dst_row, sem):
    return pltpu.make_async_copy(src_ref.at[pl.ds(src_row, 1)], dst_ref.at[pl.ds(dst_row, 1)], sem)


def _dispatch_kernel(dest_ref, h_ref, zero_ref, xs_ref, sem):
    del zero_ref

    def issue(r, carry):
        for kk in range(2):
            _row_copy(h_ref, r, xs_ref, dest_ref[0, 2 * r + kk], sem).start()
        return carry

    lax.fori_loop(0, ROW_TILE, issue, 0)

    def drain(r, carry):
        for kk in range(2):
            _row_copy(h_ref, r, xs_ref, dest_ref[0, 2 * r + kk], sem).wait()
        return carry

    lax.fori_loop(0, ROW_TILE, drain, 0)


def _dispatch(h, dest):
    return pl.pallas_call(
        _dispatch_kernel,
        grid=(NTOK // ROW_TILE,),
        in_specs=[
            pl.BlockSpec((None, 1, 2 * ROW_TILE), lambda i: (i, 0, 0), memory_space=pltpu.SMEM),
            pl.BlockSpec((ROW_TILE, D), lambda i: (i, 0)),
            pl.BlockSpec(memory_space=pl.ANY),
        ],
        out_specs=pl.BlockSpec(memory_space=pl.ANY),
        out_shape=jax.ShapeDtypeStruct((MOE_P, D), F32),
        scratch_shapes=[pltpu.SemaphoreType.DMA(())],
        input_output_aliases={2: 0},
        compiler_params=pltpu.CompilerParams(dimension_semantics=("arbitrary",), vmem_limit_bytes=VMEM_LIMIT,
                                             has_side_effects=True),
        name="moe_dispatch",
    )(dest.reshape(NTOK // ROW_TILE, 1, 2 * ROW_TILE), h, jnp.zeros((MOE_P, D), F32))


def _experts_kernel(be_ref, nu_ref, x_ref, wg_ref, wu_ref, wd_ref, y_ref, wgb, wub, wdb):
    i = pl.program_id(0)
    active = i < nu_ref[0]
    changed = (i == 0) | (be_ref[i] != be_ref[jnp.maximum(i - 1, 0)])

    @pl.when(active & changed)
    def _():
        wgb[...] = wg_ref[...].astype(BF16)
        wub[...] = wu_ref[...].astype(BF16)
        wdb[...] = wd_ref[...].astype(BF16)

    @pl.when(active)
    def _():
        x = x_ref[...].astype(BF16)
        g = jnp.dot(x, wgb[...], preferred_element_type=F32)
        u = jnp.dot(x, wub[...], preferred_element_type=F32)
        hh = (_silu(g) * u).astype(BF16)
        y_ref[...] = jnp.dot(hh, wdb[...], preferred_element_type=F32)

    @pl.when(jnp.logical_not(active))
    def _():
        y_ref[...] = jnp.zeros_like(y_ref)


def _experts(xs, block_e, n_used, w_gate, w_up, w_down):
    wspec = pl.BlockSpec((None, D, D), lambda i, be, nu: (be[i], 0, 0))
    grid_spec = pltpu.PrefetchScalarGridSpec(
        num_scalar_prefetch=2,
        grid=(MOE_NB,),
        in_specs=[pl.BlockSpec((MOE_BM, D), lambda i, be, nu: (i, 0)), wspec, wspec, wspec],
        out_specs=pl.BlockSpec((MOE_BM, D), lambda i, be, nu: (i, 0)),
        scratch_shapes=[pltpu.VMEM((D, D), BF16)] * 3,
    )
    return pl.pallas_call(
        _experts_kernel,
        grid_spec=grid_spec,
        out_shape=jax.ShapeDtypeStruct((MOE_P, D), F32),
        compiler_params=_cp(1),
        name="moe_experts",
    )(block_e, n_used, xs, w_gate, w_up, w_down)


def _combine_kernel(dest_ref, yb_ref, wt_ref, x_ref, mod_ref, modn_ref, gn_ref, *rest, final):
    if final:
        out_ref, ybuf, sem = rest
    else:
        xo_ref, h_ref, ybuf, sem = rest
    t = pl.program_id(1)

    def body():
        def issue(r, carry):
            for kk in range(2):
                _row_copy(yb_ref, dest_ref[0, 2 * r + kk], ybuf.at[kk], r, sem).start()
            return carry

        lax.fori_loop(0, ROW_TILE, issue, 0)

        def drain(r, carry):
            for kk in range(2):
                _row_copy(yb_ref, dest_ref[0, 2 * r + kk], ybuf.at[kk], r, sem).wait()
            return carry

        lax.fori_loop(0, ROW_TILE, drain, 0)
        f = wt_ref[:, 0:1] * ybuf[0] + wt_ref[:, 1:2] * ybuf[1]
        xn = x_ref[...] + mod_ref[5:6, :] * f
        if final:
            out_ref[...] = _normmod(xn, gn_ref[...], 0.0, 0.0)
        else:
            xo_ref[...] = xn
            h_ref[...] = _normmod(xn, gn_ref[...], modn_ref[0:1, :], modn_ref[1:2, :]).astype(BF16)

    if final:
        pl.when(t < LAT_TILES)(body)
    else:
        body()


def _combine(yb, dest, wts, xs, mod, mod_next, g_next, final):
    row = pl.BlockSpec((None, ROW_TILE, D), lambda b, t: (b, t, 0))
    modspec = pl.BlockSpec((None, None, 8, D), lambda b, t: (b, _mod_idx(t), 0, 0))
    if final:
        out_specs = pl.BlockSpec((None, ROW_TILE, D), lambda b, t: (b, jnp.minimum(t, LAT_TILES - 1), 0))
        out_shape = jax.ShapeDtypeStruct((BATCH, SEQ, D), F32)
    else:
        out_specs = [row, row]
        out_shape = [jax.ShapeDtypeStruct((BATCH, TT, D), F32), jax.ShapeDtypeStruct((BATCH, TT, D), BF16)]
    return pl.pallas_call(
        functools.partial(_combine_kernel, final=final),
        grid=(BATCH, TPB),
        in_specs=[
            pl.BlockSpec((None, 1, 2 * ROW_TILE), lambda b, t: (b * TPB + t, 0, 0), memory_space=pltpu.SMEM),
            pl.BlockSpec(memory_space=pl.ANY),
            pl.BlockSpec((ROW_TILE, 2), lambda b, t: (b * TPB + t, 0)),
            row, modspec, modspec,
            pl.BlockSpec((1, D), lambda b, t: (0, 0)),
        ],
        out_specs=out_specs,
        out_shape=out_shape,
        scratch_shapes=[pltpu.VMEM((2, ROW_TILE, D), F32), pltpu.SemaphoreType.DMA(())],
        compiler_params=_cp(2),
        name="moe_combine",
    )(dest.reshape(NTOK // ROW_TILE, 1, 2 * ROW_TILE), yb, wts, xs, mod, mod_next, g_next)


def _moe_plan(experts):
    flat_e = experts.reshape(-1)
    onehot = (flat_e[:, None] == jnp.arange(N_EXPERTS, dtype=jnp.int32)[None, :]).astype(jnp.int32)
    csum = jnp.cumsum(onehot, axis=0)
    rank = jnp.sum(csum * onehot, axis=1) - 1
    counts = csum[-1]
    padded = (counts + MOE_BM - 1) // MOE_BM * MOE_BM
    pend = jnp.cumsum(padded)
    pstart = pend - padded
    dest = (jnp.sum(pstart[None, :] * onehot, axis=1) + rank).astype(jnp.int32)
    block_e = jnp.minimum(
        jnp.searchsorted(pend, jnp.arange(MOE_NB, dtype=jnp.int32) * MOE_BM, side="right"), N_EXPERTS - 1
    ).astype(jnp.int32)
    n_used = (pend[-1:] // MOE_BM).astype(jnp.int32)
    return dest, block_e, n_used


def _moe(h, xs, mod, mod_next, g_next, rw_t, rb, w_gate, w_up, w_down, final):
    e, wts = _router(h.reshape(NTOK, D), rw_t, rb)
    dest, block_e, n_used = _moe_plan(e.T)
    xsorted = _dispatch(h.reshape(NTOK, D), dest)
    yb = _experts(xsorted, block_e, n_used, w_gate, w_up, w_down)
    return _combine(yb, dest, wts.T, xs, mod, mod_next, g_next, final)


def _rope_tables():
    nf = RET_DK // 4
    t = np.arange(SEQ)
    inv = ROPE_BASE ** (-np.arange(nf, dtype=np.float32) / nf)
    cos = np.ones((TT, 256), np.float32)
    sin = np.zeros((TT, 256), np.float32)
    for seg, pos in enumerate(((t // GRID_W).astype(np.float32), (t % GRID_W).astype(np.float32))):
        ang = (pos[:, None] * inv[None, :]).astype(np.float32)
        c, s = np.cos(ang), np.sin(ang)
        cos[:SEQ, seg * 128:(seg + 1) * 128] = np.concatenate([c, c], axis=1)
        sin[:SEQ, seg * 128:(seg + 1) * 128] = np.concatenate([-s, s], axis=1)
    ks = np.float32(RET_DK ** -0.5)
    return jnp.asarray(np.stack([cos, cos * ks])), jnp.asarray(np.stack([sin, sin * ks]))


def _decay_tables(decay_logit):
    C = RET_CHUNK
    log_g = jax.nn.log_sigmoid(decay_logit.astype(F32))
    pos = jnp.arange(C, dtype=F32)
    diff = pos[:, None] - pos[None, :]
    lg = log_g[:, :, None, None]
    dm_f = jnp.where(diff >= 0, jnp.exp(jnp.maximum(diff, 0.0)[None, None] * lg), 0.0)
    dm_b = jnp.where(diff <= 0, jnp.exp(jnp.maximum(-diff, 0.0)[None, None] * lg), 0.0)
    dmat = jnp.stack([dm_f[0], dm_b[1]])
    lgc = log_g[:, :, None]
    qdec = jnp.stack([jnp.exp((pos + 1.0)[None, :] * lgc[0]), jnp.exp((C - pos)[None, :] * lgc[1])])
    kdec = jnp.stack([jnp.exp((C - 1.0 - pos)[None, :] * lgc[0]), jnp.exp(pos[None, :] * lgc[1])])
    cdec = jnp.exp(C * log_g).reshape(-1)
    return dmat, qdec[..., None], kdec[..., None], cdec


def kernel(x, c, ctx, c_ctx, w_mod, b_mod, norm_mix, norm_ffn, norm_final, ret_w_in, ret_decay, ret_gn, ret_w_out,
           fn_w_out, na_w_qkv, na_rpb, na_w_out, router_w, router_bias, moe_w_gate, moe_w_up, moe_w_down):
    cvec = jnp.concatenate([c_ctx[None, :], c, jnp.zeros((8 - 1 - BATCH, D), F32)], axis=0)
    mod_all = _modulation(cvec, w_mod, b_mod).reshape(DEPTH, 8, N_MOD, D)
    pad = jnp.zeros((DEPTH, BATCH, 2, 8 - N_MOD, D), F32)
    mods = jnp.stack([jnp.broadcast_to(mod_all[:, 0:1], (DEPTH, BATCH, N_MOD, D)), mod_all[:, 1:1 + BATCH]], axis=2)
    mods = jnp.concatenate([mods, pad], axis=3)

    rope = _rope_tables()
    cs, dft_m = _dft_tables()
    rw_t = router_w.T
    rb = router_bias.reshape(N_EXPERTS, 1).astype(F32)

    xs, h = _embed(x, ctx, mods[0], norm_mix[0:1])
    out = None
    for i in range(DEPTH):
        kind, j = i % 3, i // 3
        if kind == 0:
            qkvg = _project(h.reshape(NTOK, D), ret_w_in[j], rope).reshape(BATCH, TT, 6 * D)
            dmat, qdec, kdec, cdec = _decay_tables(ret_decay[j])
            o_f, o_b = _retention(qkvg, dmat, qdec, kdec, cdec)
            xs, hf = _mixer_out("ret", (o_f, o_b, qkvg, ret_gn[j:j + 1]), ret_w_out[j], xs, mods[i], norm_ffn[i:i + 1])
        elif kind == 1:
            ab = _fn_channel(h, cs).reshape(BATCH, 2 * TT, D)
            f = _fn_token(dft_m, ab)
            xs, hf = _mixer_out("plain", (f,), fn_w_out[j], xs, mods[i], norm_ffn[i:i + 1])
        else:
            qkv = _project(h.reshape(NTOK, D), na_w_qkv[j]).reshape(BATCH, TT, 3 * D)
            o = _na_attention(qkv, _na_bias_table(na_rpb[j]))
            xs, hf = _mixer_out("plain", (o,), na_w_out[j], xs, mods[i], norm_ffn[i:i + 1])
        final = i == DEPTH - 1
        if final:
            mod_next, g_next = mods[i], norm_final[None, :]
        else:
            mod_next, g_next = mods[i + 1], norm_mix[i + 1:i + 2]
        res = _moe(hf, xs, mods[i], mod_next, g_next, rw_t, rb, moe_w_gate[i], moe_w_up[i], moe_w_down[i], final)
        if final:
            out = res
        else:
            xs, h = res
    return out
```

```python
import functools

import numpy as np
import jax
import jax.numpy as jnp
from jax import lax
from jax.experimental import pallas as pl
from jax.experimental.pallas import tpu as pltpu

F32 = jnp.float32
BF16 = jnp.bfloat16
HIGHEST = lax.Precision.HIGHEST

D = 1024
BATCH = 4
SEQ = 4096
CTX = 256
TT = SEQ + CTX
NTOK = BATCH * TT
DEPTH = 4
GRID_W = 64
EPS = 1e-6
NEG_INF = -1e30
N_MOD = 6

RET_HEADS = 4
RET_DK = D // RET_HEADS
RET_DV = 2 * RET_DK
RET_CHUNK = 128
ROPE_BASE = 10000.0

FN_GROUPS = 4
FN_GC = D // FN_GROUPS

NA_HEADS = 16
NA_HD = D // NA_HEADS
NA_KH = 8
NA_KW = 16

N_EXPERTS = 16
N_GROUPS = 4
EPG = N_EXPERTS // N_GROUPS

ROW_TILE = 256
TPB = TT // ROW_TILE
LAT_TILES = SEQ // ROW_TILE
MM_TM = TT // 4
MOE_BM = 256
MOE_S = NTOK * 2
MOE_NB = (MOE_S + N_EXPERTS * (MOE_BM - 1) + MOE_BM - 1) // MOE_BM
MOE_P = MOE_NB * MOE_BM
MOE_TAIL = MOE_NB - MOE_S // MOE_BM
MOE_NZ = N_EXPERTS + MOE_TAIL
NA_QT = GRID_W
VMEM_LIMIT = 56 * 1024 * 1024


def _cp(n_axes, vmem=VMEM_LIMIT):
    return pltpu.CompilerParams(dimension_semantics=("arbitrary",) * n_axes, vmem_limit_bytes=vmem)


def _silu(x):
    return x * jax.nn.sigmoid(x)


def _normmod(x, g, shift, scale):
    y = x * lax.rsqrt(jnp.mean(x * x, axis=-1, keepdims=True) + EPS)
    return (y * g) * (1.0 + scale) + shift


def _mod_idx(t):
    return jnp.where(t < LAT_TILES, 1, 0)


def _layer_spec(shape, layer):
    return pl.BlockSpec((None,) + shape, lambda *_: (layer,) + (0,) * len(shape))


def _mod_spec(depth):
    return pl.BlockSpec((None, None, None, 8, D), lambda b, t: (depth, b, _mod_idx(t), 0, 0))


def _mod_kernel(c_ref, w_ref, b_ref, o_ref):
    a = _silu(c_ref[...])
    o_ref[...] = jnp.dot(a, w_ref[...], precision=HIGHEST, preferred_element_type=F32) + b_ref[...]


def _modulation(cvec, w_mod, b_mod):
    return pl.pallas_call(
        _mod_kernel,
        grid=(DEPTH, N_MOD),
        in_specs=[
            pl.BlockSpec((8, D), lambda i, j: (0, 0)),
            pl.BlockSpec((None, D, D), lambda i, j: (i, 0, j)),
            pl.BlockSpec((None, 1, D), lambda i, j: (i, 0, j)),
        ],
        out_specs=pl.BlockSpec((None, 8, D), lambda i, j: (i, 0, j)),
        out_shape=jax.ShapeDtypeStruct((DEPTH, 8, N_MOD * D), F32),
        compiler_params=_cp(2),
        name="modulation",
    )(cvec, w_mod, b_mod.reshape(DEPTH, 1, N_MOD * D))


def _embed_kernel(x_ref, ctx_ref, mod_ref, g_ref, xs_ref, h_ref):
    t = pl.program_id(1)

    def emit(src):
        xs_ref[...] = src
        h_ref[...] = _normmod(src, g_ref[...], mod_ref[0:1, :], mod_ref[1:2, :]).astype(BF16)

    @pl.when(t < LAT_TILES)
    def _():
        emit(x_ref[...])

    @pl.when(t >= LAT_TILES)
    def _():
        emit(ctx_ref[...])


def _embed(x, ctx, mods, g):
    return pl.pallas_call(
        _embed_kernel,
        grid=(BATCH, TPB),
        in_specs=[
            pl.BlockSpec((None, ROW_TILE, D), lambda b, t: (b, jnp.minimum(t, LAT_TILES - 1), 0)),
            pl.BlockSpec((None, CTX, D), lambda b, t: (b, 0, 0)),
            _mod_spec(0),
            _layer_spec((1, D), 0),
        ],
        out_specs=[
            pl.BlockSpec((None, ROW_TILE, D), lambda b, t: (b, t, 0)),
            pl.BlockSpec((None, ROW_TILE, D), lambda b, t: (b, t, 0)),
        ],
        out_shape=[
            jax.ShapeDtypeStruct((BATCH, TT, D), F32),
            jax.ShapeDtypeStruct((BATCH, TT, D), BF16),
        ],
        compiler_params=_cp(2),
        name="embed",
    )(x, ctx, mods, g)


def _rope_store(acc, cos_ref, sin_ref, o_ref):
    for c in range(acc.shape[1] // 128):
        xc = acc[:, c * 128:(c + 1) * 128]
        tsl = slice((c % 2) * 128, (c % 2 + 1) * 128)
        rot = xc * cos_ref[:, tsl] + pltpu.roll(xc, 64, 1) * sin_ref[:, tsl]
        o_ref[:, c * 128:(c + 1) * 128] = rot.astype(o_ref.dtype)


def _proj_kernel(a_ref, w_ref, *rest, n_rope):
    if n_rope:
        cos_ref, sin_ref, o_ref, wb_ref = rest
    else:
        o_ref, wb_ref = rest
    j = pl.program_id(0)

    @pl.when(pl.program_id(1) == 0)
    def _():
        wb_ref[...] = w_ref[...].astype(BF16)

    acc = jnp.dot(a_ref[...], wb_ref[...], preferred_element_type=F32)
    if n_rope:
        @pl.when(j < n_rope)
        def _():
            _rope_store(acc, cos_ref, sin_ref, o_ref)

        @pl.when(j >= n_rope)
        def _():
            o_ref[...] = acc.astype(o_ref.dtype)
    else:
        o_ref[...] = acc.astype(o_ref.dtype)


def _project(a, w, layer, rope=None):
    _, k, n = w.shape
    tn = D
    n_rope = 0 if rope is None else rope[0].shape[0]
    in_specs = [
        pl.BlockSpec((MM_TM, k), lambda j, i: (i, 0)),
        pl.BlockSpec((None, k, tn), lambda j, i: (layer, 0, j)),
    ]
    args = [a, w]
    if n_rope:
        tspec = pl.BlockSpec((None, MM_TM, 256), lambda j, i: (jnp.minimum(j, n_rope - 1), i % (TT // MM_TM), 0))
        in_specs += [tspec, tspec]
        args += list(rope)
    return pl.pallas_call(
        functools.partial(_proj_kernel, n_rope=n_rope),
        grid=(n // tn, NTOK // MM_TM),
        in_specs=in_specs,
        out_specs=pl.BlockSpec((MM_TM, tn), lambda j, i: (i, j)),
        out_shape=jax.ShapeDtypeStruct((NTOK, n), BF16),
        scratch_shapes=[pltpu.VMEM((k, tn), BF16)],
        compiler_params=_cp(2),
        name="project",
    )(*args)


N_CHUNKS = TT // RET_CHUNK
LAT_CHUNKS = SEQ // RET_CHUNK
CTX_CHUNKS = CTX // RET_CHUNK


def _fwd_chunk(n):
    return jnp.where(n < CTX_CHUNKS, LAT_CHUNKS + n, n - CTX_CHUNKS)


def _bwd_chunk(n):
    return N_CHUNKS - 1 - n


def _ret_kernel(cdec_ref, qf, kf, vf, qb, kb, vb, dmat_ref, qdec_ref, kdec_ref, of_ref, ob_ref, s_ref):
    @pl.when(pl.program_id(1) == 0)
    def _():
        s_ref[...] = jnp.zeros_like(s_ref)

    for d, (q_r, k_r, v_r, o_r) in enumerate(((qf, kf, vf, of_ref), (qb, kb, vb, ob_ref))):
        for h in range(RET_HEADS):
            q = q_r[:, h * RET_DK:(h + 1) * RET_DK]
            k = k_r[:, h * RET_DK:(h + 1) * RET_DK]
            v = v_r[:, h * RET_DV:(h + 1) * RET_DV]
            att = lax.dot_general(q, k, (((1,), (1,)), ((), ())), preferred_element_type=F32) * dmat_ref[d, h]
            s = s_ref[d, h]
            qd = (q.astype(F32) * qdec_ref[d, h]).astype(BF16)
            o = (jnp.dot(att.astype(BF16), v, preferred_element_type=F32)
                 + jnp.dot(qd, s.astype(BF16), preferred_element_type=F32))
            kd = (k.astype(F32) * kdec_ref[d, h]).astype(BF16)
            s_ref[d, h] = s * cdec_ref[d * RET_HEADS + h] + lax.dot_general(
                kd, v, (((0,), (0,)), ((), ())), preferred_element_type=F32)
            o_r[:, h * RET_DV:(h + 1) * RET_DV] = o.astype(o_r.dtype)


def _retention(qkvg, dmat, qdec, kdec, cdec):
    C = RET_CHUNK

    def spec(width, col, chunk_fn):
        return pl.BlockSpec((None, C, width), lambda b, n, cd: (b, chunk_fn(n), col))

    ospec_f = pl.BlockSpec((None, C, 2 * D), lambda b, n, cd: (b, _fwd_chunk(n), 0))
    ospec_b = pl.BlockSpec((None, C, 2 * D), lambda b, n, cd: (b, _bwd_chunk(n), 0))
    full = lambda shape: pl.BlockSpec(shape, lambda b, n, cd: (0,) * len(shape))
    grid_spec = pltpu.PrefetchScalarGridSpec(
        num_scalar_prefetch=1,
        grid=(BATCH, N_CHUNKS),
        in_specs=[
            spec(D, 0, _fwd_chunk), spec(D, 1, _fwd_chunk), spec(2 * D, 1, _fwd_chunk),
            spec(D, 0, _bwd_chunk), spec(D, 1, _bwd_chunk), spec(2 * D, 1, _bwd_chunk),
            full((2, RET_HEADS, C, C)), full((2, RET_HEADS, C, 1)), full((2, RET_HEADS, C, 1)),
        ],
        out_specs=[ospec_f, ospec_b],
        scratch_shapes=[pltpu.VMEM((2, RET_HEADS, RET_DK, RET_DV), F32)],
    )
    return pl.pallas_call(
        _ret_kernel,
        grid_spec=grid_spec,
        out_shape=[jax.ShapeDtypeStruct((BATCH, TT, 2 * D), BF16)] * 2,
        compiler_params=_cp(2),
        name="retention",
    )(cdec, qkvg, qkvg, qkvg, qkvg, qkvg, qkvg, dmat, qdec, kdec)


def _mixer_out_kernel(*refs, kind):
    if kind == "ret":
        of_ref, ob_ref, g_ref, gn_ref, w_ref, x_ref, mod_ref, nf_ref, xo_ref, h_ref, wb_ref = refs
    else:
        a_ref, w_ref, x_ref, mod_ref, nf_ref, xo_ref, h_ref, wb_ref = refs

    @pl.when((pl.program_id(0) == 0) & (pl.program_id(1) == 0))
    def _():
        wb_ref[...] = w_ref[...].astype(BF16)

    if kind == "ret":
        o = of_ref[...].astype(F32) + ob_ref[...].astype(F32)
        parts = []
        for h in range(RET_HEADS):
            oh = o[:, h * RET_DV:(h + 1) * RET_DV]
            mu = jnp.mean(oh, axis=-1, keepdims=True)
            ctr = oh - mu
            var = jnp.mean(ctr * ctr, axis=-1, keepdims=True)
            parts.append(ctr * lax.rsqrt(var + EPS))
        on = jnp.concatenate(parts, axis=-1) * gn_ref[...]
        a = (on * _silu(g_ref[...].astype(F32))).astype(BF16)
    else:
        a = a_ref[...]
    y = jnp.dot(a, wb_ref[...], preferred_element_type=F32)
    xn = x_ref[...] + mod_ref[2:3, :] * y
    xo_ref[...] = xn
    h_ref[...] = _normmod(xn, nf_ref[...], mod_ref[3:4, :], mod_ref[4:5, :])


def _mixer_out(kind, srcs, w, layer, xs, mods, norm_ffn, depth):
    k = w.shape[1]
    row = lambda width, col=0: pl.BlockSpec((None, ROW_TILE, width), lambda b, t: (b, t, col))
    if kind == "ret":
        src_specs = [row(2 * D), row(2 * D), row(2 * D, 2), _layer_spec((1, 2 * D), layer)]
    else:
        src_specs = [row(k)]
    return pl.pallas_call(
        functools.partial(_mixer_out_kernel, kind=kind),
        grid=(BATCH, TPB),
        in_specs=src_specs + [_layer_spec((k, D), layer), row(D), _mod_spec(depth), _layer_spec((1, D), depth)],
        out_specs=[row(D), row(D)],
        out_shape=[jax.ShapeDtypeStruct((BATCH, TT, D), F32)] * 2,
        scratch_shapes=[pltpu.VMEM((k, D), BF16)],
        compiler_params=_cp(2),
        name="mixer_out_" + kind,
    )(*srcs, w, xs, mods, norm_ffn)


def _fn_channel_kernel(h_ref, cs_ref, o_ref):
    for g in range(FN_GROUPS):
        r = jnp.dot(h_ref[:, g * FN_GC:(g + 1) * FN_GC], cs_ref[...], preferred_element_type=F32)
        o_ref[0, :, g * FN_GC:(g + 1) * FN_GC] = r[:, :FN_GC].astype(BF16)
        o_ref[1, :, g * FN_GC:(g + 1) * FN_GC] = r[:, FN_GC:].astype(BF16)


def _fn_channel(h, cs):
    return pl.pallas_call(
        _fn_channel_kernel,
        grid=(BATCH, TPB),
        in_specs=[
            pl.BlockSpec((None, ROW_TILE, D), lambda b, t: (b, t, 0)),
            pl.BlockSpec((FN_GC, 2 * FN_GC), lambda b, t: (0, 0)),
        ],
        out_specs=pl.BlockSpec((None, 2, ROW_TILE, D), lambda b, t: (b, 0, t, 0)),
        out_shape=jax.ShapeDtypeStruct((BATCH, 2, TT, D), BF16),
        compiler_params=_cp(2),
        name="fourier_channels",
    )(h, cs)


FN_TK = 2 * TT // 4


def _fn_token_kernel(m_ref, ab_ref, o_ref, acc_ref):
    kk = pl.program_id(2)

    @pl.when(kk == 0)
    def _():
        acc_ref[...] = jnp.zeros_like(acc_ref)

    acc_ref[...] += jnp.dot(m_ref[...], ab_ref[...], preferred_element_type=F32)

    @pl.when(kk == pl.num_programs(2) - 1)
    def _():
        o_ref[...] = acc_ref[...].astype(BF16)


def _fn_token(m, ab):
    return pl.pallas_call(
        _fn_token_kernel,
        grid=(BATCH, TT // MM_TM, 2 * TT // FN_TK),
        in_specs=[
            pl.BlockSpec((MM_TM, FN_TK), lambda b, i, kk: (i, kk)),
            pl.BlockSpec((None, FN_TK, D), lambda b, i, kk: (b, kk, 0)),
        ],
        out_specs=pl.BlockSpec((None, MM_TM, D), lambda b, i, kk: (b, i, 0)),
        out_shape=jax.ShapeDtypeStruct((BATCH, TT, D), BF16),
        scratch_shapes=[pltpu.VMEM((MM_TM, D), F32)],
        compiler_params=_cp(3),
        name="fourier_tokens",
    )(m, ab)


def _dft_tables():
    c = np.arange(FN_GC)
    ang = 2.0 * np.pi * ((c[:, None] * c[None, :]) % FN_GC) / FN_GC
    cs = np.concatenate([np.cos(ang), np.sin(ang)], axis=1) / np.sqrt(FN_GC)
    m = np.zeros((TT, 2 * TT), np.float32)
    for start, n in ((0, SEQ), (SEQ, CTX)):
        t = np.arange(n, dtype=np.int64)
        a = 2.0 * np.pi * ((t[:, None] * t[None, :]) % n) / n
        m[start:start + n, start:start + n] = np.cos(a) / np.sqrt(n)
        m[start:start + n, TT + start:TT + start + n] = -np.sin(a) / np.sqrt(n)
    return jnp.asarray(cs, BF16), jnp.asarray(m, BF16)


NA_ROWS = SEQ // GRID_W
NA_STEPS = TT // NA_QT
NA_NK = NA_KH * GRID_W
NA_DR = 2 * NA_KH - 1


def _na_row_start(step):
    r = jnp.minimum(step, NA_ROWS - 1)
    return r, jnp.clip(r - NA_KH // 2, 0, NA_ROWS - NA_KH)


def _na_bias_idx(step):
    r, rs = _na_row_start(step)
    return rs - r + (NA_KH - 1)


def _na_kernel(q_ref, k_ref, v_ref, bias_ref, o_ref):
    step = pl.program_id(1)
    lane = lax.broadcasted_iota(jnp.int32, (NA_QT, 128), 1)
    nt = (((1,), (1,)), ((), ()))
    scale = NA_HD ** -0.5

    def attend(start, d0=None):
        for p in range(NA_HEADS // 2):
            sl = slice(p * 128, (p + 1) * 128)
            qp = q_ref[:, sl] * scale
            zero = jnp.zeros_like(qp)
            q2 = jnp.concatenate([jnp.where(lane < NA_HD, qp, zero), jnp.where(lane >= NA_HD, qp, zero)], axis=0)
            s_c = lax.dot_general(q2, k_ref[SEQ:TT, sl], nt, preferred_element_type=F32)
            m = jnp.max(s_c, axis=-1, keepdims=True)
            if start is not None:
                bias = jnp.concatenate(
                    [jnp.concatenate([bias_ref[2 * p + hh, d0 + 2 * j2] for j2 in range(NA_KH // 2)], axis=1)
                     for hh in range(2)], axis=0)
                s_w = lax.dot_general(q2, k_ref[pl.ds(start, NA_NK), sl], nt, preferred_element_type=F32) + bias
                m = jnp.maximum(m, jnp.max(s_w, axis=-1, keepdims=True))
            p_c = jnp.exp(s_c - m)
            l = jnp.sum(p_c, axis=-1, keepdims=True)
            acc = jnp.dot(p_c.astype(BF16), v_ref[SEQ:TT, sl], preferred_element_type=F32)
            if start is not None:
                p_w = jnp.exp(s_w - m)
                l = l + jnp.sum(p_w, axis=-1, keepdims=True)
                acc = acc + jnp.dot(p_w.astype(BF16), v_ref[pl.ds(start, NA_NK), sl], preferred_element_type=F32)
            o2 = acc / l
            o_ref[:, sl] = jnp.where(lane < NA_HD, o2[:NA_QT], o2[NA_QT:]).astype(BF16)

    @pl.when(step < NA_ROWS)
    def _():
        _, rs = _na_row_start(step)
        attend(pl.multiple_of(rs * GRID_W, GRID_W), _na_bias_idx(step))

    @pl.when(step >= NA_ROWS)
    def _():
        attend(None)


def _na_attention(qkv, bias):
    return pl.pallas_call(
        _na_kernel,
        grid=(BATCH, NA_STEPS),
        in_specs=[
            pl.BlockSpec((None, NA_QT, D), lambda b, s: (b, s, 0)),
            pl.BlockSpec((None, TT, D), lambda b, s: (b, 0, 1), pipeline_mode=pl.Buffered(1)),
            pl.BlockSpec((None, TT, D), lambda b, s: (b, 0, 2), pipeline_mode=pl.Buffered(1)),
            pl.BlockSpec((NA_HEADS, NA_DR - 1, GRID_W, 2 * GRID_W), lambda b, s: (0, 0, 0, 0),
                         pipeline_mode=pl.Buffered(1)),
        ],
        out_specs=pl.BlockSpec((None, NA_QT, D), lambda b, s: (b, s, 0)),
        out_shape=jax.ShapeDtypeStruct((BATCH, TT, D), BF16),
        compiler_params=_cp(2),
        name="na_attention",
    )(qkv, qkv, qkv, bias)


def _na_bias_table(rpb):
    qc = np.arange(GRID_W)
    kcol = np.arange(GRID_W)
    cstart = np.clip(qc - NA_KW // 2, 0, GRID_W - NA_KW)
    valid = (kcol[None, :] >= cstart[:, None]) & (kcol[None, :] < cstart[:, None] + NA_KW)
    dc = kcol[None, :] - qc[:, None] + (NA_KW - 1)
    pick = ((dc[None] == np.arange(2 * NA_KW - 1)[:, None, None]) & valid[None]).astype(np.float32)
    tab = jnp.einsum("hdj,jqk->hdqk", rpb.astype(F32), jnp.asarray(pick), precision=HIGHEST)
    tab = tab + jnp.asarray(np.where(valid, 0.0, NEG_INF).astype(np.float32))
    return jnp.concatenate([tab[:, :NA_DR - 1], tab[:, 1:]], axis=-1)


def _router_kernel(h_ref, rw_ref, rb_ref, tri_ref, e_ref, w_ref, rank_ref, cnt_ref):
    logits = lax.dot_general(rw_ref[...], h_ref[...], (((1,), (1,)), ((), ())),
                             precision=HIGHEST, preferred_element_type=F32)
    sc = jax.nn.sigmoid(logits)
    sel = sc + rb_ref[...]
    sc_r = [sc[e:e + 1, :] for e in range(N_EXPERTS)]
    sel_r = [sel[e:e + 1, :] for e in range(N_EXPERTS)]
    zero_i = jnp.zeros_like(sel_r[0]).astype(jnp.int32)

    best = gi = None
    picks = []
    for g in range(N_GROUPS):
        a = sel_r[g * EPG:(g + 1) * EPG]
        u = sc_r[g * EPG:(g + 1) * EPG]
        gs = None
        for i in range(EPG):
            for j in range(i + 1, EPG):
                pair = a[i] + a[j]
                gs = pair if gs is None else jnp.maximum(gs, pair)
        m1, i1, s1 = a[0], zero_i, u[0]
        for j in range(1, EPG):
            upd = a[j] > m1
            m1 = jnp.where(upd, a[j], m1)
            i1 = jnp.where(upd, j, i1)
            s1 = jnp.where(upd, u[j], s1)
        m2 = i2 = s2 = None
        for j in range(EPG):
            cand = jnp.where(i1 == j, -jnp.inf, a[j])
            if m2 is None:
                m2, i2, s2 = cand, zero_i, u[0]
            else:
                upd = cand > m2
                m2 = jnp.where(upd, cand, m2)
                i2 = jnp.where(upd, j, i2)
                s2 = jnp.where(upd, u[j], s2)
        picks.append((i1 + g * EPG, i2 + g * EPG, s1, s2))
        if best is None:
            best, gi = gs, zero_i
        else:
            upd = gs > best
            best = jnp.where(upd, gs, best)
            gi = jnp.where(upd, g, gi)

    e1, e2, s1, s2 = picks[0]
    for g in range(1, N_GROUPS):
        on = gi == g
        e1 = jnp.where(on, picks[g][0], e1)
        e2 = jnp.where(on, picks[g][1], e2)
        s1 = jnp.where(on, picks[g][2], s1)
        s2 = jnp.where(on, picks[g][3], s2)
    tot = s1 + s2
    e_ref[0:1, :] = e1
    e_ref[1:2, :] = e2
    w_ref[0:1, :] = s1 / tot
    w_ref[1:2, :] = s2 / tot

    @pl.when(pl.program_id(0) == 0)
    def _():
        cnt_ref[...] = jnp.zeros_like(cnt_ref)

    eid = lax.broadcasted_iota(jnp.int32, sc.shape, 0)
    oh1 = (eid == e1).astype(F32)
    oh2 = (eid == e2).astype(F32)
    before1 = jnp.dot(oh1.astype(BF16), tri_ref[...], preferred_element_type=F32)
    before2 = jnp.dot(oh2.astype(BF16), tri_ref[...], preferred_element_type=F32)
    tot1 = jnp.sum(oh1, axis=1, keepdims=True)
    tot2 = jnp.sum(oh2, axis=1, keepdims=True)
    base = cnt_ref[:, 0:1]
    rank_ref[0:1, :] = jnp.sum(oh1 * (base + before1), axis=0, keepdims=True).astype(jnp.int32)
    rank_ref[1:2, :] = jnp.sum(oh2 * (base + tot1 + before2), axis=0, keepdims=True).astype(jnp.int32)
    cnt_ref[...] = cnt_ref[...] + (tot1 + tot2)


def _router(h, rw_t, rb):
    tri = jnp.asarray(np.triu(np.ones((ROW_TILE, ROW_TILE), np.float32), 1), BF16)
    return pl.pallas_call(
        _router_kernel,
        grid=(NTOK // ROW_TILE,),
        in_specs=[
            pl.BlockSpec((ROW_TILE, D), lambda i: (i, 0)),
            pl.BlockSpec((N_EXPERTS, D), lambda i: (0, 0)),
            pl.BlockSpec((N_EXPERTS, 1), lambda i: (0, 0)),
            pl.BlockSpec((ROW_TILE, ROW_TILE), lambda i: (0, 0)),
        ],
        out_specs=[pl.BlockSpec((2, ROW_TILE), lambda i: (0, i))] * 3
        + [pl.BlockSpec((N_EXPERTS, 128), lambda i: (0, 0))],
        out_shape=[jax.ShapeDtypeStruct((2, NTOK), jnp.int32), jax.ShapeDtypeStruct((2, NTOK), F32),
                   jax.ShapeDtypeStruct((2, NTOK), jnp.int32), jax.ShapeDtypeStruct((N_EXPERTS, 128), F32)],
        compiler_params=_cp(1),
        name="router",
    )(h, rw_t, rb, tri)


DMA_UNROLL = 8


def _row_copy(src_ref, src_row, dst_ref, dst_row, sem):
    return pltpu.make_async_copy(src_ref.at[pl.ds(src_row, 1)], dst_ref.at[pl.ds(dst_row, 1)], sem)


def _for_each_row_pair(fn):
    def group(gidx, carry):
        for u in range(DMA_UNROLL):
            for kk in range(2):
                fn(gidx * DMA_UNROLL + u, kk)
        return carry

    lax.fori_loop(0, ROW_TILE // DMA_UNROLL, group, 0)


def _dispatch_kernel(zrow_ref, dest_ref, h_ref, xs_ref, zbuf, sem, zsem):
    @pl.when(pl.program_id(0) == 0)
    def _():
        zbuf[...] = jnp.zeros_like(zbuf)

        def zero_copy(e):
            start = pl.multiple_of(jnp.maximum(zrow_ref[e], 0), MOE_BM)
            return pltpu.make_async_copy(zbuf, xs_ref.at[pl.ds(start, MOE_BM)], zsem)

        for e in range(MOE_NZ):
            pl.when(zrow_ref[e] >= 0)(lambda e=e: zero_copy(e).start())
        for e in range(MOE_NZ):
            pl.when(zrow_ref[e] >= 0)(lambda e=e: zero_copy(e).wait())

    _for_each_row_pair(lambda r, kk: _row_copy(h_ref, r, xs_ref, dest_ref[kk, r], sem).start())
    _for_each_row_pair(lambda r, kk: _row_copy(h_ref, r, xs_ref, dest_ref[kk, r], sem).wait())


def _dispatch(h, dest, zrow):
    grid_spec = pltpu.PrefetchScalarGridSpec(
        num_scalar_prefetch=1,
        grid=(NTOK // ROW_TILE,),
        in_specs=[
            pl.BlockSpec((None, 2, ROW_TILE), lambda i, z: (i, 0, 0), memory_space=pltpu.SMEM),
            pl.BlockSpec((ROW_TILE, D), lambda i, z: (i, 0)),
        ],
        out_specs=pl.BlockSpec(memory_space=pl.ANY),
        scratch_shapes=[pltpu.VMEM((MOE_BM, D), F32), pltpu.SemaphoreType.DMA(()), pltpu.SemaphoreType.DMA(())],
    )
    return pl.pallas_call(
        _dispatch_kernel,
        grid_spec=grid_spec,
        out_shape=jax.ShapeDtypeStruct((MOE_P, D), F32),
        compiler_params=pltpu.CompilerParams(dimension_semantics=("arbitrary",), vmem_limit_bytes=VMEM_LIMIT,
                                             has_side_effects=True),
        name="moe_dispatch",
    )(zrow, dest, h)


def _experts_kernel(be_ref, nu_ref, x_ref, wg_ref, wu_ref, wd_ref, y_ref, wgb, wub, wdb):
    i = pl.program_id(0)
    active = i < nu_ref[0]
    changed = (i == 0) | (be_ref[i] != be_ref[jnp.maximum(i - 1, 0)])

    @pl.when(active & changed)
    def _():
        wgb[...] = wg_ref[...].astype(BF16)
        wub[...] = wu_ref[...].astype(BF16)
        wdb[...] = wd_ref[...].astype(BF16)

    @pl.when(active)
    def _():
        x = x_ref[...].astype(BF16)
        g = jnp.dot(x, wgb[...], preferred_element_type=F32)
        u = jnp.dot(x, wub[...], preferred_element_type=F32)
        hh = (_silu(g) * u).astype(BF16)
        y_ref[...] = jnp.dot(hh, wdb[...], preferred_element_type=F32)

    @pl.when(jnp.logical_not(active))
    def _():
        y_ref[...] = jnp.zeros_like(y_ref)


def _experts(xs, block_e, n_used, w_gate, w_up, w_down, depth):
    wspec = pl.BlockSpec((None, None, D, D), lambda i, be, nu: (depth, be[i], 0, 0))
    grid_spec = pltpu.PrefetchScalarGridSpec(
        num_scalar_prefetch=2,
        grid=(MOE_NB,),
        in_specs=[pl.BlockSpec((MOE_BM, D), lambda i, be, nu: (jnp.minimum(i, nu[0] - 1), 0)), wspec, wspec, wspec],
        out_specs=pl.BlockSpec((MOE_BM, D), lambda i, be, nu: (i, 0)),
        scratch_shapes=[pltpu.VMEM((D, D), BF16)] * 3,
    )
    return pl.pallas_call(
        _experts_kernel,
        grid_spec=grid_spec,
        out_shape=jax.ShapeDtypeStruct((MOE_P, D), F32),
        compiler_params=_cp(1),
        name="moe_experts",
    )(block_e, n_used, xs, w_gate, w_up, w_down)


def _combine_kernel(dest_ref, yb_ref, wt_ref, x_ref, mod_ref, modn_ref, gn_ref, *rest, final):
    if final:
        out_ref, ybuf, sem = rest
    else:
        xo_ref, h_ref, ybuf, sem = rest
    t = pl.program_id(1)

    def body():
        _for_each_row_pair(lambda r, kk: _row_copy(yb_ref, dest_ref[kk, r], ybuf.at[kk], r, sem).start())
        _for_each_row_pair(lambda r, kk: _row_copy(yb_ref, dest_ref[kk, r], ybuf.at[kk], r, sem).wait())
        f = wt_ref[:, 0:1] * ybuf[0] + wt_ref[:, 1:2] * ybuf[1]
        xn = x_ref[...] + mod_ref[5:6, :] * f
        if final:
            out_ref[...] = _normmod(xn, gn_ref[...], 0.0, 0.0)
        else:
            xo_ref[...] = xn
            h_ref[...] = _normmod(xn, gn_ref[...], modn_ref[0:1, :], modn_ref[1:2, :]).astype(BF16)

    if final:
        pl.when(t < LAT_TILES)(body)
    else:
        body()


def _combine(yb, dest, wts, xs, mods, depth, g_next, final):
    row = pl.BlockSpec((None, ROW_TILE, D), lambda b, t: (b, t, 0))
    next_depth = depth if final else depth + 1
    if final:
        out_specs = pl.BlockSpec((None, ROW_TILE, D), lambda b, t: (b, jnp.minimum(t, LAT_TILES - 1), 0))
        out_shape = jax.ShapeDtypeStruct((BATCH, SEQ, D), F32)
    else:
        out_specs = [row, row]
        out_shape = [jax.ShapeDtypeStruct((BATCH, TT, D), F32), jax.ShapeDtypeStruct((BATCH, TT, D), BF16)]
    return pl.pallas_call(
        functools.partial(_combine_kernel, final=final),
        grid=(BATCH, TPB),
        in_specs=[
            pl.BlockSpec((None, 2, ROW_TILE), lambda b, t: (b * TPB + t, 0, 0), memory_space=pltpu.SMEM),
            pl.BlockSpec(memory_space=pl.ANY),
            pl.BlockSpec((ROW_TILE, 2), lambda b, t: (b * TPB + t, 0)),
            row, _mod_spec(depth), _mod_spec(next_depth),
            g_next[1],
        ],
        out_specs=out_specs,
        out_shape=out_shape,
        scratch_shapes=[pltpu.VMEM((2, ROW_TILE, D), F32), pltpu.SemaphoreType.DMA(())],
        compiler_params=_cp(2),
        name="moe_combine",
    )(dest, yb, wts, xs, mods, mods, g_next[0])


def _moe_plan(experts, rank, counts):
    padded = (counts + MOE_BM - 1) // MOE_BM * MOE_BM
    pend = jnp.cumsum(padded)
    pstart = pend - padded
    ids = jnp.arange(N_EXPERTS, dtype=jnp.int32)[:, None, None]
    dest = rank + jnp.sum(jnp.where(experts[None] == ids, pstart[:, None, None], 0), axis=0)
    dest = jnp.transpose(dest.astype(jnp.int32).reshape(2, NTOK // ROW_TILE, ROW_TILE), (1, 0, 2))
    block_e = jnp.minimum(
        jnp.searchsorted(pend, jnp.arange(MOE_NB, dtype=jnp.int32) * MOE_BM, side="right"), N_EXPERTS - 1
    ).astype(jnp.int32)
    n_used = (pend[-1:] // MOE_BM).astype(jnp.int32)
    tail = jnp.arange(MOE_NB - MOE_TAIL, MOE_NB, dtype=jnp.int32) * MOE_BM
    zrow = jnp.concatenate([jnp.where(padded > 0, pend - MOE_BM, -1), jnp.where(tail >= pend[-1], tail, -1)])
    return dest, block_e, n_used, zrow.astype(jnp.int32)


def _moe(h, xs, mods, depth, g_next, rw_t, rb, w_gate, w_up, w_down, final):
    h2 = h.reshape(NTOK, D)
    e, wts, rank, cnt = _router(h2, rw_t, rb)
    dest, block_e, n_used, zrow = _moe_plan(e, rank, cnt[:, 0].astype(jnp.int32))
    xsorted = _dispatch(h2, dest, zrow)
    yb = _experts(xsorted, block_e, n_used, w_gate, w_up, w_down, depth)
    return _combine(yb, dest, wts.T, xs, mods, depth, g_next, final)


def _rope_tables():
    nf = RET_DK // 4
    t = np.arange(SEQ)
    inv = ROPE_BASE ** (-np.arange(nf, dtype=np.float32) / nf)
    cos = np.ones((TT, 256), np.float32)
    sin = np.zeros((TT, 256), np.float32)
    for seg, pos in enumerate(((t // GRID_W).astype(np.float32), (t % GRID_W).astype(np.float32))):
        ang = (pos[:, None] * inv[None, :]).astype(np.float32)
        c, s = np.cos(ang), np.sin(ang)
        cos[:SEQ, seg * 128:(seg + 1) * 128] = np.concatenate([c, c], axis=1)
        sin[:SEQ, seg * 128:(seg + 1) * 128] = np.concatenate([-s, s], axis=1)
    ks = np.float32(RET_DK ** -0.5)
    return jnp.asarray(np.stack([cos, cos * ks])), jnp.asarray(np.stack([sin, sin * ks]))


def _decay_tables(decay_logit):
    C = RET_CHUNK
    log_g = jax.nn.log_sigmoid(decay_logit.astype(F32))
    pos = jnp.arange(C, dtype=F32)
    diff = pos[:, None] - pos[None, :]
    lg = log_g[:, :, None, None]
    dm_f = jnp.where(diff >= 0, jnp.exp(jnp.maximum(diff, 0.0)[None, None] * lg), 0.0)
    dm_b = jnp.where(diff <= 0, jnp.exp(jnp.maximum(-diff, 0.0)[None, None] * lg), 0.0)
    dmat = jnp.stack([dm_f[0], dm_b[1]])
    lgc = log_g[:, :, None]
    qdec = jnp.stack([jnp.exp((pos + 1.0)[None, :] * lgc[0]), jnp.exp((C - pos)[None, :] * lgc[1])])
    kdec = jnp.stack([jnp.exp((C - 1.0 - pos)[None, :] * lgc[0]), jnp.exp(pos[None, :] * lgc[1])])
    cdec = jnp.exp(C * log_g).reshape(-1)
    return dmat, qdec[..., None], kdec[..., None], cdec


def kernel(x, c, ctx, c_ctx, w_mod, b_mod, norm_mix, norm_ffn, norm_final, ret_w_in, ret_decay, ret_gn, ret_w_out,
           fn_w_out, na_w_qkv, na_rpb, na_w_out, router_w, router_bias, moe_w_gate, moe_w_up, moe_w_down):
    cvec = jnp.concatenate([c_ctx[None, :], c, jnp.zeros((8 - 1 - BATCH, D), F32)], axis=0)
    mod_all = _modulation(cvec, w_mod, b_mod).reshape(DEPTH, 8, N_MOD, D)
    pad = jnp.zeros((DEPTH, BATCH, 2, 8 - N_MOD, D), F32)
    mods = jnp.stack([jnp.broadcast_to(mod_all[:, 0:1], (DEPTH, BATCH, N_MOD, D)), mod_all[:, 1:1 + BATCH]], axis=2)
    mods = jnp.concatenate([mods, pad], axis=3)

    rope = _rope_tables()
    cs, dft_m = _dft_tables()
    rw_t = router_w.T
    rb = router_bias.reshape(N_EXPERTS, 1).astype(F32)

    norm_mix3 = norm_mix.reshape(DEPTH, 1, D)
    norm_ffn3 = norm_ffn.reshape(DEPTH, 1, D)
    ret_gn3 = ret_gn.reshape(-1, 1, 2 * D)

    xs, h = _embed(x, ctx, mods, norm_mix3)
    out = None
    for i in range(DEPTH):
        kind, j = i % 3, i // 3
        if kind == 0:
            qkvg = _project(h.reshape(NTOK, D), ret_w_in, j, rope).reshape(BATCH, TT, 6 * D)
            dmat, qdec, kdec, cdec = _decay_tables(ret_decay[j])
            o_f, o_b = _retention(qkvg, dmat, qdec, kdec, cdec)
            xs, hf = _mixer_out("ret", (o_f, o_b, qkvg, ret_gn3), ret_w_out, j, xs, mods, norm_ffn3, i)
        elif kind == 1:
            ab = _fn_channel(h, cs).reshape(BATCH, 2 * TT, D)
            f = _fn_token(dft_m, ab)
            xs, hf = _mixer_out("plain", (f,), fn_w_out, j, xs, mods, norm_ffn3, i)
        else:
            qkv = _project(h.reshape(NTOK, D), na_w_qkv, j).reshape(BATCH, TT, 3 * D)
            o = _na_attention(qkv, _na_bias_table(na_rpb[j]))
            xs, hf = _mixer_out("plain", (o,), na_w_out, j, xs, mods, norm_ffn3, i)
        final = i == DEPTH - 1
        if final:
            g_next = (norm_final[None, :], pl.BlockSpec((1, D), lambda b, t: (0, 0)))
        else:
            g_next = (norm_mix3, _layer_spec((1, D), i + 1))
        res = _moe(hf, xs, mods, i, g_next, rw_t, rb, moe_w_gate, moe_w_up, moe_w_down, final)
        if final:
            out = res
        else:
            xs, h = res
    return out
```

```python
import functools

import numpy as np
import jax
import jax.numpy as jnp
from jax import lax
from jax.experimental import pallas as pl
from jax.experimental.pallas import tpu as pltpu

F32 = jnp.float32
BF16 = jnp.bfloat16
HIGHEST = lax.Precision.HIGHEST

D = 1024
BATCH = 4
SEQ = 4096
CTX = 256
TT = SEQ + CTX
NTOK = BATCH * TT
DEPTH = 4
GRID_W = 64
EPS = 1e-6
NEG_INF = -1e30
N_MOD = 6

RET_HEADS = 4
RET_DK = D // RET_HEADS
RET_DV = 2 * RET_DK
RET_CHUNK = 128
ROPE_BASE = 10000.0

FN_GROUPS = 4
FN_GC = D // FN_GROUPS

NA_HEADS = 16
NA_HD = D // NA_HEADS
NA_KH = 8
NA_KW = 16

N_EXPERTS = 16
N_GROUPS = 4
EPG = N_EXPERTS // N_GROUPS

ROW_TILE = 256
TPB = TT // ROW_TILE
LAT_TILES = SEQ // ROW_TILE
MM_TM = TT // 4
MOE_BM = 256
MOE_S = NTOK * 2
MOE_NB = (MOE_S + N_EXPERTS * (MOE_BM - 1) + MOE_BM - 1) // MOE_BM
MOE_P = MOE_NB * MOE_BM
MOE_TAIL = MOE_NB - MOE_S // MOE_BM
MOE_NZ = N_EXPERTS + MOE_TAIL
NA_QT = GRID_W
VMEM_LIMIT = 56 * 1024 * 1024


def _cp(n_axes, vmem=VMEM_LIMIT):
    return pltpu.CompilerParams(dimension_semantics=("arbitrary",) * n_axes, vmem_limit_bytes=vmem)


def _silu(x):
    return x * jax.nn.sigmoid(x)


def _normmod(x, g, shift, scale):
    y = x * lax.rsqrt(jnp.mean(x * x, axis=-1, keepdims=True) + EPS)
    return (y * g) * (1.0 + scale) + shift


def _mod_idx(t):
    return jnp.where(t < LAT_TILES, 1, 0)


def _layer_spec(shape, layer):
    return pl.BlockSpec((None,) + shape, lambda *_: (layer,) + (0,) * len(shape))


def _mod_spec(depth):
    return pl.BlockSpec((None, None, None, 8, D), lambda b, t: (depth, b, _mod_idx(t), 0, 0))


def _mod_kernel(c_ref, w_ref, b_ref, o_ref):
    a = _silu(c_ref[...])
    o_ref[...] = jnp.dot(a, w_ref[...], precision=HIGHEST, preferred_element_type=F32) + b_ref[...]


def _modulation(cvec, w_mod, b_mod):
    return pl.pallas_call(
        _mod_kernel,
        grid=(DEPTH, N_MOD),
        in_specs=[
            pl.BlockSpec((8, D), lambda i, j: (0, 0)),
            pl.BlockSpec((None, D, D), lambda i, j: (i, 0, j)),
            pl.BlockSpec((None, 1, D), lambda i, j: (i, 0, j)),
        ],
        out_specs=pl.BlockSpec((None, 8, D), lambda i, j: (i, 0, j)),
        out_shape=jax.ShapeDtypeStruct((DEPTH, 8, N_MOD * D), F32),
        compiler_params=_cp(2),
        name="modulation",
    )(cvec, w_mod, b_mod.reshape(DEPTH, 1, N_MOD * D))


def _embed_kernel(x_ref, ctx_ref, mod_ref, g_ref, xs_ref, h_ref):
    t = pl.program_id(1)

    def emit(src):
        xs_ref[...] = src
        h_ref[...] = _normmod(src, g_ref[...], mod_ref[0:1, :], mod_ref[1:2, :]).astype(BF16)

    @pl.when(t < LAT_TILES)
    def _():
        emit(x_ref[...])

    @pl.when(t >= LAT_TILES)
    def _():
        emit(ctx_ref[...])


def _embed(x, ctx, mods, g):
    return pl.pallas_call(
        _embed_kernel,
        grid=(BATCH, TPB),
        in_specs=[
            pl.BlockSpec((None, ROW_TILE, D), lambda b, t: (b, jnp.minimum(t, LAT_TILES - 1), 0)),
            pl.BlockSpec((None, CTX, D), lambda b, t: (b, 0, 0)),
            _mod_spec(0),
            _layer_spec((1, D), 0),
        ],
        out_specs=[
            pl.BlockSpec((None, ROW_TILE, D), lambda b, t: (b, t, 0)),
            pl.BlockSpec((None, ROW_TILE, D), lambda b, t: (b, t, 0)),
        ],
        out_shape=[
            jax.ShapeDtypeStruct((BATCH, TT, D), F32),
            jax.ShapeDtypeStruct((BATCH, TT, D), BF16),
        ],
        compiler_params=_cp(2),
        name="embed",
    )(x, ctx, mods, g)


def _rope_store(acc, cos_ref, sin_ref, o_ref):
    for c in range(acc.shape[1] // 128):
        xc = acc[:, c * 128:(c + 1) * 128]
        tsl = slice((c % 2) * 128, (c % 2 + 1) * 128)
        rot = xc * cos_ref[:, tsl] + pltpu.roll(xc, 64, 1) * sin_ref[:, tsl]
        o_ref[:, c * 128:(c + 1) * 128] = rot.astype(o_ref.dtype)


def _proj_kernel(a_ref, w_ref, *rest, n_rope):
    if n_rope:
        cos_ref, sin_ref, o_ref, wb_ref = rest
    else:
        o_ref, wb_ref = rest
    j = pl.program_id(0)

    @pl.when(pl.program_id(1) == 0)
    def _():
        wb_ref[...] = w_ref[...].astype(BF16)

    acc = jnp.dot(a_ref[...], wb_ref[...], preferred_element_type=F32)
    if n_rope:
        @pl.when(j < n_rope)
        def _():
            _rope_store(acc, cos_ref, sin_ref, o_ref)

        @pl.when(j >= n_rope)
        def _():
            o_ref[...] = acc.astype(o_ref.dtype)
    else:
        o_ref[...] = acc.astype(o_ref.dtype)


def _project(a, w, layer, rope=None):
    _, k, n = w.shape
    tn = D
    n_rope = 0 if rope is None else rope[0].shape[0]
    in_specs = [
        pl.BlockSpec((MM_TM, k), lambda j, i: (i, 0)),
        pl.BlockSpec((None, k, tn), lambda j, i: (layer, 0, j)),
    ]
    args = [a, w]
    if n_rope:
        tspec = pl.BlockSpec((None, MM_TM, 256), lambda j, i: (jnp.minimum(j, n_rope - 1), i % (TT // MM_TM), 0))
        in_specs += [tspec, tspec]
        args += list(rope)
    return pl.pallas_call(
        functools.partial(_proj_kernel, n_rope=n_rope),
        grid=(n // tn, NTOK // MM_TM),
        in_specs=in_specs,
        out_specs=pl.BlockSpec((MM_TM, tn), lambda j, i: (i, j)),
        out_shape=jax.ShapeDtypeStruct((NTOK, n), BF16),
        scratch_shapes=[pltpu.VMEM((k, tn), BF16)],
        compiler_params=_cp(2),
        name="project",
    )(*args)


N_CHUNKS = TT // RET_CHUNK
LAT_CHUNKS = SEQ // RET_CHUNK
CTX_CHUNKS = CTX // RET_CHUNK


def _fwd_chunk(n):
    return jnp.where(n < CTX_CHUNKS, LAT_CHUNKS + n, n - CTX_CHUNKS)


def _bwd_chunk(n):
    return N_CHUNKS - 1 - n


def _ret_kernel(cdec_ref, qf, kf, vf, qb, kb, vb, dmat_ref, qdec_ref, kdec_ref, of_ref, ob_ref, s_ref):
    @pl.when(pl.program_id(1) == 0)
    def _():
        s_ref[...] = jnp.zeros_like(s_ref)

    for d, (q_r, k_r, v_r, o_r) in enumerate(((qf, kf, vf, of_ref), (qb, kb, vb, ob_ref))):
        for h in range(RET_HEADS):
            q = q_r[:, h * RET_DK:(h + 1) * RET_DK]
            k = k_r[:, h * RET_DK:(h + 1) * RET_DK]
            v = v_r[:, h * RET_DV:(h + 1) * RET_DV]
            att = lax.dot_general(q, k, (((1,), (1,)), ((), ())), preferred_element_type=F32) * dmat_ref[d, h]
            s = s_ref[d, h]
            qd = (q.astype(F32) * qdec_ref[d, h]).astype(BF16)
            o = (jnp.dot(att.astype(BF16), v, preferred_element_type=F32)
                 + jnp.dot(qd, s.astype(BF16), preferred_element_type=F32))
            kd = (k.astype(F32) * kdec_ref[d, h]).astype(BF16)
            s_ref[d, h] = s * cdec_ref[d * RET_HEADS + h] + lax.dot_general(
                kd, v, (((0,), (0,)), ((), ())), preferred_element_type=F32)
            o_r[:, h * RET_DV:(h + 1) * RET_DV] = o.astype(o_r.dtype)


def _retention(qkvg, dmat, qdec, kdec, cdec):
    C = RET_CHUNK

    def spec(width, col, chunk_fn):
        return pl.BlockSpec((None, C, width), lambda b, n, cd: (b, chunk_fn(n), col))

    ospec_f = pl.BlockSpec((None, C, 2 * D), lambda b, n, cd: (b, _fwd_chunk(n), 0))
    ospec_b = pl.BlockSpec((None, C, 2 * D), lambda b, n, cd: (b, _bwd_chunk(n), 0))
    full = lambda shape: pl.BlockSpec(shape, lambda b, n, cd: (0,) * len(shape))
    grid_spec = pltpu.PrefetchScalarGridSpec(
        num_scalar_prefetch=1,
        grid=(BATCH, N_CHUNKS),
        in_specs=[
            spec(D, 0, _fwd_chunk), spec(D, 1, _fwd_chunk), spec(2 * D, 1, _fwd_chunk),
            spec(D, 0, _bwd_chunk), spec(D, 1, _bwd_chunk), spec(2 * D, 1, _bwd_chunk),
            full((2, RET_HEADS, C, C)), full((2, RET_HEADS, C, 1)), full((2, RET_HEADS, C, 1)),
        ],
        out_specs=[ospec_f, ospec_b],
        scratch_shapes=[pltpu.VMEM((2, RET_HEADS, RET_DK, RET_DV), F32)],
    )
    return pl.pallas_call(
        _ret_kernel,
        grid_spec=grid_spec,
        out_shape=[jax.ShapeDtypeStruct((BATCH, TT, 2 * D), BF16)] * 2,
        compiler_params=_cp(2),
        name="retention",
    )(cdec, qkvg, qkvg, qkvg, qkvg, qkvg, qkvg, dmat, qdec, kdec)


def _mixer_out_kernel(*refs, kind):
    if kind == "ret":
        of_ref, ob_ref, g_ref, gn_ref, w_ref, x_ref, mod_ref, nf_ref, xo_ref, h_ref, wb_ref = refs
    elif kind == "split":
        al_ref, ac_ref, w_ref, x_ref, mod_ref, nf_ref, xo_ref, h_ref, wb_ref = refs
    else:
        a_ref, w_ref, x_ref, mod_ref, nf_ref, xo_ref, h_ref, wb_ref = refs

    @pl.when((pl.program_id(0) == 0) & (pl.program_id(1) == 0))
    def _():
        wb_ref[...] = w_ref[...].astype(BF16)

    if kind == "ret":
        o = of_ref[...].astype(F32) + ob_ref[...].astype(F32)
        parts = []
        for h in range(RET_HEADS):
            oh = o[:, h * RET_DV:(h + 1) * RET_DV]
            mu = jnp.mean(oh, axis=-1, keepdims=True)
            ctr = oh - mu
            var = jnp.mean(ctr * ctr, axis=-1, keepdims=True)
            parts.append(ctr * lax.rsqrt(var + EPS))
        on = jnp.concatenate(parts, axis=-1) * gn_ref[...]
        a = (on * _silu(g_ref[...].astype(F32))).astype(BF16)
    elif kind == "split":
        a = jnp.where(pl.program_id(1) < LAT_TILES, al_ref[...], ac_ref[...])
    else:
        a = a_ref[...]
    y = jnp.dot(a, wb_ref[...], preferred_element_type=F32)
    xn = x_ref[...] + mod_ref[2:3, :] * y
    xo_ref[...] = xn
    h_ref[...] = _normmod(xn, nf_ref[...], mod_ref[3:4, :], mod_ref[4:5, :])


def _mixer_out(kind, srcs, w, layer, xs, mods, norm_ffn, depth):
    k = w.shape[1]
    row = lambda width, col=0: pl.BlockSpec((None, ROW_TILE, width), lambda b, t: (b, t, col))
    if kind == "ret":
        src_specs = [row(2 * D), row(2 * D), row(2 * D, 2), _layer_spec((1, 2 * D), layer)]
    elif kind == "split":
        src_specs = [pl.BlockSpec((None, ROW_TILE, k), lambda b, t: (b, jnp.minimum(t, LAT_TILES - 1), 0)),
                     pl.BlockSpec((None, CTX, k), lambda b, t: (b, 0, 0))]
    else:
        src_specs = [row(k)]
    return pl.pallas_call(
        functools.partial(_mixer_out_kernel, kind=kind),
        grid=(BATCH, TPB),
        in_specs=src_specs + [_layer_spec((k, D), layer), row(D), _mod_spec(depth), _layer_spec((1, D), depth)],
        out_specs=[row(D), row(D)],
        out_shape=[jax.ShapeDtypeStruct((BATCH, TT, D), F32)] * 2,
        scratch_shapes=[pltpu.VMEM((k, D), BF16)],
        compiler_params=_cp(2),
        name="mixer_out_" + kind,
    )(*srcs, w, xs, mods, norm_ffn)


def _fn_channel_kernel(h_ref, cs_ref, o_ref):
    for g in range(FN_GROUPS):
        r = jnp.dot(h_ref[:, g * FN_GC:(g + 1) * FN_GC], cs_ref[...], preferred_element_type=F32)
        o_ref[0, :, g * FN_GC:(g + 1) * FN_GC] = r[:, :FN_GC].astype(BF16)
        o_ref[1, :, g * FN_GC:(g + 1) * FN_GC] = r[:, FN_GC:].astype(BF16)


def _fn_channel(h, cs):
    return pl.pallas_call(
        _fn_channel_kernel,
        grid=(BATCH, TPB),
        in_specs=[
            pl.BlockSpec((None, ROW_TILE, D), lambda b, t: (b, t, 0)),
            pl.BlockSpec((FN_GC, 2 * FN_GC), lambda b, t: (0, 0)),
        ],
        out_specs=pl.BlockSpec((None, 2, ROW_TILE, D), lambda b, t: (b, 0, t, 0)),
        out_shape=jax.ShapeDtypeStruct((BATCH, 2, TT, D), BF16),
        compiler_params=_cp(2),
        name="fourier_channels",
    )(h, cs)


FN_R = 64
FN_CB = 16


def _fn_rows_kernel(ab_ref, f1_ref, twc_ref, tws_ref, y_ref):
    for j in range(FN_CB):
        st = jnp.concatenate([ab_ref[0, :, j, :], ab_ref[1, :, j, :]], axis=0)
        y = jnp.dot(f1_ref[...], st, preferred_element_type=F32)
        yr, yi = y[:FN_R], y[FN_R:]
        cw, sw = twc_ref[j], tws_ref[j]
        y_ref[0, j] = (yr * cw - yi * sw).astype(BF16)
        y_ref[1, j] = (yi * cw + yr * sw).astype(BF16)


def _fn_rows(ab, f1, twc, tws):
    ab5 = ab.reshape(BATCH, 2, TT // FN_R, FN_R, D)
    return pl.pallas_call(
        _fn_rows_kernel,
        grid=(BATCH, FN_R // FN_CB),
        in_specs=[
            pl.BlockSpec((None, 2, FN_R, FN_CB, D), lambda b, cb: (b, 0, 0, cb, 0)),
            pl.BlockSpec((2 * FN_R, 2 * FN_R), lambda b, cb: (0, 0)),
            pl.BlockSpec((FN_CB, FN_R, 1), lambda b, cb: (cb, 0, 0)),
            pl.BlockSpec((FN_CB, FN_R, 1), lambda b, cb: (cb, 0, 0)),
        ],
        out_specs=pl.BlockSpec((None, 2, FN_CB, FN_R, D), lambda b, cb: (b, 0, cb, 0, 0)),
        out_shape=jax.ShapeDtypeStruct((BATCH, 2, FN_R, FN_R, D), BF16),
        compiler_params=_cp(2),
        name="fourier_rows",
    )(ab5, f1, twc, tws)


def _fn_cols_kernel(y_ref, f2_ref, o_ref):
    for j in range(FN_CB):
        st = jnp.concatenate([y_ref[0, :, j, :], y_ref[1, :, j, :]], axis=0)
        o_ref[:, j, :] = jnp.dot(f2_ref[...], st, preferred_element_type=F32).astype(BF16)


def _fn_cols(y, f2):
    out = pl.pallas_call(
        _fn_cols_kernel,
        grid=(BATCH, FN_R // FN_CB),
        in_specs=[
            pl.BlockSpec((None, 2, FN_R, FN_CB, D), lambda b, kb: (b, 0, 0, kb, 0)),
            pl.BlockSpec((FN_R, 2 * FN_R), lambda b, kb: (0, 0)),
        ],
        out_specs=pl.BlockSpec((None, FN_R, FN_CB, D), lambda b, kb: (b, 0, kb, 0)),
        out_shape=jax.ShapeDtypeStruct((BATCH, FN_R, FN_R, D), BF16),
        compiler_params=_cp(2),
        name="fourier_cols",
    )(y, f2)
    return out.reshape(BATCH, SEQ, D)


def _fn_ctx_kernel(ab_ref, m_ref, o_ref):
    st = jnp.concatenate([ab_ref[0], ab_ref[1]], axis=0)
    o_ref[...] = jnp.dot(m_ref[...], st, preferred_element_type=F32).astype(BF16)


def _fn_ctx(ab, m_ctx):
    return pl.pallas_call(
        _fn_ctx_kernel,
        grid=(BATCH,),
        in_specs=[
            pl.BlockSpec((None, 2, CTX, D), lambda b: (b, 0, SEQ // CTX, 0)),
            pl.BlockSpec((CTX, 2 * CTX), lambda b: (0, 0)),
        ],
        out_specs=pl.BlockSpec((None, CTX, D), lambda b: (b, 0, 0)),
        out_shape=jax.ShapeDtypeStruct((BATCH, CTX, D), BF16),
        compiler_params=_cp(1),
        name="fourier_ctx",
    )(ab, m_ctx)


def _dft_tables():
    def cs_pair(n, scale):
        i = np.arange(n)
        ang = 2.0 * np.pi * ((i[:, None] * i[None, :]) % n) / n
        return np.cos(ang) * scale, np.sin(ang) * scale

    assert FN_GC == CTX
    c, s = cs_pair(FN_GC, FN_GC ** -0.5)
    cs = np.concatenate([c, s], axis=1)
    m_ctx = np.concatenate([c, -s], axis=1)
    c, s = cs_pair(FN_R, FN_R ** -0.5)
    f1 = np.block([[c, -s], [s, c]])
    f2 = np.concatenate([c, -s], axis=1)
    i = np.arange(FN_R)
    phi = 2.0 * np.pi * (i[:, None] * i[None, :]) / SEQ
    tw = [jnp.asarray(f(phi)[..., None], F32) for f in (np.cos, np.sin)]
    return jnp.asarray(cs, BF16), jnp.asarray(m_ctx, BF16), jnp.asarray(f1, BF16), jnp.asarray(f2, BF16), tw


NA_ROWS = SEQ // GRID_W
NA_STEPS = TT // NA_QT
NA_NK = NA_KH * GRID_W
NA_DR = 2 * NA_KH - 1


def _na_row_start(step):
    r = jnp.minimum(step, NA_ROWS - 1)
    return r, jnp.clip(r - NA_KH // 2, 0, NA_ROWS - NA_KH)


def _na_bias_idx(step):
    r, rs = _na_row_start(step)
    return rs - r + (NA_KH - 1)


def _na_kernel(q_ref, k_ref, v_ref, bias_ref, o_ref):
    step = pl.program_id(1)
    lane = lax.broadcasted_iota(jnp.int32, (NA_QT, 128), 1)
    nt = (((1,), (1,)), ((), ()))
    scale = NA_HD ** -0.5

    def attend(start, d0=None):
        for p in range(NA_HEADS // 2):
            sl = slice(p * 128, (p + 1) * 128)
            qp = q_ref[:, sl] * scale
            zero = jnp.zeros_like(qp)
            q2 = jnp.concatenate([jnp.where(lane < NA_HD, qp, zero), jnp.where(lane >= NA_HD, qp, zero)], axis=0)
            s_c = lax.dot_general(q2, k_ref[SEQ:TT, sl], nt, preferred_element_type=F32)
            m = jnp.max(s_c, axis=-1, keepdims=True)
            if start is not None:
                bias = jnp.concatenate(
                    [jnp.concatenate([bias_ref[2 * p + hh, d0 + 2 * j2] for j2 in range(NA_KH // 2)], axis=1)
                     for hh in range(2)], axis=0)
                s_w = lax.dot_general(q2, k_ref[pl.ds(start, NA_NK), sl], nt, preferred_element_type=F32) + bias
                m = jnp.maximum(m, jnp.max(s_w, axis=-1, keepdims=True))
            p_c = jnp.exp(s_c - m)
            l = jnp.sum(p_c, axis=-1, keepdims=True)
            acc = jnp.dot(p_c.astype(BF16), v_ref[SEQ:TT, sl], preferred_element_type=F32)
            if start is not None:
                p_w = jnp.exp(s_w - m)
                l = l + jnp.sum(p_w, axis=-1, keepdims=True)
                acc = acc + jnp.dot(p_w.astype(BF16), v_ref[pl.ds(start, NA_NK), sl], preferred_element_type=F32)
            o2 = acc / l
            o_ref[:, sl] = jnp.where(lane < NA_HD, o2[:NA_QT], o2[NA_QT:]).astype(BF16)

    @pl.when(step < NA_ROWS)
    def _():
        _, rs = _na_row_start(step)
        attend(pl.multiple_of(rs * GRID_W, GRID_W), _na_bias_idx(step))

    @pl.when(step >= NA_ROWS)
    def _():
        attend(None)


def _na_attention(qkv, bias):
    return pl.pallas_call(
        _na_kernel,
        grid=(BATCH, NA_STEPS),
        in_specs=[
            pl.BlockSpec((None, NA_QT, D), lambda b, s: (b, s, 0)),
            pl.BlockSpec((None, TT, D), lambda b, s: (b, 0, 1), pipeline_mode=pl.Buffered(1)),
            pl.BlockSpec((None, TT, D), lambda b, s: (b, 0, 2), pipeline_mode=pl.Buffered(1)),
            pl.BlockSpec((NA_HEADS, NA_DR - 1, GRID_W, 2 * GRID_W), lambda b, s: (0, 0, 0, 0),
                         pipeline_mode=pl.Buffered(1)),
        ],
        out_specs=pl.BlockSpec((None, NA_QT, D), lambda b, s: (b, s, 0)),
        out_shape=jax.ShapeDtypeStruct((BATCH, TT, D), BF16),
        compiler_params=_cp(2),
        name="na_attention",
    )(qkv, qkv, qkv, bias)


def _na_bias_table(rpb):
    qc = np.arange(GRID_W)
    kcol = np.arange(GRID_W)
    cstart = np.clip(qc - NA_KW // 2, 0, GRID_W - NA_KW)
    valid = (kcol[None, :] >= cstart[:, None]) & (kcol[None, :] < cstart[:, None] + NA_KW)
    dc = kcol[None, :] - qc[:, None] + (NA_KW - 1)
    pick = ((dc[None] == np.arange(2 * NA_KW - 1)[:, None, None]) & valid[None]).astype(np.float32)
    tab = jnp.einsum("hdj,jqk->hdqk", rpb.astype(F32), jnp.asarray(pick), precision=HIGHEST)
    tab = tab + jnp.asarray(np.where(valid, 0.0, NEG_INF).astype(np.float32))
    return jnp.concatenate([tab[:, :NA_DR - 1], tab[:, 1:]], axis=-1)


def _router_kernel(h_ref, rw_ref, rb_ref, tri_ref, e_ref, w_ref, rank_ref, cnt_ref):
    logits = lax.dot_general(rw_ref[...], h_ref[...], (((1,), (1,)), ((), ())),
                             precision=HIGHEST, preferred_element_type=F32)
    sc = jax.nn.sigmoid(logits)
    sel = sc + rb_ref[...]
    sc_r = [sc[e:e + 1, :] for e in range(N_EXPERTS)]
    sel_r = [sel[e:e + 1, :] for e in range(N_EXPERTS)]
    zero_i = jnp.zeros_like(sel_r[0]).astype(jnp.int32)

    best = gi = None
    picks = []
    for g in range(N_GROUPS):
        a = sel_r[g * EPG:(g + 1) * EPG]
        u = sc_r[g * EPG:(g + 1) * EPG]
        gs = None
        for i in range(EPG):
            for j in range(i + 1, EPG):
                pair = a[i] + a[j]
                gs = pair if gs is None else jnp.maximum(gs, pair)
        m1, i1, s1 = a[0], zero_i, u[0]
        for j in range(1, EPG):
            upd = a[j] > m1
            m1 = jnp.where(upd, a[j], m1)
            i1 = jnp.where(upd, j, i1)
            s1 = jnp.where(upd, u[j], s1)
        m2 = i2 = s2 = None
        for j in range(EPG):
            cand = jnp.where(i1 == j, -jnp.inf, a[j])
            if m2 is None:
                m2, i2, s2 = cand, zero_i, u[0]
            else:
                upd = cand > m2
                m2 = jnp.where(upd, cand, m2)
                i2 = jnp.where(upd, j, i2)
                s2 = jnp.where(upd, u[j], s2)
        picks.append((i1 + g * EPG, i2 + g * EPG, s1, s2))
        if best is None:
            best, gi = gs, zero_i
        else:
            upd = gs > best
            best = jnp.where(upd, gs, best)
            gi = jnp.where(upd, g, gi)

    e1, e2, s1, s2 = picks[0]
    for g in range(1, N_GROUPS):
        on = gi == g
        e1 = jnp.where(on, picks[g][0], e1)
        e2 = jnp.where(on, picks[g][1], e2)
        s1 = jnp.where(on, picks[g][2], s1)
        s2 = jnp.where(on, picks[g][3], s2)
    tot = s1 + s2
    e_ref[0:1, :] = e1
    e_ref[1:2, :] = e2
    w_ref[0:1, :] = s1 / tot
    w_ref[1:2, :] = s2 / tot

    @pl.when(pl.program_id(0) == 0)
    def _():
        cnt_ref[...] = jnp.zeros_like(cnt_ref)

    eid = lax.broadcasted_iota(jnp.int32, sc.shape, 0)
    oh1 = (eid == e1).astype(F32)
    oh2 = (eid == e2).astype(F32)
    before1 = jnp.dot(oh1.astype(BF16), tri_ref[...], preferred_element_type=F32)
    before2 = jnp.dot(oh2.astype(BF16), tri_ref[...], preferred_element_type=F32)
    tot1 = jnp.sum(oh1, axis=1, keepdims=True)
    tot2 = jnp.sum(oh2, axis=1, keepdims=True)
    base = cnt_ref[:, 0:1]
    rank_ref[0:1, :] = jnp.sum(oh1 * (base + before1), axis=0, keepdims=True).astype(jnp.int32)
    rank_ref[1:2, :] = jnp.sum(oh2 * (base + tot1 + before2), axis=0, keepdims=True).astype(jnp.int32)
    cnt_ref[...] = cnt_ref[...] + (tot1 + tot2)


def _router(h, rw_t, rb):
    tri = jnp.asarray(np.triu(np.ones((ROW_TILE, ROW_TILE), np.float32), 1), BF16)
    return pl.pallas_call(
        _router_kernel,
        grid=(NTOK // ROW_TILE,),
        in_specs=[
            pl.BlockSpec((ROW_TILE, D), lambda i: (i, 0)),
            pl.BlockSpec((N_EXPERTS, D), lambda i: (0, 0)),
            pl.BlockSpec((N_EXPERTS, 1), lambda i: (0, 0)),
            pl.BlockSpec((ROW_TILE, ROW_TILE), lambda i: (0, 0)),
        ],
        out_specs=[pl.BlockSpec((2, ROW_TILE), lambda i: (0, i))] * 3
        + [pl.BlockSpec((N_EXPERTS, 128), lambda i: (0, 0))],
        out_shape=[jax.ShapeDtypeStruct((2, NTOK), jnp.int32), jax.ShapeDtypeStruct((2, NTOK), F32),
                   jax.ShapeDtypeStruct((2, NTOK), jnp.int32), jax.ShapeDtypeStruct((N_EXPERTS, 128), F32)],
        compiler_params=_cp(1),
        name="router",
    )(h, rw_t, rb, tri)


DMA_UNROLL = 8


def _row_copy(src_ref, src_row, dst_ref, dst_row, sem):
    return pltpu.make_async_copy(src_ref.at[pl.ds(src_row, 1)], dst_ref.at[pl.ds(dst_row, 1)], sem)


def _for_each_row_pair(fn):
    def group(gidx, carry):
        for u in range(DMA_UNROLL):
            for kk in range(2):
                fn(gidx * DMA_UNROLL + u, kk)
        return carry

    lax.fori_loop(0, ROW_TILE // DMA_UNROLL, group, 0)


def _dispatch_kernel(zrow_ref, dest_ref, h_ref, xs_ref, zbuf, sem, zsem):
    @pl.when(pl.program_id(0) == 0)
    def _():
        zbuf[...] = jnp.zeros_like(zbuf)

        def zero_copy(e):
            start = pl.multiple_of(jnp.maximum(zrow_ref[e], 0), MOE_BM)
            return pltpu.make_async_copy(zbuf, xs_ref.at[pl.ds(start, MOE_BM)], zsem)

        for e in range(MOE_NZ):
            pl.when(zrow_ref[e] >= 0)(lambda e=e: zero_copy(e).start())
        for e in range(MOE_NZ):
            pl.when(zrow_ref[e] >= 0)(lambda e=e: zero_copy(e).wait())

    _for_each_row_pair(lambda r, kk: _row_copy(h_ref, r, xs_ref, dest_ref[kk, r], sem).start())
    _for_each_row_pair(lambda r, kk: _row_copy(h_ref, r, xs_ref, dest_ref[kk, r], sem).wait())


def _dispatch(h, dest, zrow):
    grid_spec = pltpu.PrefetchScalarGridSpec(
        num_scalar_prefetch=1,
        grid=(NTOK // ROW_TILE,),
        in_specs=[
            pl.BlockSpec((None, 2, ROW_TILE), lambda i, z: (i, 0, 0), memory_space=pltpu.SMEM),
            pl.BlockSpec((ROW_TILE, D), lambda i, z: (i, 0)),
        ],
        out_specs=pl.BlockSpec(memory_space=pl.ANY),
        scratch_shapes=[pltpu.VMEM((MOE_BM, D), F32), pltpu.SemaphoreType.DMA(()), pltpu.SemaphoreType.DMA(())],
    )
    return pl.pallas_call(
        _dispatch_kernel,
        grid_spec=grid_spec,
        out_shape=jax.ShapeDtypeStruct((MOE_P, D), F32),
        compiler_params=pltpu.CompilerParams(dimension_semantics=("arbitrary",), vmem_limit_bytes=VMEM_LIMIT,
                                             has_side_effects=True),
        name="moe_dispatch",
    )(zrow, dest, h)


def _experts_kernel(be_ref, nu_ref, x_ref, wg_ref, wu_ref, wd_ref, y_ref, wgb, wub, wdb):
    i = pl.program_id(0)
    active = i < nu_ref[0]
    changed = (i == 0) | (be_ref[i] != be_ref[jnp.maximum(i - 1, 0)])

    @pl.when(active & changed)
    def _():
        wgb[...] = wg_ref[...].astype(BF16)
        wub[...] = wu_ref[...].astype(BF16)
        wdb[...] = wd_ref[...].astype(BF16)

    @pl.when(active)
    def _():
        x = x_ref[...].astype(BF16)
        g = jnp.dot(x, wgb[...], preferred_element_type=F32)
        u = jnp.dot(x, wub[...], preferred_element_type=F32)
        hh = (_silu(g) * u).astype(BF16)
        y_ref[...] = jnp.dot(hh, wdb[...], preferred_element_type=F32)

    @pl.when(jnp.logical_not(active))
    def _():
        y_ref[...] = jnp.zeros_like(y_ref)


def _experts(xs, block_e, n_used, w_gate, w_up, w_down, depth):
    wspec = pl.BlockSpec((None, None, D, D), lambda i, be, nu: (depth, be[i], 0, 0))
    grid_spec = pltpu.PrefetchScalarGridSpec(
        num_scalar_prefetch=2,
        grid=(MOE_NB,),
        in_specs=[pl.BlockSpec((MOE_BM, D), lambda i, be, nu: (jnp.minimum(i, nu[0] - 1), 0)), wspec, wspec, wspec],
        out_specs=pl.BlockSpec((MOE_BM, D), lambda i, be, nu: (i, 0)),
        scratch_shapes=[pltpu.VMEM((D, D), BF16)] * 3,
    )
    return pl.pallas_call(
        _experts_kernel,
        grid_spec=grid_spec,
        out_shape=jax.ShapeDtypeStruct((MOE_P, D), F32),
        compiler_params=_cp(1),
        name="moe_experts",
    )(block_e, n_used, xs, w_gate, w_up, w_down)


def _combine_kernel(dest_ref, yb_ref, wt_ref, x_ref, mod_ref, modn_ref, gn_ref, *rest, final):
    if final:
        out_ref, ybuf, sem = rest
    else:
        xo_ref, h_ref, ybuf, sem = rest
    t = pl.program_id(1)

    def body():
        _for_each_row_pair(lambda r, kk: _row_copy(yb_ref, dest_ref[kk, r], ybuf.at[kk], r, sem).start())
        _for_each_row_pair(lambda r, kk: _row_copy(yb_ref, dest_ref[kk, r], ybuf.at[kk], r, sem).wait())
        f = wt_ref[:, 0:1] * ybuf[0] + wt_ref[:, 1:2] * ybuf[1]
        xn = x_ref[...] + mod_ref[5:6, :] * f
        if final:
            out_ref[...] = _normmod(xn, gn_ref[...], 0.0, 0.0)
        else:
            xo_ref[...] = xn
            h_ref[...] = _normmod(xn, gn_ref[...], modn_ref[0:1, :], modn_ref[1:2, :]).astype(BF16)

    if final:
        pl.when(t < LAT_TILES)(body)
    else:
        body()


def _combine(yb, dest, wts, xs, mods, depth, g_next, final):
    row = pl.BlockSpec((None, ROW_TILE, D), lambda b, t: (b, t, 0))
    next_depth = depth if final else depth + 1
    if final:
        out_specs = pl.BlockSpec((None, ROW_TILE, D), lambda b, t: (b, jnp.minimum(t, LAT_TILES - 1), 0))
        out_shape = jax.ShapeDtypeStruct((BATCH, SEQ, D), F32)
    else:
        out_specs = [row, row]
        out_shape = [jax.ShapeDtypeStruct((BATCH, TT, D), F32), jax.ShapeDtypeStruct((BATCH, TT, D), BF16)]
    return pl.pallas_call(
        functools.partial(_combine_kernel, final=final),
        grid=(BATCH, TPB),
        in_specs=[
            pl.BlockSpec((None, 2, ROW_TILE), lambda b, t: (b * TPB + t, 0, 0), memory_space=pltpu.SMEM),
            pl.BlockSpec(memory_space=pl.ANY),
            pl.BlockSpec((ROW_TILE, 2), lambda b, t: (b * TPB + t, 0)),
            row, _mod_spec(depth), _mod_spec(next_depth),
            g_next[1],
        ],
        out_specs=out_specs,
        out_shape=out_shape,
        scratch_shapes=[pltpu.VMEM((2, ROW_TILE, D), F32), pltpu.SemaphoreType.DMA(())],
        compiler_params=_cp(2),
        name="moe_combine",
    )(dest, yb, wts, xs, mods, mods, g_next[0])


def _moe_plan(experts, rank, counts):
    padded = (counts + MOE_BM - 1) // MOE_BM * MOE_BM
    pend = jnp.cumsum(padded)
    pstart = pend - padded
    ids = jnp.arange(N_EXPERTS, dtype=jnp.int32)[:, None, None]
    dest = rank + jnp.sum(jnp.where(experts[None] == ids, pstart[:, None, None], 0), axis=0)
    dest = jnp.transpose(dest.astype(jnp.int32).reshape(2, NTOK // ROW_TILE, ROW_TILE), (1, 0, 2))
    first_row = jnp.arange(MOE_NB, dtype=jnp.int32) * MOE_BM
    block_e = jnp.minimum(jnp.sum(pend[None, :] <= first_row[:, None], axis=1), N_EXPERTS - 1).astype(jnp.int32)
    n_used = (pend[-1:] // MOE_BM).astype(jnp.int32)
    tail = jnp.arange(MOE_NB - MOE_TAIL, MOE_NB, dtype=jnp.int32) * MOE_BM
    zrow = jnp.concatenate([jnp.where(padded > 0, pend - MOE_BM, -1), jnp.where(tail >= pend[-1], tail, -1)])
    return dest, block_e, n_used, zrow.astype(jnp.int32)


def _moe(h, xs, mods, depth, g_next, rw_t, rb, w_gate, w_up, w_down, final):
    h2 = h.reshape(NTOK, D)
    e, wts, rank, cnt = _router(h2, rw_t, rb)
    dest, block_e, n_used, zrow = _moe_plan(e, rank, cnt[:, 0].astype(jnp.int32))
    xsorted = _dispatch(h2, dest, zrow)
    yb = _experts(xsorted, block_e, n_used, w_gate, w_up, w_down, depth)
    return _combine(yb, dest, wts.T, xs, mods, depth, g_next, final)


def _rope_tables():
    nf = RET_DK // 4
    t = np.arange(SEQ)
    inv = ROPE_BASE ** (-np.arange(nf, dtype=np.float32) / nf)
    cos = np.ones((TT, 256), np.float32)
    sin = np.zeros((TT, 256), np.float32)
    for seg, pos in enumerate(((t // GRID_W).astype(np.float32), (t % GRID_W).astype(np.float32))):
        ang = (pos[:, None] * inv[None, :]).astype(np.float32)
        c, s = np.cos(ang), np.sin(ang)
        cos[:SEQ, seg * 128:(seg + 1) * 128] = np.concatenate([c, c], axis=1)
        sin[:SEQ, seg * 128:(seg + 1) * 128] = np.concatenate([-s, s], axis=1)
    ks = np.float32(RET_DK ** -0.5)
    return jnp.asarray(np.stack([cos, cos * ks])), jnp.asarray(np.stack([sin, sin * ks]))


def _decay_tables(decay_logit):
    C = RET_CHUNK
    log_g = jax.nn.log_sigmoid(decay_logit.astype(F32))
    pos = jnp.arange(C, dtype=F32)
    diff = pos[:, None] - pos[None, :]
    lg = log_g[:, :, None, None]
    dm_f = jnp.where(diff >= 0, jnp.exp(jnp.maximum(diff, 0.0)[None, None] * lg), 0.0)
    dm_b = jnp.where(diff <= 0, jnp.exp(jnp.maximum(-diff, 0.0)[None, None] * lg), 0.0)
    dmat = jnp.stack([dm_f[0], dm_b[1]])
    lgc = log_g[:, :, None]
    qdec = jnp.stack([jnp.exp((pos + 1.0)[None, :] * lgc[0]), jnp.exp((C - pos)[None, :] * lgc[1])])
    kdec = jnp.stack([jnp.exp((C - 1.0 - pos)[None, :] * lgc[0]), jnp.exp(pos[None, :] * lgc[1])])
    cdec = jnp.exp(C * log_g).reshape(-1)
    return dmat, qdec[..., None], kdec[..., None], cdec


def kernel(x, c, ctx, c_ctx, w_mod, b_mod, norm_mix, norm_ffn, norm_final, ret_w_in, ret_decay, ret_gn, ret_w_out,
           fn_w_out, na_w_qkv, na_rpb, na_w_out, router_w, router_bias, moe_w_gate, moe_w_up, moe_w_down):
    cvec = jnp.concatenate([c_ctx[None, :], c, jnp.zeros((8 - 1 - BATCH, D), F32)], axis=0)
    mod_all = _modulation(cvec, w_mod, b_mod).reshape(DEPTH, 8, N_MOD, D)
    pad = jnp.zeros((DEPTH, BATCH, 2, 8 - N_MOD, D), F32)
    mods = jnp.stack([jnp.broadcast_to(mod_all[:, 0:1], (DEPTH, BATCH, N_MOD, D)), mod_all[:, 1:1 + BATCH]], axis=2)
    mods = jnp.concatenate([mods, pad], axis=3)

    rope = _rope_tables()
    cs, dft_ctx, dft_f1, dft_f2, dft_tw = _dft_tables()
    rw_t = router_w.T
    rb = router_bias.reshape(N_EXPERTS, 1).astype(F32)

    norm_mix3 = norm_mix.reshape(DEPTH, 1, D)
    norm_ffn3 = norm_ffn.reshape(DEPTH, 1, D)
    ret_gn3 = ret_gn.reshape(-1, 1, 2 * D)

    xs, h = _embed(x, ctx, mods, norm_mix3)
    out = None
    for i in range(DEPTH):
        kind, j = i % 3, i // 3
        if kind == 0:
            qkvg = _project(h.reshape(NTOK, D), ret_w_in, j, rope).reshape(BATCH, TT, 6 * D)
            dmat, qdec, kdec, cdec = _decay_tables(ret_decay[j])
            o_f, o_b = _retention(qkvg, dmat, qdec, kdec, cdec)
            xs, hf = _mixer_out("ret", (o_f, o_b, qkvg, ret_gn3), ret_w_out, j, xs, mods, norm_ffn3, i)
        elif kind == 1:
            ab = _fn_channel(h, cs)
            f_lat = _fn_cols(_fn_rows(ab, dft_f1, *dft_tw), dft_f2)
            xs, hf = _mixer_out("split", (f_lat, _fn_ctx(ab, dft_ctx)), fn_w_out, j, xs, mods, norm_ffn3, i)
        else:
            qkv = _project(h.reshape(NTOK, D), na_w_qkv, j).reshape(BATCH, TT, 3 * D)
            o = _na_attention(qkv, _na_bias_table(na_rpb[j]))
            xs, hf = _mixer_out("plain", (o,), na_w_out, j, xs, mods, norm_ffn3, i)
        final = i == DEPTH - 1
        if final:
            g_next = (norm_final[None, :], pl.BlockSpec((1, D), lambda b, t: (0, 0)))
        else:
            g_next = (norm_mix3, _layer_spec((1, D), i + 1))
        res = _moe(hf, xs, mods, i, g_next, rw_t, rb, moe_w_gate, moe_w_up, moe_w_down, final)
        if final:
            out = res
        else:
            xs, h = res
    return out
```

```python
import functools

import numpy as np
import jax
import jax.numpy as jnp
from jax import lax
from jax.experimental import pallas as pl
from jax.experimental.pallas import tpu as pltpu

F32 = jnp.float32
BF16 = jnp.bfloat16
HIGHEST = lax.Precision.HIGHEST

D = 1024
BATCH = 4
SEQ = 4096
CTX = 256
TT = SEQ + CTX
NTOK = BATCH * TT
DEPTH = 4
GRID_W = 64
EPS = 1e-6
NEG_INF = -1e30
N_MOD = 6

RET_HEADS = 4
RET_DK = D // RET_HEADS
RET_DV = 2 * RET_DK
RET_CHUNK = 256
ROPE_BASE = 10000.0

FN_GROUPS = 4
FN_GC = D // FN_GROUPS

NA_HEADS = 16
NA_HD = D // NA_HEADS
NA_KH = 8
NA_KW = 16

N_EXPERTS = 16
N_GROUPS = 4
EPG = N_EXPERTS // N_GROUPS

ROW_TILE = 256
TPB = TT // ROW_TILE
LAT_TILES = SEQ // ROW_TILE
MM_TM = TT // 4
MOE_BM = 256
MOE_S = NTOK * 2
MOE_NB = (MOE_S + N_EXPERTS * (MOE_BM - 1) + MOE_BM - 1) // MOE_BM
MOE_P = MOE_NB * MOE_BM
MOE_TAIL = MOE_NB - MOE_S // MOE_BM
MOE_NZ = N_EXPERTS + MOE_TAIL
NA_QT = GRID_W
VMEM_LIMIT = 56 * 1024 * 1024


def _cp(n_axes, vmem=VMEM_LIMIT):
    return pltpu.CompilerParams(dimension_semantics=("arbitrary",) * n_axes, vmem_limit_bytes=vmem)


def _silu(x):
    return x * jax.nn.sigmoid(x)


def _normmod(x, g, shift, scale):
    y = x * lax.rsqrt(jnp.mean(x * x, axis=-1, keepdims=True) + EPS)
    return (y * g) * (1.0 + scale) + shift


def _mod_idx(t):
    return jnp.where(t < LAT_TILES, 1, 0)


def _layer_spec(shape, layer):
    return pl.BlockSpec((None,) + shape, lambda *_: (layer,) + (0,) * len(shape))


def _mod_spec(depth):
    return pl.BlockSpec((None, None, None, 8, D), lambda b, t: (depth, b, _mod_idx(t), 0, 0))


def _mod_kernel(c_ref, w_ref, b_ref, o_ref):
    a = _silu(c_ref[...])
    o_ref[...] = jnp.dot(a, w_ref[...], precision=HIGHEST, preferred_element_type=F32) + b_ref[...]


def _modulation(cvec, w_mod, b_mod):
    return pl.pallas_call(
        _mod_kernel,
        grid=(DEPTH, N_MOD),
        in_specs=[
            pl.BlockSpec((8, D), lambda i, j: (0, 0)),
            pl.BlockSpec((None, D, D), lambda i, j: (i, 0, j)),
            pl.BlockSpec((None, 1, D), lambda i, j: (i, 0, j)),
        ],
        out_specs=pl.BlockSpec((None, 8, D), lambda i, j: (i, 0, j)),
        out_shape=jax.ShapeDtypeStruct((DEPTH, 8, N_MOD * D), F32),
        compiler_params=_cp(2),
        name="modulation",
    )(cvec, w_mod, b_mod.reshape(DEPTH, 1, N_MOD * D))


def _embed_kernel(x_ref, ctx_ref, mod_ref, g_ref, xs_ref, h_ref):
    t = pl.program_id(1)

    def emit(src):
        xs_ref[...] = src
        h_ref[...] = _normmod(src, g_ref[...], mod_ref[0:1, :], mod_ref[1:2, :]).astype(BF16)

    @pl.when(t < LAT_TILES)
    def _():
        emit(x_ref[...])

    @pl.when(t >= LAT_TILES)
    def _():
        emit(ctx_ref[...])


def _embed(x, ctx, mods, g):
    return pl.pallas_call(
        _embed_kernel,
        grid=(BATCH, TPB),
        in_specs=[
            pl.BlockSpec((None, ROW_TILE, D), lambda b, t: (b, jnp.minimum(t, LAT_TILES - 1), 0)),
            pl.BlockSpec((None, CTX, D), lambda b, t: (b, 0, 0)),
            _mod_spec(0),
            _layer_spec((1, D), 0),
        ],
        out_specs=[
            pl.BlockSpec((None, ROW_TILE, D), lambda b, t: (b, t, 0)),
            pl.BlockSpec((None, ROW_TILE, D), lambda b, t: (b, t, 0)),
        ],
        out_shape=[
            jax.ShapeDtypeStruct((BATCH, TT, D), F32),
            jax.ShapeDtypeStruct((BATCH, TT, D), BF16),
        ],
        compiler_params=_cp(2),
        name="embed",
    )(x, ctx, mods, g)


def _rope_store(acc, cos_ref, sin_ref, o_ref):
    for c in range(acc.shape[1] // 128):
        xc = acc[:, c * 128:(c + 1) * 128]
        tsl = slice((c % 2) * 128, (c % 2 + 1) * 128)
        rot = xc * cos_ref[:, tsl] + pltpu.roll(xc, 64, 1) * sin_ref[:, tsl]
        o_ref[:, c * 128:(c + 1) * 128] = rot.astype(o_ref.dtype)


def _proj_kernel(a_ref, w_ref, *rest, n_rope):
    if n_rope:
        cos_ref, sin_ref, o_ref, wb_ref = rest
    else:
        o_ref, wb_ref = rest
    j = pl.program_id(0)

    @pl.when(pl.program_id(1) == 0)
    def _():
        wb_ref[...] = w_ref[...].astype(BF16)

    acc = jnp.dot(a_ref[...], wb_ref[...], preferred_element_type=F32)
    if n_rope:
        @pl.when(j < n_rope)
        def _():
            _rope_store(acc, cos_ref, sin_ref, o_ref)

        @pl.when(j >= n_rope)
        def _():
            o_ref[...] = acc.astype(o_ref.dtype)
    else:
        o_ref[...] = acc.astype(o_ref.dtype)


def _project(a, w, layer, rope=None):
    _, k, n = w.shape
    tn = D
    n_rope = 0 if rope is None else rope[0].shape[0]
    in_specs = [
        pl.BlockSpec((MM_TM, k), lambda j, i: (i, 0)),
        pl.BlockSpec((None, k, tn), lambda j, i: (layer, 0, j)),
    ]
    args = [a, w]
    if n_rope:
        tspec = pl.BlockSpec((None, MM_TM, 256), lambda j, i: (jnp.minimum(j, n_rope - 1), i % (TT // MM_TM), 0))
        in_specs += [tspec, tspec]
        args += list(rope)
    return pl.pallas_call(
        functools.partial(_proj_kernel, n_rope=n_rope),
        grid=(n // tn, NTOK // MM_TM),
        in_specs=in_specs,
        out_specs=pl.BlockSpec((MM_TM, tn), lambda j, i: (i, j)),
        out_shape=jax.ShapeDtypeStruct((NTOK, n), BF16),
        scratch_shapes=[pltpu.VMEM((k, tn), BF16)],
        compiler_params=_cp(2),
        name="project",
    )(*args)


N_CHUNKS = TT // RET_CHUNK
LAT_CHUNKS = SEQ // RET_CHUNK
CTX_CHUNKS = CTX // RET_CHUNK


def _fwd_chunk(n):
    return jnp.where(n < CTX_CHUNKS, LAT_CHUNKS + n, n - CTX_CHUNKS)


def _bwd_chunk(n):
    return N_CHUNKS - 1 - n


def _ret_kernel(cdec_ref, qf, kf, vf, qb, kb, vb, dmat_ref, qdec_ref, kdec_ref, of_ref, ob_ref, s_ref):
    @pl.when(pl.program_id(1) == 0)
    def _():
        s_ref[...] = jnp.zeros_like(s_ref)

    for d, (q_r, k_r, v_r, o_r) in enumerate(((qf, kf, vf, of_ref), (qb, kb, vb, ob_ref))):
        for h in range(RET_HEADS):
            q = q_r[:, h * RET_DK:(h + 1) * RET_DK]
            k = k_r[:, h * RET_DK:(h + 1) * RET_DK]
            v = v_r[:, h * RET_DV:(h + 1) * RET_DV]
            att = lax.dot_general(q, k, (((1,), (1,)), ((), ())), preferred_element_type=F32) * dmat_ref[d, h]
            s = s_ref[d, h]
            qd = (q.astype(F32) * qdec_ref[d, h]).astype(BF16)
            o = (jnp.dot(att.astype(BF16), v, preferred_element_type=F32)
                 + jnp.dot(qd, s.astype(BF16), preferred_element_type=F32))
            kd = (k.astype(F32) * kdec_ref[d, h]).astype(BF16)
            s_ref[d, h] = s * cdec_ref[d * RET_HEADS + h] + lax.dot_general(
                kd, v, (((0,), (0,)), ((), ())), preferred_element_type=F32)
            o_r[:, h * RET_DV:(h + 1) * RET_DV] = o.astype(o_r.dtype)


def _retention(qkvg, dmat, qdec, kdec, cdec):
    C = RET_CHUNK

    def spec(width, col, chunk_fn):
        return pl.BlockSpec((None, C, width), lambda b, n, cd: (b, chunk_fn(n), col))

    ospec_f = pl.BlockSpec((None, C, 2 * D), lambda b, n, cd: (b, _fwd_chunk(n), 0))
    ospec_b = pl.BlockSpec((None, C, 2 * D), lambda b, n, cd: (b, _bwd_chunk(n), 0))
    full = lambda shape: pl.BlockSpec(shape, lambda b, n, cd: (0,) * len(shape))
    grid_spec = pltpu.PrefetchScalarGridSpec(
        num_scalar_prefetch=1,
        grid=(BATCH, N_CHUNKS),
        in_specs=[
            spec(D, 0, _fwd_chunk), spec(D, 1, _fwd_chunk), spec(2 * D, 1, _fwd_chunk),
            spec(D, 0, _bwd_chunk), spec(D, 1, _bwd_chunk), spec(2 * D, 1, _bwd_chunk),
            full((2, RET_HEADS, C, C)), full((2, RET_HEADS, C, 1)), full((2, RET_HEADS, C, 1)),
        ],
        out_specs=[ospec_f, ospec_b],
        scratch_shapes=[pltpu.VMEM((2, RET_HEADS, RET_DK, RET_DV), F32)],
    )
    return pl.pallas_call(
        _ret_kernel,
        grid_spec=grid_spec,
        out_shape=[jax.ShapeDtypeStruct((BATCH, TT, 2 * D), BF16)] * 2,
        compiler_params=_cp(2),
        name="retention",
    )(cdec, qkvg, qkvg, qkvg, qkvg, qkvg, qkvg, dmat, qdec, kdec)


def _mixer_out_kernel(*refs, kind):
    n_src = {"ret": 4, "split": 2, "plain": 1}[kind]
    srcs, (w_ref, x_ref, mod_ref, nf_ref), route_in = refs[:n_src], refs[n_src:n_src + 4], refs[n_src + 4:n_src + 7]
    xo_ref, h_ref, e_ref, wt_ref, rank_ref, cnt_ref, wb_ref = refs[n_src + 7:]
    first = (pl.program_id(0) == 0) & (pl.program_id(1) == 0)

    @pl.when(first)
    def _():
        wb_ref[...] = w_ref[...].astype(BF16)

    if kind == "ret":
        of_ref, ob_ref, g_ref, gn_ref = srcs
        o = of_ref[...].astype(F32) + ob_ref[...].astype(F32)
        parts = []
        for h in range(RET_HEADS):
            oh = o[:, h * RET_DV:(h + 1) * RET_DV]
            mu = jnp.mean(oh, axis=-1, keepdims=True)
            ctr = oh - mu
            var = jnp.mean(ctr * ctr, axis=-1, keepdims=True)
            parts.append(ctr * lax.rsqrt(var + EPS))
        on = jnp.concatenate(parts, axis=-1) * gn_ref[...]
        a = (on * _silu(g_ref[...].astype(F32))).astype(BF16)
    elif kind == "split":
        a = jnp.where(pl.program_id(1) < LAT_TILES, srcs[0][...], srcs[1][...])
    else:
        a = srcs[0][...]
    y = jnp.dot(a, wb_ref[...], preferred_element_type=F32)
    xn = x_ref[...] + mod_ref[2:3, :] * y
    xo_ref[...] = xn
    h = _normmod(xn, nf_ref[...], mod_ref[3:4, :], mod_ref[4:5, :])
    h_ref[...] = h
    _route(h, first, *route_in, e_ref, wt_ref, rank_ref, cnt_ref)


def _mixer_out(kind, srcs, w, layer, xs, mods, norm_ffn, depth, route):
    k = w.shape[1]
    row = lambda width, col=0: pl.BlockSpec((None, ROW_TILE, width), lambda b, t: (b, t, col))
    const = lambda shape: pl.BlockSpec(shape, lambda b, t: (0,) * len(shape))
    per_tok = pl.BlockSpec((2, ROW_TILE), lambda b, t: (0, b * TPB + t))
    tri = jnp.asarray(np.triu(np.ones((ROW_TILE, ROW_TILE), np.float32), 1), BF16)
    if kind == "ret":
        src_specs = [row(2 * D), row(2 * D), row(2 * D, 2), _layer_spec((1, 2 * D), layer)]
    elif kind == "split":
        src_specs = [pl.BlockSpec((None, ROW_TILE, k), lambda b, t: (b, jnp.minimum(t, LAT_TILES - 1), 0)),
                     pl.BlockSpec((None, CTX, k), lambda b, t: (b, 0, 0))]
    else:
        src_specs = [row(k)]
    return pl.pallas_call(
        functools.partial(_mixer_out_kernel, kind=kind),
        grid=(BATCH, TPB),
        in_specs=src_specs + [_layer_spec((k, D), layer), row(D), _mod_spec(depth), _layer_spec((1, D), depth),
                              const((N_EXPERTS, D)), const((N_EXPERTS, 1)), const((ROW_TILE, ROW_TILE))],
        out_specs=[row(D), row(D), per_tok, per_tok, per_tok, const((N_EXPERTS, 128))],
        out_shape=[jax.ShapeDtypeStruct((BATCH, TT, D), F32)] * 2
        + [jax.ShapeDtypeStruct((2, NTOK), jnp.int32), jax.ShapeDtypeStruct((2, NTOK), F32),
           jax.ShapeDtypeStruct((2, NTOK), jnp.int32), jax.ShapeDtypeStruct((N_EXPERTS, 128), F32)],
        scratch_shapes=[pltpu.VMEM((k, D), BF16)],
        compiler_params=_cp(2),
        name="mixer_out_" + kind,
    )(*srcs, w, xs, mods, norm_ffn, *route, tri)


def _fn_channel_kernel(h_ref, cs_ref, o_ref):
    for g in range(FN_GROUPS):
        r = jnp.dot(h_ref[:, g * FN_GC:(g + 1) * FN_GC], cs_ref[...], preferred_element_type=F32)
        o_ref[0, :, g * FN_GC:(g + 1) * FN_GC] = r[:, :FN_GC].astype(BF16)
        o_ref[1, :, g * FN_GC:(g + 1) * FN_GC] = r[:, FN_GC:].astype(BF16)


def _fn_channel(h, cs):
    return pl.pallas_call(
        _fn_channel_kernel,
        grid=(BATCH, TPB),
        in_specs=[
            pl.BlockSpec((None, ROW_TILE, D), lambda b, t: (b, t, 0)),
            pl.BlockSpec((FN_GC, 2 * FN_GC), lambda b, t: (0, 0)),
        ],
        out_specs=pl.BlockSpec((None, 2, ROW_TILE, D), lambda b, t: (b, 0, t, 0)),
        out_shape=jax.ShapeDtypeStruct((BATCH, 2, TT, D), BF16),
        compiler_params=_cp(2),
        name="fourier_channels",
    )(h, cs)


FN_R = 64
FN_CB = 16


def _fn_rows_kernel(ab_ref, f1_ref, twc_ref, tws_ref, y_ref, u32):
    u32[...] = ab_ref[...].astype(F32)
    for j in range(FN_CB):
        st = jnp.concatenate([u32[0, :, j, :], u32[1, :, j, :]], axis=0).astype(BF16)
        y = jnp.dot(f1_ref[...], st, preferred_element_type=F32)
        yr, yi = y[:FN_R], y[FN_R:]
        cw, sw = twc_ref[j], tws_ref[j]
        y_ref[0, j] = (yr * cw - yi * sw).astype(BF16)
        y_ref[1, j] = (yi * cw + yr * sw).astype(BF16)


def _fn_rows(ab, f1, twc, tws):
    ab5 = ab.reshape(BATCH, 2, TT // FN_R, FN_R, D)
    return pl.pallas_call(
        _fn_rows_kernel,
        grid=(BATCH, FN_R // FN_CB),
        in_specs=[
            pl.BlockSpec((None, 2, FN_R, FN_CB, D), lambda b, cb: (b, 0, 0, cb, 0)),
            pl.BlockSpec((2 * FN_R, 2 * FN_R), lambda b, cb: (0, 0)),
            pl.BlockSpec((FN_CB, FN_R, 1), lambda b, cb: (cb, 0, 0)),
            pl.BlockSpec((FN_CB, FN_R, 1), lambda b, cb: (cb, 0, 0)),
        ],
        out_specs=pl.BlockSpec((None, 2, FN_CB, FN_R, D), lambda b, cb: (b, 0, cb, 0, 0)),
        out_shape=jax.ShapeDtypeStruct((BATCH, 2, FN_R, FN_R, D), BF16),
        scratch_shapes=[pltpu.VMEM((2, FN_R, FN_CB, D), F32)],
        compiler_params=_cp(2),
        name="fourier_rows",
    )(ab5, f1, twc, tws)


def _fn_cols_kernel(y_ref, f2_ref, o_ref, y32, z32):
    y32[...] = y_ref[...].astype(F32)
    for j in range(FN_CB):
        st = jnp.concatenate([y32[0, :, j, :], y32[1, :, j, :]], axis=0).astype(BF16)
        z32[:, j, :] = jnp.dot(f2_ref[...], st, preferred_element_type=F32)
    o_ref[...] = z32[...].astype(BF16)


def _fn_cols(y, f2):
    out = pl.pallas_call(
        _fn_cols_kernel,
        grid=(BATCH, FN_R // FN_CB),
        in_specs=[
            pl.BlockSpec((None, 2, FN_R, FN_CB, D), lambda b, kb: (b, 0, 0, kb, 0)),
            pl.BlockSpec((FN_R, 2 * FN_R), lambda b, kb: (0, 0)),
        ],
        out_specs=pl.BlockSpec((None, FN_R, FN_CB, D), lambda b, kb: (b, 0, kb, 0)),
        out_shape=jax.ShapeDtypeStruct((BATCH, FN_R, FN_R, D), BF16),
        scratch_shapes=[pltpu.VMEM((2, FN_R, FN_CB, D), F32), pltpu.VMEM((FN_R, FN_CB, D), F32)],
        compiler_params=_cp(2),
        name="fourier_cols",
    )(y, f2)
    return out.reshape(BATCH, SEQ, D)


def _fn_ctx_kernel(ab_ref, m_ref, o_ref):
    st = jnp.concatenate([ab_ref[0], ab_ref[1]], axis=0)
    o_ref[...] = jnp.dot(m_ref[...], st, preferred_element_type=F32).astype(BF16)


def _fn_ctx(ab, m_ctx):
    return pl.pallas_call(
        _fn_ctx_kernel,
        grid=(BATCH,),
        in_specs=[
            pl.BlockSpec((None, 2, CTX, D), lambda b: (b, 0, SEQ // CTX, 0)),
            pl.BlockSpec((CTX, 2 * CTX), lambda b: (0, 0)),
        ],
        out_specs=pl.BlockSpec((None, CTX, D), lambda b: (b, 0, 0)),
        out_shape=jax.ShapeDtypeStruct((BATCH, CTX, D), BF16),
        compiler_params=_cp(1),
        name="fourier_ctx",
    )(ab, m_ctx)


def _dft_tables():
    def cs_pair(n, scale):
        i = np.arange(n)
        ang = 2.0 * np.pi * ((i[:, None] * i[None, :]) % n) / n
        return np.cos(ang) * scale, np.sin(ang) * scale

    assert FN_GC == CTX
    c, s = cs_pair(FN_GC, FN_GC ** -0.5)
    cs = np.concatenate([c, s], axis=1)
    m_ctx = np.concatenate([c, -s], axis=1)
    c, s = cs_pair(FN_R, FN_R ** -0.5)
    f1 = np.block([[c, -s], [s, c]])
    f2 = np.concatenate([c, -s], axis=1)
    i = np.arange(FN_R)
    phi = 2.0 * np.pi * (i[:, None] * i[None, :]) / SEQ
    tw = [jnp.asarray(f(phi)[..., None], F32) for f in (np.cos, np.sin)]
    return jnp.asarray(cs, BF16), jnp.asarray(m_ctx, BF16), jnp.asarray(f1, BF16), jnp.asarray(f2, BF16), tw


NA_ROWS = SEQ // GRID_W
NA_STEPS = TT // NA_QT
NA_NK = NA_KH * GRID_W
NA_DR = 2 * NA_KH - 1


def _na_row_start(step):
    r = jnp.minimum(step, NA_ROWS - 1)
    return r, jnp.clip(r - NA_KH // 2, 0, NA_ROWS - NA_KH)


def _na_bias_idx(step):
    r, rs = _na_row_start(step)
    return rs - r + (NA_KH - 1)


def _na_kernel(q_ref, k_ref, v_ref, bias_ref, o_ref):
    step = pl.program_id(1)
    lane = lax.broadcasted_iota(jnp.int32, (NA_QT, 128), 1)
    nt = (((1,), (1,)), ((), ()))
    scale = NA_HD ** -0.5

    def attend(start, d0=None):
        for p in range(NA_HEADS // 2):
            sl = slice(p * 128, (p + 1) * 128)
            qp = q_ref[:, sl] * scale
            zero = jnp.zeros_like(qp)
            q2 = jnp.concatenate([jnp.where(lane < NA_HD, qp, zero), jnp.where(lane >= NA_HD, qp, zero)], axis=0)
            s_c = lax.dot_general(q2, k_ref[SEQ:TT, sl], nt, preferred_element_type=F32)
            m = jnp.max(s_c, axis=-1, keepdims=True)
            if start is not None:
                bias = jnp.concatenate(
                    [jnp.concatenate([bias_ref[2 * p + hh, d0 + 2 * j2] for j2 in range(NA_KH // 2)], axis=1)
                     for hh in range(2)], axis=0)
                s_w = lax.dot_general(q2, k_ref[pl.ds(start, NA_NK), sl], nt, preferred_element_type=F32) + bias
                m = jnp.maximum(m, jnp.max(s_w, axis=-1, keepdims=True))
            p_c = jnp.exp(s_c - m)
            l = jnp.sum(p_c, axis=-1, keepdims=True)
            acc = jnp.dot(p_c.astype(BF16), v_ref[SEQ:TT, sl], preferred_element_type=F32)
            if start is not None:
                p_w = jnp.exp(s_w - m)
                l = l + jnp.sum(p_w, axis=-1, keepdims=True)
                acc = acc + jnp.dot(p_w.astype(BF16), v_ref[pl.ds(start, NA_NK), sl], preferred_element_type=F32)
            o2 = acc / l
            o_ref[:, sl] = jnp.where(lane < NA_HD, o2[:NA_QT], o2[NA_QT:]).astype(BF16)

    @pl.when(step < NA_ROWS)
    def _():
        _, rs = _na_row_start(step)
        attend(pl.multiple_of(rs * GRID_W, GRID_W), _na_bias_idx(step))

    @pl.when(step >= NA_ROWS)
    def _():
        attend(None)


def _na_attention(qkv, bias):
    return pl.pallas_call(
        _na_kernel,
        grid=(BATCH, NA_STEPS),
        in_specs=[
            pl.BlockSpec((None, NA_QT, D), lambda b, s: (b, s, 0)),
            pl.BlockSpec((None, TT, D), lambda b, s: (b, 0, 1), pipeline_mode=pl.Buffered(1)),
            pl.BlockSpec((None, TT, D), lambda b, s: (b, 0, 2), pipeline_mode=pl.Buffered(1)),
            pl.BlockSpec((NA_HEADS, NA_DR - 1, GRID_W, 2 * GRID_W), lambda b, s: (0, 0, 0, 0),
                         pipeline_mode=pl.Buffered(1)),
        ],
        out_specs=pl.BlockSpec((None, NA_QT, D), lambda b, s: (b, s, 0)),
        out_shape=jax.ShapeDtypeStruct((BATCH, TT, D), BF16),
        compiler_params=_cp(2),
        name="na_attention",
    )(qkv, qkv, qkv, bias)


def _na_bias_table(rpb):
    qc = np.arange(GRID_W)
    kcol = np.arange(GRID_W)
    cstart = np.clip(qc - NA_KW // 2, 0, GRID_W - NA_KW)
    valid = (kcol[None, :] >= cstart[:, None]) & (kcol[None, :] < cstart[:, None] + NA_KW)
    dc = kcol[None, :] - qc[:, None] + (NA_KW - 1)
    pick = ((dc[None] == np.arange(2 * NA_KW - 1)[:, None, None]) & valid[None]).astype(np.float32)
    tab = jnp.einsum("hdj,jqk->hdqk", rpb.astype(F32), jnp.asarray(pick), precision=HIGHEST)
    tab = tab + jnp.asarray(np.where(valid, 0.0, NEG_INF).astype(np.float32))
    return jnp.concatenate([tab[:, :NA_DR - 1], tab[:, 1:]], axis=-1)


def _route(h, first, rw_ref, rb_ref, tri_ref, e_ref, w_ref, rank_ref, cnt_ref):
    logits = lax.dot_general(rw_ref[...], h.astype(BF16), (((1,), (1,)), ((), ())),
                             preferred_element_type=F32)
    sc = jax.nn.sigmoid(logits)
    sel = sc + rb_ref[...]
    sc_r = [sc[e:e + 1, :] for e in range(N_EXPERTS)]
    sel_r = [sel[e:e + 1, :] for e in range(N_EXPERTS)]
    zero_i = jnp.zeros_like(sel_r[0]).astype(jnp.int32)

    best = gi = None
    picks = []
    for g in range(N_GROUPS):
        a = sel_r[g * EPG:(g + 1) * EPG]
        u = sc_r[g * EPG:(g + 1) * EPG]
        gs = None
        for i in range(EPG):
            for j in range(i + 1, EPG):
                pair = a[i] + a[j]
                gs = pair if gs is None else jnp.maximum(gs, pair)
        m1, i1, s1 = a[0], zero_i, u[0]
        for j in range(1, EPG):
            upd = a[j] > m1
            m1 = jnp.where(upd, a[j], m1)
            i1 = jnp.where(upd, j, i1)
            s1 = jnp.where(upd, u[j], s1)
        m2 = i2 = s2 = None
        for j in range(EPG):
            cand = jnp.where(i1 == j, -jnp.inf, a[j])
            if m2 is None:
                m2, i2, s2 = cand, zero_i, u[0]
            else:
                upd = cand > m2
                m2 = jnp.where(upd, cand, m2)
                i2 = jnp.where(upd, j, i2)
                s2 = jnp.where(upd, u[j], s2)
        picks.append((i1 + g * EPG, i2 + g * EPG, s1, s2))
        if best is None:
            best, gi = gs, zero_i
        else:
            upd = gs > best
            best = jnp.where(upd, gs, best)
            gi = jnp.where(upd, g, gi)

    e1, e2, s1, s2 = picks[0]
    for g in range(1, N_GROUPS):
        on = gi == g
        e1 = jnp.where(on, picks[g][0], e1)
        e2 = jnp.where(on, picks[g][1], e2)
        s1 = jnp.where(on, picks[g][2], s1)
        s2 = jnp.where(on, picks[g][3], s2)
    tot = s1 + s2
    e_ref[0:1, :] = e1
    e_ref[1:2, :] = e2
    w_ref[0:1, :] = s1 / tot
    w_ref[1:2, :] = s2 / tot

    @pl.when(first)
    def _():
        cnt_ref[...] = jnp.zeros_like(cnt_ref)

    eid = lax.broadcasted_iota(jnp.int32, sc.shape, 0)
    oh1 = (eid == e1).astype(F32)
    oh2 = (eid == e2).astype(F32)
    before1 = jnp.dot(oh1.astype(BF16), tri_ref[...], preferred_element_type=F32)
    before2 = jnp.dot(oh2.astype(BF16), tri_ref[...], preferred_element_type=F32)
    tot1 = jnp.sum(oh1, axis=1, keepdims=True)
    tot2 = jnp.sum(oh2, axis=1, keepdims=True)
    base = cnt_ref[:, 0:1]
    rank_ref[0:1, :] = jnp.sum(oh1 * (base + before1), axis=0, keepdims=True).astype(jnp.int32)
    rank_ref[1:2, :] = jnp.sum(oh2 * (base + tot1 + before2), axis=0, keepdims=True).astype(jnp.int32)
    cnt_ref[...] = cnt_ref[...] + (tot1 + tot2)


DMA_UNROLL = 8


def _row_copy(src_ref, src_row, dst_ref, dst_row, sem):
    return pltpu.make_async_copy(src_ref.at[pl.ds(src_row, 1)], dst_ref.at[pl.ds(dst_row, 1)], sem)


def _for_each_row_pair(fn):
    def group(gidx, carry):
        for u in range(DMA_UNROLL):
            for kk in range(2):
                fn(gidx * DMA_UNROLL + u, kk)
        return carry

    lax.fori_loop(0, ROW_TILE // DMA_UNROLL, group, 0)


def _dispatch_kernel(zrow_ref, dest_ref, h_ref, xs_ref, zbuf, sem, zsem):
    @pl.when(pl.program_id(0) == 0)
    def _():
        zbuf[...] = jnp.zeros_like(zbuf)

        def zero_copy(e):
            start = pl.multiple_of(jnp.maximum(zrow_ref[e], 0), MOE_BM)
            return pltpu.make_async_copy(zbuf, xs_ref.at[pl.ds(start, MOE_BM)], zsem)

        for e in range(MOE_NZ):
            pl.when(zrow_ref[e] >= 0)(lambda e=e: zero_copy(e).start())
        for e in range(MOE_NZ):
            pl.when(zrow_ref[e] >= 0)(lambda e=e: zero_copy(e).wait())

    _for_each_row_pair(lambda r, kk: _row_copy(h_ref, r, xs_ref, dest_ref[kk, r], sem).start())
    _for_each_row_pair(lambda r, kk: _row_copy(h_ref, r, xs_ref, dest_ref[kk, r], sem).wait())


def _dispatch(h, dest, zrow):
    grid_spec = pltpu.PrefetchScalarGridSpec(
        num_scalar_prefetch=1,
        grid=(NTOK // ROW_TILE,),
        in_specs=[
            pl.BlockSpec((None, 2, ROW_TILE), lambda i, z: (i, 0, 0), memory_space=pltpu.SMEM),
            pl.BlockSpec((ROW_TILE, D), lambda i, z: (i, 0)),
        ],
        out_specs=pl.BlockSpec(memory_space=pl.ANY),
        scratch_shapes=[pltpu.VMEM((MOE_BM, D), F32), pltpu.SemaphoreType.DMA(()), pltpu.SemaphoreType.DMA(())],
    )
    return pl.pallas_call(
        _dispatch_kernel,
        grid_spec=grid_spec,
        out_shape=jax.ShapeDtypeStruct((MOE_P, D), F32),
        compiler_params=pltpu.CompilerParams(dimension_semantics=("arbitrary",), vmem_limit_bytes=VMEM_LIMIT,
                                             has_side_effects=True),
        name="moe_dispatch",
    )(zrow, dest, h)


def _experts_kernel(be_ref, nu_ref, x_ref, wg_ref, wu_ref, wd_ref, y_ref, wgb, wub, wdb):
    i = pl.program_id(0)
    active = i < nu_ref[0]
    changed = (i == 0) | (be_ref[i] != be_ref[jnp.maximum(i - 1, 0)])

    @pl.when(active & changed)
    def _():
        wgb[...] = wg_ref[...].astype(BF16)
        wub[...] = wu_ref[...].astype(BF16)
        wdb[...] = wd_ref[...].astype(BF16)

    @pl.when(active)
    def _():
        x = x_ref[...].astype(BF16)
        g = jnp.dot(x, wgb[...], preferred_element_type=F32)
        u = jnp.dot(x, wub[...], preferred_element_type=F32)
        hh = (_silu(g) * u).astype(BF16)
        y_ref[...] = jnp.dot(hh, wdb[...], preferred_element_type=F32)

    @pl.when(jnp.logical_not(active))
    def _():
        y_ref[...] = jnp.zeros_like(y_ref)


def _experts(xs, block_e, n_used, w_gate, w_up, w_down, depth):
    wspec = pl.BlockSpec((None, None, D, D), lambda i, be, nu: (depth, be[i], 0, 0))
    grid_spec = pltpu.PrefetchScalarGridSpec(
        num_scalar_prefetch=2,
        grid=(MOE_NB,),
        in_specs=[pl.BlockSpec((MOE_BM, D), lambda i, be, nu: (jnp.minimum(i, nu[0] - 1), 0)), wspec, wspec, wspec],
        out_specs=pl.BlockSpec((MOE_BM, D), lambda i, be, nu: (i, 0)),
        scratch_shapes=[pltpu.VMEM((D, D), BF16)] * 3,
    )
    return pl.pallas_call(
        _experts_kernel,
        grid_spec=grid_spec,
        out_shape=jax.ShapeDtypeStruct((MOE_P, D), F32),
        compiler_params=_cp(1),
        name="moe_experts",
    )(block_e, n_used, xs, w_gate, w_up, w_down)


def _combine_kernel(dest_ref, yb_ref, wt_ref, x_ref, mod_ref, modn_ref, gn_ref, *rest, final):
    if final:
        out_ref, ybuf, sem = rest
    else:
        xo_ref, h_ref, ybuf, sem = rest
    t = pl.program_id(1)

    def body():
        _for_each_row_pair(lambda r, kk: _row_copy(yb_ref, dest_ref[kk, r], ybuf.at[kk], r, sem).start())
        _for_each_row_pair(lambda r, kk: _row_copy(yb_ref, dest_ref[kk, r], ybuf.at[kk], r, sem).wait())
        f = wt_ref[:, 0:1] * ybuf[0] + wt_ref[:, 1:2] * ybuf[1]
        xn = x_ref[...] + mod_ref[5:6, :] * f
        if final:
            out_ref[...] = _normmod(xn, gn_ref[...], 0.0, 0.0)
        else:
            xo_ref[...] = xn
            h_ref[...] = _normmod(xn, gn_ref[...], modn_ref[0:1, :], modn_ref[1:2, :]).astype(BF16)

    if final:
        pl.when(t < LAT_TILES)(body)
    else:
        body()


def _combine(yb, dest, wts, xs, mods, depth, g_next, final):
    row = pl.BlockSpec((None, ROW_TILE, D), lambda b, t: (b, t, 0))
    next_depth = depth if final else depth + 1
    if final:
        out_specs = pl.BlockSpec((None, ROW_TILE, D), lambda b, t: (b, jnp.minimum(t, LAT_TILES - 1), 0))
        out_shape = jax.ShapeDtypeStruct((BATCH, SEQ, D), F32)
    else:
        out_specs = [row, row]
        out_shape = [jax.ShapeDtypeStruct((BATCH, TT, D), F32), jax.ShapeDtypeStruct((BATCH, TT, D), BF16)]
    return pl.pallas_call(
        functools.partial(_combine_kernel, final=final),
        grid=(BATCH, TPB),
        in_specs=[
            pl.BlockSpec((None, 2, ROW_TILE), lambda b, t: (b * TPB + t, 0, 0), memory_space=pltpu.SMEM),
            pl.BlockSpec(memory_space=pl.ANY),
            pl.BlockSpec((ROW_TILE, 2), lambda b, t: (b * TPB + t, 0)),
            row, _mod_spec(depth), _mod_spec(next_depth),
            g_next[1],
        ],
        out_specs=out_specs,
        out_shape=out_shape,
        scratch_shapes=[pltpu.VMEM((2, ROW_TILE, D), F32), pltpu.SemaphoreType.DMA(())],
        compiler_params=_cp(2),
        name="moe_combine",
    )(dest, yb, wts, xs, mods, mods, g_next[0])


def _moe_plan(experts, rank, counts):
    padded = (counts + MOE_BM - 1) // MOE_BM * MOE_BM
    pend = jnp.cumsum(padded)
    pstart = pend - padded
    ids = jnp.arange(N_EXPERTS, dtype=jnp.int32)[:, None, None]
    dest = rank + jnp.sum(jnp.where(experts[None] == ids, pstart[:, None, None], 0), axis=0)
    dest = jnp.transpose(dest.astype(jnp.int32).reshape(2, NTOK // ROW_TILE, ROW_TILE), (1, 0, 2))
    first_row = jnp.arange(MOE_NB, dtype=jnp.int32) * MOE_BM
    block_e = jnp.minimum(jnp.sum(pend[None, :] <= first_row[:, None], axis=1), N_EXPERTS - 1).astype(jnp.int32)
    n_used = (pend[-1:] // MOE_BM).astype(jnp.int32)
    tail = jnp.arange(MOE_NB - MOE_TAIL, MOE_NB, dtype=jnp.int32) * MOE_BM
    zrow = jnp.concatenate([jnp.where(padded > 0, pend - MOE_BM, -1), jnp.where(tail >= pend[-1], tail, -1)])
    return dest, block_e, n_used, zrow.astype(jnp.int32)


def _moe(h, routing, xs, mods, depth, g_next, w_gate, w_up, w_down, final):
    e, wts, rank, cnt = routing
    dest, block_e, n_used, zrow = _moe_plan(e, rank, cnt[:, 0].astype(jnp.int32))
    xsorted = _dispatch(h.reshape(NTOK, D), dest, zrow)
    yb = _experts(xsorted, block_e, n_used, w_gate, w_up, w_down, depth)
    return _combine(yb, dest, wts.T, xs, mods, depth, g_next, final)


def _rope_tables():
    nf = RET_DK // 4
    t = np.arange(SEQ)
    inv = ROPE_BASE ** (-np.arange(nf, dtype=np.float32) / nf)
    cos = np.ones((TT, 256), np.float32)
    sin = np.zeros((TT, 256), np.float32)
    for seg, pos in enumerate(((t // GRID_W).astype(np.float32), (t % GRID_W).astype(np.float32))):
        ang = (pos[:, None] * inv[None, :]).astype(np.float32)
        c, s = np.cos(ang), np.sin(ang)
        cos[:SEQ, seg * 128:(seg + 1) * 128] = np.concatenate([c, c], axis=1)
        sin[:SEQ, seg * 128:(seg + 1) * 128] = np.concatenate([-s, s], axis=1)
    ks = np.float32(RET_DK ** -0.5)
    return jnp.asarray(np.stack([cos, cos * ks])), jnp.asarray(np.stack([sin, sin * ks]))


def _decay_tables(decay_logit):
    C = RET_CHUNK
    log_g = jax.nn.log_sigmoid(decay_logit.astype(F32))
    pos = jnp.arange(C, dtype=F32)
    diff = pos[:, None] - pos[None, :]
    lg = log_g[:, :, None, None]
    dm_f = jnp.where(diff >= 0, jnp.exp(jnp.maximum(diff, 0.0)[None, None] * lg), 0.0)
    dm_b = jnp.where(diff <= 0, jnp.exp(jnp.maximum(-diff, 0.0)[None, None] * lg), 0.0)
    dmat = jnp.stack([dm_f[0], dm_b[1]])
    lgc = log_g[:, :, None]
    qdec = jnp.stack([jnp.exp((pos + 1.0)[None, :] * lgc[0]), jnp.exp((C - pos)[None, :] * lgc[1])])
    kdec = jnp.stack([jnp.exp((C - 1.0 - pos)[None, :] * lgc[0]), jnp.exp(pos[None, :] * lgc[1])])
    cdec = jnp.exp(C * log_g).reshape(-1)
    return dmat, qdec[..., None], kdec[..., None], cdec


def kernel(x, c, ctx, c_ctx, w_mod, b_mod, norm_mix, norm_ffn, norm_final, ret_w_in, ret_decay, ret_gn, ret_w_out,
           fn_w_out, na_w_qkv, na_rpb, na_w_out, router_w, router_bias, moe_w_gate, moe_w_up, moe_w_down):
    cvec = jnp.concatenate([c_ctx[None, :], c, jnp.zeros((8 - 1 - BATCH, D), F32)], axis=0)
    mod_all = _modulation(cvec, w_mod, b_mod).reshape(DEPTH, 8, N_MOD, D)
    pad = jnp.zeros((DEPTH, BATCH, 2, 8 - N_MOD, D), F32)
    mods = jnp.stack([jnp.broadcast_to(mod_all[:, 0:1], (DEPTH, BATCH, N_MOD, D)), mod_all[:, 1:1 + BATCH]], axis=2)
    mods = jnp.concatenate([mods, pad], axis=3)

    rope = _rope_tables()
    cs, dft_ctx, dft_f1, dft_f2, dft_tw = _dft_tables()
    route = (router_w.T.astype(BF16), router_bias.reshape(N_EXPERTS, 1).astype(F32))

    norm_mix3 = norm_mix.reshape(DEPTH, 1, D)
    norm_ffn3 = norm_ffn.reshape(DEPTH, 1, D)
    ret_gn3 = ret_gn.reshape(-1, 1, 2 * D)

    xs, h = _embed(x, ctx, mods, norm_mix3)
    out = None
    for i in range(DEPTH):
        kind, j = i % 3, i // 3
        if kind == 0:
            qkvg = _project(h.reshape(NTOK, D), ret_w_in, j, rope).reshape(BATCH, TT, 6 * D)
            dmat, qdec, kdec, cdec = _decay_tables(ret_decay[j])
            o_f, o_b = _retention(qkvg, dmat, qdec, kdec, cdec)
            xs, hf, *routing = _mixer_out("ret", (o_f, o_b, qkvg, ret_gn3), ret_w_out, j, xs, mods, norm_ffn3, i, route)
        elif kind == 1:
            ab = _fn_channel(h, cs)
            f_lat = _fn_cols(_fn_rows(ab, dft_f1, *dft_tw), dft_f2)
            xs, hf, *routing = _mixer_out(
                "split", (f_lat, _fn_ctx(ab, dft_ctx)), fn_w_out, j, xs, mods, norm_ffn3, i, route)
        else:
            qkv = _project(h.reshape(NTOK, D), na_w_qkv, j).reshape(BATCH, TT, 3 * D)
            o = _na_attention(qkv, _na_bias_table(na_rpb[j]))
            xs, hf, *routing = _mixer_out("plain", (o,), na_w_out, j, xs, mods, norm_ffn3, i, route)
        final = i == DEPTH - 1
        if final:
            g_next = (norm_final[None, :], pl.BlockSpec((1, D), lambda b, t: (0, 0)))
        else:
            g_next = (norm_mix3, _layer_spec((1, D), i + 1))
        res = _moe(hf, routing, xs, mods, i, g_next, moe_w_gate, moe_w_up, moe_w_down, final)
        if final:
            out = res
        else:
            xs, h = res
    return out
```

```python
import functools

import numpy as np
import jax
import jax.numpy as jnp
from jax import lax
from jax.experimental import pallas as pl
from jax.experimental.pallas import tpu as pltpu

F32 = jnp.float32
BF16 = jnp.bfloat16
HIGHEST = lax.Precision.HIGHEST

D = 1024
BATCH = 4
SEQ = 4096
CTX = 256
TT = SEQ + CTX
NTOK = BATCH * TT
DEPTH = 4
GRID_W = 64
EPS = 1e-6
NEG_INF = -1e30
N_MOD = 6

RET_HEADS = 4
RET_DK = D // RET_HEADS
RET_DV = 2 * RET_DK
RET_CHUNK = 256
ROPE_BASE = 10000.0

FN_GROUPS = 4
FN_GC = D // FN_GROUPS

NA_HEADS = 16
NA_HD = D // NA_HEADS
NA_KH = 8
NA_KW = 16

N_EXPERTS = 16
N_GROUPS = 4
EPG = N_EXPERTS // N_GROUPS

ROW_TILE = 256
TPB = TT // ROW_TILE
LAT_TILES = SEQ // ROW_TILE
MM_TM = TT // 4
MOE_BM = 256
PAIRS = EPG * (EPG - 1) // 2
N_CLS = N_GROUPS * PAIRS
CLS_PAD = 32
MOE_NB = (NTOK + N_CLS * (MOE_BM - 1) + MOE_BM - 1) // MOE_BM
MOE_P = MOE_NB * MOE_BM
MOE_TAIL = MOE_NB - NTOK // MOE_BM
MOE_NZ = N_CLS + MOE_TAIL
DX = D + 128
NA_QT = GRID_W
VMEM_LIMIT = 56 * 1024 * 1024


def _cp(n_axes, vmem=VMEM_LIMIT):
    return pltpu.CompilerParams(dimension_semantics=("arbitrary",) * n_axes, vmem_limit_bytes=vmem)


def _silu(x):
    return x * jax.nn.sigmoid(x)


def _normmod(x, g, shift, scale):
    y = x * lax.rsqrt(jnp.mean(x * x, axis=-1, keepdims=True) + EPS)
    return (y * g) * (1.0 + scale) + shift


def _mod_idx(t):
    return jnp.where(t < LAT_TILES, 1, 0)


def _layer_spec(shape, layer):
    return pl.BlockSpec((None,) + shape, lambda *_: (layer,) + (0,) * len(shape))


def _mod_spec(depth):
    return pl.BlockSpec((None, None, None, 8, D), lambda b, t: (depth, b, _mod_idx(t), 0, 0))


def _mod_kernel(c_ref, w_ref, b_ref, o_ref):
    a = _silu(c_ref[...])
    o_ref[...] = jnp.dot(a, w_ref[...], precision=HIGHEST, preferred_element_type=F32) + b_ref[...]


def _modulation(cvec, w_mod, b_mod):
    return pl.pallas_call(
        _mod_kernel,
        grid=(DEPTH, N_MOD),
        in_specs=[
            pl.BlockSpec((8, D), lambda i, j: (0, 0)),
            pl.BlockSpec((None, D, D), lambda i, j: (i, 0, j)),
            pl.BlockSpec((None, 1, D), lambda i, j: (i, 0, j)),
        ],
        out_specs=pl.BlockSpec((None, 8, D), lambda i, j: (i, 0, j)),
        out_shape=jax.ShapeDtypeStruct((DEPTH, 8, N_MOD * D), F32),
        compiler_params=_cp(2),
        name="modulation",
    )(cvec, w_mod, b_mod.reshape(DEPTH, 1, N_MOD * D))


def _embed_kernel(x_ref, ctx_ref, mod_ref, g_ref, xs_ref, h_ref):
    t = pl.program_id(1)

    def emit(src):
        xs_ref[...] = src
        h_ref[...] = _normmod(src, g_ref[...], mod_ref[0:1, :], mod_ref[1:2, :]).astype(BF16)

    @pl.when(t < LAT_TILES)
    def _():
        emit(x_ref[...])

    @pl.when(t >= LAT_TILES)
    def _():
        emit(ctx_ref[...])


def _embed(x, ctx, mods, g):
    return pl.pallas_call(
        _embed_kernel,
        grid=(BATCH, TPB),
        in_specs=[
            pl.BlockSpec((None, ROW_TILE, D), lambda b, t: (b, jnp.minimum(t, LAT_TILES - 1), 0)),
            pl.BlockSpec((None, CTX, D), lambda b, t: (b, 0, 0)),
            _mod_spec(0),
            _layer_spec((1, D), 0),
        ],
        out_specs=[
            pl.BlockSpec((None, ROW_TILE, D), lambda b, t: (b, t, 0)),
            pl.BlockSpec((None, ROW_TILE, D), lambda b, t: (b, t, 0)),
        ],
        out_shape=[
            jax.ShapeDtypeStruct((BATCH, TT, D), F32),
            jax.ShapeDtypeStruct((BATCH, TT, D), BF16),
        ],
        compiler_params=_cp(2),
        name="embed",
    )(x, ctx, mods, g)


def _rope_store(acc, cos_ref, sin_ref, o_ref):
    for c in range(acc.shape[1] // 128):
        xc = acc[:, c * 128:(c + 1) * 128]
        tsl = slice((c % 2) * 128, (c % 2 + 1) * 128)
        rot = xc * cos_ref[:, tsl] + pltpu.roll(xc, 64, 1) * sin_ref[:, tsl]
        o_ref[:, c * 128:(c + 1) * 128] = rot.astype(o_ref.dtype)


def _proj_kernel(a_ref, w_ref, *rest, n_rope):
    if n_rope:
        cos_ref, sin_ref, o_ref, wb_ref = rest
    else:
        o_ref, wb_ref = rest
    j = pl.program_id(0)

    @pl.when(pl.program_id(1) == 0)
    def _():
        wb_ref[...] = w_ref[...].astype(BF16)

    acc = jnp.dot(a_ref[...], wb_ref[...], preferred_element_type=F32)
    if n_rope:
        @pl.when(j < n_rope)
        def _():
            _rope_store(acc, cos_ref, sin_ref, o_ref)

        @pl.when(j >= n_rope)
        def _():
            o_ref[...] = acc.astype(o_ref.dtype)
    else:
        o_ref[...] = acc.astype(o_ref.dtype)


def _project(a, w, layer, rope=None):
    _, k, n = w.shape
    tn = D
    n_rope = 0 if rope is None else rope[0].shape[0]
    in_specs = [
        pl.BlockSpec((MM_TM, k), lambda j, i: (i, 0)),
        pl.BlockSpec((None, k, tn), lambda j, i: (layer, 0, j)),
    ]
    args = [a, w]
    if n_rope:
        tspec = pl.BlockSpec((None, MM_TM, 256), lambda j, i: (jnp.minimum(j, n_rope - 1), i % (TT // MM_TM), 0))
        in_specs += [tspec, tspec]
        args += list(rope)
    return pl.pallas_call(
        functools.partial(_proj_kernel, n_rope=n_rope),
        grid=(n // tn, NTOK // MM_TM),
        in_specs=in_specs,
        out_specs=pl.BlockSpec((MM_TM, tn), lambda j, i: (i, j)),
        out_shape=jax.ShapeDtypeStruct((NTOK, n), BF16),
        scratch_shapes=[pltpu.VMEM((k, tn), BF16)],
        compiler_params=_cp(2),
        name="project",
    )(*args)


N_CHUNKS = TT // RET_CHUNK
LAT_CHUNKS = SEQ // RET_CHUNK
CTX_CHUNKS = CTX // RET_CHUNK


def _fwd_chunk(n):
    return jnp.where(n < CTX_CHUNKS, LAT_CHUNKS + n, n - CTX_CHUNKS)


def _bwd_chunk(n):
    return N_CHUNKS - 1 - n


def _ret_kernel(cdec_ref, qf, kf, vf, qb, kb, vb, dmat_ref, qdec_ref, kdec_ref, of_ref, ob_ref, s_ref):
    @pl.when(pl.program_id(1) == 0)
    def _():
        s_ref[...] = jnp.zeros_like(s_ref)

    for d, (q_r, k_r, v_r, o_r) in enumerate(((qf, kf, vf, of_ref), (qb, kb, vb, ob_ref))):
        for h in range(RET_HEADS):
            q = q_r[:, h * RET_DK:(h + 1) * RET_DK]
            k = k_r[:, h * RET_DK:(h + 1) * RET_DK]
            v = v_r[:, h * RET_DV:(h + 1) * RET_DV]
            att = lax.dot_general(q, k, (((1,), (1,)), ((), ())), preferred_element_type=F32) * dmat_ref[d, h]
            s = s_ref[d, h]
            qd = (q.astype(F32) * qdec_ref[d, h]).astype(BF16)
            o = (jnp.dot(att.astype(BF16), v, preferred_element_type=F32)
                 + jnp.dot(qd, s.astype(BF16), preferred_element_type=F32))
            kd = (k.astype(F32) * kdec_ref[d, h]).astype(BF16)
            s_ref[d, h] = s * cdec_ref[d * RET_HEADS + h] + lax.dot_general(
                kd, v, (((0,), (0,)), ((), ())), preferred_element_type=F32)
            o_r[:, h * RET_DV:(h + 1) * RET_DV] = o.astype(o_r.dtype)


def _retention(qkvg, dmat, qdec, kdec, cdec):
    C = RET_CHUNK

    def spec(width, col, chunk_fn):
        return pl.BlockSpec((None, C, width), lambda b, n, cd: (b, chunk_fn(n), col))

    ospec_f = pl.BlockSpec((None, C, 2 * D), lambda b, n, cd: (b, _fwd_chunk(n), 0))
    ospec_b = pl.BlockSpec((None, C, 2 * D), lambda b, n, cd: (b, _bwd_chunk(n), 0))
    full = lambda shape: pl.BlockSpec(shape, lambda b, n, cd: (0,) * len(shape))
    grid_spec = pltpu.PrefetchScalarGridSpec(
        num_scalar_prefetch=1,
        grid=(BATCH, N_CHUNKS),
        in_specs=[
            spec(D, 0, _fwd_chunk), spec(D, 1, _fwd_chunk), spec(2 * D, 1, _fwd_chunk),
            spec(D, 0, _bwd_chunk), spec(D, 1, _bwd_chunk), spec(2 * D, 1, _bwd_chunk),
            full((2, RET_HEADS, C, C)), full((2, RET_HEADS, C, 1)), full((2, RET_HEADS, C, 1)),
        ],
        out_specs=[ospec_f, ospec_b],
        scratch_shapes=[pltpu.VMEM((2, RET_HEADS, RET_DK, RET_DV), F32)],
    )
    return pl.pallas_call(
        _ret_kernel,
        grid_spec=grid_spec,
        out_shape=[jax.ShapeDtypeStruct((BATCH, TT, 2 * D), BF16)] * 2,
        compiler_params=_cp(2),
        name="retention",
    )(cdec, qkvg, qkvg, qkvg, qkvg, qkvg, qkvg, dmat, qdec, kdec)


def _mixer_out_kernel(*refs, kind):
    n_src = {"ret": 4, "split": 2, "plain": 1}[kind]
    srcs, (w_ref, x_ref, mod_ref, nf_ref), route_in = refs[:n_src], refs[n_src:n_src + 4], refs[n_src + 4:n_src + 7]
    xo_ref, h_ref, cr_ref, cnt_ref, wb_ref = refs[n_src + 7:]
    first = (pl.program_id(0) == 0) & (pl.program_id(1) == 0)

    @pl.when(first)
    def _():
        wb_ref[...] = w_ref[...].astype(BF16)

    if kind == "ret":
        of_ref, ob_ref, g_ref, gn_ref = srcs
        o = of_ref[...].astype(F32) + ob_ref[...].astype(F32)
        parts = []
        for h in range(RET_HEADS):
            oh = o[:, h * RET_DV:(h + 1) * RET_DV]
            mu = jnp.mean(oh, axis=-1, keepdims=True)
            ctr = oh - mu
            var = jnp.mean(ctr * ctr, axis=-1, keepdims=True)
            parts.append(ctr * lax.rsqrt(var + EPS))
        on = jnp.concatenate(parts, axis=-1) * gn_ref[...]
        a = (on * _silu(g_ref[...].astype(F32))).astype(BF16)
    elif kind == "split":
        a = jnp.where(pl.program_id(1) < LAT_TILES, srcs[0][...], srcs[1][...])
    else:
        a = srcs[0][...]
    y = jnp.dot(a, wb_ref[...], preferred_element_type=F32)
    xn = x_ref[...] + mod_ref[2:3, :] * y
    xo_ref[...] = xn
    h = _normmod(xn, nf_ref[...], mod_ref[3:4, :], mod_ref[4:5, :])
    wa, wb = _route(h, first, *route_in, cr_ref, cnt_ref)
    sub = lax.broadcasted_iota(jnp.int32, (DX - D, ROW_TILE), 0)
    h_ref[:, :D] = h
    h_ref[:, D:] = jnp.where(sub == 0, wa, jnp.where(sub == 1, wb, 0.0)).T


def _mixer_out(kind, srcs, w, layer, xs, mods, norm_ffn, depth, route):
    k = w.shape[1]
    row = lambda width, col=0: pl.BlockSpec((None, ROW_TILE, width), lambda b, t: (b, t, col))
    const = lambda shape: pl.BlockSpec(shape, lambda b, t: (0,) * len(shape))
    per_tok = pl.BlockSpec((2, ROW_TILE), lambda b, t: (0, b * TPB + t))
    tri = jnp.asarray(np.triu(np.ones((ROW_TILE, ROW_TILE), np.float32), 1), BF16)
    if kind == "ret":
        src_specs = [row(2 * D), row(2 * D), row(2 * D, 2), _layer_spec((1, 2 * D), layer)]
    elif kind == "split":
        src_specs = [pl.BlockSpec((None, ROW_TILE, k), lambda b, t: (b, jnp.minimum(t, LAT_TILES - 1), 0)),
                     pl.BlockSpec((None, CTX, k), lambda b, t: (b, 0, 0))]
    else:
        src_specs = [row(k)]
    return pl.pallas_call(
        functools.partial(_mixer_out_kernel, kind=kind),
        grid=(BATCH, TPB),
        in_specs=src_specs + [_layer_spec((k, D), layer), row(D), _mod_spec(depth), _layer_spec((1, D), depth),
                              const((N_EXPERTS, D)), const((N_EXPERTS, 1)), const((ROW_TILE, ROW_TILE))],
        out_specs=[row(D), row(DX), per_tok, const((CLS_PAD, 128))],
        out_shape=[jax.ShapeDtypeStruct((BATCH, TT, D), F32), jax.ShapeDtypeStruct((BATCH, TT, DX), F32),
                   jax.ShapeDtypeStruct((2, NTOK), jnp.int32), jax.ShapeDtypeStruct((CLS_PAD, 128), F32)],
        scratch_shapes=[pltpu.VMEM((k, D), BF16)],
        compiler_params=_cp(2),
        name="mixer_out_" + kind,
    )(*srcs, w, xs, mods, norm_ffn, *route, tri)


def _fn_channel_kernel(h_ref, cs_ref, o_ref):
    for g in range(FN_GROUPS):
        r = jnp.dot(h_ref[:, g * FN_GC:(g + 1) * FN_GC], cs_ref[...], preferred_element_type=F32)
        o_ref[0, :, g * FN_GC:(g + 1) * FN_GC] = r[:, :FN_GC].astype(BF16)
        o_ref[1, :, g * FN_GC:(g + 1) * FN_GC] = r[:, FN_GC:].astype(BF16)


def _fn_channel(h, cs):
    return pl.pallas_call(
        _fn_channel_kernel,
        grid=(BATCH, TPB),
        in_specs=[
            pl.BlockSpec((None, ROW_TILE, D), lambda b, t: (b, t, 0)),
            pl.BlockSpec((FN_GC, 2 * FN_GC), lambda b, t: (0, 0)),
        ],
        out_specs=pl.BlockSpec((None, 2, ROW_TILE, D), lambda b, t: (b, 0, t, 0)),
        out_shape=jax.ShapeDtypeStruct((BATCH, 2, TT, D), BF16),
        compiler_params=_cp(2),
        name="fourier_channels",
    )(h, cs)


FN_R = 64
FN_CB = 16


def _fn_rows_kernel(ab_ref, f1_ref, twc_ref, tws_ref, y_ref, u32):
    u32[...] = ab_ref[...].astype(F32)
    for j in range(FN_CB):
        st = jnp.concatenate([u32[0, :, j, :], u32[1, :, j, :]], axis=0).astype(BF16)
        y = jnp.dot(f1_ref[...], st, preferred_element_type=F32)
        yr, yi = y[:FN_R], y[FN_R:]
        cw, sw = twc_ref[j], tws_ref[j]
        y_ref[0, j] = (yr * cw - yi * sw).astype(BF16)
        y_ref[1, j] = (yi * cw + yr * sw).astype(BF16)


def _fn_rows(ab, f1, twc, tws):
    ab5 = ab.reshape(BATCH, 2, TT // FN_R, FN_R, D)
    return pl.pallas_call(
        _fn_rows_kernel,
        grid=(BATCH, FN_R // FN_CB),
        in_specs=[
            pl.BlockSpec((None, 2, FN_R, FN_CB, D), lambda b, cb: (b, 0, 0, cb, 0)),
            pl.BlockSpec((2 * FN_R, 2 * FN_R), lambda b, cb: (0, 0)),
            pl.BlockSpec((FN_CB, FN_R, 1), lambda b, cb: (cb, 0, 0)),
            pl.BlockSpec((FN_CB, FN_R, 1), lambda b, cb: (cb, 0, 0)),
        ],
        out_specs=pl.BlockSpec((None, 2, FN_CB, FN_R, D), lambda b, cb: (b, 0, cb, 0, 0)),
        out_shape=jax.ShapeDtypeStruct((BATCH, 2, FN_R, FN_R, D), BF16),
        scratch_shapes=[pltpu.VMEM((2, FN_R, FN_CB, D), F32)],
        compiler_params=_cp(2),
        name="fourier_rows",
    )(ab5, f1, twc, tws)


def _fn_cols_kernel(y_ref, f2_ref, o_ref, y32, z32):
    y32[...] = y_ref[...].astype(F32)
    for j in range(FN_CB):
        st = jnp.concatenate([y32[0, :, j, :], y32[1, :, j, :]], axis=0).astype(BF16)
        z32[:, j, :] = jnp.dot(f2_ref[...], st, preferred_element_type=F32)
    o_ref[...] = z32[...].astype(BF16)


def _fn_cols(y, f2):
    out = pl.pallas_call(
        _fn_cols_kernel,
        grid=(BATCH, FN_R // FN_CB),
        in_specs=[
            pl.BlockSpec((None, 2, FN_R, FN_CB, D), lambda b, kb: (b, 0, 0, kb, 0)),
            pl.BlockSpec((FN_R, 2 * FN_R), lambda b, kb: (0, 0)),
        ],
        out_specs=pl.BlockSpec((None, FN_R, FN_CB, D), lambda b, kb: (b, 0, kb, 0)),
        out_shape=jax.ShapeDtypeStruct((BATCH, FN_R, FN_R, D), BF16),
        scratch_shapes=[pltpu.VMEM((2, FN_R, FN_CB, D), F32), pltpu.VMEM((FN_R, FN_CB, D), F32)],
        compiler_params=_cp(2),
        name="fourier_cols",
    )(y, f2)
    return out.reshape(BATCH, SEQ, D)


def _fn_ctx_kernel(ab_ref, m_ref, o_ref):
    st = jnp.concatenate([ab_ref[0], ab_ref[1]], axis=0)
    o_ref[...] = jnp.dot(m_ref[...], st, preferred_element_type=F32).astype(BF16)


def _fn_ctx(ab, m_ctx):
    return pl.pallas_call(
        _fn_ctx_kernel,
        grid=(BATCH,),
        in_specs=[
            pl.BlockSpec((None, 2, CTX, D), lambda b: (b, 0, SEQ // CTX, 0)),
            pl.BlockSpec((CTX, 2 * CTX), lambda b: (0, 0)),
        ],
        out_specs=pl.BlockSpec((None, CTX, D), lambda b: (b, 0, 0)),
        out_shape=jax.ShapeDtypeStruct((BATCH, CTX, D), BF16),
        compiler_params=_cp(1),
        name="fourier_ctx",
    )(ab, m_ctx)


def _dft_tables():
    def cs_pair(n, scale):
        i = np.arange(n)
        ang = 2.0 * np.pi * ((i[:, None] * i[None, :]) % n) / n
        return np.cos(ang) * scale, np.sin(ang) * scale

    assert FN_GC == CTX
    c, s = cs_pair(FN_GC, FN_GC ** -0.5)
    cs = np.concatenate([c, s], axis=1)
    m_ctx = np.concatenate([c, -s], axis=1)
    c, s = cs_pair(FN_R, FN_R ** -0.5)
    f1 = np.block([[c, -s], [s, c]])
    f2 = np.concatenate([c, -s], axis=1)
    i = np.arange(FN_R)
    phi = 2.0 * np.pi * (i[:, None] * i[None, :]) / SEQ
    tw = [jnp.asarray(f(phi)[..., None], F32) for f in (np.cos, np.sin)]
    return jnp.asarray(cs, BF16), jnp.asarray(m_ctx, BF16), jnp.asarray(f1, BF16), jnp.asarray(f2, BF16), tw


NA_ROWS = SEQ // GRID_W
NA_STEPS = TT // NA_QT
NA_NK = NA_KH * GRID_W
NA_DR = 2 * NA_KH - 1


def _na_row_start(step):
    r = jnp.minimum(step, NA_ROWS - 1)
    return r, jnp.clip(r - NA_KH // 2, 0, NA_ROWS - NA_KH)


def _na_bias_idx(step):
    r, rs = _na_row_start(step)
    return rs - r + (NA_KH - 1)


def _na_kernel(q_ref, k_ref, v_ref, bias_ref, o_ref):
    step = pl.program_id(1)
    lane = lax.broadcasted_iota(jnp.int32, (NA_QT, 128), 1)
    nt = (((1,), (1,)), ((), ()))
    scale = NA_HD ** -0.5

    def attend(start, d0=None):
        for p in range(NA_HEADS // 2):
            sl = slice(p * 128, (p + 1) * 128)
            qp = q_ref[:, sl] * scale
            zero = jnp.zeros_like(qp)
            q2 = jnp.concatenate([jnp.where(lane < NA_HD, qp, zero), jnp.where(lane >= NA_HD, qp, zero)], axis=0)
            s_c = lax.dot_general(q2, k_ref[SEQ:TT, sl], nt, preferred_element_type=F32)
            m = jnp.max(s_c, axis=-1, keepdims=True)
            if start is not None:
                bias = jnp.concatenate(
                    [jnp.concatenate([bias_ref[2 * p + hh, d0 + 2 * j2] for j2 in range(NA_KH // 2)], axis=1)
                     for hh in range(2)], axis=0)
                s_w = lax.dot_general(q2, k_ref[pl.ds(start, NA_NK), sl], nt, preferred_element_type=F32) + bias
                m = jnp.maximum(m, jnp.max(s_w, axis=-1, keepdims=True))
            p_c = jnp.exp(s_c - m)
            l = jnp.sum(p_c, axis=-1, keepdims=True)
            acc = jnp.dot(p_c.astype(BF16), v_ref[SEQ:TT, sl], preferred_element_type=F32)
            if start is not None:
                p_w = jnp.exp(s_w - m)
                l = l + jnp.sum(p_w, axis=-1, keepdims=True)
                acc = acc + jnp.dot(p_w.astype(BF16), v_ref[pl.ds(start, NA_NK), sl], preferred_element_type=F32)
            o2 = acc / l
            o_ref[:, sl] = jnp.where(lane < NA_HD, o2[:NA_QT], o2[NA_QT:]).astype(BF16)

    @pl.when(step < NA_ROWS)
    def _():
        _, rs = _na_row_start(step)
        attend(pl.multiple_of(rs * GRID_W, GRID_W), _na_bias_idx(step))

    @pl.when(step >= NA_ROWS)
    def _():
        attend(None)


def _na_attention(qkv, bias):
    return pl.pallas_call(
        _na_kernel,
        grid=(BATCH, NA_STEPS),
        in_specs=[
            pl.BlockSpec((None, NA_QT, D), lambda b, s: (b, s, 0)),
            pl.BlockSpec((None, TT, D), lambda b, s: (b, 0, 1), pipeline_mode=pl.Buffered(1)),
            pl.BlockSpec((None, TT, D), lambda b, s: (b, 0, 2), pipeline_mode=pl.Buffered(1)),
            pl.BlockSpec((NA_HEADS, NA_DR - 1, GRID_W, 2 * GRID_W), lambda b, s: (0, 0, 0, 0),
                         pipeline_mode=pl.Buffered(1)),
        ],
        out_specs=pl.BlockSpec((None, NA_QT, D), lambda b, s: (b, s, 0)),
        out_shape=jax.ShapeDtypeStruct((BATCH, TT, D), BF16),
        compiler_params=_cp(2),
        name="na_attention",
    )(qkv, qkv, qkv, bias)


def _na_bias_table(rpb):
    qc = np.arange(GRID_W)
    kcol = np.arange(GRID_W)
    cstart = np.clip(qc - NA_KW // 2, 0, GRID_W - NA_KW)
    valid = (kcol[None, :] >= cstart[:, None]) & (kcol[None, :] < cstart[:, None] + NA_KW)
    dc = kcol[None, :] - qc[:, None] + (NA_KW - 1)
    pick = ((dc[None] == np.arange(2 * NA_KW - 1)[:, None, None]) & valid[None]).astype(np.float32)
    tab = jnp.einsum("hdj,jqk->hdqk", rpb.astype(F32), jnp.asarray(pick), precision=HIGHEST)
    tab = tab + jnp.asarray(np.where(valid, 0.0, NEG_INF).astype(np.float32))
    return jnp.concatenate([tab[:, :NA_DR - 1], tab[:, 1:]], axis=-1)


def _route(h, first, rw_ref, rb_ref, tri_ref, cr_ref, cnt_ref):
    logits = lax.dot_general(rw_ref[...], h.astype(BF16), (((1,), (1,)), ((), ())),
                             preferred_element_type=F32)
    sc = jax.nn.sigmoid(logits)
    sel = sc + rb_ref[...]
    sc_r = [sc[e:e + 1, :] for e in range(N_EXPERTS)]
    sel_r = [sel[e:e + 1, :] for e in range(N_EXPERTS)]
    zero_i = jnp.zeros_like(sel_r[0]).astype(jnp.int32)

    best = gi = None
    picks = []
    for g in range(N_GROUPS):
        a = sel_r[g * EPG:(g + 1) * EPG]
        u = sc_r[g * EPG:(g + 1) * EPG]
        gs = None
        for i in range(EPG):
            for j in range(i + 1, EPG):
                pair = a[i] + a[j]
                gs = pair if gs is None else jnp.maximum(gs, pair)
        m1, i1, s1 = a[0], zero_i, u[0]
        for j in range(1, EPG):
            upd = a[j] > m1
            m1 = jnp.where(upd, a[j], m1)
            i1 = jnp.where(upd, j, i1)
            s1 = jnp.where(upd, u[j], s1)
        m2 = i2 = s2 = None
        for j in range(EPG):
            cand = jnp.where(i1 == j, -jnp.inf, a[j])
            if m2 is None:
                m2, i2, s2 = cand, zero_i, u[0]
            else:
                upd = cand > m2
                m2 = jnp.where(upd, cand, m2)
                i2 = jnp.where(upd, j, i2)
                s2 = jnp.where(upd, u[j], s2)
        picks.append((i1 + g * EPG, i2 + g * EPG, s1, s2))
        if best is None:
            best, gi = gs, zero_i
        else:
            upd = gs > best
            best = jnp.where(upd, gs, best)
            gi = jnp.where(upd, g, gi)

    e1, e2, s1, s2 = picks[0]
    for g in range(1, N_GROUPS):
        on = gi == g
        e1 = jnp.where(on, picks[g][0], e1)
        e2 = jnp.where(on, picks[g][1], e2)
        s1 = jnp.where(on, picks[g][2], s1)
        s2 = jnp.where(on, picks[g][3], s2)
    tot = s1 + s2
    w1, w2 = s1 / tot, s2 / tot

    swap = e2 < e1
    la = jnp.where(swap, e2, e1) - gi * EPG
    lb = jnp.where(swap, e1, e2) - gi * EPG
    cls = gi * PAIRS + jnp.where(la == 0, 0, jnp.where(la == 1, 3, 5)) + lb - la - 1
    wa, wb = jnp.where(swap, w2, w1), jnp.where(swap, w1, w2)

    @pl.when(first)
    def _():
        cnt_ref[...] = jnp.zeros_like(cnt_ref)

    oh = (lax.broadcasted_iota(jnp.int32, (cnt_ref.shape[0], cls.shape[1]), 0) == cls).astype(F32)
    before = jnp.dot(oh.astype(BF16), tri_ref[...], preferred_element_type=F32)
    base = cnt_ref[:, 0:1]
    cr_ref[0:1, :] = cls
    cr_ref[1:2, :] = jnp.sum(oh * (base + before), axis=0, keepdims=True).astype(jnp.int32)
    cnt_ref[...] = cnt_ref[...] + jnp.sum(oh, axis=1, keepdims=True)
    return wa, wb


DMA_UNROLL = 8


def _row_copy(src_ref, src_row, dst_ref, dst_row, sem):
    return pltpu.make_async_copy(src_ref.at[pl.ds(src_row, 1)], dst_ref.at[pl.ds(dst_row, 1)], sem)


def _for_each_row(fn):
    def group(gidx, carry):
        for u in range(DMA_UNROLL):
            fn(gidx * DMA_UNROLL + u)
        return carry

    lax.fori_loop(0, ROW_TILE // DMA_UNROLL, group, 0)


def _dispatch_kernel(zrow_ref, dest_ref, h_ref, xs_ref, zbuf, sem, zsem):
    @pl.when(pl.program_id(0) == 0)
    def _():
        zbuf[...] = jnp.zeros_like(zbuf)

        def zero_copy(e):
            start = pl.multiple_of(jnp.maximum(zrow_ref[e], 0), MOE_BM)
            return pltpu.make_async_copy(zbuf, xs_ref.at[pl.ds(start, MOE_BM)], zsem)

        for e in range(MOE_NZ):
            pl.when(zrow_ref[e] >= 0)(lambda e=e: zero_copy(e).start())
        for e in range(MOE_NZ):
            pl.when(zrow_ref[e] >= 0)(lambda e=e: zero_copy(e).wait())

    _for_each_row(lambda r: _row_copy(h_ref, r, xs_ref, dest_ref[0, r], sem).start())
    _for_each_row(lambda r: _row_copy(h_ref, r, xs_ref, dest_ref[0, r], sem).wait())


def _dispatch(h, dest, zrow):
    grid_spec = pltpu.PrefetchScalarGridSpec(
        num_scalar_prefetch=1,
        grid=(NTOK // ROW_TILE,),
        in_specs=[
            pl.BlockSpec((None, 1, ROW_TILE), lambda i, z: (i, 0, 0), memory_space=pltpu.SMEM),
            pl.BlockSpec((ROW_TILE, DX), lambda i, z: (i, 0)),
        ],
        out_specs=pl.BlockSpec(memory_space=pl.ANY),
        scratch_shapes=[pltpu.VMEM((MOE_BM, DX), F32), pltpu.SemaphoreType.DMA(()), pltpu.SemaphoreType.DMA(())],
    )
    return pl.pallas_call(
        _dispatch_kernel,
        grid_spec=grid_spec,
        out_shape=jax.ShapeDtypeStruct((MOE_P, DX), F32),
        compiler_params=pltpu.CompilerParams(dimension_semantics=("arbitrary",), vmem_limit_bytes=VMEM_LIMIT,
                                             has_side_effects=True),
        name="moe_dispatch",
    )(zrow, dest, h)


def _experts_kernel(blk_ref, ex_ref, nu_ref, x_ref, *rest, which):
    if which:
        prev_ref, wg_ref, wu_ref, wd_ref, y_ref, wgb, wub, wdb = rest
    else:
        wg_ref, wu_ref, wd_ref, y_ref, wgb, wub, wdb = rest
    del blk_ref
    v = pl.program_id(0)
    active = v < nu_ref[0]
    changed = (v == 0) | (ex_ref[v] != ex_ref[jnp.maximum(v - 1, 0)])

    @pl.when(active & changed)
    def _():
        wgb[...] = wg_ref[...].astype(BF16)
        wub[...] = wu_ref[...].astype(BF16)
        wdb[...] = wd_ref[...].astype(BF16)

    @pl.when(active)
    def _():
        x = x_ref[:, :D].astype(BF16)
        g = jnp.dot(x, wgb[...], preferred_element_type=F32)
        u = jnp.dot(x, wub[...], preferred_element_type=F32)
        hh = (_silu(g) * u).astype(BF16)
        y = jnp.dot(hh, wdb[...], preferred_element_type=F32) * x_ref[:, D + which:D + which + 1]
        y_ref[...] = prev_ref[...] + y if which else y

    @pl.when(jnp.logical_not(active))
    def _():
        y_ref[...] = jnp.zeros_like(y_ref)


def _experts(xs, prev, which, blocks, experts, n_used, w_gate, w_up, w_down, depth):
    wspec = pl.BlockSpec((None, None, D, D), lambda v, bl, ex, nu: (depth, ex[v], 0, 0))
    yspec = pl.BlockSpec((MOE_BM, D), lambda v, bl, ex, nu: (bl[v], 0))
    grid_spec = pltpu.PrefetchScalarGridSpec(
        num_scalar_prefetch=3,
        grid=(MOE_NB,),
        in_specs=[pl.BlockSpec((MOE_BM, DX), lambda v, bl, ex, nu: (bl[v], 0))] + ([yspec] if which else [])
        + [wspec, wspec, wspec],
        out_specs=yspec,
        scratch_shapes=[pltpu.VMEM((D, D), BF16)] * 3,
    )
    return pl.pallas_call(
        functools.partial(_experts_kernel, which=which),
        grid_spec=grid_spec,
        out_shape=jax.ShapeDtypeStruct((MOE_P, D), F32),
        compiler_params=_cp(1),
        name="moe_experts_%d" % which,
    )(blocks, experts, n_used, xs, *([prev] if which else []), w_gate, w_up, w_down)


def _combine_kernel(dest_ref, yb_ref, x_ref, mod_ref, modn_ref, gn_ref, *rest, final):
    if final:
        out_ref, ybuf, sem = rest
    else:
        xo_ref, h_ref, ybuf, sem = rest
    t = pl.program_id(1)

    def body():
        _for_each_row(lambda r: _row_copy(yb_ref, dest_ref[0, r], ybuf, r, sem).start())
        _for_each_row(lambda r: _row_copy(yb_ref, dest_ref[0, r], ybuf, r, sem).wait())
        xn = x_ref[...] + mod_ref[5:6, :] * ybuf[...]
        if final:
            out_ref[...] = _normmod(xn, gn_ref[...], 0.0, 0.0)
        else:
            xo_ref[...] = xn
            h_ref[...] = _normmod(xn, gn_ref[...], modn_ref[0:1, :], modn_ref[1:2, :]).astype(BF16)

    if final:
        pl.when(t < LAT_TILES)(body)
    else:
        body()


def _combine(yb, dest, xs, mods, depth, g_next, final):
    row = pl.BlockSpec((None, ROW_TILE, D), lambda b, t: (b, t, 0))
    next_depth = depth if final else depth + 1
    if final:
        out_specs = pl.BlockSpec((None, ROW_TILE, D), lambda b, t: (b, jnp.minimum(t, LAT_TILES - 1), 0))
        out_shape = jax.ShapeDtypeStruct((BATCH, SEQ, D), F32)
    else:
        out_specs = [row, row]
        out_shape = [jax.ShapeDtypeStruct((BATCH, TT, D), F32), jax.ShapeDtypeStruct((BATCH, TT, D), BF16)]
    return pl.pallas_call(
        functools.partial(_combine_kernel, final=final),
        grid=(BATCH, TPB),
        in_specs=[
            pl.BlockSpec((None, 1, ROW_TILE), lambda b, t: (b * TPB + t, 0, 0), memory_space=pltpu.SMEM),
            pl.BlockSpec(memory_space=pl.ANY),
            row, _mod_spec(depth), _mod_spec(next_depth),
            g_next[1],
        ],
        out_specs=out_specs,
        out_shape=out_shape,
        scratch_shapes=[pltpu.VMEM((ROW_TILE, D), F32), pltpu.SemaphoreType.DMA(())],
        compiler_params=_cp(2),
        name="moe_combine",
    )(dest, yb, xs, mods, mods, g_next[0])


def _class_experts():
    pairs = [(a, b) for a in range(EPG) for b in range(a + 1, EPG)]
    lo = [g * EPG + a for g in range(N_GROUPS) for a, _ in pairs]
    hi = [g * EPG + b for g in range(N_GROUPS) for _, b in pairs]
    return jnp.asarray(lo, jnp.int32), jnp.asarray(hi, jnp.int32)


def _moe_plan(cls, rank, counts):
    padded = (counts + MOE_BM - 1) // MOE_BM * MOE_BM
    pend = jnp.cumsum(padded)
    pstart = pend - padded
    cid = jnp.arange(N_CLS, dtype=jnp.int32)
    dest = rank + jnp.sum(jnp.where(cls[None, :] == cid[:, None], pstart[:, None], 0), axis=0)
    dest = dest.astype(jnp.int32).reshape(NTOK // ROW_TILE, 1, ROW_TILE)
    blk = jnp.arange(MOE_NB, dtype=jnp.int32)
    block_c = jnp.minimum(jnp.sum(pend[None, :] <= (blk * MOE_BM)[:, None], axis=1), N_CLS - 1)
    n_used = (pend[-1] // MOE_BM).astype(jnp.int32)
    used = blk < n_used
    lo, hi = _class_experts()
    onehot_c = block_c[:, None] == cid[None, :]
    e_lo = jnp.sum(jnp.where(onehot_c, lo[None, :], 0), axis=1)
    e_hi = jnp.sum(jnp.where(onehot_c, hi[None, :], 0), axis=1)

    def hold_last(e):
        return jnp.where(used, e, jnp.sum(jnp.where(blk == n_used - 1, e, 0))).astype(jnp.int32)

    key = jnp.where(used, e_hi, N_EXPERTS)
    earlier = (key[None, :] < key[:, None]) | ((key[None, :] == key[:, None]) & (blk[None, :] < blk[:, None]))
    pos = jnp.sum(earlier, axis=1)
    at = pos[None, :] == blk[:, None]
    order1 = jnp.sum(jnp.where(at, blk[None, :], 0), axis=1).astype(jnp.int32)
    e1 = jnp.sum(jnp.where(at, e_hi[None, :], 0), axis=1)
    tail = blk[MOE_NB - MOE_TAIL:] * MOE_BM
    zrow = jnp.concatenate([jnp.where(padded > 0, pend - MOE_BM, -1), jnp.where(tail >= pend[-1], tail, -1)])
    return dest, (blk, hold_last(e_lo)), (order1, hold_last(e1)), n_used[None], zrow.astype(jnp.int32)


def _moe(h, routing, xs, mods, depth, g_next, w_gate, w_up, w_down, final):
    cr, cnt = routing
    dest, visit0, visit1, n_used, zrow = _moe_plan(cr[0], cr[1], cnt[:N_CLS, 0].astype(jnp.int32))
    xsorted = _dispatch(h.reshape(NTOK, DX), dest, zrow)
    y0 = _experts(xsorted, None, 0, *visit0, n_used, w_gate, w_up, w_down, depth)
    yb = _experts(xsorted, y0, 1, *visit1, n_used, w_gate, w_up, w_down, depth)
    return _combine(yb, dest, xs, mods, depth, g_next, final)


def _rope_tables():
    nf = RET_DK // 4
    t = np.arange(SEQ)
    inv = ROPE_BASE ** (-np.arange(nf, dtype=np.float32) / nf)
    cos = np.ones((TT, 256), np.float32)
    sin = np.zeros((TT, 256), np.float32)
    for seg, pos in enumerate(((t // GRID_W).astype(np.float32), (t % GRID_W).astype(np.float32))):
        ang = (pos[:, None] * inv[None, :]).astype(np.float32)
        c, s = np.cos(ang), np.sin(ang)
        cos[:SEQ, seg * 128:(seg + 1) * 128] = np.concatenate([c, c], axis=1)
        sin[:SEQ, seg * 128:(seg + 1) * 128] = np.concatenate([-s, s], axis=1)
    ks = np.float32(RET_DK ** -0.5)
    return jnp.asarray(np.stack([cos, cos * ks])), jnp.asarray(np.stack([sin, sin * ks]))


def _decay_tables(decay_logit):
    C = RET_CHUNK
    log_g = jax.nn.log_sigmoid(decay_logit.astype(F32))
    pos = jnp.arange(C, dtype=F32)
    diff = pos[:, None] - pos[None, :]
    lg = log_g[:, :, None, None]
    dm_f = jnp.where(diff >= 0, jnp.exp(jnp.maximum(diff, 0.0)[None, None] * lg), 0.0)
    dm_b = jnp.where(diff <= 0, jnp.exp(jnp.maximum(-diff, 0.0)[None, None] * lg), 0.0)
    dmat = jnp.stack([dm_f[0], dm_b[1]])
    lgc = log_g[:, :, None]
    qdec = jnp.stack([jnp.exp((pos + 1.0)[None, :] * lgc[0]), jnp.exp((C - pos)[None, :] * lgc[1])])
    kdec = jnp.stack([jnp.exp((C - 1.0 - pos)[None, :] * lgc[0]), jnp.exp(pos[None, :] * lgc[1])])
    cdec = jnp.exp(C * log_g).reshape(-1)
    return dmat, qdec[..., None], kdec[..., None], cdec


def kernel(x, c, ctx, c_ctx, w_mod, b_mod, norm_mix, norm_ffn, norm_final, ret_w_in, ret_decay, ret_gn, ret_w_out,
           fn_w_out, na_w_qkv, na_rpb, na_w_out, router_w, router_bias, moe_w_gate, moe_w_up, moe_w_down):
    cvec = jnp.concatenate([c_ctx[None, :], c, jnp.zeros((8 - 1 - BATCH, D), F32)], axis=0)
    mod_all = _modulation(cvec, w_mod, b_mod).reshape(DEPTH, 8, N_MOD, D)
    pad = jnp.zeros((DEPTH, BATCH, 2, 8 - N_MOD, D), F32)
    mods = jnp.stack([jnp.broadcast_to(mod_all[:, 0:1], (DEPTH, BATCH, N_MOD, D)), mod_all[:, 1:1 + BATCH]], axis=2)
    mods = jnp.concatenate([mods, pad], axis=3)

    rope = _rope_tables()
    cs, dft_ctx, dft_f1, dft_f2, dft_tw = _dft_tables()
    route = (router_w.T.astype(BF16), router_bias.reshape(N_EXPERTS, 1).astype(F32))

    norm_mix3 = norm_mix.reshape(DEPTH, 1, D)
    norm_ffn3 = norm_ffn.reshape(DEPTH, 1, D)
    ret_gn3 = ret_gn.reshape(-1, 1, 2 * D)

    xs, h = _embed(x, ctx, mods, norm_mix3)
    out = None
    for i in range(DEPTH):
        kind, j = i % 3, i // 3
        if kind == 0:
            qkvg = _project(h.reshape(NTOK, D), ret_w_in, j, rope).reshape(BATCH, TT, 6 * D)
            dmat, qdec, kdec, cdec = _decay_tables(ret_decay[j])
            o_f, o_b = _retention(qkvg, dmat, qdec, kdec, cdec)
            xs, hf, *routing = _mixer_out("ret", (o_f, o_b, qkvg, ret_gn3), ret_w_out, j, xs, mods, norm_ffn3, i, route)
        elif kind == 1:
            ab = _fn_channel(h, cs)
            f_lat = _fn_cols(_fn_rows(ab, dft_f1, *dft_tw), dft_f2)
            xs, hf, *routing = _mixer_out(
                "split", (f_lat, _fn_ctx(ab, dft_ctx)), fn_w_out, j, xs, mods, norm_ffn3, i, route)
        else:
            qkv = _project(h.reshape(NTOK, D), na_w_qkv, j).reshape(BATCH, TT, 3 * D)
            o = _na_attention(qkv, _na_bias_table(na_rpb[j]))
            xs, hf, *routing = _mixer_out("plain", (o,), na_w_out, j, xs, mods, norm_ffn3, i, route)
        final = i == DEPTH - 1
        if final:
            g_next = (norm_final[None, :], pl.BlockSpec((1, D), lambda b, t: (0, 0)))
        else:
            g_next = (norm_mix3, _layer_spec((1, D), i + 1))
        res = _moe(hf, routing, xs, mods, i, g_next, moe_w_gate, moe_w_up, moe_w_down, final)
        if final:
            out = res
        else:
            xs, h = res
    return out
```

```python
import functools

import numpy as np
import jax
import jax.numpy as jnp
from jax import lax
from jax.experimental import pallas as pl
from jax.experimental.pallas import tpu as pltpu

F32 = jnp.float32
BF16 = jnp.bfloat16
HIGHEST = lax.Precision.HIGHEST

D = 1024
BATCH = 4
SEQ = 4096
CTX = 256
TT = SEQ + CTX
NTOK = BATCH * TT
DEPTH = 4
GRID_W = 64
EPS = 1e-6
NEG_INF = -1e30
N_MOD = 6

RET_HEADS = 4
RET_DK = D // RET_HEADS
RET_DV = 2 * RET_DK
RET_CHUNK = 256
ROPE_BASE = 10000.0

FN_GROUPS = 4
FN_GC = D // FN_GROUPS

NA_HEADS = 16
NA_HD = D // NA_HEADS
NA_KH = 8
NA_KW = 16

N_EXPERTS = 16
N_GROUPS = 4
EPG = N_EXPERTS // N_GROUPS

ROW_TILE = 256
TPB = TT // ROW_TILE
LAT_TILES = SEQ // ROW_TILE
MM_TM = TT // 4
MOE_BM = 256
PAIRS = EPG * (EPG - 1) // 2
N_CLS = N_GROUPS * PAIRS
CLS_PAD = 32
MOE_NB = (NTOK + N_CLS * (MOE_BM - 1) + MOE_BM - 1) // MOE_BM
MOE_P = MOE_NB * MOE_BM
MOE_TAIL = MOE_NB - NTOK // MOE_BM
MOE_NZ = N_CLS + MOE_TAIL
DX = D + 128
NA_QT = GRID_W
VMEM_LIMIT = 56 * 1024 * 1024


def _cp(n_axes, vmem=VMEM_LIMIT):
    return pltpu.CompilerParams(dimension_semantics=("arbitrary",) * n_axes, vmem_limit_bytes=vmem)


def _silu(x):
    return x * jax.nn.sigmoid(x)


def _normmod(x, g, shift, scale):
    y = x * lax.rsqrt(jnp.mean(x * x, axis=-1, keepdims=True) + EPS)
    return (y * g) * (1.0 + scale) + shift


def _mod_idx(t):
    return jnp.where(t < LAT_TILES, 1, 0)


def _layer_spec(shape, layer):
    return pl.BlockSpec((None,) + shape, lambda *_: (layer,) + (0,) * len(shape))


def _mod_spec(depth):
    return pl.BlockSpec((None, None, None, 8, D), lambda b, t: (depth, b, _mod_idx(t), 0, 0))


def _mod_kernel(c_ref, w_ref, b_ref, o_ref):
    a = _silu(c_ref[...])
    o_ref[...] = jnp.dot(a, w_ref[...], precision=HIGHEST, preferred_element_type=F32) + b_ref[...]


def _modulation(cvec, w_mod, b_mod):
    return pl.pallas_call(
        _mod_kernel,
        grid=(DEPTH, N_MOD),
        in_specs=[
            pl.BlockSpec((8, D), lambda i, j: (0, 0)),
            pl.BlockSpec((None, D, D), lambda i, j: (i, 0, j)),
            pl.BlockSpec((None, 1, D), lambda i, j: (i, 0, j)),
        ],
        out_specs=pl.BlockSpec((None, 8, D), lambda i, j: (i, 0, j)),
        out_shape=jax.ShapeDtypeStruct((DEPTH, 8, N_MOD * D), F32),
        compiler_params=_cp(2),
        name="modulation",
    )(cvec, w_mod, b_mod.reshape(DEPTH, 1, N_MOD * D))


def _embed_kernel(x_ref, ctx_ref, mod_ref, g_ref, xs_ref, h_ref):
    t = pl.program_id(1)

    def emit(src):
        xs_ref[...] = src
        h_ref[...] = _normmod(src, g_ref[...], mod_ref[0:1, :], mod_ref[1:2, :]).astype(BF16)

    @pl.when(t < LAT_TILES)
    def _():
        emit(x_ref[...])

    @pl.when(t >= LAT_TILES)
    def _():
        emit(ctx_ref[...])


def _embed(x, ctx, mods, g):
    return pl.pallas_call(
        _embed_kernel,
        grid=(BATCH, TPB),
        in_specs=[
            pl.BlockSpec((None, ROW_TILE, D), lambda b, t: (b, jnp.minimum(t, LAT_TILES - 1), 0)),
            pl.BlockSpec((None, CTX, D), lambda b, t: (b, 0, 0)),
            _mod_spec(0),
            _layer_spec((1, D), 0),
        ],
        out_specs=[
            pl.BlockSpec((None, ROW_TILE, D), lambda b, t: (b, t, 0)),
            pl.BlockSpec((None, ROW_TILE, D), lambda b, t: (b, t, 0)),
        ],
        out_shape=[
            jax.ShapeDtypeStruct((BATCH, TT, D), F32),
            jax.ShapeDtypeStruct((BATCH, TT, D), BF16),
        ],
        compiler_params=_cp(2),
        name="embed",
    )(x, ctx, mods, g)


def _rope_store(acc, cos_ref, sin_ref, o_ref):
    for c in range(acc.shape[1] // 128):
        xc = acc[:, c * 128:(c + 1) * 128]
        tsl = slice((c % 2) * 128, (c % 2 + 1) * 128)
        rot = xc * cos_ref[:, tsl] + pltpu.roll(xc, 64, 1) * sin_ref[:, tsl]
        o_ref[:, c * 128:(c + 1) * 128] = rot.astype(o_ref.dtype)


def _proj_kernel(a_ref, w_ref, *rest, n_rope):
    if n_rope:
        cos_ref, sin_ref, o_ref, wb_ref = rest
    else:
        o_ref, wb_ref = rest
    j = pl.program_id(0)

    @pl.when(pl.program_id(1) == 0)
    def _():
        wb_ref[...] = w_ref[...].astype(BF16)

    acc = jnp.dot(a_ref[...], wb_ref[...], preferred_element_type=F32)
    if n_rope:
        @pl.when(j < n_rope)
        def _():
            _rope_store(acc, cos_ref, sin_ref, o_ref)

        @pl.when(j >= n_rope)
        def _():
            o_ref[...] = acc.astype(o_ref.dtype)
    else:
        o_ref[...] = acc.astype(o_ref.dtype)


def _project(a, w, layer, rope=None):
    _, k, n = w.shape
    tn = D
    n_rope = 0 if rope is None else rope[0].shape[0]
    in_specs = [
        pl.BlockSpec((MM_TM, k), lambda j, i: (i, 0)),
        pl.BlockSpec((None, k, tn), lambda j, i: (layer, 0, j)),
    ]
    args = [a, w]
    if n_rope:
        tspec = pl.BlockSpec((None, MM_TM, 256), lambda j, i: (jnp.minimum(j, n_rope - 1), i % (TT // MM_TM), 0))
        in_specs += [tspec, tspec]
        args += list(rope)
    return pl.pallas_call(
        functools.partial(_proj_kernel, n_rope=n_rope),
        grid=(n // tn, NTOK // MM_TM),
        in_specs=in_specs,
        out_specs=pl.BlockSpec((MM_TM, tn), lambda j, i: (i, j)),
        out_shape=jax.ShapeDtypeStruct((NTOK, n), BF16),
        scratch_shapes=[pltpu.VMEM((k, tn), BF16)],
        compiler_params=_cp(2),
        name="project",
    )(*args)


N_CHUNKS = TT // RET_CHUNK
LAT_CHUNKS = SEQ // RET_CHUNK
CTX_CHUNKS = CTX // RET_CHUNK


def _fwd_chunk(n):
    return jnp.where(n < CTX_CHUNKS, LAT_CHUNKS + n, n - CTX_CHUNKS)


def _bwd_chunk(n):
    return N_CHUNKS - 1 - n


def _ret_kernel(cdec_ref, qf, kf, vf, qb, kb, vb, dmat_ref, qdec_ref, kdec_ref, of_ref, ob_ref, s_ref):
    @pl.when(pl.program_id(1) == 0)
    def _():
        s_ref[...] = jnp.zeros_like(s_ref)

    for d, (q_r, k_r, v_r, o_r) in enumerate(((qf, kf, vf, of_ref), (qb, kb, vb, ob_ref))):
        for h in range(RET_HEADS):
            q = q_r[:, h * RET_DK:(h + 1) * RET_DK]
            k = k_r[:, h * RET_DK:(h + 1) * RET_DK]
            v = v_r[:, h * RET_DV:(h + 1) * RET_DV]
            att = lax.dot_general(q, k, (((1,), (1,)), ((), ())), preferred_element_type=F32) * dmat_ref[d, h]
            s = s_ref[d, h]
            qd = (q.astype(F32) * qdec_ref[d, h]).astype(BF16)
            o = (jnp.dot(att.astype(BF16), v, preferred_element_type=F32)
                 + jnp.dot(qd, s.astype(BF16), preferred_element_type=F32))
            kd = (k.astype(F32) * kdec_ref[d, h]).astype(BF16)
            s_ref[d, h] = s * cdec_ref[d * RET_HEADS + h] + lax.dot_general(
                kd, v, (((0,), (0,)), ((), ())), preferred_element_type=F32)
            o_r[:, h * RET_DV:(h + 1) * RET_DV] = o.astype(o_r.dtype)


def _retention(qkvg, dmat, qdec, kdec, cdec):
    C = RET_CHUNK

    def spec(width, col, chunk_fn):
        return pl.BlockSpec((None, C, width), lambda b, n, cd: (b, chunk_fn(n), col))

    ospec_f = pl.BlockSpec((None, C, 2 * D), lambda b, n, cd: (b, _fwd_chunk(n), 0))
    ospec_b = pl.BlockSpec((None, C, 2 * D), lambda b, n, cd: (b, _bwd_chunk(n), 0))
    full = lambda shape: pl.BlockSpec(shape, lambda b, n, cd: (0,) * len(shape))
    grid_spec = pltpu.PrefetchScalarGridSpec(
        num_scalar_prefetch=1,
        grid=(BATCH, N_CHUNKS),
        in_specs=[
            spec(D, 0, _fwd_chunk), spec(D, 1, _fwd_chunk), spec(2 * D, 1, _fwd_chunk),
            spec(D, 0, _bwd_chunk), spec(D, 1, _bwd_chunk), spec(2 * D, 1, _bwd_chunk),
            full((2, RET_HEADS, C, C)), full((2, RET_HEADS, C, 1)), full((2, RET_HEADS, C, 1)),
        ],
        out_specs=[ospec_f, ospec_b],
        scratch_shapes=[pltpu.VMEM((2, RET_HEADS, RET_DK, RET_DV), F32)],
    )
    return pl.pallas_call(
        _ret_kernel,
        grid_spec=grid_spec,
        out_shape=[jax.ShapeDtypeStruct((BATCH, TT, 2 * D), BF16)] * 2,
        compiler_params=_cp(2),
        name="retention",
    )(cdec, qkvg, qkvg, qkvg, qkvg, qkvg, qkvg, dmat, qdec, kdec)


def _mixer_out_kernel(*refs, kind):
    n_src = {"ret": 4, "split": 2, "plain": 1}[kind]
    srcs, (w_ref, x_ref, mod_ref, nf_ref), route_in = refs[:n_src], refs[n_src:n_src + 4], refs[n_src + 4:n_src + 7]
    xo_ref, h_ref, cr_ref, cnt_ref, wb_ref = refs[n_src + 7:]
    first = (pl.program_id(0) == 0) & (pl.program_id(1) == 0)

    @pl.when(first)
    def _():
        wb_ref[...] = w_ref[...].astype(BF16)

    if kind == "ret":
        of_ref, ob_ref, g_ref, gn_ref = srcs
        o = of_ref[...].astype(F32) + ob_ref[...].astype(F32)
        parts = []
        for h in range(RET_HEADS):
            oh = o[:, h * RET_DV:(h + 1) * RET_DV]
            mu = jnp.mean(oh, axis=-1, keepdims=True)
            ctr = oh - mu
            var = jnp.mean(ctr * ctr, axis=-1, keepdims=True)
            parts.append(ctr * lax.rsqrt(var + EPS))
        on = jnp.concatenate(parts, axis=-1) * gn_ref[...]
        a = (on * _silu(g_ref[...].astype(F32))).astype(BF16)
    elif kind == "split":
        a = jnp.where(pl.program_id(1) < LAT_TILES, srcs[0][...], srcs[1][...])
    else:
        a = srcs[0][...]
    y = jnp.dot(a, wb_ref[...], preferred_element_type=F32)
    xn = x_ref[...] + mod_ref[2:3, :] * y
    xo_ref[...] = xn
    h = _normmod(xn, nf_ref[...], mod_ref[3:4, :], mod_ref[4:5, :])
    wa, wb = _route(h, first, *route_in, cr_ref, cnt_ref)
    sub = lax.broadcasted_iota(jnp.int32, (DX - D, ROW_TILE), 0)
    h_ref[:, :D] = h
    h_ref[:, D:] = jnp.where(sub == 0, wa, jnp.where(sub == 1, wb, 0.0)).T


def _mixer_out(kind, srcs, w, layer, xs, mods, norm_ffn, depth, route):
    k = w.shape[1]
    row = lambda width, col=0: pl.BlockSpec((None, ROW_TILE, width), lambda b, t: (b, t, col))
    const = lambda shape: pl.BlockSpec(shape, lambda b, t: (0,) * len(shape))
    per_tok = pl.BlockSpec((2, ROW_TILE), lambda b, t: (0, b * TPB + t))
    tri = jnp.asarray(np.triu(np.ones((ROW_TILE, ROW_TILE), np.float32), 1), BF16)
    if kind == "ret":
        src_specs = [row(2 * D), row(2 * D), row(2 * D, 2), _layer_spec((1, 2 * D), layer)]
    elif kind == "split":
        src_specs = [pl.BlockSpec((None, ROW_TILE, k), lambda b, t: (b, jnp.minimum(t, LAT_TILES - 1), 0)),
                     pl.BlockSpec((None, CTX, k), lambda b, t: (b, 0, 0))]
    else:
        src_specs = [row(k)]
    return pl.pallas_call(
        functools.partial(_mixer_out_kernel, kind=kind),
        grid=(BATCH, TPB),
        in_specs=src_specs + [_layer_spec((k, D), layer), row(D), _mod_spec(depth), _layer_spec((1, D), depth),
                              const((N_EXPERTS, D)), const((N_EXPERTS, 1)), const((ROW_TILE, ROW_TILE))],
        out_specs=[row(D), row(DX), per_tok, const((CLS_PAD, 128))],
        out_shape=[jax.ShapeDtypeStruct((BATCH, TT, D), F32), jax.ShapeDtypeStruct((BATCH, TT, DX), F32),
                   jax.ShapeDtypeStruct((2, NTOK), jnp.int32), jax.ShapeDtypeStruct((CLS_PAD, 128), F32)],
        scratch_shapes=[pltpu.VMEM((k, D), BF16)],
        compiler_params=_cp(2),
        name="mixer_out_" + kind,
    )(*srcs, w, xs, mods, norm_ffn, *route, tri)


def _fn_channel_kernel(h_ref, cs_ref, o_ref):
    for g in range(FN_GROUPS):
        r = jnp.dot(h_ref[:, g * FN_GC:(g + 1) * FN_GC], cs_ref[...], preferred_element_type=F32)
        o_ref[0, :, g * FN_GC:(g + 1) * FN_GC] = r[:, :FN_GC].astype(BF16)
        o_ref[1, :, g * FN_GC:(g + 1) * FN_GC] = r[:, FN_GC:].astype(BF16)


def _fn_channel(h, cs):
    return pl.pallas_call(
        _fn_channel_kernel,
        grid=(BATCH, TPB),
        in_specs=[
            pl.BlockSpec((None, ROW_TILE, D), lambda b, t: (b, t, 0)),
            pl.BlockSpec((FN_GC, 2 * FN_GC), lambda b, t: (0, 0)),
        ],
        out_specs=pl.BlockSpec((None, 2, ROW_TILE, D), lambda b, t: (b, 0, t, 0)),
        out_shape=jax.ShapeDtypeStruct((BATCH, 2, TT, D), BF16),
        compiler_params=_cp(2),
        name="fourier_channels",
    )(h, cs)


FN_R = 64
FN_CB = 16


def _fn_rows_kernel(ab_ref, f1_ref, twc_ref, tws_ref, y_ref, u32):
    u32[...] = ab_ref[...].astype(F32)
    for j in range(FN_CB):
        st = jnp.concatenate([u32[0, :, j, :], u32[1, :, j, :]], axis=0).astype(BF16)
        y = jnp.dot(f1_ref[...], st, preferred_element_type=F32)
        yr, yi = y[:FN_R], y[FN_R:]
        cw, sw = twc_ref[j], tws_ref[j]
        y_ref[0, j] = (yr * cw - yi * sw).astype(BF16)
        y_ref[1, j] = (yi * cw + yr * sw).astype(BF16)


def _fn_rows(ab, f1, twc, tws):
    ab5 = ab.reshape(BATCH, 2, TT // FN_R, FN_R, D)
    return pl.pallas_call(
        _fn_rows_kernel,
        grid=(BATCH, FN_R // FN_CB),
        in_specs=[
            pl.BlockSpec((None, 2, FN_R, FN_CB, D), lambda b, cb: (b, 0, 0, cb, 0)),
            pl.BlockSpec((2 * FN_R, 2 * FN_R), lambda b, cb: (0, 0)),
            pl.BlockSpec((FN_CB, FN_R, 1), lambda b, cb: (cb, 0, 0)),
            pl.BlockSpec((FN_CB, FN_R, 1), lambda b, cb: (cb, 0, 0)),
        ],
        out_specs=pl.BlockSpec((None, 2, FN_CB, FN_R, D), lambda b, cb: (b, 0, cb, 0, 0)),
        out_shape=jax.ShapeDtypeStruct((BATCH, 2, FN_R, FN_R, D), BF16),
        scratch_shapes=[pltpu.VMEM((2, FN_R, FN_CB, D), F32)],
        compiler_params=_cp(2),
        name="fourier_rows",
    )(ab5, f1, twc, tws)


def _fn_cols_kernel(y_ref, f2_ref, o_ref, y32, z32):
    y32[...] = y_ref[...].astype(F32)
    for j in range(FN_CB):
        st = jnp.concatenate([y32[0, :, j, :], y32[1, :, j, :]], axis=0).astype(BF16)
        z32[:, j, :] = jnp.dot(f2_ref[...], st, preferred_element_type=F32)
    o_ref[...] = z32[...].astype(BF16)


def _fn_cols(y, f2):
    out = pl.pallas_call(
        _fn_cols_kernel,
        grid=(BATCH, FN_R // FN_CB),
        in_specs=[
            pl.BlockSpec((None, 2, FN_R, FN_CB, D), lambda b, kb: (b, 0, 0, kb, 0)),
            pl.BlockSpec((FN_R, 2 * FN_R), lambda b, kb: (0, 0)),
        ],
        out_specs=pl.BlockSpec((None, FN_R, FN_CB, D), lambda b, kb: (b, 0, kb, 0)),
        out_shape=jax.ShapeDtypeStruct((BATCH, FN_R, FN_R, D), BF16),
        scratch_shapes=[pltpu.VMEM((2, FN_R, FN_CB, D), F32), pltpu.VMEM((FN_R, FN_CB, D), F32)],
        compiler_params=_cp(2),
        name="fourier_cols",
    )(y, f2)
    return out.reshape(BATCH, SEQ, D)


def _fn_ctx_kernel(ab_ref, m_ref, o_ref):
    st = jnp.concatenate([ab_ref[0], ab_ref[1]], axis=0)
    o_ref[...] = jnp.dot(m_ref[...], st, preferred_element_type=F32).astype(BF16)


def _fn_ctx(ab, m_ctx):
    return pl.pallas_call(
        _fn_ctx_kernel,
        grid=(BATCH,),
        in_specs=[
            pl.BlockSpec((None, 2, CTX, D), lambda b: (b, 0, SEQ // CTX, 0)),
            pl.BlockSpec((CTX, 2 * CTX), lambda b: (0, 0)),
        ],
        out_specs=pl.BlockSpec((None, CTX, D), lambda b: (b, 0, 0)),
        out_shape=jax.ShapeDtypeStruct((BATCH, CTX, D), BF16),
        compiler_params=_cp(1),
        name="fourier_ctx",
    )(ab, m_ctx)


def _dft_tables():
    def cs_pair(n, scale):
        i = np.arange(n)
        ang = 2.0 * np.pi * ((i[:, None] * i[None, :]) % n) / n
        return np.cos(ang) * scale, np.sin(ang) * scale

    assert FN_GC == CTX
    c, s = cs_pair(FN_GC, FN_GC ** -0.5)
    cs = np.concatenate([c, s], axis=1)
    m_ctx = np.concatenate([c, -s], axis=1)
    c, s = cs_pair(FN_R, FN_R ** -0.5)
    f1 = np.block([[c, -s], [s, c]])
    f2 = np.concatenate([c, -s], axis=1)
    i = np.arange(FN_R)
    phi = 2.0 * np.pi * (i[:, None] * i[None, :]) / SEQ
    tw = [jnp.asarray(f(phi)[..., None], F32) for f in (np.cos, np.sin)]
    return jnp.asarray(cs, BF16), jnp.asarray(m_ctx, BF16), jnp.asarray(f1, BF16), jnp.asarray(f2, BF16), tw


NA_ROWS = SEQ // GRID_W
NA_STEPS = TT // NA_QT
NA_NK = NA_KH * GRID_W
NA_DR = 2 * NA_KH - 1


def _na_row_start(step):
    r = jnp.minimum(step, NA_ROWS - 1)
    return r, jnp.clip(r - NA_KH // 2, 0, NA_ROWS - NA_KH)


def _na_bias_idx(step):
    r, rs = _na_row_start(step)
    return rs - r + (NA_KH - 1)


def _na_kernel(q_ref, k_ref, v_ref, bias_ref, o_ref):
    step = pl.program_id(1)
    lane = lax.broadcasted_iota(jnp.int32, (NA_QT, 128), 1)
    nt = (((1,), (1,)), ((), ()))
    scale = NA_HD ** -0.5

    def attend(start, d0=None):
        pairs = [slice(p * 128, (p + 1) * 128) for p in range(NA_HEADS // 2)]
        scores = []
        for p, sl in enumerate(pairs):
            qp = q_ref[:, sl] * scale
            zero = jnp.zeros_like(qp)
            q2 = jnp.concatenate([jnp.where(lane < NA_HD, qp, zero), jnp.where(lane >= NA_HD, qp, zero)], axis=0)
            s_c = lax.dot_general(q2, k_ref[SEQ:TT, sl], nt, preferred_element_type=F32)
            s_w = None
            if start is not None:
                bias = jnp.concatenate(
                    [jnp.concatenate([bias_ref[2 * p + hh, d0 + 2 * j2] for j2 in range(NA_KH // 2)], axis=1)
                     for hh in range(2)], axis=0)
                s_w = lax.dot_general(q2, k_ref[pl.ds(start, NA_NK), sl], nt, preferred_element_type=F32) + bias
            scores.append((s_c, s_w))
        probs = []
        for s_c, s_w in scores:
            m = jnp.max(s_c, axis=-1, keepdims=True)
            if s_w is not None:
                m = jnp.maximum(m, jnp.max(s_w, axis=-1, keepdims=True))
            p_c = jnp.exp(s_c - m)
            l = jnp.sum(p_c, axis=-1, keepdims=True)
            p_w = None
            if s_w is not None:
                p_w = jnp.exp(s_w - m)
                l = l + jnp.sum(p_w, axis=-1, keepdims=True)
                p_w = p_w.astype(BF16)
            probs.append((p_c.astype(BF16), p_w, l))
        for sl, (p_c, p_w, l) in zip(pairs, probs):
            acc = jnp.dot(p_c, v_ref[SEQ:TT, sl], preferred_element_type=F32)
            if p_w is not None:
                acc = acc + jnp.dot(p_w, v_ref[pl.ds(start, NA_NK), sl], preferred_element_type=F32)
            o2 = acc / l
            o_ref[:, sl] = jnp.where(lane < NA_HD, o2[:NA_QT], o2[NA_QT:]).astype(BF16)

    @pl.when(step < NA_ROWS)
    def _():
        _, rs = _na_row_start(step)
        attend(pl.multiple_of(rs * GRID_W, GRID_W), _na_bias_idx(step))

    @pl.when(step >= NA_ROWS)
    def _():
        attend(None)


def _na_attention(qkv, bias):
    return pl.pallas_call(
        _na_kernel,
        grid=(BATCH, NA_STEPS),
        in_specs=[
            pl.BlockSpec((None, NA_QT, D), lambda b, s: (b, s, 0)),
            pl.BlockSpec((None, TT, D), lambda b, s: (b, 0, 1), pipeline_mode=pl.Buffered(1)),
            pl.BlockSpec((None, TT, D), lambda b, s: (b, 0, 2), pipeline_mode=pl.Buffered(1)),
            pl.BlockSpec((NA_HEADS, NA_DR - 1, GRID_W, 2 * GRID_W), lambda b, s: (0, 0, 0, 0),
                         pipeline_mode=pl.Buffered(1)),
        ],
        out_specs=pl.BlockSpec((None, NA_QT, D), lambda b, s: (b, s, 0)),
        out_shape=jax.ShapeDtypeStruct((BATCH, TT, D), BF16),
        compiler_params=_cp(2),
        name="na_attention",
    )(qkv, qkv, qkv, bias)


def _na_bias_table(rpb):
    qc = np.arange(GRID_W)
    kcol = np.arange(GRID_W)
    cstart = np.clip(qc - NA_KW // 2, 0, GRID_W - NA_KW)
    valid = (kcol[None, :] >= cstart[:, None]) & (kcol[None, :] < cstart[:, None] + NA_KW)
    dc = kcol[None, :] - qc[:, None] + (NA_KW - 1)
    pick = ((dc[None] == np.arange(2 * NA_KW - 1)[:, None, None]) & valid[None]).astype(np.float32)
    tab = jnp.einsum("hdj,jqk->hdqk", rpb.astype(F32), jnp.asarray(pick), precision=HIGHEST)
    tab = tab + jnp.asarray(np.where(valid, 0.0, NEG_INF).astype(np.float32))
    return jnp.concatenate([tab[:, :NA_DR - 1], tab[:, 1:]], axis=-1)


def _route(h, first, rw_ref, rb_ref, tri_ref, cr_ref, cnt_ref):
    logits = lax.dot_general(rw_ref[...], h.astype(BF16), (((1,), (1,)), ((), ())),
                             preferred_element_type=F32)
    sc = jax.nn.sigmoid(logits)
    sel = sc + rb_ref[...]
    sc_r = [sc[e:e + 1, :] for e in range(N_EXPERTS)]
    sel_r = [sel[e:e + 1, :] for e in range(N_EXPERTS)]
    zero_i = jnp.zeros_like(sel_r[0]).astype(jnp.int32)

    best = gi = None
    picks = []
    for g in range(N_GROUPS):
        a = sel_r[g * EPG:(g + 1) * EPG]
        u = sc_r[g * EPG:(g + 1) * EPG]
        gs = None
        for i in range(EPG):
            for j in range(i + 1, EPG):
                pair = a[i] + a[j]
                gs = pair if gs is None else jnp.maximum(gs, pair)
        m1, i1, s1 = a[0], zero_i, u[0]
        for j in range(1, EPG):
            upd = a[j] > m1
            m1 = jnp.where(upd, a[j], m1)
            i1 = jnp.where(upd, j, i1)
            s1 = jnp.where(upd, u[j], s1)
        m2 = i2 = s2 = None
        for j in range(EPG):
            cand = jnp.where(i1 == j, -jnp.inf, a[j])
            if m2 is None:
                m2, i2, s2 = cand, zero_i, u[0]
            else:
                upd = cand > m2
                m2 = jnp.where(upd, cand, m2)
                i2 = jnp.where(upd, j, i2)
                s2 = jnp.where(upd, u[j], s2)
        picks.append((i1 + g * EPG, i2 + g * EPG, s1, s2))
        if best is None:
            best, gi = gs, zero_i
        else:
            upd = gs > best
            best = jnp.where(upd, gs, best)
            gi = jnp.where(upd, g, gi)

    e1, e2, s1, s2 = picks[0]
    for g in range(1, N_GROUPS):
        on = gi == g
        e1 = jnp.where(on, picks[g][0], e1)
        e2 = jnp.where(on, picks[g][1], e2)
        s1 = jnp.where(on, picks[g][2], s1)
        s2 = jnp.where(on, picks[g][3], s2)
    tot = s1 + s2
    w1, w2 = s1 / tot, s2 / tot

    swap = e2 < e1
    la = jnp.where(swap, e2, e1) - gi * EPG
    lb = jnp.where(swap, e1, e2) - gi * EPG
    cls = gi * PAIRS + jnp.where(la == 0, 0, jnp.where(la == 1, 3, 5)) + lb - la - 1
    wa, wb = jnp.where(swap, w2, w1), jnp.where(swap, w1, w2)

    @pl.when(first)
    def _():
        cnt_ref[...] = jnp.zeros_like(cnt_ref)

    oh = (lax.broadcasted_iota(jnp.int32, (cnt_ref.shape[0], cls.shape[1]), 0) == cls).astype(F32)
    before = jnp.dot(oh.astype(BF16), tri_ref[...], preferred_element_type=F32)
    base = cnt_ref[:, 0:1]
    cr_ref[0:1, :] = cls
    cr_ref[1:2, :] = jnp.sum(oh * (base + before), axis=0, keepdims=True).astype(jnp.int32)
    cnt_ref[...] = cnt_ref[...] + jnp.sum(oh, axis=1, keepdims=True)
    return wa, wb


DMA_UNROLL = 8


def _row_copy(src_ref, src_row, dst_ref, dst_row, sem):
    return pltpu.make_async_copy(src_ref.at[pl.ds(src_row, 1)], dst_ref.at[pl.ds(dst_row, 1)], sem)


def _for_each_row(fn):
    def group(gidx, carry):
        for u in range(DMA_UNROLL):
            fn(gidx * DMA_UNROLL + u)
        return carry

    lax.fori_loop(0, ROW_TILE // DMA_UNROLL, group, 0)


def _dispatch_kernel(zrow_ref, dest_ref, prev_ref, h_ref, xs_ref, zbuf, sem, zsem):
    i = pl.program_id(0)

    @pl.when(i == 0)
    def _():
        zbuf[...] = jnp.zeros_like(zbuf)

        def zero_copy(e):
            start = pl.multiple_of(jnp.maximum(zrow_ref[e], 0), MOE_BM)
            return pltpu.make_async_copy(zbuf, xs_ref.at[pl.ds(start, MOE_BM)], zsem)

        for e in range(MOE_NZ):
            pl.when(zrow_ref[e] >= 0)(lambda e=e: zero_copy(e).start())
        for e in range(MOE_NZ):
            pl.when(zrow_ref[e] >= 0)(lambda e=e: zero_copy(e).wait())

    def copy(tile, slots_ref, r):
        return _row_copy(h_ref, tile * ROW_TILE + r, xs_ref, slots_ref[0, r], sem)

    _for_each_row(lambda r: copy(i, dest_ref, r).start())

    @pl.when(i > 0)
    def _():
        _for_each_row(lambda r: copy(i - 1, prev_ref, r).wait())

    @pl.when(i == pl.num_programs(0) - 1)
    def _():
        _for_each_row(lambda r: copy(i, dest_ref, r).wait())


def _dispatch(h, dest, zrow):
    slots = lambda fn: pl.BlockSpec((None, 1, ROW_TILE), fn, memory_space=pltpu.SMEM)
    grid_spec = pltpu.PrefetchScalarGridSpec(
        num_scalar_prefetch=1,
        grid=(NTOK // ROW_TILE,),
        in_specs=[
            slots(lambda i, z: (i, 0, 0)),
            slots(lambda i, z: (jnp.maximum(i - 1, 0), 0, 0)),
            pl.BlockSpec(memory_space=pl.ANY),
        ],
        out_specs=pl.BlockSpec(memory_space=pl.ANY),
        scratch_shapes=[pltpu.VMEM((MOE_BM, DX), F32), pltpu.SemaphoreType.DMA(()), pltpu.SemaphoreType.DMA(())],
    )
    return pl.pallas_call(
        _dispatch_kernel,
        grid_spec=grid_spec,
        out_shape=jax.ShapeDtypeStruct((MOE_P, DX), F32),
        compiler_params=pltpu.CompilerParams(dimension_semantics=("arbitrary",), vmem_limit_bytes=VMEM_LIMIT,
                                             has_side_effects=True),
        name="moe_dispatch",
    )(zrow, dest, dest, h)


def _experts_kernel(blk_ref, ex_ref, nu_ref, x_ref, *rest, which):
    if which:
        prev_ref, wg_ref, wu_ref, wd_ref, y_ref, wgb, wub, wdb = rest
    else:
        wg_ref, wu_ref, wd_ref, y_ref, wgb, wub, wdb = rest
    del blk_ref
    v = pl.program_id(0)
    active = v < nu_ref[0]
    changed = (v == 0) | (ex_ref[v] != ex_ref[jnp.maximum(v - 1, 0)])

    @pl.when(active & changed)
    def _():
        wgb[...] = wg_ref[...].astype(BF16)
        wub[...] = wu_ref[...].astype(BF16)
        wdb[...] = wd_ref[...].astype(BF16)

    @pl.when(active)
    def _():
        x = x_ref[:, :D].astype(BF16)
        g = jnp.dot(x, wgb[...], preferred_element_type=F32)
        u = jnp.dot(x, wub[...], preferred_element_type=F32)
        hh = (_silu(g) * u).astype(BF16)
        y = jnp.dot(hh, wdb[...], preferred_element_type=F32) * x_ref[:, D + which:D + which + 1]
        y_ref[...] = prev_ref[...] + y if which else y

    @pl.when(jnp.logical_not(active))
    def _():
        y_ref[...] = jnp.zeros_like(y_ref)


def _experts(xs, prev, which, blocks, experts, n_used, w_gate, w_up, w_down, depth):
    wspec = pl.BlockSpec((None, None, D, D), lambda v, bl, ex, nu: (depth, ex[v], 0, 0))
    yspec = pl.BlockSpec((MOE_BM, D), lambda v, bl, ex, nu: (bl[v], 0))
    grid_spec = pltpu.PrefetchScalarGridSpec(
        num_scalar_prefetch=3,
        grid=(MOE_NB,),
        in_specs=[pl.BlockSpec((MOE_BM, DX), lambda v, bl, ex, nu: (bl[v], 0))] + ([yspec] if which else [])
        + [wspec, wspec, wspec],
        out_specs=yspec,
        scratch_shapes=[pltpu.VMEM((D, D), BF16)] * 3,
    )
    return pl.pallas_call(
        functools.partial(_experts_kernel, which=which),
        grid_spec=grid_spec,
        out_shape=jax.ShapeDtypeStruct((MOE_P, D), F32),
        compiler_params=_cp(1),
        name="moe_experts_%d" % which,
    )(blocks, experts, n_used, xs, *([prev] if which else []), w_gate, w_up, w_down)


def _combine_kernel(dest_ref, next_ref, yb_ref, x_ref, mod_ref, modn_ref, gn_ref, *rest, final):
    if final:
        out_ref, ybuf, sem = rest
    else:
        xo_ref, h_ref, ybuf, sem = rest
    t = pl.program_id(1)
    step = pl.program_id(0) * TPB + t
    slot = lax.rem(step, 2)

    def gather(slots_ref, buf):
        return lambda r: _row_copy(yb_ref, slots_ref[0, r], ybuf.at[buf], r, sem.at[buf])

    @pl.when(step == 0)
    def _():
        _for_each_row(lambda r: gather(dest_ref, slot)(r).start())

    @pl.when(step + 1 < BATCH * TPB)
    def _():
        _for_each_row(lambda r: gather(next_ref, 1 - slot)(r).start())

    _for_each_row(lambda r: gather(dest_ref, slot)(r).wait())

    def body():
        xn = x_ref[...] + mod_ref[5:6, :] * ybuf[slot]
        if final:
            out_ref[...] = _normmod(xn, gn_ref[...], 0.0, 0.0)
        else:
            xo_ref[...] = xn
            h_ref[...] = _normmod(xn, gn_ref[...], modn_ref[0:1, :], modn_ref[1:2, :]).astype(BF16)

    if final:
        pl.when(t < LAT_TILES)(body)
    else:
        body()


def _combine(yb, dest, xs, mods, depth, g_next, final):
    row = pl.BlockSpec((None, ROW_TILE, D), lambda b, t: (b, t, 0))
    next_depth = depth if final else depth + 1
    if final:
        out_specs = pl.BlockSpec((None, ROW_TILE, D), lambda b, t: (b, jnp.minimum(t, LAT_TILES - 1), 0))
        out_shape = jax.ShapeDtypeStruct((BATCH, SEQ, D), F32)
    else:
        out_specs = [row, row]
        out_shape = [jax.ShapeDtypeStruct((BATCH, TT, D), F32), jax.ShapeDtypeStruct((BATCH, TT, D), BF16)]
    last = BATCH * TPB - 1
    return pl.pallas_call(
        functools.partial(_combine_kernel, final=final),
        grid=(BATCH, TPB),
        in_specs=[
            pl.BlockSpec((None, 1, ROW_TILE), lambda b, t: (b * TPB + t, 0, 0), memory_space=pltpu.SMEM),
            pl.BlockSpec((None, 1, ROW_TILE), lambda b, t: (jnp.minimum(b * TPB + t + 1, last), 0, 0),
                         memory_space=pltpu.SMEM),
            pl.BlockSpec(memory_space=pl.ANY),
            row, _mod_spec(depth), _mod_spec(next_depth),
            g_next[1],
        ],
        out_specs=out_specs,
        out_shape=out_shape,
        scratch_shapes=[pltpu.VMEM((2, ROW_TILE, D), F32), pltpu.SemaphoreType.DMA((2,))],
        compiler_params=_cp(2),
        name="moe_combine",
    )(dest, dest, yb, xs, mods, mods, g_next[0])


def _class_experts():
    pairs = [(a, b) for a in range(EPG) for b in range(a + 1, EPG)]
    lo = [g * EPG + a for g in range(N_GROUPS) for a, _ in pairs]
    hi = [g * EPG + b for g in range(N_GROUPS) for _, b in pairs]
    return jnp.asarray(lo, jnp.int32), jnp.asarray(hi, jnp.int32)


def _moe_plan(cls, rank, counts):
    padded = (counts + MOE_BM - 1) // MOE_BM * MOE_BM
    pend = jnp.cumsum(padded)
    pstart = pend - padded
    cid = jnp.arange(N_CLS, dtype=jnp.int32)
    dest = rank + jnp.sum(jnp.where(cls[None, :] == cid[:, None], pstart[:, None], 0), axis=0)
    dest = dest.astype(jnp.int32).reshape(NTOK // ROW_TILE, 1, ROW_TILE)
    blk = jnp.arange(MOE_NB, dtype=jnp.int32)
    block_c = jnp.minimum(jnp.sum(pend[None, :] <= (blk * MOE_BM)[:, None], axis=1), N_CLS - 1)
    n_used = (pend[-1] // MOE_BM).astype(jnp.int32)
    used = blk < n_used
    lo, hi = _class_experts()
    onehot_c = block_c[:, None] == cid[None, :]
    e_lo = jnp.sum(jnp.where(onehot_c, lo[None, :], 0), axis=1)
    e_hi = jnp.sum(jnp.where(onehot_c, hi[None, :], 0), axis=1)

    def hold_last(e):
        return jnp.where(used, e, jnp.sum(jnp.where(blk == n_used - 1, e, 0))).astype(jnp.int32)

    key = jnp.where(used, e_hi, N_EXPERTS)
    earlier = (key[None, :] < key[:, None]) | ((key[None, :] == key[:, None]) & (blk[None, :] < blk[:, None]))
    pos = jnp.sum(earlier, axis=1)
    at = pos[None, :] == blk[:, None]
    order1 = jnp.sum(jnp.where(at, blk[None, :], 0), axis=1).astype(jnp.int32)
    e1 = jnp.sum(jnp.where(at, e_hi[None, :], 0), axis=1)
    tail = blk[MOE_NB - MOE_TAIL:] * MOE_BM
    zrow = jnp.concatenate([jnp.where(padded > 0, pend - MOE_BM, -1), jnp.where(tail >= pend[-1], tail, -1)])
    return dest, (blk, hold_last(e_lo)), (order1, hold_last(e1)), n_used[None], zrow.astype(jnp.int32)


def _moe(h, routing, xs, mods, depth, g_next, w_gate, w_up, w_down, final):
    cr, cnt = routing
    dest, visit0, visit1, n_used, zrow = _moe_plan(cr[0], cr[1], cnt[:N_CLS, 0].astype(jnp.int32))
    xsorted = _dispatch(h.reshape(NTOK, DX), dest, zrow)
    y0 = _experts(xsorted, None, 0, *visit0, n_used, w_gate, w_up, w_down, depth)
    yb = _experts(xsorted, y0, 1, *visit1, n_used, w_gate, w_up, w_down, depth)
    return _combine(yb, dest, xs, mods, depth, g_next, final)


def _rope_tables():
    nf = RET_DK // 4
    t = np.arange(SEQ)
    inv = ROPE_BASE ** (-np.arange(nf, dtype=np.float32) / nf)
    cos = np.ones((TT, 256), np.float32)
    sin = np.zeros((TT, 256), np.float32)
    for seg, pos in enumerate(((t // GRID_W).astype(np.float32), (t % GRID_W).astype(np.float32))):
        ang = (pos[:, None] * inv[None, :]).astype(np.float32)
        c, s = np.cos(ang), np.sin(ang)
        cos[:SEQ, seg * 128:(seg + 1) * 128] = np.concatenate([c, c], axis=1)
        sin[:SEQ, seg * 128:(seg + 1) * 128] = np.concatenate([-s, s], axis=1)
    ks = np.float32(RET_DK ** -0.5)
    return jnp.asarray(np.stack([cos, cos * ks])), jnp.asarray(np.stack([sin, sin * ks]))


def _decay_tables(decay_logit):
    C = RET_CHUNK
    log_g = jax.nn.log_sigmoid(decay_logit.astype(F32))
    pos = jnp.arange(C, dtype=F32)
    diff = pos[:, None] - pos[None, :]
    lg = log_g[:, :, None, None]
    dm_f = jnp.where(diff >= 0, jnp.exp(jnp.maximum(diff, 0.0)[None, None] * lg), 0.0)
    dm_b = jnp.where(diff <= 0, jnp.exp(jnp.maximum(-diff, 0.0)[None, None] * lg), 0.0)
    dmat = jnp.stack([dm_f[0], dm_b[1]])
    lgc = log_g[:, :, None]
    qdec = jnp.stack([jnp.exp((pos + 1.0)[None, :] * lgc[0]), jnp.exp((C - pos)[None, :] * lgc[1])])
    kdec = jnp.stack([jnp.exp((C - 1.0 - pos)[None, :] * lgc[0]), jnp.exp(pos[None, :] * lgc[1])])
    cdec = jnp.exp(C * log_g).reshape(-1)
    return dmat, qdec[..., None], kdec[..., None], cdec


def kernel(x, c, ctx, c_ctx, w_mod, b_mod, norm_mix, norm_ffn, norm_final, ret_w_in, ret_decay, ret_gn, ret_w_out,
           fn_w_out, na_w_qkv, na_rpb, na_w_out, router_w, router_bias, moe_w_gate, moe_w_up, moe_w_down):
    cvec = jnp.concatenate([c_ctx[None, :], c, jnp.zeros((8 - 1 - BATCH, D), F32)], axis=0)
    mod_all = _modulation(cvec, w_mod, b_mod).reshape(DEPTH, 8, N_MOD, D)
    pad = jnp.zeros((DEPTH, BATCH, 2, 8 - N_MOD, D), F32)
    mods = jnp.stack([jnp.broadcast_to(mod_all[:, 0:1], (DEPTH, BATCH, N_MOD, D)), mod_all[:, 1:1 + BATCH]], axis=2)
    mods = jnp.concatenate([mods, pad], axis=3)

    rope = _rope_tables()
    cs, dft_ctx, dft_f1, dft_f2, dft_tw = _dft_tables()
    route = (router_w.T.astype(BF16), router_bias.reshape(N_EXPERTS, 1).astype(F32))

    norm_mix3 = norm_mix.reshape(DEPTH, 1, D)
    norm_ffn3 = norm_ffn.reshape(DEPTH, 1, D)
    ret_gn3 = ret_gn.reshape(-1, 1, 2 * D)

    xs, h = _embed(x, ctx, mods, norm_mix3)
    out = None
    for i in range(DEPTH):
        kind, j = i % 3, i // 3
        if kind == 0:
            qkvg = _project(h.reshape(NTOK, D), ret_w_in, j, rope).reshape(BATCH, TT, 6 * D)
            dmat, qdec, kdec, cdec = _decay_tables(ret_decay[j])
            o_f, o_b = _retention(qkvg, dmat, qdec, kdec, cdec)
            xs, hf, *routing = _mixer_out("ret", (o_f, o_b, qkvg, ret_gn3), ret_w_out, j, xs, mods, norm_ffn3, i, route)
        elif kind == 1:
            ab = _fn_channel(h, cs)
            f_lat = _fn_cols(_fn_rows(ab, dft_f1, *dft_tw), dft_f2)
            xs, hf, *routing = _mixer_out(
                "split", (f_lat, _fn_ctx(ab, dft_ctx)), fn_w_out, j, xs, mods, norm_ffn3, i, route)
        else:
            qkv = _project(h.reshape(NTOK, D), na_w_qkv, j).reshape(BATCH, TT, 3 * D)
            o = _na_attention(qkv, _na_bias_table(na_rpb[j]))
            xs, hf, *routing = _mixer_out("plain", (o,), na_w_out, j, xs, mods, norm_ffn3, i, route)
        final = i == DEPTH - 1
        if final:
            g_next = (norm_final[None, :], pl.BlockSpec((1, D), lambda b, t: (0, 0)))
        else:
            g_next = (norm_mix3, _layer_spec((1, D), i + 1))
        res = _moe(hf, routing, xs, mods, i, g_next, moe_w_gate, moe_w_up, moe_w_down, final)
        if final:
            out = res
        else:
            xs, h = res
    return out
```

```python
import functools

import numpy as np
import jax
import jax.numpy as jnp
from jax import lax
from jax.experimental import pallas as pl
from jax.experimental.pallas import tpu as pltpu

F32 = jnp.float32
BF16 = jnp.bfloat16
HIGHEST = lax.Precision.HIGHEST

D = 1024
BATCH = 4
SEQ = 4096
CTX = 256
TT = SEQ + CTX
NTOK = BATCH * TT
DEPTH = 4
GRID_W = 64
EPS = 1e-6
NEG_INF = -1e30
N_MOD = 6

RET_HEADS = 4
RET_DK = D // RET_HEADS
RET_DV = 2 * RET_DK
RET_CHUNK = 256
ROPE_BASE = 10000.0

FN_GROUPS = 4
FN_GC = D // FN_GROUPS

NA_HEADS = 16
NA_HD = D // NA_HEADS
NA_KH = 8
NA_KW = 16

N_EXPERTS = 16
N_GROUPS = 4
EPG = N_EXPERTS // N_GROUPS

ROW_TILE = 256
TPB = TT // ROW_TILE
LAT_TILES = SEQ // ROW_TILE
MM_TM = TT // 4
MOE_BM = 256
PAIRS = EPG * (EPG - 1) // 2
N_CLS = N_GROUPS * PAIRS
CLS_PAD = 32
MOE_NB = (NTOK + N_CLS * (MOE_BM - 1) + MOE_BM - 1) // MOE_BM
MOE_P = MOE_NB * MOE_BM
MOE_TAIL = MOE_NB - NTOK // MOE_BM
MOE_NZ = N_CLS + MOE_TAIL
DISPATCH_TILE = 512
DX = D + 128
NA_QT = GRID_W
VMEM_LIMIT = 56 * 1024 * 1024


def _cp(n_axes, vmem=VMEM_LIMIT):
    return pltpu.CompilerParams(dimension_semantics=("arbitrary",) * n_axes, vmem_limit_bytes=vmem)


def _silu(x):
    return x * jax.nn.sigmoid(x)


def _normmod(x, g, shift, scale):
    y = x * lax.rsqrt(jnp.mean(x * x, axis=-1, keepdims=True) + EPS)
    return (y * g) * (1.0 + scale) + shift


def _mod_idx(t):
    return jnp.where(t < LAT_TILES, 1, 0)


def _layer_spec(shape, layer):
    return pl.BlockSpec((None,) + shape, lambda *_: (layer,) + (0,) * len(shape))


def _mod_spec(depth):
    return pl.BlockSpec((None, None, None, 8, D), lambda b, t: (depth, b, _mod_idx(t), 0, 0))


def _mod_kernel(c_ref, w_ref, b_ref, o_ref):
    a = _silu(c_ref[...])
    o_ref[...] = jnp.dot(a, w_ref[...], precision=HIGHEST, preferred_element_type=F32) + b_ref[...]


def _modulation(cvec, w_mod, b_mod):
    return pl.pallas_call(
        _mod_kernel,
        grid=(DEPTH, N_MOD),
        in_specs=[
            pl.BlockSpec((8, D), lambda i, j: (0, 0)),
            pl.BlockSpec((None, D, D), lambda i, j: (i, 0, j)),
            pl.BlockSpec((None, 1, D), lambda i, j: (i, 0, j)),
        ],
        out_specs=pl.BlockSpec((None, 8, D), lambda i, j: (i, 0, j)),
        out_shape=jax.ShapeDtypeStruct((DEPTH, 8, N_MOD * D), F32),
        compiler_params=_cp(2),
        name="modulation",
    )(cvec, w_mod, b_mod.reshape(DEPTH, 1, N_MOD * D))


def _embed_kernel(x_ref, ctx_ref, mod_ref, g_ref, xs_ref, h_ref):
    t = pl.program_id(1)

    def emit(src):
        xs_ref[...] = src
        h_ref[...] = _normmod(src, g_ref[...], mod_ref[0:1, :], mod_ref[1:2, :]).astype(BF16)

    @pl.when(t < LAT_TILES)
    def _():
        emit(x_ref[...])

    @pl.when(t >= LAT_TILES)
    def _():
        emit(ctx_ref[...])


def _embed(x, ctx, mods, g):
    return pl.pallas_call(
        _embed_kernel,
        grid=(BATCH, TPB),
        in_specs=[
            pl.BlockSpec((None, ROW_TILE, D), lambda b, t: (b, jnp.minimum(t, LAT_TILES - 1), 0)),
            pl.BlockSpec((None, CTX, D), lambda b, t: (b, 0, 0)),
            _mod_spec(0),
            _layer_spec((1, D), 0),
        ],
        out_specs=[
            pl.BlockSpec((None, ROW_TILE, D), lambda b, t: (b, t, 0)),
            pl.BlockSpec((None, ROW_TILE, D), lambda b, t: (b, t, 0)),
        ],
        out_shape=[
            jax.ShapeDtypeStruct((BATCH, TT, D), F32),
            jax.ShapeDtypeStruct((BATCH, TT, D), BF16),
        ],
        compiler_params=_cp(2),
        name="embed",
    )(x, ctx, mods, g)


def _rope_store(acc, cos_ref, sin_ref, o_ref):
    for c in range(acc.shape[1] // 128):
        xc = acc[:, c * 128:(c + 1) * 128]
        tsl = slice((c % 2) * 128, (c % 2 + 1) * 128)
        rot = xc * cos_ref[:, tsl] + pltpu.roll(xc, 64, 1) * sin_ref[:, tsl]
        o_ref[:, c * 128:(c + 1) * 128] = rot.astype(o_ref.dtype)


def _proj_kernel(a_ref, w_ref, *rest, n_rope):
    if n_rope:
        cos_ref, sin_ref, o_ref, wb_ref = rest
    else:
        o_ref, wb_ref = rest
    j = pl.program_id(0)

    @pl.when(pl.program_id(1) == 0)
    def _():
        wb_ref[...] = w_ref[...].astype(BF16)

    acc = jnp.dot(a_ref[...], wb_ref[...], preferred_element_type=F32)
    if n_rope:
        @pl.when(j < n_rope)
        def _():
            _rope_store(acc, cos_ref, sin_ref, o_ref)

        @pl.when(j >= n_rope)
        def _():
            o_ref[...] = acc.astype(o_ref.dtype)
    else:
        o_ref[...] = acc.astype(o_ref.dtype)


def _project(a, w, layer, rope=None):
    _, k, n = w.shape
    tn = D
    n_rope = 0 if rope is None else rope[0].shape[0]
    in_specs = [
        pl.BlockSpec((MM_TM, k), lambda j, i: (i, 0)),
        pl.BlockSpec((None, k, tn), lambda j, i: (layer, 0, j)),
    ]
    args = [a, w]
    if n_rope:
        tspec = pl.BlockSpec((None, MM_TM, 256), lambda j, i: (jnp.minimum(j, n_rope - 1), i % (TT // MM_TM), 0))
        in_specs += [tspec, tspec]
        args += list(rope)
    return pl.pallas_call(
        functools.partial(_proj_kernel, n_rope=n_rope),
        grid=(n // tn, NTOK // MM_TM),
        in_specs=in_specs,
        out_specs=pl.BlockSpec((MM_TM, tn), lambda j, i: (i, j)),
        out_shape=jax.ShapeDtypeStruct((NTOK, n), BF16),
        scratch_shapes=[pltpu.VMEM((k, tn), BF16)],
        compiler_params=_cp(2),
        name="project",
    )(*args)


N_CHUNKS = TT // RET_CHUNK
LAT_CHUNKS = SEQ // RET_CHUNK
CTX_CHUNKS = CTX // RET_CHUNK


def _fwd_chunk(n):
    return jnp.where(n < CTX_CHUNKS, LAT_CHUNKS + n, n - CTX_CHUNKS)


def _bwd_chunk(n):
    return N_CHUNKS - 1 - n


def _ret_kernel(cdec_ref, qf, kf, vf, qb, kb, vb, dmat_ref, qdec_ref, kdec_ref, of_ref, ob_ref, s_ref):
    @pl.when(pl.program_id(1) == 0)
    def _():
        s_ref[...] = jnp.zeros_like(s_ref)

    for d, (q_r, k_r, v_r, o_r) in enumerate(((qf, kf, vf, of_ref), (qb, kb, vb, ob_ref))):
        for h in range(RET_HEADS):
            q = q_r[:, h * RET_DK:(h + 1) * RET_DK]
            k = k_r[:, h * RET_DK:(h + 1) * RET_DK]
            v = v_r[:, h * RET_DV:(h + 1) * RET_DV]
            att = lax.dot_general(q, k, (((1,), (1,)), ((), ())), preferred_element_type=F32) * dmat_ref[d, h]
            s = s_ref[d, h]
            qd = (q.astype(F32) * qdec_ref[d, h]).astype(BF16)
            o = (jnp.dot(att.astype(BF16), v, preferred_element_type=F32)
                 + jnp.dot(qd, s.astype(BF16), preferred_element_type=F32))
            kd = (k.astype(F32) * kdec_ref[d, h]).astype(BF16)
            s_ref[d, h] = s * cdec_ref[d * RET_HEADS + h] + lax.dot_general(
                kd, v, (((0,), (0,)), ((), ())), preferred_element_type=F32)
            o_r[:, h * RET_DV:(h + 1) * RET_DV] = o.astype(o_r.dtype)


def _retention(qkvg, dmat, qdec, kdec, cdec):
    C = RET_CHUNK

    def spec(width, col, chunk_fn):
        return pl.BlockSpec((None, C, width), lambda b, n, cd: (b, chunk_fn(n), col))

    ospec_f = pl.BlockSpec((None, C, 2 * D), lambda b, n, cd: (b, _fwd_chunk(n), 0))
    ospec_b = pl.BlockSpec((None, C, 2 * D), lambda b, n, cd: (b, _bwd_chunk(n), 0))
    full = lambda shape: pl.BlockSpec(shape, lambda b, n, cd: (0,) * len(shape))
    grid_spec = pltpu.PrefetchScalarGridSpec(
        num_scalar_prefetch=1,
        grid=(BATCH, N_CHUNKS),
        in_specs=[
            spec(D, 0, _fwd_chunk), spec(D, 1, _fwd_chunk), spec(2 * D, 1, _fwd_chunk),
            spec(D, 0, _bwd_chunk), spec(D, 1, _bwd_chunk), spec(2 * D, 1, _bwd_chunk),
            full((2, RET_HEADS, C, C)), full((2, RET_HEADS, C, 1)), full((2, RET_HEADS, C, 1)),
        ],
        out_specs=[ospec_f, ospec_b],
        scratch_shapes=[pltpu.VMEM((2, RET_HEADS, RET_DK, RET_DV), F32)],
    )
    return pl.pallas_call(
        _ret_kernel,
        grid_spec=grid_spec,
        out_shape=[jax.ShapeDtypeStruct((BATCH, TT, 2 * D), BF16)] * 2,
        compiler_params=_cp(2),
        name="retention",
    )(cdec, qkvg, qkvg, qkvg, qkvg, qkvg, qkvg, dmat, qdec, kdec)


def _mixer_out_kernel(*refs, kind):
    n_src = {"ret": 4, "split": 2, "plain": 1}[kind]
    srcs, (w_ref, x_ref, mod_ref, nf_ref), route_in = refs[:n_src], refs[n_src:n_src + 4], refs[n_src + 4:n_src + 7]
    xo_ref, h_ref, cr_ref, cnt_ref, wb_ref = refs[n_src + 7:]
    first = (pl.program_id(0) == 0) & (pl.program_id(1) == 0)

    @pl.when(first)
    def _():
        wb_ref[...] = w_ref[...].astype(BF16)

    if kind == "ret":
        of_ref, ob_ref, g_ref, gn_ref = srcs
        o = of_ref[...].astype(F32) + ob_ref[...].astype(F32)
        parts = []
        for h in range(RET_HEADS):
            oh = o[:, h * RET_DV:(h + 1) * RET_DV]
            mu = jnp.mean(oh, axis=-1, keepdims=True)
            ctr = oh - mu
            var = jnp.mean(ctr * ctr, axis=-1, keepdims=True)
            parts.append(ctr * lax.rsqrt(var + EPS))
        on = jnp.concatenate(parts, axis=-1) * gn_ref[...]
        a = (on * _silu(g_ref[...].astype(F32))).astype(BF16)
    elif kind == "split":
        a = jnp.where(pl.program_id(1) < LAT_TILES, srcs[0][...], srcs[1][...])
    else:
        a = srcs[0][...]
    y = jnp.dot(a, wb_ref[...], preferred_element_type=F32)
    xn = x_ref[...] + mod_ref[2:3, :] * y
    xo_ref[...] = xn
    h = _normmod(xn, nf_ref[...], mod_ref[3:4, :], mod_ref[4:5, :])
    wa, wb = _route(h, first, *route_in, cr_ref, cnt_ref)
    sub = lax.broadcasted_iota(jnp.int32, (DX - D, ROW_TILE), 0)
    h_ref[:, :D] = h
    h_ref[:, D:] = jnp.where(sub == 0, wa, jnp.where(sub == 1, wb, 0.0)).T


def _mixer_out(kind, srcs, w, layer, xs, mods, norm_ffn, depth, route):
    k = w.shape[1]
    row = lambda width, col=0: pl.BlockSpec((None, ROW_TILE, width), lambda b, t: (b, t, col))
    const = lambda shape: pl.BlockSpec(shape, lambda b, t: (0,) * len(shape))
    per_tok = pl.BlockSpec((2, ROW_TILE), lambda b, t: (0, b * TPB + t))
    tri = jnp.asarray(np.triu(np.ones((ROW_TILE, ROW_TILE), np.float32), 1), BF16)
    if kind == "ret":
        src_specs = [row(2 * D), row(2 * D), row(2 * D, 2), _layer_spec((1, 2 * D), layer)]
    elif kind == "split":
        src_specs = [pl.BlockSpec((None, ROW_TILE, k), lambda b, t: (b, jnp.minimum(t, LAT_TILES - 1), 0)),
                     pl.BlockSpec((None, CTX, k), lambda b, t: (b, 0, 0))]
    else:
        src_specs = [row(k)]
    return pl.pallas_call(
        functools.partial(_mixer_out_kernel, kind=kind),
        grid=(BATCH, TPB),
        in_specs=src_specs + [_layer_spec((k, D), layer), row(D), _mod_spec(depth), _layer_spec((1, D), depth),
                              const((N_EXPERTS, D)), const((N_EXPERTS, 1)), const((ROW_TILE, ROW_TILE))],
        out_specs=[row(D), row(DX), per_tok, const((CLS_PAD, 128))],
        out_shape=[jax.ShapeDtypeStruct((BATCH, TT, D), F32), jax.ShapeDtypeStruct((BATCH, TT, DX), F32),
                   jax.ShapeDtypeStruct((2, NTOK), jnp.int32), jax.ShapeDtypeStruct((CLS_PAD, 128), F32)],
        scratch_shapes=[pltpu.VMEM((k, D), BF16)],
        compiler_params=_cp(2),
        name="mixer_out_" + kind,
    )(*srcs, w, xs, mods, norm_ffn, *route, tri)


def _fn_channel_kernel(h_ref, cs_ref, o_ref):
    for g in range(FN_GROUPS):
        r = jnp.dot(h_ref[:, g * FN_GC:(g + 1) * FN_GC], cs_ref[...], preferred_element_type=F32)
        o_ref[0, :, g * FN_GC:(g + 1) * FN_GC] = r[:, :FN_GC].astype(BF16)
        o_ref[1, :, g * FN_GC:(g + 1) * FN_GC] = r[:, FN_GC:].astype(BF16)


def _fn_channel(h, cs):
    return pl.pallas_call(
        _fn_channel_kernel,
        grid=(BATCH, TPB),
        in_specs=[
            pl.BlockSpec((None, ROW_TILE, D), lambda b, t: (b, t, 0)),
            pl.BlockSpec((FN_GC, 2 * FN_GC), lambda b, t: (0, 0)),
        ],
        out_specs=pl.BlockSpec((None, 2, ROW_TILE, D), lambda b, t: (b, 0, t, 0)),
        out_shape=jax.ShapeDtypeStruct((BATCH, 2, TT, D), BF16),
        compiler_params=_cp(2),
        name="fourier_channels",
    )(h, cs)


FN_R = 64
FN_CB = 16


def _fn_rows_kernel(ab_ref, f1_ref, twc_ref, tws_ref, y_ref, u32):
    u32[...] = ab_ref[...].astype(F32)
    for j in range(FN_CB):
        st = jnp.concatenate([u32[0, :, j, :], u32[1, :, j, :]], axis=0).astype(BF16)
        y = jnp.dot(f1_ref[...], st, preferred_element_type=F32)
        yr, yi = y[:FN_R], y[FN_R:]
        cw, sw = twc_ref[j], tws_ref[j]
        y_ref[0, j] = (yr * cw - yi * sw).astype(BF16)
        y_ref[1, j] = (yi * cw + yr * sw).astype(BF16)


def _fn_rows(ab, f1, twc, tws):
    ab5 = ab.reshape(BATCH, 2, TT // FN_R, FN_R, D)
    return pl.pallas_call(
        _fn_rows_kernel,
        grid=(BATCH, FN_R // FN_CB),
        in_specs=[
            pl.BlockSpec((None, 2, FN_R, FN_CB, D), lambda b, cb: (b, 0, 0, cb, 0)),
            pl.BlockSpec((2 * FN_R, 2 * FN_R), lambda b, cb: (0, 0)),
            pl.BlockSpec((FN_CB, FN_R, 1), lambda b, cb: (cb, 0, 0)),
            pl.BlockSpec((FN_CB, FN_R, 1), lambda b, cb: (cb, 0, 0)),
        ],
        out_specs=pl.BlockSpec((None, 2, FN_CB, FN_R, D), lambda b, cb: (b, 0, cb, 0, 0)),
        out_shape=jax.ShapeDtypeStruct((BATCH, 2, FN_R, FN_R, D), BF16),
        scratch_shapes=[pltpu.VMEM((2, FN_R, FN_CB, D), F32)],
        compiler_params=_cp(2),
        name="fourier_rows",
    )(ab5, f1, twc, tws)


def _fn_cols_kernel(y_ref, f2_ref, o_ref, y32, z32):
    y32[...] = y_ref[...].astype(F32)
    for j in range(FN_CB):
        st = jnp.concatenate([y32[0, :, j, :], y32[1, :, j, :]], axis=0).astype(BF16)
        z32[:, j, :] = jnp.dot(f2_ref[...], st, preferred_element_type=F32)
    o_ref[...] = z32[...].astype(BF16)


def _fn_cols(y, f2):
    out = pl.pallas_call(
        _fn_cols_kernel,
        grid=(BATCH, FN_R // FN_CB),
        in_specs=[
            pl.BlockSpec((None, 2, FN_R, FN_CB, D), lambda b, kb: (b, 0, 0, kb, 0)),
            pl.BlockSpec((FN_R, 2 * FN_R), lambda b, kb: (0, 0)),
        ],
        out_specs=pl.BlockSpec((None, FN_R, FN_CB, D), lambda b, kb: (b, 0, kb, 0)),
        out_shape=jax.ShapeDtypeStruct((BATCH, FN_R, FN_R, D), BF16),
        scratch_shapes=[pltpu.VMEM((2, FN_R, FN_CB, D), F32), pltpu.VMEM((FN_R, FN_CB, D), F32)],
        compiler_params=_cp(2),
        name="fourier_cols",
    )(y, f2)
    return out.reshape(BATCH, SEQ, D)


def _fn_ctx_kernel(ab_ref, m_ref, o_ref):
    st = jnp.concatenate([ab_ref[0], ab_ref[1]], axis=0)
    o_ref[...] = jnp.dot(m_ref[...], st, preferred_element_type=F32).astype(BF16)


def _fn_ctx(ab, m_ctx):
    return pl.pallas_call(
        _fn_ctx_kernel,
        grid=(BATCH,),
        in_specs=[
            pl.BlockSpec((None, 2, CTX, D), lambda b: (b, 0, SEQ // CTX, 0)),
            pl.BlockSpec((CTX, 2 * CTX), lambda b: (0, 0)),
        ],
        out_specs=pl.BlockSpec((None, CTX, D), lambda b: (b, 0, 0)),
        out_shape=jax.ShapeDtypeStruct((BATCH, CTX, D), BF16),
        compiler_params=_cp(1),
        name="fourier_ctx",
    )(ab, m_ctx)


def _dft_tables():
    def cs_pair(n, scale):
        i = np.arange(n)
        ang = 2.0 * np.pi * ((i[:, None] * i[None, :]) % n) / n
        return np.cos(ang) * scale, np.sin(ang) * scale

    assert FN_GC == CTX
    c, s = cs_pair(FN_GC, FN_GC ** -0.5)
    cs = np.concatenate([c, s], axis=1)
    m_ctx = np.concatenate([c, -s], axis=1)
    c, s = cs_pair(FN_R, FN_R ** -0.5)
    f1 = np.block([[c, -s], [s, c]])
    f2 = np.concatenate([c, -s], axis=1)
    i = np.arange(FN_R)
    phi = 2.0 * np.pi * (i[:, None] * i[None, :]) / SEQ
    tw = [jnp.asarray(f(phi)[..., None], F32) for f in (np.cos, np.sin)]
    return jnp.asarray(cs, BF16), jnp.asarray(m_ctx, BF16), jnp.asarray(f1, BF16), jnp.asarray(f2, BF16), tw


NA_ROWS = SEQ // GRID_W
NA_STEPS = TT // NA_QT
NA_NK = NA_KH * GRID_W
NA_DR = 2 * NA_KH - 1


def _na_row_start(step):
    r = jnp.minimum(step, NA_ROWS - 1)
    return r, jnp.clip(r - NA_KH // 2, 0, NA_ROWS - NA_KH)


def _na_bias_idx(step):
    r, rs = _na_row_start(step)
    return rs - r + (NA_KH - 1)


def _na_kernel(q_ref, k_ref, v_ref, bias_ref, o_ref):
    step = pl.program_id(1)
    lane = lax.broadcasted_iota(jnp.int32, (NA_QT, 128), 1)
    nt = (((1,), (1,)), ((), ()))
    scale = NA_HD ** -0.5

    def attend(start, d0=None):
        pairs = [slice(p * 128, (p + 1) * 128) for p in range(NA_HEADS // 2)]
        scores = []
        for p, sl in enumerate(pairs):
            qp = q_ref[:, sl] * scale
            zero = jnp.zeros_like(qp)
            q2 = jnp.concatenate([jnp.where(lane < NA_HD, qp, zero), jnp.where(lane >= NA_HD, qp, zero)], axis=0)
            s_c = lax.dot_general(q2, k_ref[SEQ:TT, sl], nt, preferred_element_type=F32)
            s_w = None
            if start is not None:
                bias = jnp.concatenate(
                    [jnp.concatenate([bias_ref[2 * p + hh, d0 + 2 * j2] for j2 in range(NA_KH // 2)], axis=1)
                     for hh in range(2)], axis=0)
                s_w = lax.dot_general(q2, k_ref[pl.ds(start, NA_NK), sl], nt, preferred_element_type=F32) + bias
            scores.append((s_c, s_w))
        probs = []
        for s_c, s_w in scores:
            m = jnp.max(s_c, axis=-1, keepdims=True)
            if s_w is not None:
                m = jnp.maximum(m, jnp.max(s_w, axis=-1, keepdims=True))
            p_c = jnp.exp(s_c - m)
            l = jnp.sum(p_c, axis=-1, keepdims=True)
            p_w = None
            if s_w is not None:
                p_w = jnp.exp(s_w - m)
                l = l + jnp.sum(p_w, axis=-1, keepdims=True)
                p_w = p_w.astype(BF16)
            probs.append((p_c.astype(BF16), p_w, l))
        for sl, (p_c, p_w, l) in zip(pairs, probs):
            acc = jnp.dot(p_c, v_ref[SEQ:TT, sl], preferred_element_type=F32)
            if p_w is not None:
                acc = acc + jnp.dot(p_w, v_ref[pl.ds(start, NA_NK), sl], preferred_element_type=F32)
            o2 = acc / l
            o_ref[:, sl] = jnp.where(lane < NA_HD, o2[:NA_QT], o2[NA_QT:]).astype(BF16)

    @pl.when(step < NA_ROWS)
    def _():
        _, rs = _na_row_start(step)
        attend(pl.multiple_of(rs * GRID_W, GRID_W), _na_bias_idx(step))

    @pl.when(step >= NA_ROWS)
    def _():
        attend(None)


def _na_attention(qkv, bias):
    return pl.pallas_call(
        _na_kernel,
        grid=(BATCH, NA_STEPS),
        in_specs=[
            pl.BlockSpec((None, NA_QT, D), lambda b, s: (b, s, 0)),
            pl.BlockSpec((None, TT, D), lambda b, s: (b, 0, 1), pipeline_mode=pl.Buffered(1)),
            pl.BlockSpec((None, TT, D), lambda b, s: (b, 0, 2), pipeline_mode=pl.Buffered(1)),
            pl.BlockSpec((NA_HEADS, NA_DR - 1, GRID_W, 2 * GRID_W), lambda b, s: (0, 0, 0, 0),
                         pipeline_mode=pl.Buffered(1)),
        ],
        out_specs=pl.BlockSpec((None, NA_QT, D), lambda b, s: (b, s, 0)),
        out_shape=jax.ShapeDtypeStruct((BATCH, TT, D), BF16),
        compiler_params=_cp(2),
        name="na_attention",
    )(qkv, qkv, qkv, bias)


def _na_bias_table(rpb):
    qc = np.arange(GRID_W)
    kcol = np.arange(GRID_W)
    cstart = np.clip(qc - NA_KW // 2, 0, GRID_W - NA_KW)
    valid = (kcol[None, :] >= cstart[:, None]) & (kcol[None, :] < cstart[:, None] + NA_KW)
    dc = kcol[None, :] - qc[:, None] + (NA_KW - 1)
    pick = ((dc[None] == np.arange(2 * NA_KW - 1)[:, None, None]) & valid[None]).astype(np.float32)
    tab = jnp.einsum("hdj,jqk->hdqk", rpb.astype(F32), jnp.asarray(pick), precision=HIGHEST)
    tab = tab + jnp.asarray(np.where(valid, 0.0, NEG_INF).astype(np.float32))
    return jnp.concatenate([tab[:, :NA_DR - 1], tab[:, 1:]], axis=-1)


def _route(h, first, rw_ref, rb_ref, tri_ref, cr_ref, cnt_ref):
    logits = lax.dot_general(rw_ref[...], h.astype(BF16), (((1,), (1,)), ((), ())),
                             preferred_element_type=F32)
    sc = jax.nn.sigmoid(logits)
    sel = sc + rb_ref[...]
    sc_r = [sc[e:e + 1, :] for e in range(N_EXPERTS)]
    sel_r = [sel[e:e + 1, :] for e in range(N_EXPERTS)]
    zero_i = jnp.zeros_like(sel_r[0]).astype(jnp.int32)

    best = gi = None
    picks = []
    for g in range(N_GROUPS):
        a = sel_r[g * EPG:(g + 1) * EPG]
        u = sc_r[g * EPG:(g + 1) * EPG]
        gs = None
        for i in range(EPG):
            for j in range(i + 1, EPG):
                pair = a[i] + a[j]
                gs = pair if gs is None else jnp.maximum(gs, pair)
        m1, i1, s1 = a[0], zero_i, u[0]
        for j in range(1, EPG):
            upd = a[j] > m1
            m1 = jnp.where(upd, a[j], m1)
            i1 = jnp.where(upd, j, i1)
            s1 = jnp.where(upd, u[j], s1)
        m2 = i2 = s2 = None
        for j in range(EPG):
            cand = jnp.where(i1 == j, -jnp.inf, a[j])
            if m2 is None:
                m2, i2, s2 = cand, zero_i, u[0]
            else:
                upd = cand > m2
                m2 = jnp.where(upd, cand, m2)
                i2 = jnp.where(upd, j, i2)
                s2 = jnp.where(upd, u[j], s2)
        picks.append((i1 + g * EPG, i2 + g * EPG, s1, s2))
        if best is None:
            best, gi = gs, zero_i
        else:
            upd = gs > best
            best = jnp.where(upd, gs, best)
            gi = jnp.where(upd, g, gi)

    e1, e2, s1, s2 = picks[0]
    for g in range(1, N_GROUPS):
        on = gi == g
        e1 = jnp.where(on, picks[g][0], e1)
        e2 = jnp.where(on, picks[g][1], e2)
        s1 = jnp.where(on, picks[g][2], s1)
        s2 = jnp.where(on, picks[g][3], s2)
    tot = s1 + s2
    w1, w2 = s1 / tot, s2 / tot

    swap = e2 < e1
    la = jnp.where(swap, e2, e1) - gi * EPG
    lb = jnp.where(swap, e1, e2) - gi * EPG
    cls = gi * PAIRS + jnp.where(la == 0, 0, jnp.where(la == 1, 3, 5)) + lb - la - 1
    wa, wb = jnp.where(swap, w2, w1), jnp.where(swap, w1, w2)

    @pl.when(first)
    def _():
        cnt_ref[...] = jnp.zeros_like(cnt_ref)

    oh = (lax.broadcasted_iota(jnp.int32, (cnt_ref.shape[0], cls.shape[1]), 0) == cls).astype(F32)
    before = jnp.dot(oh.astype(BF16), tri_ref[...], preferred_element_type=F32)
    base = cnt_ref[:, 0:1]
    cr_ref[0:1, :] = cls
    cr_ref[1:2, :] = jnp.sum(oh * (base + before), axis=0, keepdims=True).astype(jnp.int32)
    cnt_ref[...] = cnt_ref[...] + jnp.sum(oh, axis=1, keepdims=True)
    return wa, wb


DMA_UNROLL = 8


def _row_copy(src_ref, src_row, dst_ref, dst_row, sem):
    return pltpu.make_async_copy(src_ref.at[pl.ds(src_row, 1)], dst_ref.at[pl.ds(dst_row, 1)], sem)


def _for_each_row(fn, rows=ROW_TILE):
    def group(gidx, carry):
        for u in range(DMA_UNROLL):
            fn(gidx * DMA_UNROLL + u)
        return carry

    lax.fori_loop(0, rows // DMA_UNROLL, group, 0)


def _dispatch_kernel(zrow_ref, dest_ref, h_ref, xs_ref, zbuf, sem, zsem):
    @pl.when(pl.program_id(0) == 0)
    def _():
        zbuf[...] = jnp.zeros_like(zbuf)

        def zero_copy(e):
            start = pl.multiple_of(jnp.maximum(zrow_ref[e], 0), MOE_BM)
            return pltpu.make_async_copy(zbuf, xs_ref.at[pl.ds(start, MOE_BM)], zsem)

        for e in range(MOE_NZ):
            pl.when(zrow_ref[e] >= 0)(lambda e=e: zero_copy(e).start())
        for e in range(MOE_NZ):
            pl.when(zrow_ref[e] >= 0)(lambda e=e: zero_copy(e).wait())

    _for_each_row(lambda r: _row_copy(h_ref, r, xs_ref, dest_ref[0, r], sem).start(), DISPATCH_TILE)
    _for_each_row(lambda r: _row_copy(h_ref, r, xs_ref, dest_ref[0, r], sem).wait(), DISPATCH_TILE)


def _dispatch(h, dest, zrow):
    grid_spec = pltpu.PrefetchScalarGridSpec(
        num_scalar_prefetch=1,
        grid=(NTOK // DISPATCH_TILE,),
        in_specs=[
            pl.BlockSpec((None, 1, DISPATCH_TILE), lambda i, z: (i, 0, 0), memory_space=pltpu.SMEM),
            pl.BlockSpec((DISPATCH_TILE, DX), lambda i, z: (i, 0)),
        ],
        out_specs=pl.BlockSpec(memory_space=pl.ANY),
        scratch_shapes=[pltpu.VMEM((MOE_BM, DX), F32), pltpu.SemaphoreType.DMA(()), pltpu.SemaphoreType.DMA(())],
    )
    return pl.pallas_call(
        _dispatch_kernel,
        grid_spec=grid_spec,
        out_shape=jax.ShapeDtypeStruct((MOE_P, DX), F32),
        compiler_params=pltpu.CompilerParams(dimension_semantics=("arbitrary",), vmem_limit_bytes=VMEM_LIMIT,
                                             has_side_effects=True),
        name="moe_dispatch",
    )(zrow, dest.reshape(NTOK // DISPATCH_TILE, 1, DISPATCH_TILE), h)


def _experts_kernel(blk_ref, ex_ref, nu_ref, x_ref, *rest, which):
    if which:
        prev_ref, wg_ref, wu_ref, wd_ref, y_ref, wgb, wub, wdb = rest
    else:
        wg_ref, wu_ref, wd_ref, y_ref, wgb, wub, wdb = rest
    del blk_ref
    v = pl.program_id(0)
    active = v < nu_ref[0]
    changed = (v == 0) | (ex_ref[v] != ex_ref[jnp.maximum(v - 1, 0)])

    @pl.when(active & changed)
    def _():
        wgb[...] = wg_ref[...].astype(BF16)
        wub[...] = wu_ref[...].astype(BF16)
        wdb[...] = wd_ref[...].astype(BF16)

    @pl.when(active)
    def _():
        x = x_ref[:, :D].astype(BF16)
        g = jnp.dot(x, wgb[...], preferred_element_type=F32)
        u = jnp.dot(x, wub[...], preferred_element_type=F32)
        hh = (_silu(g) * u).astype(BF16)
        y = jnp.dot(hh, wdb[...], preferred_element_type=F32) * x_ref[:, D + which:D + which + 1]
        y_ref[...] = prev_ref[...] + y if which else y

    @pl.when(jnp.logical_not(active))
    def _():
        y_ref[...] = jnp.zeros_like(y_ref)


def _experts(xs, prev, which, blocks, experts, n_used, w_gate, w_up, w_down, depth):
    wspec = pl.BlockSpec((None, None, D, D), lambda v, bl, ex, nu: (depth, ex[v], 0, 0))
    yspec = pl.BlockSpec((MOE_BM, D), lambda v, bl, ex, nu: (bl[v], 0))
    grid_spec = pltpu.PrefetchScalarGridSpec(
        num_scalar_prefetch=3,
        grid=(MOE_NB,),
        in_specs=[pl.BlockSpec((MOE_BM, DX), lambda v, bl, ex, nu: (bl[v], 0))] + ([yspec] if which else [])
        + [wspec, wspec, wspec],
        out_specs=yspec,
        scratch_shapes=[pltpu.VMEM((D, D), BF16)] * 3,
    )
    return pl.pallas_call(
        functools.partial(_experts_kernel, which=which),
        grid_spec=grid_spec,
        out_shape=jax.ShapeDtypeStruct((MOE_P, D), F32),
        compiler_params=_cp(1),
        name="moe_experts_%d" % which,
    )(blocks, experts, n_used, xs, *([prev] if which else []), w_gate, w_up, w_down)


def _combine_kernel(dest_ref, next_ref, yb_ref, x_ref, mod_ref, modn_ref, gn_ref, *rest, final):
    if final:
        out_ref, ybuf, sem = rest
    else:
        xo_ref, h_ref, ybuf, sem = rest
    t = pl.program_id(1)
    step = pl.program_id(0) * TPB + t
    slot = lax.rem(step, 2)

    def gather(slots_ref, buf):
        return lambda r: _row_copy(yb_ref, slots_ref[0, r], ybuf.at[buf], r, sem.at[buf])

    @pl.when(step == 0)
    def _():
        _for_each_row(lambda r: gather(dest_ref, slot)(r).start())

    @pl.when(step + 1 < BATCH * TPB)
    def _():
        _for_each_row(lambda r: gather(next_ref, 1 - slot)(r).start())

    _for_each_row(lambda r: gather(dest_ref, slot)(r).wait())

    def body():
        xn = x_ref[...] + mod_ref[5:6, :] * ybuf[slot]
        if final:
            out_ref[...] = _normmod(xn, gn_ref[...], 0.0, 0.0)
        else:
            xo_ref[...] = xn
            h_ref[...] = _normmod(xn, gn_ref[...], modn_ref[0:1, :], modn_ref[1:2, :]).astype(BF16)

    if final:
        pl.when(t < LAT_TILES)(body)
    else:
        body()


def _combine(yb, dest, xs, mods, depth, g_next, final):
    row = pl.BlockSpec((None, ROW_TILE, D), lambda b, t: (b, t, 0))
    next_depth = depth if final else depth + 1
    if final:
        out_specs = pl.BlockSpec((None, ROW_TILE, D), lambda b, t: (b, jnp.minimum(t, LAT_TILES - 1), 0))
        out_shape = jax.ShapeDtypeStruct((BATCH, SEQ, D), F32)
    else:
        out_specs = [row, row]
        out_shape = [jax.ShapeDtypeStruct((BATCH, TT, D), F32), jax.ShapeDtypeStruct((BATCH, TT, D), BF16)]
    last = BATCH * TPB - 1
    return pl.pallas_call(
        functools.partial(_combine_kernel, final=final),
        grid=(BATCH, TPB),
        in_specs=[
            pl.BlockSpec((None, 1, ROW_TILE), lambda b, t: (b * TPB + t, 0, 0), memory_space=pltpu.SMEM),
            pl.BlockSpec((None, 1, ROW_TILE), lambda b, t: (jnp.minimum(b * TPB + t + 1, last), 0, 0),
                         memory_space=pltpu.SMEM),
            pl.BlockSpec(memory_space=pl.ANY),
            row, _mod_spec(depth), _mod_spec(next_depth),
            g_next[1],
        ],
        out_specs=out_specs,
        out_shape=out_shape,
        scratch_shapes=[pltpu.VMEM((2, ROW_TILE, D), F32), pltpu.SemaphoreType.DMA((2,))],
        compiler_params=_cp(2),
        name="moe_combine",
    )(dest, dest, yb, xs, mods, mods, g_next[0])


def _class_experts():
    pairs = [(a, b) for a in range(EPG) for b in range(a + 1, EPG)]
    lo = [g * EPG + a for g in range(N_GROUPS) for a, _ in pairs]
    hi = [g * EPG + b for g in range(N_GROUPS) for _, b in pairs]
    return jnp.asarray(lo, jnp.int32), jnp.asarray(hi, jnp.int32)


def _moe_plan(cls, rank, counts):
    padded = (counts + MOE_BM - 1) // MOE_BM * MOE_BM
    pend = jnp.cumsum(padded)
    pstart = pend - padded
    cid = jnp.arange(N_CLS, dtype=jnp.int32)
    dest = rank + jnp.sum(jnp.where(cls[None, :] == cid[:, None], pstart[:, None], 0), axis=0)
    dest = dest.astype(jnp.int32).reshape(NTOK // ROW_TILE, 1, ROW_TILE)
    blk = jnp.arange(MOE_NB, dtype=jnp.int32)
    block_c = jnp.minimum(jnp.sum(pend[None, :] <= (blk * MOE_BM)[:, None], axis=1), N_CLS - 1)
    n_used = (pend[-1] // MOE_BM).astype(jnp.int32)
    used = blk < n_used
    lo, hi = _class_experts()
    onehot_c = block_c[:, None] == cid[None, :]
    e_lo = jnp.sum(jnp.where(onehot_c, lo[None, :], 0), axis=1)
    e_hi = jnp.sum(jnp.where(onehot_c, hi[None, :], 0), axis=1)

    def hold_last(e):
        return jnp.where(used, e, jnp.sum(jnp.where(blk == n_used - 1, e, 0))).astype(jnp.int32)

    key = jnp.where(used, e_hi, N_EXPERTS)
    earlier = (key[None, :] < key[:, None]) | ((key[None, :] == key[:, None]) & (blk[None, :] < blk[:, None]))
    pos = jnp.sum(earlier, axis=1)
    at = pos[None, :] == blk[:, None]
    order1 = jnp.sum(jnp.where(at, blk[None, :], 0), axis=1).astype(jnp.int32)
    e1 = jnp.sum(jnp.where(at, e_hi[None, :], 0), axis=1)
    tail = blk[MOE_NB - MOE_TAIL:] * MOE_BM
    zrow = jnp.concatenate([jnp.where(padded > 0, pend - MOE_BM, -1), jnp.where(tail >= pend[-1], tail, -1)])
    return dest, (blk, hold_last(e_lo)), (order1, hold_last(e1)), n_used[None], zrow.astype(jnp.int32)


def _moe(h, routing, xs, mods, depth, g_next, w_gate, w_up, w_down, final):
    cr, cnt = routing
    dest, visit0, visit1, n_used, zrow = _moe_plan(cr[0], cr[1], cnt[:N_CLS, 0].astype(jnp.int32))
    xsorted = _dispatch(h.reshape(NTOK, DX), dest, zrow)
    y0 = _experts(xsorted, None, 0, *visit0, n_used, w_gate, w_up, w_down, depth)
    yb = _experts(xsorted, y0, 1, *visit1, n_used, w_gate, w_up, w_down, depth)
    return _combine(yb, dest, xs, mods, depth, g_next, final)


def _rope_tables():
    nf = RET_DK // 4
    t = np.arange(SEQ)
    inv = ROPE_BASE ** (-np.arange(nf, dtype=np.float32) / nf)
    cos = np.ones((TT, 256), np.float32)
    sin = np.zeros((TT, 256), np.float32)
    for seg, pos in enumerate(((t // GRID_W).astype(np.float32), (t % GRID_W).astype(np.float32))):
        ang = (pos[:, None] * inv[None, :]).astype(np.float32)
        c, s = np.cos(ang), np.sin(ang)
        cos[:SEQ, seg * 128:(seg + 1) * 128] = np.concatenate([c, c], axis=1)
        sin[:SEQ, seg * 128:(seg + 1) * 128] = np.concatenate([-s, s], axis=1)
    ks = np.float32(RET_DK ** -0.5)
    return jnp.asarray(np.stack([cos, cos * ks])), jnp.asarray(np.stack([sin, sin * ks]))


def _decay_tables(decay_logit):
    C = RET_CHUNK
    log_g = jax.nn.log_sigmoid(decay_logit.astype(F32))
    pos = jnp.arange(C, dtype=F32)
    diff = pos[:, None] - pos[None, :]
    lg = log_g[:, :, None, None]
    dm_f = jnp.where(diff >= 0, jnp.exp(jnp.maximum(diff, 0.0)[None, None] * lg), 0.0)
    dm_b = jnp.where(diff <= 0, jnp.exp(jnp.maximum(-diff, 0.0)[None, None] * lg), 0.0)
    dmat = jnp.stack([dm_f[0], dm_b[1]])
    lgc = log_g[:, :, None]
    qdec = jnp.stack([jnp.exp((pos + 1.0)[None, :] * lgc[0]), jnp.exp((C - pos)[None, :] * lgc[1])])
    kdec = jnp.stack([jnp.exp((C - 1.0 - pos)[None, :] * lgc[0]), jnp.exp(pos[None, :] * lgc[1])])
    cdec = jnp.exp(C * log_g).reshape(-1)
    return dmat, qdec[..., None], kdec[..., None], cdec


def kernel(x, c, ctx, c_ctx, w_mod, b_mod, norm_mix, norm_ffn, norm_final, ret_w_in, ret_decay, ret_gn, ret_w_out,
           fn_w_out, na_w_qkv, na_rpb, na_w_out, router_w, router_bias, moe_w_gate, moe_w_up, moe_w_down):
    cvec = jnp.concatenate([c_ctx[None, :], c, jnp.zeros((8 - 1 - BATCH, D), F32)], axis=0)
    mod_all = _modulation(cvec, w_mod, b_mod).reshape(DEPTH, 8, N_MOD, D)
    pad = jnp.zeros((DEPTH, BATCH, 2, 8 - N_MOD, D), F32)
    mods = jnp.stack([jnp.broadcast_to(mod_all[:, 0:1], (DEPTH, BATCH, N_MOD, D)), mod_all[:, 1:1 + BATCH]], axis=2)
    mods = jnp.concatenate([mods, pad], axis=3)

    rope = _rope_tables()
    cs, dft_ctx, dft_f1, dft_f2, dft_tw = _dft_tables()
    route = (router_w.T.astype(BF16), router_bias.reshape(N_EXPERTS, 1).astype(F32))

    norm_mix3 = norm_mix.reshape(DEPTH, 1, D)
    norm_ffn3 = norm_ffn.reshape(DEPTH, 1, D)
    ret_gn3 = ret_gn.reshape(-1, 1, 2 * D)

    xs, h = _embed(x, ctx, mods, norm_mix3)
    out = None
    for i in range(DEPTH):
        kind, j = i % 3, i // 3
        if kind == 0:
            qkvg = _project(h.reshape(NTOK, D), ret_w_in, j, rope).reshape(BATCH, TT, 6 * D)
            dmat, qdec, kdec, cdec = _decay_tables(ret_decay[j])
            o_f, o_b = _retention(qkvg, dmat, qdec, kdec, cdec)
            xs, hf, *routing = _mixer_out("ret", (o_f, o_b, qkvg, ret_gn3), ret_w_out, j, xs, mods, norm_ffn3, i, route)
        elif kind == 1:
            ab = _fn_channel(h, cs)
            f_lat = _fn_cols(_fn_rows(ab, dft_f1, *dft_tw), dft_f2)
            xs, hf, *routing = _mixer_out(
                "split", (f_lat, _fn_ctx(ab, dft_ctx)), fn_w_out, j, xs, mods, norm_ffn3, i, route)
        else:
            qkv = _project(h.reshape(NTOK, D), na_w_qkv, j).reshape(BATCH, TT, 3 * D)
            o = _na_attention(qkv, _na_bias_table(na_rpb[j]))
            xs, hf, *routing = _mixer_out("plain", (o,), na_w_out, j, xs, mods, norm_ffn3, i, route)
        final = i == DEPTH - 1
        if final:
            g_next = (norm_final[None, :], pl.BlockSpec((1, D), lambda b, t: (0, 0)))
        else:
            g_next = (norm_mix3, _layer_spec((1, D), i + 1))
        res = _moe(hf, routing, xs, mods, i, g_next, moe_w_gate, moe_w_up, moe_w_down, final)
        if final:
            out = res
        else:
            xs, h = res
    return out
```

```python
import functools

import numpy as np
import jax
import jax.numpy as jnp
from jax import lax
from jax.experimental import pallas as pl
from jax.experimental.pallas import tpu as pltpu

F32 = jnp.float32
BF16 = jnp.bfloat16
HIGHEST = lax.Precision.HIGHEST

D = 1024
BATCH = 4
SEQ = 4096
CTX = 256
TT = SEQ + CTX
NTOK = BATCH * TT
DEPTH = 4
GRID_W = 64
EPS = 1e-6
NEG_INF = -1e30
N_MOD = 6

RET_HEADS = 4
RET_DK = D // RET_HEADS
RET_DV = 2 * RET_DK
RET_CHUNK = 256
ROPE_BASE = 10000.0

FN_GROUPS = 4
FN_GC = D // FN_GROUPS

NA_HEADS = 16
NA_HD = D // NA_HEADS
NA_KH = 8
NA_KW = 16

N_EXPERTS = 16
N_GROUPS = 4
EPG = N_EXPERTS // N_GROUPS

ROW_TILE = 256
TPB = TT // ROW_TILE
LAT_TILES = SEQ // ROW_TILE
MM_TM = TT // 4
MOE_BM = 256
PAIRS = EPG * (EPG - 1) // 2
N_CLS = N_GROUPS * PAIRS
CLS_PAD = 32
MOE_NB = (NTOK + N_CLS * (MOE_BM - 1) + MOE_BM - 1) // MOE_BM
MOE_P = MOE_NB * MOE_BM
MOE_TAIL = MOE_NB - NTOK // MOE_BM
MOE_NZ = N_CLS + MOE_TAIL
DISPATCH_TILE = 512
DX = D + 128
NA_QT = GRID_W
VMEM_LIMIT = 56 * 1024 * 1024


def _cp(n_axes, vmem=VMEM_LIMIT):
    return pltpu.CompilerParams(dimension_semantics=("arbitrary",) * n_axes, vmem_limit_bytes=vmem)


def _silu(x):
    return x * jax.nn.sigmoid(x)


def _normmod(x, g, shift, scale):
    y = x * lax.rsqrt(jnp.mean(x * x, axis=-1, keepdims=True) + EPS)
    return (y * g) * (1.0 + scale) + shift


def _mod_idx(t):
    return jnp.where(t < LAT_TILES, 1, 0)


def _layer_spec(shape, layer):
    return pl.BlockSpec((None,) + shape, lambda *_: (layer,) + (0,) * len(shape))


def _mod_spec(depth):
    return pl.BlockSpec((None, None, None, 8, D), lambda b, t: (depth, b, _mod_idx(t), 0, 0))


def _mod_kernel(c_ref, w_ref, b_ref, o_ref):
    a = _silu(c_ref[...])
    o_ref[...] = jnp.dot(a, w_ref[...], precision=HIGHEST, preferred_element_type=F32) + b_ref[...]


def _modulation(cvec, w_mod, b_mod):
    return pl.pallas_call(
        _mod_kernel,
        grid=(DEPTH, N_MOD),
        in_specs=[
            pl.BlockSpec((8, D), lambda i, j: (0, 0)),
            pl.BlockSpec((None, D, D), lambda i, j: (i, 0, j)),
            pl.BlockSpec((None, 1, D), lambda i, j: (i, 0, j)),
        ],
        out_specs=pl.BlockSpec((None, 8, D), lambda i, j: (i, 0, j)),
        out_shape=jax.ShapeDtypeStruct((DEPTH, 8, N_MOD * D), F32),
        compiler_params=_cp(2),
        name="modulation",
    )(cvec, w_mod, b_mod.reshape(DEPTH, 1, N_MOD * D))


def _embed_kernel(x_ref, ctx_ref, mod_ref, g_ref, xs_ref, h_ref):
    t = pl.program_id(1)

    def emit(src):
        xs_ref[...] = src
        h_ref[...] = _normmod(src, g_ref[...], mod_ref[0:1, :], mod_ref[1:2, :]).astype(BF16)

    @pl.when(t < LAT_TILES)
    def _():
        emit(x_ref[...])

    @pl.when(t >= LAT_TILES)
    def _():
        emit(ctx_ref[...])


def _embed(x, ctx, mods, g):
    return pl.pallas_call(
        _embed_kernel,
        grid=(BATCH, TPB),
        in_specs=[
            pl.BlockSpec((None, ROW_TILE, D), lambda b, t: (b, jnp.minimum(t, LAT_TILES - 1), 0)),
            pl.BlockSpec((None, CTX, D), lambda b, t: (b, 0, 0)),
            _mod_spec(0),
            _layer_spec((1, D), 0),
        ],
        out_specs=[
            pl.BlockSpec((None, ROW_TILE, D), lambda b, t: (b, t, 0)),
            pl.BlockSpec((None, ROW_TILE, D), lambda b, t: (b, t, 0)),
        ],
        out_shape=[
            jax.ShapeDtypeStruct((BATCH, TT, D), F32),
            jax.ShapeDtypeStruct((BATCH, TT, D), BF16),
        ],
        compiler_params=_cp(2),
        name="embed",
    )(x, ctx, mods, g)


def _rope_store(acc, cos_ref, sin_ref, o_ref):
    cos, sin = cos_ref[...], sin_ref[...]
    for hd in range(acc.shape[1] // RET_DK):
        a = acc[:, hd * RET_DK:hd * RET_DK + 128]
        b = acc[:, hd * RET_DK + 128:(hd + 1) * RET_DK]
        o_ref[:, hd * RET_DK:hd * RET_DK + 128] = (a * cos - b * sin).astype(o_ref.dtype)
        o_ref[:, hd * RET_DK + 128:(hd + 1) * RET_DK] = (a * sin + b * cos).astype(o_ref.dtype)


def _proj_kernel(a_ref, w_ref, *rest, n_rope):
    if n_rope:
        cos_ref, sin_ref, o_ref, wb_ref = rest
    else:
        o_ref, wb_ref = rest
    j = pl.program_id(0)

    @pl.when(pl.program_id(1) == 0)
    def _():
        wb_ref[...] = w_ref[...].astype(BF16)

    acc = jnp.dot(a_ref[...], wb_ref[...], preferred_element_type=F32)
    if n_rope:
        @pl.when(j < n_rope)
        def _():
            _rope_store(acc, cos_ref, sin_ref, o_ref)

        @pl.when(j >= n_rope)
        def _():
            o_ref[...] = acc.astype(o_ref.dtype)
    else:
        o_ref[...] = acc.astype(o_ref.dtype)


def _project(a, w, layer, rope=None):
    _, k, n = w.shape
    tn = D
    n_rope = 0 if rope is None else rope[0].shape[0]
    in_specs = [
        pl.BlockSpec((MM_TM, k), lambda j, i: (i, 0)),
        pl.BlockSpec((None, k, tn), lambda j, i: (layer, 0, j)),
    ]
    args = [a, w]
    if n_rope:
        tspec = pl.BlockSpec((None, MM_TM, 128), lambda j, i: (jnp.minimum(j, n_rope - 1), i % (TT // MM_TM), 0))
        in_specs += [tspec, tspec]
        args += list(rope)
    return pl.pallas_call(
        functools.partial(_proj_kernel, n_rope=n_rope),
        grid=(n // tn, NTOK // MM_TM),
        in_specs=in_specs,
        out_specs=pl.BlockSpec((MM_TM, tn), lambda j, i: (i, j)),
        out_shape=jax.ShapeDtypeStruct((NTOK, n), BF16),
        scratch_shapes=[pltpu.VMEM((k, tn), BF16)],
        compiler_params=_cp(2),
        name="project",
    )(*args)


N_CHUNKS = TT // RET_CHUNK
LAT_CHUNKS = SEQ // RET_CHUNK
CTX_CHUNKS = CTX // RET_CHUNK


def _fwd_chunk(n):
    return jnp.where(n < CTX_CHUNKS, LAT_CHUNKS + n, n - CTX_CHUNKS)


def _bwd_chunk(n):
    return N_CHUNKS - 1 - n


def _ret_kernel(cdec_ref, qf, kf, vf, qb, kb, vb, dmat_ref, qdec_ref, kdec_ref, of_ref, ob_ref, s_ref):
    @pl.when(pl.program_id(1) == 0)
    def _():
        s_ref[...] = jnp.zeros_like(s_ref)

    for d, (q_r, k_r, v_r, o_r) in enumerate(((qf, kf, vf, of_ref), (qb, kb, vb, ob_ref))):
        for h in range(RET_HEADS):
            q = q_r[:, h * RET_DK:(h + 1) * RET_DK]
            k = k_r[:, h * RET_DK:(h + 1) * RET_DK]
            v = v_r[:, h * RET_DV:(h + 1) * RET_DV]
            att = lax.dot_general(q, k, (((1,), (1,)), ((), ())), preferred_element_type=F32) * dmat_ref[d, h]
            s = s_ref[d, h]
            qd = (q.astype(F32) * qdec_ref[d, h]).astype(BF16)
            o = (jnp.dot(att.astype(BF16), v, preferred_element_type=F32)
                 + jnp.dot(qd, s.astype(BF16), preferred_element_type=F32))
            kd = (k.astype(F32) * kdec_ref[d, h]).astype(BF16)
            s_ref[d, h] = s * cdec_ref[d * RET_HEADS + h] + lax.dot_general(
                kd, v, (((0,), (0,)), ((), ())), preferred_element_type=F32)
            o_r[:, h * RET_DV:(h + 1) * RET_DV] = o.astype(o_r.dtype)


def _retention(qkvg, dmat, qdec, kdec, cdec):
    C = RET_CHUNK

    def spec(width, col, chunk_fn):
        return pl.BlockSpec((None, C, width), lambda b, n, cd: (b, chunk_fn(n), col))

    ospec_f = pl.BlockSpec((None, C, 2 * D), lambda b, n, cd: (b, _fwd_chunk(n), 0))
    ospec_b = pl.BlockSpec((None, C, 2 * D), lambda b, n, cd: (b, _bwd_chunk(n), 0))
    full = lambda shape: pl.BlockSpec(shape, lambda b, n, cd: (0,) * len(shape))
    grid_spec = pltpu.PrefetchScalarGridSpec(
        num_scalar_prefetch=1,
        grid=(BATCH, N_CHUNKS),
        in_specs=[
            spec(D, 0, _fwd_chunk), spec(D, 1, _fwd_chunk), spec(2 * D, 1, _fwd_chunk),
            spec(D, 0, _bwd_chunk), spec(D, 1, _bwd_chunk), spec(2 * D, 1, _bwd_chunk),
            full((2, RET_HEADS, C, C)), full((2, RET_HEADS, C, 1)), full((2, RET_HEADS, C, 1)),
        ],
        out_specs=[ospec_f, ospec_b],
        scratch_shapes=[pltpu.VMEM((2, RET_HEADS, RET_DK, RET_DV), F32)],
    )
    return pl.pallas_call(
        _ret_kernel,
        grid_spec=grid_spec,
        out_shape=[jax.ShapeDtypeStruct((BATCH, TT, 2 * D), BF16)] * 2,
        compiler_params=_cp(2),
        name="retention",
    )(cdec, qkvg, qkvg, qkvg, qkvg, qkvg, qkvg, dmat, qdec, kdec)


def _mixer_out_kernel(*refs, kind):
    n_src = {"ret": 4, "split": 2, "plain": 1}[kind]
    srcs, (w_ref, x_ref, mod_ref, nf_ref), route_in = refs[:n_src], refs[n_src:n_src + 4], refs[n_src + 4:n_src + 7]
    xo_ref, h_ref, cr_ref, cnt_ref, wb_ref = refs[n_src + 7:]
    first = (pl.program_id(0) == 0) & (pl.program_id(1) == 0)

    @pl.when(first)
    def _():
        wb_ref[...] = w_ref[...].astype(BF16)

    if kind == "ret":
        of_ref, ob_ref, g_ref, gn_ref = srcs
        o = of_ref[...].astype(F32) + ob_ref[...].astype(F32)
        parts = []
        for h in range(RET_HEADS):
            oh = o[:, h * RET_DV:(h + 1) * RET_DV]
            mu = jnp.mean(oh, axis=-1, keepdims=True)
            ctr = oh - mu
            var = jnp.mean(ctr * ctr, axis=-1, keepdims=True)
            parts.append(ctr * lax.rsqrt(var + EPS))
        on = jnp.concatenate(parts, axis=-1) * gn_ref[...]
        a = (on * _silu(g_ref[...].astype(F32))).astype(BF16)
    elif kind == "split":
        a = jnp.where(pl.program_id(1) < LAT_TILES, srcs[0][...], srcs[1][...])
    else:
        a = srcs[0][...]
    y = jnp.dot(a, wb_ref[...], preferred_element_type=F32)
    xn = x_ref[...] + mod_ref[2:3, :] * y
    xo_ref[...] = xn
    h = _normmod(xn, nf_ref[...], mod_ref[3:4, :], mod_ref[4:5, :])
    wa, wb = _route(h, first, *route_in, cr_ref, cnt_ref)
    sub = lax.broadcasted_iota(jnp.int32, (DX - D, ROW_TILE), 0)
    h_ref[:, :D] = h
    h_ref[:, D:] = jnp.where(sub == 0, wa, jnp.where(sub == 1, wb, 0.0)).T


def _mixer_out(kind, srcs, w, layer, xs, mods, norm_ffn, depth, route):
    k = w.shape[1]
    row = lambda width, col=0: pl.BlockSpec((None, ROW_TILE, width), lambda b, t: (b, t, col))
    const = lambda shape: pl.BlockSpec(shape, lambda b, t: (0,) * len(shape))
    per_tok = pl.BlockSpec((2, ROW_TILE), lambda b, t: (0, b * TPB + t))
    tri = jnp.asarray(np.triu(np.ones((ROW_TILE, ROW_TILE), np.float32), 1), BF16)
    if kind == "ret":
        src_specs = [row(2 * D), row(2 * D), row(2 * D, 2), _layer_spec((1, 2 * D), layer)]
    elif kind == "split":
        src_specs = [pl.BlockSpec((None, ROW_TILE, k), lambda b, t: (b, jnp.minimum(t, LAT_TILES - 1), 0)),
                     pl.BlockSpec((None, CTX, k), lambda b, t: (b, 0, 0))]
    else:
        src_specs = [row(k)]
    return pl.pallas_call(
        functools.partial(_mixer_out_kernel, kind=kind),
        grid=(BATCH, TPB),
        in_specs=src_specs + [_layer_spec((k, D), layer), row(D), _mod_spec(depth), _layer_spec((1, D), depth),
                              const((N_EXPERTS, D)), const((N_EXPERTS, 1)), const((ROW_TILE, ROW_TILE))],
        out_specs=[row(D), row(DX), per_tok, const((CLS_PAD, 128))],
        out_shape=[jax.ShapeDtypeStruct((BATCH, TT, D), F32), jax.ShapeDtypeStruct((BATCH, TT, DX), F32),
                   jax.ShapeDtypeStruct((2, NTOK), jnp.int32), jax.ShapeDtypeStruct((CLS_PAD, 128), F32)],
        scratch_shapes=[pltpu.VMEM((k, D), BF16)],
        compiler_params=_cp(2),
        name="mixer_out_" + kind,
    )(*srcs, w, xs, mods, norm_ffn, *route, tri)


def _fn_channel_kernel(h_ref, cs_ref, o_ref):
    for g in range(FN_GROUPS):
        r = jnp.dot(h_ref[:, g * FN_GC:(g + 1) * FN_GC], cs_ref[...], preferred_element_type=F32)
        o_ref[0, :, g * FN_GC:(g + 1) * FN_GC] = r[:, :FN_GC].astype(BF16)
        o_ref[1, :, g * FN_GC:(g + 1) * FN_GC] = r[:, FN_GC:].astype(BF16)


def _fn_channel(h, cs):
    return pl.pallas_call(
        _fn_channel_kernel,
        grid=(BATCH, TPB),
        in_specs=[
            pl.BlockSpec((None, ROW_TILE, D), lambda b, t: (b, t, 0)),
            pl.BlockSpec((FN_GC, 2 * FN_GC), lambda b, t: (0, 0)),
        ],
        out_specs=pl.BlockSpec((None, 2, ROW_TILE, D), lambda b, t: (b, 0, t, 0)),
        out_shape=jax.ShapeDtypeStruct((BATCH, 2, TT, D), BF16),
        compiler_params=_cp(2),
        name="fourier_channels",
    )(h, cs)


FN_R = 64
FN_CB = 16


def _fn_rows_kernel(ab_ref, f1_ref, twc_ref, tws_ref, y_ref, u32):
    u32[...] = ab_ref[...].astype(F32)
    for j in range(FN_CB):
        st = jnp.concatenate([u32[0, :, j, :], u32[1, :, j, :]], axis=0).astype(BF16)
        y = jnp.dot(f1_ref[...], st, preferred_element_type=F32)
        yr, yi = y[:FN_R], y[FN_R:]
        cw, sw = twc_ref[j], tws_ref[j]
        y_ref[0, j] = (yr * cw - yi * sw).astype(BF16)
        y_ref[1, j] = (yi * cw + yr * sw).astype(BF16)


def _fn_rows(ab, f1, twc, tws):
    ab5 = ab.reshape(BATCH, 2, TT // FN_R, FN_R, D)
    return pl.pallas_call(
        _fn_rows_kernel,
        grid=(BATCH, FN_R // FN_CB),
        in_specs=[
            pl.BlockSpec((None, 2, FN_R, FN_CB, D), lambda b, cb: (b, 0, 0, cb, 0)),
            pl.BlockSpec((2 * FN_R, 2 * FN_R), lambda b, cb: (0, 0)),
            pl.BlockSpec((FN_CB, FN_R, 1), lambda b, cb: (cb, 0, 0)),
            pl.BlockSpec((FN_CB, FN_R, 1), lambda b, cb: (cb, 0, 0)),
        ],
        out_specs=pl.BlockSpec((None, 2, FN_CB, FN_R, D), lambda b, cb: (b, 0, cb, 0, 0)),
        out_shape=jax.ShapeDtypeStruct((BATCH, 2, FN_R, FN_R, D), BF16),
        scratch_shapes=[pltpu.VMEM((2, FN_R, FN_CB, D), F32)],
        compiler_params=_cp(2),
        name="fourier_rows",
    )(ab5, f1, twc, tws)


def _fn_cols_kernel(y_ref, f2_ref, o_ref, y32, z32):
    y32[...] = y_ref[...].astype(F32)
    for j in range(FN_CB):
        st = jnp.concatenate([y32[0, :, j, :], y32[1, :, j, :]], axis=0).astype(BF16)
        z32[:, j, :] = jnp.dot(f2_ref[...], st, preferred_element_type=F32)
    o_ref[...] = z32[...].astype(BF16)


def _fn_cols(y, f2):
    out = pl.pallas_call(
        _fn_cols_kernel,
        grid=(BATCH, FN_R // FN_CB),
        in_specs=[
            pl.BlockSpec((None, 2, FN_R, FN_CB, D), lambda b, kb: (b, 0, 0, kb, 0)),
            pl.BlockSpec((FN_R, 2 * FN_R), lambda b, kb: (0, 0)),
        ],
        out_specs=pl.BlockSpec((None, FN_R, FN_CB, D), lambda b, kb: (b, 0, kb, 0)),
        out_shape=jax.ShapeDtypeStruct((BATCH, FN_R, FN_R, D), BF16),
        scratch_shapes=[pltpu.VMEM((2, FN_R, FN_CB, D), F32), pltpu.VMEM((FN_R, FN_CB, D), F32)],
        compiler_params=_cp(2),
        name="fourier_cols",
    )(y, f2)
    return out.reshape(BATCH, SEQ, D)


def _fn_ctx_kernel(ab_ref, m_ref, o_ref):
    st = jnp.concatenate([ab_ref[0], ab_ref[1]], axis=0)
    o_ref[...] = jnp.dot(m_ref[...], st, preferred_element_type=F32).astype(BF16)


def _fn_ctx(ab, m_ctx):
    return pl.pallas_call(
        _fn_ctx_kernel,
        grid=(BATCH,),
        in_specs=[
            pl.BlockSpec((None, 2, CTX, D), lambda b: (b, 0, SEQ // CTX, 0)),
            pl.BlockSpec((CTX, 2 * CTX), lambda b: (0, 0)),
        ],
        out_specs=pl.BlockSpec((None, CTX, D), lambda b: (b, 0, 0)),
        out_shape=jax.ShapeDtypeStruct((BATCH, CTX, D), BF16),
        compiler_params=_cp(1),
        name="fourier_ctx",
    )(ab, m_ctx)


def _dft_tables():
    def cs_pair(n, scale):
        i = np.arange(n)
        ang = 2.0 * np.pi * ((i[:, None] * i[None, :]) % n) / n
        return np.cos(ang) * scale, np.sin(ang) * scale

    assert FN_GC == CTX
    c, s = cs_pair(FN_GC, FN_GC ** -0.5)
    cs = np.concatenate([c, s], axis=1)
    m_ctx = np.concatenate([c, -s], axis=1)
    c, s = cs_pair(FN_R, FN_R ** -0.5)
    f1 = np.block([[c, -s], [s, c]])
    f2 = np.concatenate([c, -s], axis=1)
    i = np.arange(FN_R)
    phi = 2.0 * np.pi * (i[:, None] * i[None, :]) / SEQ
    tw = [jnp.asarray(f(phi)[..., None], F32) for f in (np.cos, np.sin)]
    return jnp.asarray(cs, BF16), jnp.asarray(m_ctx, BF16), jnp.asarray(f1, BF16), jnp.asarray(f2, BF16), tw


NA_ROWS = SEQ // GRID_W
NA_STEPS = TT // NA_QT
NA_NK = NA_KH * GRID_W
NA_DR = 2 * NA_KH - 1


def _na_row_start(step):
    r = jnp.minimum(step, NA_ROWS - 1)
    return r, jnp.clip(r - NA_KH // 2, 0, NA_ROWS - NA_KH)


def _na_bias_idx(step):
    r, rs = _na_row_start(step)
    return rs - r + (NA_KH - 1)


def _na_kernel(q_ref, k_ref, v_ref, bias_ref, o_ref):
    step = pl.program_id(1)
    lane = lax.broadcasted_iota(jnp.int32, (NA_QT, 128), 1)
    nt = (((1,), (1,)), ((), ()))
    scale = NA_HD ** -0.5

    def attend(start, d0=None):
        pairs = [slice(p * 128, (p + 1) * 128) for p in range(NA_HEADS // 2)]
        scores = []
        for p, sl in enumerate(pairs):
            qp = q_ref[:, sl] * scale
            zero = jnp.zeros_like(qp)
            q2 = jnp.concatenate([jnp.where(lane < NA_HD, qp, zero), jnp.where(lane >= NA_HD, qp, zero)], axis=0)
            s_c = lax.dot_general(q2, k_ref[SEQ:TT, sl], nt, preferred_element_type=F32)
            s_w = None
            if start is not None:
                bias = jnp.concatenate(
                    [jnp.concatenate([bias_ref[2 * p + hh, d0 + 2 * j2] for j2 in range(NA_KH // 2)], axis=1)
                     for hh in range(2)], axis=0)
                s_w = lax.dot_general(q2, k_ref[pl.ds(start, NA_NK), sl], nt, preferred_element_type=F32) + bias
            scores.append((s_c, s_w))
        probs = []
        for s_c, s_w in scores:
            m = jnp.max(s_c, axis=-1, keepdims=True)
            if s_w is not None:
                m = jnp.maximum(m, jnp.max(s_w, axis=-1, keepdims=True))
            p_c = jnp.exp(s_c - m)
            l = jnp.sum(p_c, axis=-1, keepdims=True)
            p_w = None
            if s_w is not None:
                p_w = jnp.exp(s_w - m)
                l = l + jnp.sum(p_w, axis=-1, keepdims=True)
                p_w = p_w.astype(BF16)
            probs.append((p_c.astype(BF16), p_w, l))
        for sl, (p_c, p_w, l) in zip(pairs, probs):
            acc = jnp.dot(p_c, v_ref[SEQ:TT, sl], preferred_element_type=F32)
            if p_w is not None:
                acc = acc + jnp.dot(p_w, v_ref[pl.ds(start, NA_NK), sl], preferred_element_type=F32)
            o2 = acc / l
            o_ref[:, sl] = jnp.where(lane < NA_HD, o2[:NA_QT], o2[NA_QT:]).astype(BF16)

    @pl.when(step < NA_ROWS)
    def _():
        _, rs = _na_row_start(step)
        attend(pl.multiple_of(rs * GRID_W, GRID_W), _na_bias_idx(step))

    @pl.when(step >= NA_ROWS)
    def _():
        attend(None)


def _na_attention(qkv, bias):
    return pl.pallas_call(
        _na_kernel,
        grid=(BATCH, NA_STEPS),
        in_specs=[
            pl.BlockSpec((None, NA_QT, D), lambda b, s: (b, s, 0)),
            pl.BlockSpec((None, TT, D), lambda b, s: (b, 0, 1), pipeline_mode=pl.Buffered(1)),
            pl.BlockSpec((None, TT, D), lambda b, s: (b, 0, 2), pipeline_mode=pl.Buffered(1)),
            pl.BlockSpec((NA_HEADS, NA_DR - 1, GRID_W, 2 * GRID_W), lambda b, s: (0, 0, 0, 0),
                         pipeline_mode=pl.Buffered(1)),
        ],
        out_specs=pl.BlockSpec((None, NA_QT, D), lambda b, s: (b, s, 0)),
        out_shape=jax.ShapeDtypeStruct((BATCH, TT, D), BF16),
        compiler_params=_cp(2),
        name="na_attention",
    )(qkv, qkv, qkv, bias)


def _na_bias_table(rpb):
    qc = np.arange(GRID_W)
    kcol = np.arange(GRID_W)
    cstart = np.clip(qc - NA_KW // 2, 0, GRID_W - NA_KW)
    valid = (kcol[None, :] >= cstart[:, None]) & (kcol[None, :] < cstart[:, None] + NA_KW)
    dc = kcol[None, :] - qc[:, None] + (NA_KW - 1)
    pick = ((dc[None] == np.arange(2 * NA_KW - 1)[:, None, None]) & valid[None]).astype(np.float32)
    tab = jnp.einsum("hdj,jqk->hdqk", rpb.astype(F32), jnp.asarray(pick), precision=HIGHEST)
    tab = tab + jnp.asarray(np.where(valid, 0.0, NEG_INF).astype(np.float32))
    return jnp.concatenate([tab[:, :NA_DR - 1], tab[:, 1:]], axis=-1)


def _route(h, first, rw_ref, rb_ref, tri_ref, cr_ref, cnt_ref):
    logits = lax.dot_general(rw_ref[...], h.astype(BF16), (((1,), (1,)), ((), ())),
                             preferred_element_type=F32)
    sc = jax.nn.sigmoid(logits)
    sel = sc + rb_ref[...]
    sc_r = [sc[e:e + 1, :] for e in range(N_EXPERTS)]
    sel_r = [sel[e:e + 1, :] for e in range(N_EXPERTS)]
    zero_i = jnp.zeros_like(sel_r[0]).astype(jnp.int32)

    best = gi = None
    picks = []
    for g in range(N_GROUPS):
        a = sel_r[g * EPG:(g + 1) * EPG]
        u = sc_r[g * EPG:(g + 1) * EPG]
        gs = None
        for i in range(EPG):
            for j in range(i + 1, EPG):
                pair = a[i] + a[j]
                gs = pair if gs is None else jnp.maximum(gs, pair)
        m1, i1, s1 = a[0], zero_i, u[0]
        for j in range(1, EPG):
            upd = a[j] > m1
            m1 = jnp.where(upd, a[j], m1)
            i1 = jnp.where(upd, j, i1)
            s1 = jnp.where(upd, u[j], s1)
        m2 = i2 = s2 = None
        for j in range(EPG):
            cand = jnp.where(i1 == j, -jnp.inf, a[j])
            if m2 is None:
                m2, i2, s2 = cand, zero_i, u[0]
            else:
                upd = cand > m2
                m2 = jnp.where(upd, cand, m2)
                i2 = jnp.where(upd, j, i2)
                s2 = jnp.where(upd, u[j], s2)
        picks.append((i1 + g * EPG, i2 + g * EPG, s1, s2))
        if best is None:
            best, gi = gs, zero_i
        else:
            upd = gs > best
            best = jnp.where(upd, gs, best)
            gi = jnp.where(upd, g, gi)

    e1, e2, s1, s2 = picks[0]
    for g in range(1, N_GROUPS):
        on = gi == g
        e1 = jnp.where(on, picks[g][0], e1)
        e2 = jnp.where(on, picks[g][1], e2)
        s1 = jnp.where(on, picks[g][2], s1)
        s2 = jnp.where(on, picks[g][3], s2)
    tot = s1 + s2
    w1, w2 = s1 / tot, s2 / tot

    swap = e2 < e1
    la = jnp.where(swap, e2, e1) - gi * EPG
    lb = jnp.where(swap, e1, e2) - gi * EPG
    cls = gi * PAIRS + jnp.where(la == 0, 0, jnp.where(la == 1, 3, 5)) + lb - la - 1
    wa, wb = jnp.where(swap, w2, w1), jnp.where(swap, w1, w2)

    @pl.when(first)
    def _():
        cnt_ref[...] = jnp.zeros_like(cnt_ref)

    oh = (lax.broadcasted_iota(jnp.int32, (cnt_ref.shape[0], cls.shape[1]), 0) == cls).astype(F32)
    before = jnp.dot(oh.astype(BF16), tri_ref[...], preferred_element_type=F32)
    base = cnt_ref[:, 0:1]
    cr_ref[0:1, :] = cls
    cr_ref[1:2, :] = jnp.sum(oh * (base + before), axis=0, keepdims=True).astype(jnp.int32)
    cnt_ref[...] = cnt_ref[...] + jnp.sum(oh, axis=1, keepdims=True)
    return wa, wb


DMA_UNROLL = 8


def _row_copy(src_ref, src_row, dst_ref, dst_row, sem):
    return pltpu.make_async_copy(src_ref.at[pl.ds(src_row, 1)], dst_ref.at[pl.ds(dst_row, 1)], sem)


def _for_each_row(fn, rows=ROW_TILE):
    def group(gidx, carry):
        for u in range(DMA_UNROLL):
            fn(gidx * DMA_UNROLL + u)
        return carry

    lax.fori_loop(0, rows // DMA_UNROLL, group, 0)


def _dispatch_kernel(zrow_ref, dest_ref, h_ref, xs_ref, zbuf, sem, zsem):
    @pl.when(pl.program_id(0) == 0)
    def _():
        zbuf[...] = jnp.zeros_like(zbuf)

        def zero_copy(e):
            start = pl.multiple_of(jnp.maximum(zrow_ref[e], 0), MOE_BM)
            return pltpu.make_async_copy(zbuf, xs_ref.at[pl.ds(start, MOE_BM)], zsem)

        for e in range(MOE_NZ):
            pl.when(zrow_ref[e] >= 0)(lambda e=e: zero_copy(e).start())
        for e in range(MOE_NZ):
            pl.when(zrow_ref[e] >= 0)(lambda e=e: zero_copy(e).wait())

    _for_each_row(lambda r: _row_copy(h_ref, r, xs_ref, dest_ref[0, r], sem).start(), DISPATCH_TILE)
    _for_each_row(lambda r: _row_copy(h_ref, r, xs_ref, dest_ref[0, r], sem).wait(), DISPATCH_TILE)


def _dispatch(h, dest, zrow):
    grid_spec = pltpu.PrefetchScalarGridSpec(
        num_scalar_prefetch=1,
        grid=(NTOK // DISPATCH_TILE,),
        in_specs=[
            pl.BlockSpec((None, 1, DISPATCH_TILE), lambda i, z: (i, 0, 0), memory_space=pltpu.SMEM),
            pl.BlockSpec((DISPATCH_TILE, DX), lambda i, z: (i, 0)),
        ],
        out_specs=pl.BlockSpec(memory_space=pl.ANY),
        scratch_shapes=[pltpu.VMEM((MOE_BM, DX), F32), pltpu.SemaphoreType.DMA(()), pltpu.SemaphoreType.DMA(())],
    )
    return pl.pallas_call(
        _dispatch_kernel,
        grid_spec=grid_spec,
        out_shape=jax.ShapeDtypeStruct((MOE_P, DX), F32),
        compiler_params=pltpu.CompilerParams(dimension_semantics=("arbitrary",), vmem_limit_bytes=VMEM_LIMIT,
                                             has_side_effects=True),
        name="moe_dispatch",
    )(zrow, dest.reshape(NTOK // DISPATCH_TILE, 1, DISPATCH_TILE), h)


def _experts_kernel(blk_ref, ex_ref, nv_ref, x_ref, *rest, which):
    if which:
        prev_ref, wg_ref, wu_ref, wd_ref, y_ref, wgb, wub, wdb = rest
    else:
        wg_ref, wu_ref, wd_ref, y_ref, wgb, wub, wdb = rest
    del blk_ref
    v = pl.program_id(0)
    nv = nv_ref[v]
    changed = (v == 0) | (ex_ref[v] != ex_ref[jnp.maximum(v - 1, 0)])
    half = MOE_BM // 2

    @pl.when((nv > 0) & changed)
    def _():
        wgb[...] = wg_ref[...].astype(BF16)
        wub[...] = wu_ref[...].astype(BF16)
        wdb[...] = wd_ref[...].astype(BF16)

    def ffn(rows):
        x = x_ref[:rows, :D].astype(BF16)
        g = jnp.dot(x, wgb[...], preferred_element_type=F32)
        u = jnp.dot(x, wub[...], preferred_element_type=F32)
        hh = (_silu(g) * u).astype(BF16)
        y = jnp.dot(hh, wdb[...], preferred_element_type=F32) * x_ref[:rows, D + which:D + which + 1]
        return prev_ref[:rows, :] + y if which else y

    @pl.when(nv > half)
    def _():
        y_ref[...] = ffn(MOE_BM)

    @pl.when((nv > 0) & (nv <= half))
    def _():
        y_ref[:half, :] = ffn(half)
        y_ref[half:, :] = jnp.zeros((MOE_BM - half, D), F32)

    @pl.when(nv == 0)
    def _():
        y_ref[...] = jnp.zeros_like(y_ref)


def _experts(xs, prev, which, blocks, experts, n_valid, w_gate, w_up, w_down, depth):
    wspec = pl.BlockSpec((None, None, D, D), lambda v, bl, ex, nu: (depth, ex[v], 0, 0))
    yspec = pl.BlockSpec((MOE_BM, D), lambda v, bl, ex, nu: (bl[v], 0))
    grid_spec = pltpu.PrefetchScalarGridSpec(
        num_scalar_prefetch=3,
        grid=(MOE_NB,),
        in_specs=[pl.BlockSpec((MOE_BM, DX), lambda v, bl, ex, nu: (bl[v], 0))] + ([yspec] if which else [])
        + [wspec, wspec, wspec],
        out_specs=yspec,
        scratch_shapes=[pltpu.VMEM((D, D), BF16)] * 3,
    )
    return pl.pallas_call(
        functools.partial(_experts_kernel, which=which),
        grid_spec=grid_spec,
        out_shape=jax.ShapeDtypeStruct((MOE_P, D), F32),
        compiler_params=_cp(1),
        name="moe_experts_%d" % which,
    )(blocks, experts, n_valid, xs, *([prev] if which else []), w_gate, w_up, w_down)


def _combine_kernel(dest_ref, next_ref, yb_ref, x_ref, mod_ref, modn_ref, gn_ref, *rest, final):
    if final:
        out_ref, ybuf, sem = rest
    else:
        xo_ref, h_ref, ybuf, sem = rest
    t = pl.program_id(1)
    step = pl.program_id(0) * TPB + t
    slot = lax.rem(step, 2)

    def gather(slots_ref, buf):
        return lambda r: _row_copy(yb_ref, slots_ref[0, r], ybuf.at[buf], r, sem.at[buf])

    @pl.when(step == 0)
    def _():
        _for_each_row(lambda r: gather(dest_ref, slot)(r).start())

    @pl.when(step + 1 < BATCH * TPB)
    def _():
        _for_each_row(lambda r: gather(next_ref, 1 - slot)(r).start())

    _for_each_row(lambda r: gather(dest_ref, slot)(r).wait())

    def body():
        xn = x_ref[...] + mod_ref[5:6, :] * ybuf[slot]
        if final:
            out_ref[...] = _normmod(xn, gn_ref[...], 0.0, 0.0)
        else:
            xo_ref[...] = xn
            h_ref[...] = _normmod(xn, gn_ref[...], modn_ref[0:1, :], modn_ref[1:2, :]).astype(BF16)

    if final:
        pl.when(t < LAT_TILES)(body)
    else:
        body()


def _combine(yb, dest, xs, mods, depth, g_next, final):
    row = pl.BlockSpec((None, ROW_TILE, D), lambda b, t: (b, t, 0))
    next_depth = depth if final else depth + 1
    if final:
        out_specs = pl.BlockSpec((None, ROW_TILE, D), lambda b, t: (b, jnp.minimum(t, LAT_TILES - 1), 0))
        out_shape = jax.ShapeDtypeStruct((BATCH, SEQ, D), F32)
    else:
        out_specs = [row, row]
        out_shape = [jax.ShapeDtypeStruct((BATCH, TT, D), F32), jax.ShapeDtypeStruct((BATCH, TT, D), BF16)]
    last = BATCH * TPB - 1
    return pl.pallas_call(
        functools.partial(_combine_kernel, final=final),
        grid=(BATCH, TPB),
        in_specs=[
            pl.BlockSpec((None, 1, ROW_TILE), lambda b, t: (b * TPB + t, 0, 0), memory_space=pltpu.SMEM),
            pl.BlockSpec((None, 1, ROW_TILE), lambda b, t: (jnp.minimum(b * TPB + t + 1, last), 0, 0),
                         memory_space=pltpu.SMEM),
            pl.BlockSpec(memory_space=pl.ANY),
            row, _mod_spec(depth), _mod_spec(next_depth),
            g_next[1],
        ],
        out_specs=out_specs,
        out_shape=out_shape,
        scratch_shapes=[pltpu.VMEM((2, ROW_TILE, D), F32), pltpu.SemaphoreType.DMA((2,))],
        compiler_params=_cp(2),
        name="moe_combine",
    )(dest, dest, yb, xs, mods, mods, g_next[0])


def _class_experts():
    pairs = [(a, b) for a in range(EPG) for b in range(a + 1, EPG)]
    lo = [g * EPG + a for g in range(N_GROUPS) for a, _ in pairs]
    hi = [g * EPG + b for g in range(N_GROUPS) for _, b in pairs]
    return jnp.asarray(lo, jnp.int32), jnp.asarray(hi, jnp.int32)


def _moe_plan(cls, rank, counts):
    padded = (counts + MOE_BM - 1) // MOE_BM * MOE_BM
    pend = jnp.cumsum(padded)
    pstart = pend - padded
    cid = jnp.arange(N_CLS, dtype=jnp.int32)
    dest = rank + jnp.sum(jnp.where(cls[None, :] == cid[:, None], pstart[:, None], 0), axis=0)
    dest = dest.astype(jnp.int32).reshape(NTOK // ROW_TILE, 1, ROW_TILE)
    blk = jnp.arange(MOE_NB, dtype=jnp.int32)
    block_c = jnp.minimum(jnp.sum(pend[None, :] <= (blk * MOE_BM)[:, None], axis=1), N_CLS - 1)
    n_used = (pend[-1] // MOE_BM).astype(jnp.int32)
    used = blk < n_used
    lo, hi = _class_experts()
    onehot_c = block_c[:, None] == cid[None, :]
    e_lo = jnp.sum(jnp.where(onehot_c, lo[None, :], 0), axis=1)
    e_hi = jnp.sum(jnp.where(onehot_c, hi[None, :], 0), axis=1)

    def hold_last(e):
        return jnp.where(used, e, jnp.sum(jnp.where(blk == n_used - 1, e, 0))).astype(jnp.int32)

    key = jnp.where(used, e_hi, N_EXPERTS)
    earlier = (key[None, :] < key[:, None]) | ((key[None, :] == key[:, None]) & (blk[None, :] < blk[:, None]))
    pos = jnp.sum(earlier, axis=1)
    at = pos[None, :] == blk[:, None]
    order1 = jnp.sum(jnp.where(at, blk[None, :], 0), axis=1).astype(jnp.int32)
    e1 = jnp.sum(jnp.where(at, e_hi[None, :], 0), axis=1)
    class_end = jnp.sum(jnp.where(onehot_c, (pstart + counts)[None, :], 0), axis=1)
    nv0 = jnp.where(used, jnp.clip(class_end - blk * MOE_BM, 0, MOE_BM), 0).astype(jnp.int32)
    nv1 = jnp.sum(jnp.where(at, nv0[None, :], 0), axis=1).astype(jnp.int32)
    tail = blk[MOE_NB - MOE_TAIL:] * MOE_BM
    zrow = jnp.concatenate([jnp.where(padded > 0, pend - MOE_BM, -1), jnp.where(tail >= pend[-1], tail, -1)])
    return dest, (blk, hold_last(e_lo), nv0), (order1, hold_last(e1), nv1), zrow.astype(jnp.int32)


def _moe(h, routing, xs, mods, depth, g_next, w_gate, w_up, w_down, final):
    cr, cnt = routing
    dest, visit0, visit1, zrow = _moe_plan(cr[0], cr[1], cnt[:N_CLS, 0].astype(jnp.int32))
    xsorted = _dispatch(h.reshape(NTOK, DX), dest, zrow)
    y0 = _experts(xsorted, None, 0, *visit0, w_gate, w_up, w_down, depth)
    yb = _experts(xsorted, y0, 1, *visit1, w_gate, w_up, w_down, depth)
    return _combine(yb, dest, xs, mods, depth, g_next, final)


def _rope_tables():
    nf = RET_DK // 4
    t = np.arange(SEQ)
    inv = ROPE_BASE ** (-np.arange(nf, dtype=np.float32) / nf)
    cos = np.ones((TT, 2 * nf), np.float32)
    sin = np.zeros((TT, 2 * nf), np.float32)
    for seg, pos in enumerate(((t // GRID_W).astype(np.float32), (t % GRID_W).astype(np.float32))):
        ang = (pos[:, None] * inv[None, :]).astype(np.float32)
        cos[:SEQ, seg * nf:(seg + 1) * nf] = np.cos(ang)
        sin[:SEQ, seg * nf:(seg + 1) * nf] = np.sin(ang)
    ks = np.float32(RET_DK ** -0.5)
    return jnp.asarray(np.stack([cos, cos * ks])), jnp.asarray(np.stack([sin, sin * ks]))


def _rope_weight_order(w_in):
    n = w_in.shape[0]
    nf = RET_DK // 4
    qk = w_in[:, :, :2 * D].reshape(n, D, 2 * RET_HEADS, 2, 2, nf)
    qk = jnp.transpose(qk, (0, 1, 2, 4, 3, 5)).reshape(n, D, 2 * D)
    return jnp.concatenate([qk, w_in[:, :, 2 * D:]], axis=-1)


def _decay_tables(decay_logit):
    C = RET_CHUNK
    log_g = jax.nn.log_sigmoid(decay_logit.astype(F32))
    pos = jnp.arange(C, dtype=F32)
    diff = pos[:, None] - pos[None, :]
    lg = log_g[:, :, None, None]
    dm_f = jnp.where(diff >= 0, jnp.exp(jnp.maximum(diff, 0.0)[None, None] * lg), 0.0)
    dm_b = jnp.where(diff <= 0, jnp.exp(jnp.maximum(-diff, 0.0)[None, None] * lg), 0.0)
    dmat = jnp.stack([dm_f[0], dm_b[1]])
    lgc = log_g[:, :, None]
    qdec = jnp.stack([jnp.exp((pos + 1.0)[None, :] * lgc[0]), jnp.exp((C - pos)[None, :] * lgc[1])])
    kdec = jnp.stack([jnp.exp((C - 1.0 - pos)[None, :] * lgc[0]), jnp.exp(pos[None, :] * lgc[1])])
    cdec = jnp.exp(C * log_g).reshape(-1)
    return dmat, qdec[..., None], kdec[..., None], cdec


def kernel(x, c, ctx, c_ctx, w_mod, b_mod, norm_mix, norm_ffn, norm_final, ret_w_in, ret_decay, ret_gn, ret_w_out,
           fn_w_out, na_w_qkv, na_rpb, na_w_out, router_w, router_bias, moe_w_gate, moe_w_up, moe_w_down):
    cvec = jnp.concatenate([c_ctx[None, :], c, jnp.zeros((8 - 1 - BATCH, D), F32)], axis=0)
    mod_all = _modulation(cvec, w_mod, b_mod).reshape(DEPTH, 8, N_MOD, D)
    pad = jnp.zeros((DEPTH, BATCH, 2, 8 - N_MOD, D), F32)
    mods = jnp.stack([jnp.broadcast_to(mod_all[:, 0:1], (DEPTH, BATCH, N_MOD, D)), mod_all[:, 1:1 + BATCH]], axis=2)
    mods = jnp.concatenate([mods, pad], axis=3)

    rope = _rope_tables()
    ret_w_rope = _rope_weight_order(ret_w_in)
    cs, dft_ctx, dft_f1, dft_f2, dft_tw = _dft_tables()
    route = (router_w.T.astype(BF16), router_bias.reshape(N_EXPERTS, 1).astype(F32))

    norm_mix3 = norm_mix.reshape(DEPTH, 1, D)
    norm_ffn3 = norm_ffn.reshape(DEPTH, 1, D)
    ret_gn3 = ret_gn.reshape(-1, 1, 2 * D)

    xs, h = _embed(x, ctx, mods, norm_mix3)
    out = None
    for i in range(DEPTH):
        kind, j = i % 3, i // 3
        if kind == 0:
            qkvg = _project(h.reshape(NTOK, D), ret_w_rope, j, rope).reshape(BATCH, TT, 6 * D)
            dmat, qdec, kdec, cdec = _decay_tables(ret_decay[j])
            o_f, o_b = _retention(qkvg, dmat, qdec, kdec, cdec)
            xs, hf, *routing = _mixer_out("ret", (o_f, o_b, qkvg, ret_gn3), ret_w_out, j, xs, mods, norm_ffn3, i, route)
        elif kind == 1:
            ab = _fn_channel(h, cs)
            f_lat = _fn_cols(_fn_rows(ab, dft_f1, *dft_tw), dft_f2)
            xs, hf, *routing = _mixer_out(
                "split", (f_lat, _fn_ctx(ab, dft_ctx)), fn_w_out, j, xs, mods, norm_ffn3, i, route)
        else:
            qkv = _project(h.reshape(NTOK, D), na_w_qkv, j).reshape(BATCH, TT, 3 * D)
            o = _na_attention(qkv, _na_bias_table(na_rpb[j]))
            xs, hf, *routing = _mixer_out("plain", (o,), na_w_out, j, xs, mods, norm_ffn3, i, route)
        final = i == DEPTH - 1
        if final:
            g_next = (norm_final[None, :], pl.BlockSpec((1, D), lambda b, t: (0, 0)))
        else:
            g_next = (norm_mix3, _layer_spec((1, D), i + 1))
        res = _moe(hf, routing, xs, mods, i, g_next, moe_w_gate, moe_w_up, moe_w_down, final)
        if final:
            out = res
        else:
            xs, h = res
    return out
```

```python
import functools

import numpy as np
import jax
import jax.numpy as jnp
from jax import lax
from jax.experimental import pallas as pl
from jax.experimental.pallas import tpu as pltpu

F32 = jnp.float32
BF16 = jnp.bfloat16
HIGHEST = lax.Precision.HIGHEST

D = 1024
BATCH = 4
SEQ = 4096
CTX = 256
TT = SEQ + CTX
NTOK = BATCH * TT
DEPTH = 4
GRID_W = 64
EPS = 1e-6
NEG_INF = -1e30
N_MOD = 6

RET_HEADS = 4
RET_DK = D // RET_HEADS
RET_DV = 2 * RET_DK
RET_CHUNK = 256
ROPE_BASE = 10000.0

FN_GROUPS = 4
FN_GC = D // FN_GROUPS

NA_HEADS = 16
NA_HD = D // NA_HEADS
NA_KH = 8
NA_KW = 16

N_EXPERTS = 16
N_GROUPS = 4
EPG = N_EXPERTS // N_GROUPS

ROW_TILE = 256
TPB = TT // ROW_TILE
LAT_TILES = SEQ // ROW_TILE
MM_TM = TT // 4
MOE_BM = 512
PAIRS = EPG * (EPG - 1) // 2
N_CLS = N_GROUPS * PAIRS
CLS_PAD = 32
MOE_NB = (NTOK + N_CLS * (MOE_BM - 1) + MOE_BM - 1) // MOE_BM
MOE_P = MOE_NB * MOE_BM
MOE_TAIL = MOE_NB - NTOK // MOE_BM
MOE_NZ = N_CLS + MOE_TAIL
DISPATCH_TILE = 512
DX = D + 128
NA_QT = GRID_W
VMEM_LIMIT = 56 * 1024 * 1024


def _cp(n_axes, vmem=VMEM_LIMIT):
    return pltpu.CompilerParams(dimension_semantics=("arbitrary",) * n_axes, vmem_limit_bytes=vmem)


def _silu(x):
    return x * jax.nn.sigmoid(x)


def _normmod(x, g, shift, scale):
    y = x * lax.rsqrt(jnp.mean(x * x, axis=-1, keepdims=True) + EPS)
    return (y * g) * (1.0 + scale) + shift


def _mod_idx(t):
    return jnp.where(t < LAT_TILES, 1, 0)


def _layer_spec(shape, layer):
    return pl.BlockSpec((None,) + shape, lambda *_: (layer,) + (0,) * len(shape))


def _mod_spec(depth):
    return pl.BlockSpec((None, None, None, 8, D), lambda b, t: (depth, b, _mod_idx(t), 0, 0))


def _mod_kernel(c_ref, w_ref, b_ref, o_ref):
    a = _silu(c_ref[...])
    o_ref[...] = jnp.dot(a, w_ref[...], precision=HIGHEST, preferred_element_type=F32) + b_ref[...]


def _modulation(cvec, w_mod, b_mod):
    return pl.pallas_call(
        _mod_kernel,
        grid=(DEPTH, N_MOD),
        in_specs=[
            pl.BlockSpec((8, D), lambda i, j: (0, 0)),
            pl.BlockSpec((None, D, D), lambda i, j: (i, 0, j)),
            pl.BlockSpec((None, 1, D), lambda i, j: (i, 0, j)),
        ],
        out_specs=pl.BlockSpec((None, 8, D), lambda i, j: (i, 0, j)),
        out_shape=jax.ShapeDtypeStruct((DEPTH, 8, N_MOD * D), F32),
        compiler_params=_cp(2),
        name="modulation",
    )(cvec, w_mod, b_mod.reshape(DEPTH, 1, N_MOD * D))


def _embed_kernel(x_ref, ctx_ref, mod_ref, g_ref, xs_ref, h_ref):
    t = pl.program_id(1)

    def emit(src):
        xs_ref[...] = src
        h_ref[...] = _normmod(src, g_ref[...], mod_ref[0:1, :], mod_ref[1:2, :]).astype(BF16)

    @pl.when(t < LAT_TILES)
    def _():
        emit(x_ref[...])

    @pl.when(t >= LAT_TILES)
    def _():
        emit(ctx_ref[...])


def _embed(x, ctx, mods, g):
    return pl.pallas_call(
        _embed_kernel,
        grid=(BATCH, TPB),
        in_specs=[
            pl.BlockSpec((None, ROW_TILE, D), lambda b, t: (b, jnp.minimum(t, LAT_TILES - 1), 0)),
            pl.BlockSpec((None, CTX, D), lambda b, t: (b, 0, 0)),
            _mod_spec(0),
            _layer_spec((1, D), 0),
        ],
        out_specs=[
            pl.BlockSpec((None, ROW_TILE, D), lambda b, t: (b, t, 0)),
            pl.BlockSpec((None, ROW_TILE, D), lambda b, t: (b, t, 0)),
        ],
        out_shape=[
            jax.ShapeDtypeStruct((BATCH, TT, D), F32),
            jax.ShapeDtypeStruct((BATCH, TT, D), BF16),
        ],
        compiler_params=_cp(2),
        name="embed",
    )(x, ctx, mods, g)


def _rope_store(acc, cos_ref, sin_ref, o_ref):
    for c in range(acc.shape[1] // 128):
        xc = acc[:, c * 128:(c + 1) * 128]
        tsl = slice((c % 2) * 128, (c % 2 + 1) * 128)
        rot = xc * cos_ref[:, tsl] + pltpu.roll(xc, 64, 1) * sin_ref[:, tsl]
        o_ref[:, c * 128:(c + 1) * 128] = rot.astype(o_ref.dtype)


def _proj_kernel(a_ref, w_ref, *rest, n_rope):
    if n_rope:
        cos_ref, sin_ref, o_ref, wb_ref = rest
    else:
        o_ref, wb_ref = rest
    j = pl.program_id(0)

    @pl.when(pl.program_id(1) == 0)
    def _():
        wb_ref[...] = w_ref[...].astype(BF16)

    acc = jnp.dot(a_ref[...], wb_ref[...], preferred_element_type=F32)
    if n_rope:
        @pl.when(j < n_rope)
        def _():
            _rope_store(acc, cos_ref, sin_ref, o_ref)

        @pl.when(j >= n_rope)
        def _():
            o_ref[...] = acc.astype(o_ref.dtype)
    else:
        o_ref[...] = acc.astype(o_ref.dtype)


def _project(a, w, layer, rope=None):
    _, k, n = w.shape
    tn = D
    n_rope = 0 if rope is None else rope[0].shape[0]
    in_specs = [
        pl.BlockSpec((MM_TM, k), lambda j, i: (i, 0)),
        pl.BlockSpec((None, k, tn), lambda j, i: (layer, 0, j)),
    ]
    args = [a, w]
    if n_rope:
        tspec = pl.BlockSpec((None, MM_TM, 256), lambda j, i: (jnp.minimum(j, n_rope - 1), i % (TT // MM_TM), 0))
        in_specs += [tspec, tspec]
        args += list(rope)
    return pl.pallas_call(
        functools.partial(_proj_kernel, n_rope=n_rope),
        grid=(n // tn, NTOK // MM_TM),
        in_specs=in_specs,
        out_specs=pl.BlockSpec((MM_TM, tn), lambda j, i: (i, j)),
        out_shape=jax.ShapeDtypeStruct((NTOK, n), BF16),
        scratch_shapes=[pltpu.VMEM((k, tn), BF16)],
        compiler_params=_cp(2),
        name="project",
    )(*args)


N_CHUNKS = TT // RET_CHUNK
LAT_CHUNKS = SEQ // RET_CHUNK
CTX_CHUNKS = CTX // RET_CHUNK


def _fwd_chunk(n):
    return jnp.where(n < CTX_CHUNKS, LAT_CHUNKS + n, n - CTX_CHUNKS)


def _bwd_chunk(n):
    return N_CHUNKS - 1 - n


def _ret_kernel(cdec_ref, qf, kf, vf, qb, kb, vb, dmat_ref, qdec_ref, kdec_ref, of_ref, ob_ref, s_ref):
    @pl.when(pl.program_id(1) == 0)
    def _():
        s_ref[...] = jnp.zeros_like(s_ref)

    for d, (q_r, k_r, v_r, o_r) in enumerate(((qf, kf, vf, of_ref), (qb, kb, vb, ob_ref))):
        for h in range(RET_HEADS):
            q = q_r[:, h * RET_DK:(h + 1) * RET_DK]
            k = k_r[:, h * RET_DK:(h + 1) * RET_DK]
            v = v_r[:, h * RET_DV:(h + 1) * RET_DV]
            att = lax.dot_general(q, k, (((1,), (1,)), ((), ())), preferred_element_type=F32) * dmat_ref[d, h]
            s = s_ref[d, h]
            qd = (q.astype(F32) * qdec_ref[d, h]).astype(BF16)
            o = (jnp.dot(att.astype(BF16), v, preferred_element_type=F32)
                 + jnp.dot(qd, s.astype(BF16), preferred_element_type=F32))
            kd = (k.astype(F32) * kdec_ref[d, h]).astype(BF16)
            s_ref[d, h] = s * cdec_ref[d * RET_HEADS + h] + lax.dot_general(
                kd, v, (((0,), (0,)), ((), ())), preferred_element_type=F32)
            o_r[:, h * RET_DV:(h + 1) * RET_DV] = o.astype(o_r.dtype)


def _retention(qkvg, dmat, qdec, kdec, cdec):
    C = RET_CHUNK

    def spec(width, col, chunk_fn):
        return pl.BlockSpec((None, C, width), lambda b, n, cd: (b, chunk_fn(n), col))

    ospec_f = pl.BlockSpec((None, C, 2 * D), lambda b, n, cd: (b, _fwd_chunk(n), 0))
    ospec_b = pl.BlockSpec((None, C, 2 * D), lambda b, n, cd: (b, _bwd_chunk(n), 0))
    full = lambda shape: pl.BlockSpec(shape, lambda b, n, cd: (0,) * len(shape))
    grid_spec = pltpu.PrefetchScalarGridSpec(
        num_scalar_prefetch=1,
        grid=(BATCH, N_CHUNKS),
        in_specs=[
            spec(D, 0, _fwd_chunk), spec(D, 1, _fwd_chunk), spec(2 * D, 1, _fwd_chunk),
            spec(D, 0, _bwd_chunk), spec(D, 1, _bwd_chunk), spec(2 * D, 1, _bwd_chunk),
            full((2, RET_HEADS, C, C)), full((2, RET_HEADS, C, 1)), full((2, RET_HEADS, C, 1)),
        ],
        out_specs=[ospec_f, ospec_b],
        scratch_shapes=[pltpu.VMEM((2, RET_HEADS, RET_DK, RET_DV), F32)],
    )
    return pl.pallas_call(
        _ret_kernel,
        grid_spec=grid_spec,
        out_shape=[jax.ShapeDtypeStruct((BATCH, TT, 2 * D), BF16)] * 2,
        compiler_params=_cp(2),
        name="retention",
    )(cdec, qkvg, qkvg, qkvg, qkvg, qkvg, qkvg, dmat, qdec, kdec)


def _mixer_out_kernel(*refs, kind):
    n_src = {"ret": 4, "split": 2, "plain": 1}[kind]
    srcs, (w_ref, x_ref, mod_ref, nf_ref), route_in = refs[:n_src], refs[n_src:n_src + 4], refs[n_src + 4:n_src + 7]
    xo_ref, h_ref, cr_ref, cnt_ref, wb_ref = refs[n_src + 7:]
    first = (pl.program_id(0) == 0) & (pl.program_id(1) == 0)

    @pl.when(first)
    def _():
        wb_ref[...] = w_ref[...].astype(BF16)

    if kind == "ret":
        of_ref, ob_ref, g_ref, gn_ref = srcs
        o = of_ref[...].astype(F32) + ob_ref[...].astype(F32)
        parts = []
        for h in range(RET_HEADS):
            oh = o[:, h * RET_DV:(h + 1) * RET_DV]
            mu = jnp.mean(oh, axis=-1, keepdims=True)
            ctr = oh - mu
            var = jnp.mean(ctr * ctr, axis=-1, keepdims=True)
            parts.append(ctr * lax.rsqrt(var + EPS))
        on = jnp.concatenate(parts, axis=-1) * gn_ref[...]
        a = (on * _silu(g_ref[...].astype(F32))).astype(BF16)
    elif kind == "split":
        a = jnp.where(pl.program_id(1) < LAT_TILES, srcs[0][...], srcs[1][...])
    else:
        a = srcs[0][...]
    y = jnp.dot(a, wb_ref[...], preferred_element_type=F32)
    xn = x_ref[...] + mod_ref[2:3, :] * y
    xo_ref[...] = xn
    h = _normmod(xn, nf_ref[...], mod_ref[3:4, :], mod_ref[4:5, :])
    wa, wb = _route(h, first, *route_in, cr_ref, cnt_ref)
    sub = lax.broadcasted_iota(jnp.int32, (DX - D, ROW_TILE), 0)
    h_ref[:, :D] = h
    h_ref[:, D:] = jnp.where(sub == 0, wa, jnp.where(sub == 1, wb, 0.0)).T


def _mixer_out(kind, srcs, w, layer, xs, mods, norm_ffn, depth, route):
    k = w.shape[1]
    row = lambda width, col=0: pl.BlockSpec((None, ROW_TILE, width), lambda b, t: (b, t, col))
    const = lambda shape: pl.BlockSpec(shape, lambda b, t: (0,) * len(shape))
    per_tok = pl.BlockSpec((2, ROW_TILE), lambda b, t: (0, b * TPB + t))
    tri = jnp.asarray(np.triu(np.ones((ROW_TILE, ROW_TILE), np.float32), 1), BF16)
    if kind == "ret":
        src_specs = [row(2 * D), row(2 * D), row(2 * D, 2), _layer_spec((1, 2 * D), layer)]
    elif kind == "split":
        src_specs = [pl.BlockSpec((None, ROW_TILE, k), lambda b, t: (b, jnp.minimum(t, LAT_TILES - 1), 0)),
                     pl.BlockSpec((None, CTX, k), lambda b, t: (b, 0, 0))]
    else:
        src_specs = [row(k)]
    return pl.pallas_call(
        functools.partial(_mixer_out_kernel, kind=kind),
        grid=(BATCH, TPB),
        in_specs=src_specs + [_layer_spec((k, D), layer), row(D), _mod_spec(depth), _layer_spec((1, D), depth),
                              const((N_EXPERTS, D)), const((N_EXPERTS, 1)), const((ROW_TILE, ROW_TILE))],
        out_specs=[row(D), row(DX), per_tok, const((CLS_PAD, 128))],
        out_shape=[jax.ShapeDtypeStruct((BATCH, TT, D), F32), jax.ShapeDtypeStruct((BATCH, TT, DX), F32),
                   jax.ShapeDtypeStruct((2, NTOK), jnp.int32), jax.ShapeDtypeStruct((CLS_PAD, 128), F32)],
        scratch_shapes=[pltpu.VMEM((k, D), BF16)],
        compiler_params=_cp(2),
        name="mixer_out_" + kind,
    )(*srcs, w, xs, mods, norm_ffn, *route, tri)


def _fn_channel_kernel(h_ref, cs_ref, o_ref):
    for g in range(FN_GROUPS):
        r = jnp.dot(h_ref[:, g * FN_GC:(g + 1) * FN_GC], cs_ref[...], preferred_element_type=F32)
        o_ref[0, :, g * FN_GC:(g + 1) * FN_GC] = r[:, :FN_GC].astype(BF16)
        o_ref[1, :, g * FN_GC:(g + 1) * FN_GC] = r[:, FN_GC:].astype(BF16)


def _fn_channel(h, cs):
    return pl.pallas_call(
        _fn_channel_kernel,
        grid=(BATCH, TPB),
        in_specs=[
            pl.BlockSpec((None, ROW_TILE, D), lambda b, t: (b, t, 0)),
            pl.BlockSpec((FN_GC, 2 * FN_GC), lambda b, t: (0, 0)),
        ],
        out_specs=pl.BlockSpec((None, 2, ROW_TILE, D), lambda b, t: (b, 0, t, 0)),
        out_shape=jax.ShapeDtypeStruct((BATCH, 2, TT, D), BF16),
        compiler_params=_cp(2),
        name="fourier_channels",
    )(h, cs)


FN_R = 64
FN_CB = 16


def _fn_rows_kernel(ab_ref, f1_ref, twc_ref, tws_ref, y_ref, u32):
    u32[...] = ab_ref[...].astype(F32)
    for j in range(FN_CB):
        st = jnp.concatenate([u32[0, :, j, :], u32[1, :, j, :]], axis=0).astype(BF16)
        y = jnp.dot(f1_ref[...], st, preferred_element_type=F32)
        yr, yi = y[:FN_R], y[FN_R:]
        cw, sw = twc_ref[j], tws_ref[j]
        y_ref[0, j] = (yr * cw - yi * sw).astype(BF16)
        y_ref[1, j] = (yi * cw + yr * sw).astype(BF16)


def _fn_rows(ab, f1, twc, tws):
    ab5 = ab.reshape(BATCH, 2, TT // FN_R, FN_R, D)
    return pl.pallas_call(
        _fn_rows_kernel,
        grid=(BATCH, FN_R // FN_CB),
        in_specs=[
            pl.BlockSpec((None, 2, FN_R, FN_CB, D), lambda b, cb: (b, 0, 0, cb, 0)),
            pl.BlockSpec((2 * FN_R, 2 * FN_R), lambda b, cb: (0, 0)),
            pl.BlockSpec((FN_CB, FN_R, 1), lambda b, cb: (cb, 0, 0)),
            pl.BlockSpec((FN_CB, FN_R, 1), lambda b, cb: (cb, 0, 0)),
        ],
        out_specs=pl.BlockSpec((None, 2, FN_CB, FN_R, D), lambda b, cb: (b, 0, cb, 0, 0)),
        out_shape=jax.ShapeDtypeStruct((BATCH, 2, FN_R, FN_R, D), BF16),
        scratch_shapes=[pltpu.VMEM((2, FN_R, FN_CB, D), F32)],
        compiler_params=_cp(2),
        name="fourier_rows",
    )(ab5, f1, twc, tws)


def _fn_cols_kernel(y_ref, f2_ref, o_ref, y32, z32):
    y32[...] = y_ref[...].astype(F32)
    for j in range(FN_CB):
        st = jnp.concatenate([y32[0, :, j, :], y32[1, :, j, :]], axis=0).astype(BF16)
        z32[:, j, :] = jnp.dot(f2_ref[...], st, preferred_element_type=F32)
    o_ref[...] = z32[...].astype(BF16)


def _fn_cols(y, f2):
    out = pl.pallas_call(
        _fn_cols_kernel,
        grid=(BATCH, FN_R // FN_CB),
        in_specs=[
            pl.BlockSpec((None, 2, FN_R, FN_CB, D), lambda b, kb: (b, 0, 0, kb, 0)),
            pl.BlockSpec((FN_R, 2 * FN_R), lambda b, kb: (0, 0)),
        ],
        out_specs=pl.BlockSpec((None, FN_R, FN_CB, D), lambda b, kb: (b, 0, kb, 0)),
        out_shape=jax.ShapeDtypeStruct((BATCH, FN_R, FN_R, D), BF16),
        scratch_shapes=[pltpu.VMEM((2, FN_R, FN_CB, D), F32), pltpu.VMEM((FN_R, FN_CB, D), F32)],
        compiler_params=_cp(2),
        name="fourier_cols",
    )(y, f2)
    return out.reshape(BATCH, SEQ, D)


def _fn_ctx_kernel(ab_ref, m_ref, o_ref):
    st = jnp.concatenate([ab_ref[0], ab_ref[1]], axis=0)
    o_ref[...] = jnp.dot(m_ref[...], st, preferred_element_type=F32).astype(BF16)


def _fn_ctx(ab, m_ctx):
    return pl.pallas_call(
        _fn_ctx_kernel,
        grid=(BATCH,),
        in_specs=[
            pl.BlockSpec((None, 2, CTX, D), lambda b: (b, 0, SEQ // CTX, 0)),
            pl.BlockSpec((CTX, 2 * CTX), lambda b: (0, 0)),
        ],
        out_specs=pl.BlockSpec((None, CTX, D), lambda b: (b, 0, 0)),
        out_shape=jax.ShapeDtypeStruct((BATCH, CTX, D), BF16),
        compiler_params=_cp(1),
        name="fourier_ctx",
    )(ab, m_ctx)


def _dft_tables():
    def cs_pair(n, scale):
        i = np.arange(n)
        ang = 2.0 * np.pi * ((i[:, None] * i[None, :]) % n) / n
        return np.cos(ang) * scale, np.sin(ang) * scale

    assert FN_GC == CTX
    c, s = cs_pair(FN_GC, FN_GC ** -0.5)
    cs = np.concatenate([c, s], axis=1)
    m_ctx = np.concatenate([c, -s], axis=1)
    c, s = cs_pair(FN_R, FN_R ** -0.5)
    f1 = np.block([[c, -s], [s, c]])
    f2 = np.concatenate([c, -s], axis=1)
    i = np.arange(FN_R)
    phi = 2.0 * np.pi * (i[:, None] * i[None, :]) / SEQ
    tw = [jnp.asarray(f(phi)[..., None], F32) for f in (np.cos, np.sin)]
    return jnp.asarray(cs, BF16), jnp.asarray(m_ctx, BF16), jnp.asarray(f1, BF16), jnp.asarray(f2, BF16), tw


NA_ROWS = SEQ // GRID_W
NA_STEPS = TT // NA_QT
NA_NK = NA_KH * GRID_W
NA_DR = 2 * NA_KH - 1


def _na_row_start(step):
    r = jnp.minimum(step, NA_ROWS - 1)
    return r, jnp.clip(r - NA_KH // 2, 0, NA_ROWS - NA_KH)


def _na_bias_idx(step):
    r, rs = _na_row_start(step)
    return rs - r + (NA_KH - 1)


def _na_kernel(q_ref, k_ref, v_ref, bias_ref, o_ref):
    step = pl.program_id(1)
    lane = lax.broadcasted_iota(jnp.int32, (NA_QT, 128), 1)
    nt = (((1,), (1,)), ((), ()))
    scale = NA_HD ** -0.5

    def attend(start, d0=None):
        pairs = [slice(p * 128, (p + 1) * 128) for p in range(NA_HEADS // 2)]
        scores = []
        for p, sl in enumerate(pairs):
            qp = q_ref[:, sl] * scale
            zero = jnp.zeros_like(qp)
            q2 = jnp.concatenate([jnp.where(lane < NA_HD, qp, zero), jnp.where(lane >= NA_HD, qp, zero)], axis=0)
            s_c = lax.dot_general(q2, k_ref[SEQ:TT, sl], nt, preferred_element_type=F32)
            s_w = None
            if start is not None:
                bias = jnp.concatenate(
                    [jnp.concatenate([bias_ref[2 * p + hh, d0 + 2 * j2] for j2 in range(NA_KH // 2)], axis=1)
                     for hh in range(2)], axis=0)
                s_w = lax.dot_general(q2, k_ref[pl.ds(start, NA_NK), sl], nt, preferred_element_type=F32) + bias
            scores.append((s_c, s_w))
        probs = []
        for s_c, s_w in scores:
            m = jnp.max(s_c, axis=-1, keepdims=True)
            if s_w is not None:
                m = jnp.maximum(m, jnp.max(s_w, axis=-1, keepdims=True))
            p_c = jnp.exp(s_c - m)
            l = jnp.sum(p_c, axis=-1, keepdims=True)
            p_w = None
            if s_w is not None:
                p_w = jnp.exp(s_w - m)
                l = l + jnp.sum(p_w, axis=-1, keepdims=True)
                p_w = p_w.astype(BF16)
            probs.append((p_c.astype(BF16), p_w, l))
        for sl, (p_c, p_w, l) in zip(pairs, probs):
            acc = jnp.dot(p_c, v_ref[SEQ:TT, sl], preferred_element_type=F32)
            if p_w is not None:
                acc = acc + jnp.dot(p_w, v_ref[pl.ds(start, NA_NK), sl], preferred_element_type=F32)
            o2 = acc / l
            o_ref[:, sl] = jnp.where(lane < NA_HD, o2[:NA_QT], o2[NA_QT:]).astype(BF16)

    @pl.when(step < NA_ROWS)
    def _():
        _, rs = _na_row_start(step)
        attend(pl.multiple_of(rs * GRID_W, GRID_W), _na_bias_idx(step))

    @pl.when(step >= NA_ROWS)
    def _():
        attend(None)


def _na_attention(qkv, bias):
    return pl.pallas_call(
        _na_kernel,
        grid=(BATCH, NA_STEPS),
        in_specs=[
            pl.BlockSpec((None, NA_QT, D), lambda b, s: (b, s, 0)),
            pl.BlockSpec((None, TT, D), lambda b, s: (b, 0, 1), pipeline_mode=pl.Buffered(1)),
            pl.BlockSpec((None, TT, D), lambda b, s: (b, 0, 2), pipeline_mode=pl.Buffered(1)),
            pl.BlockSpec((NA_HEADS, NA_DR - 1, GRID_W, 2 * GRID_W), lambda b, s: (0, 0, 0, 0),
                         pipeline_mode=pl.Buffered(1)),
        ],
        out_specs=pl.BlockSpec((None, NA_QT, D), lambda b, s: (b, s, 0)),
        out_shape=jax.ShapeDtypeStruct((BATCH, TT, D), BF16),
        compiler_params=_cp(2),
        name="na_attention",
    )(qkv, qkv, qkv, bias)


def _na_bias_table(rpb):
    qc = np.arange(GRID_W)
    kcol = np.arange(GRID_W)
    cstart = np.clip(qc - NA_KW // 2, 0, GRID_W - NA_KW)
    valid = (kcol[None, :] >= cstart[:, None]) & (kcol[None, :] < cstart[:, None] + NA_KW)
    dc = kcol[None, :] - qc[:, None] + (NA_KW - 1)
    pick = ((dc[None] == np.arange(2 * NA_KW - 1)[:, None, None]) & valid[None]).astype(np.float32)
    tab = jnp.einsum("hdj,jqk->hdqk", rpb.astype(F32), jnp.asarray(pick), precision=HIGHEST)
    tab = tab + jnp.asarray(np.where(valid, 0.0, NEG_INF).astype(np.float32))
    return jnp.concatenate([tab[:, :NA_DR - 1], tab[:, 1:]], axis=-1)


def _route(h, first, rw_ref, rb_ref, tri_ref, cr_ref, cnt_ref):
    logits = lax.dot_general(rw_ref[...], h.astype(BF16), (((1,), (1,)), ((), ())),
                             preferred_element_type=F32)
    sc = jax.nn.sigmoid(logits)
    sel = sc + rb_ref[...]
    sc_r = [sc[e:e + 1, :] for e in range(N_EXPERTS)]
    sel_r = [sel[e:e + 1, :] for e in range(N_EXPERTS)]
    zero_i = jnp.zeros_like(sel_r[0]).astype(jnp.int32)

    best = gi = None
    picks = []
    for g in range(N_GROUPS):
        a = sel_r[g * EPG:(g + 1) * EPG]
        u = sc_r[g * EPG:(g + 1) * EPG]
        gs = None
        for i in range(EPG):
            for j in range(i + 1, EPG):
                pair = a[i] + a[j]
                gs = pair if gs is None else jnp.maximum(gs, pair)
        m1, i1, s1 = a[0], zero_i, u[0]
        for j in range(1, EPG):
            upd = a[j] > m1
            m1 = jnp.where(upd, a[j], m1)
            i1 = jnp.where(upd, j, i1)
            s1 = jnp.where(upd, u[j], s1)
        m2 = i2 = s2 = None
        for j in range(EPG):
            cand = jnp.where(i1 == j, -jnp.inf, a[j])
            if m2 is None:
                m2, i2, s2 = cand, zero_i, u[0]
            else:
                upd = cand > m2
                m2 = jnp.where(upd, cand, m2)
                i2 = jnp.where(upd, j, i2)
                s2 = jnp.where(upd, u[j], s2)
        picks.append((i1 + g * EPG, i2 + g * EPG, s1, s2))
        if best is None:
            best, gi = gs, zero_i
        else:
            upd = gs > best
            best = jnp.where(upd, gs, best)
            gi = jnp.where(upd, g, gi)

    e1, e2, s1, s2 = picks[0]
    for g in range(1, N_GROUPS):
        on = gi == g
        e1 = jnp.where(on, picks[g][0], e1)
        e2 = jnp.where(on, picks[g][1], e2)
        s1 = jnp.where(on, picks[g][2], s1)
        s2 = jnp.where(on, picks[g][3], s2)
    tot = s1 + s2
    w1, w2 = s1 / tot, s2 / tot

    swap = e2 < e1
    la = jnp.where(swap, e2, e1) - gi * EPG
    lb = jnp.where(swap, e1, e2) - gi * EPG
    cls = gi * PAIRS + jnp.where(la == 0, 0, jnp.where(la == 1, 3, 5)) + lb - la - 1
    wa, wb = jnp.where(swap, w2, w1), jnp.where(swap, w1, w2)

    @pl.when(first)
    def _():
        cnt_ref[...] = jnp.zeros_like(cnt_ref)

    oh = (lax.broadcasted_iota(jnp.int32, (cnt_ref.shape[0], cls.shape[1]), 0) == cls).astype(F32)
    before = jnp.dot(oh.astype(BF16), tri_ref[...], preferred_element_type=F32)
    base = cnt_ref[:, 0:1]
    cr_ref[0:1, :] = cls
    cr_ref[1:2, :] = jnp.sum(oh * (base + before), axis=0, keepdims=True).astype(jnp.int32)
    cnt_ref[...] = cnt_ref[...] + jnp.sum(oh, axis=1, keepdims=True)
    return wa, wb


DMA_UNROLL = 8


def _row_copy(src_ref, src_row, dst_ref, dst_row, sem):
    return pltpu.make_async_copy(src_ref.at[pl.ds(src_row, 1)], dst_ref.at[pl.ds(dst_row, 1)], sem)


def _for_each_row(fn, rows=ROW_TILE):
    def group(gidx, carry):
        for u in range(DMA_UNROLL):
            fn(gidx * DMA_UNROLL + u)
        return carry

    lax.fori_loop(0, rows // DMA_UNROLL, group, 0)


def _dispatch_kernel(zrow_ref, dest_ref, h_ref, xs_ref, zbuf, sem, zsem):
    @pl.when(pl.program_id(0) == 0)
    def _():
        zbuf[...] = jnp.zeros_like(zbuf)

        def zero_copy(e):
            start = pl.multiple_of(jnp.maximum(zrow_ref[e], 0), MOE_BM)
            return pltpu.make_async_copy(zbuf, xs_ref.at[pl.ds(start, MOE_BM)], zsem)

        for e in range(MOE_NZ):
            pl.when(zrow_ref[e] >= 0)(lambda e=e: zero_copy(e).start())
        for e in range(MOE_NZ):
            pl.when(zrow_ref[e] >= 0)(lambda e=e: zero_copy(e).wait())

    _for_each_row(lambda r: _row_copy(h_ref, r, xs_ref, dest_ref[0, r], sem).start(), DISPATCH_TILE)
    _for_each_row(lambda r: _row_copy(h_ref, r, xs_ref, dest_ref[0, r], sem).wait(), DISPATCH_TILE)


def _dispatch(h, dest, zrow):
    grid_spec = pltpu.PrefetchScalarGridSpec(
        num_scalar_prefetch=1,
        grid=(NTOK // DISPATCH_TILE,),
        in_specs=[
            pl.BlockSpec((None, 1, DISPATCH_TILE), lambda i, z: (i, 0, 0), memory_space=pltpu.SMEM),
            pl.BlockSpec((DISPATCH_TILE, DX), lambda i, z: (i, 0)),
        ],
        out_specs=pl.BlockSpec(memory_space=pl.ANY),
        scratch_shapes=[pltpu.VMEM((MOE_BM, DX), F32), pltpu.SemaphoreType.DMA(()), pltpu.SemaphoreType.DMA(())],
    )
    return pl.pallas_call(
        _dispatch_kernel,
        grid_spec=grid_spec,
        out_shape=jax.ShapeDtypeStruct((MOE_P, DX), F32),
        compiler_params=pltpu.CompilerParams(dimension_semantics=("arbitrary",), vmem_limit_bytes=VMEM_LIMIT,
                                             has_side_effects=True),
        name="moe_dispatch",
    )(zrow, dest.reshape(NTOK // DISPATCH_TILE, 1, DISPATCH_TILE), h)


def _experts_kernel(blk_ref, ex_ref, nv_ref, x_ref, *rest, which):
    if which:
        prev_ref, wg_ref, wu_ref, wd_ref, y_ref, wgb, wub, wdb = rest
    else:
        wg_ref, wu_ref, wd_ref, y_ref, wgb, wub, wdb = rest
    del blk_ref
    v = pl.program_id(0)
    nv = nv_ref[v]
    changed = (v == 0) | (ex_ref[v] != ex_ref[jnp.maximum(v - 1, 0)])
    half = MOE_BM // 2

    @pl.when((nv > 0) & changed)
    def _():
        wgb[...] = wg_ref[...].astype(BF16)
        wub[...] = wu_ref[...].astype(BF16)
        wdb[...] = wd_ref[...].astype(BF16)

    def ffn(rows):
        x = x_ref[:rows, :D].astype(BF16)
        g = jnp.dot(x, wgb[...], preferred_element_type=F32)
        u = jnp.dot(x, wub[...], preferred_element_type=F32)
        hh = (_silu(g) * u).astype(BF16)
        y = jnp.dot(hh, wdb[...], preferred_element_type=F32) * x_ref[:rows, D + which:D + which + 1]
        return prev_ref[:rows, :] + y if which else y

    @pl.when(nv > half)
    def _():
        y_ref[...] = ffn(MOE_BM)

    @pl.when((nv > 0) & (nv <= half))
    def _():
        y_ref[:half, :] = ffn(half)
        y_ref[half:, :] = jnp.zeros((MOE_BM - half, D), F32)

    @pl.when(nv == 0)
    def _():
        y_ref[...] = jnp.zeros_like(y_ref)


def _experts(xs, prev, which, blocks, experts, n_valid, w_gate, w_up, w_down, depth):
    wspec = pl.BlockSpec((None, None, D, D), lambda v, bl, ex, nu: (depth, ex[v], 0, 0))
    yspec = pl.BlockSpec((MOE_BM, D), lambda v, bl, ex, nu: (bl[v], 0))
    grid_spec = pltpu.PrefetchScalarGridSpec(
        num_scalar_prefetch=3,
        grid=(MOE_NB,),
        in_specs=[pl.BlockSpec((MOE_BM, DX), lambda v, bl, ex, nu: (bl[v], 0))] + ([yspec] if which else [])
        + [wspec, wspec, wspec],
        out_specs=yspec,
        scratch_shapes=[pltpu.VMEM((D, D), BF16)] * 3,
    )
    return pl.pallas_call(
        functools.partial(_experts_kernel, which=which),
        grid_spec=grid_spec,
        out_shape=jax.ShapeDtypeStruct((MOE_P, D), F32),
        compiler_params=_cp(1),
        name="moe_experts_%d" % which,
    )(blocks, experts, n_valid, xs, *([prev] if which else []), w_gate, w_up, w_down)


def _combine_kernel(dest_ref, next_ref, yb_ref, x_ref, mod_ref, modn_ref, gn_ref, *rest, final):
    if final:
        out_ref, ybuf, sem = rest
    else:
        xo_ref, h_ref, ybuf, sem = rest
    t = pl.program_id(1)
    step = pl.program_id(0) * TPB + t
    slot = lax.rem(step, 2)

    def gather(slots_ref, buf):
        return lambda r: _row_copy(yb_ref, slots_ref[0, r], ybuf.at[buf], r, sem.at[buf])

    @pl.when(step == 0)
    def _():
        _for_each_row(lambda r: gather(dest_ref, slot)(r).start())

    @pl.when(step + 1 < BATCH * TPB)
    def _():
        _for_each_row(lambda r: gather(next_ref, 1 - slot)(r).start())

    _for_each_row(lambda r: gather(dest_ref, slot)(r).wait())

    def body():
        xn = x_ref[...] + mod_ref[5:6, :] * ybuf[slot]
        if final:
            out_ref[...] = _normmod(xn, gn_ref[...], 0.0, 0.0)
        else:
            xo_ref[...] = xn
            h_ref[...] = _normmod(xn, gn_ref[...], modn_ref[0:1, :], modn_ref[1:2, :]).astype(BF16)

    if final:
        pl.when(t < LAT_TILES)(body)
    else:
        body()


def _combine(yb, dest, xs, mods, depth, g_next, final):
    row = pl.BlockSpec((None, ROW_TILE, D), lambda b, t: (b, t, 0))
    next_depth = depth if final else depth + 1
    if final:
        out_specs = pl.BlockSpec((None, ROW_TILE, D), lambda b, t: (b, jnp.minimum(t, LAT_TILES - 1), 0))
        out_shape = jax.ShapeDtypeStruct((BATCH, SEQ, D), F32)
    else:
        out_specs = [row, row]
        out_shape = [jax.ShapeDtypeStruct((BATCH, TT, D), F32), jax.ShapeDtypeStruct((BATCH, TT, D), BF16)]
    last = BATCH * TPB - 1
    return pl.pallas_call(
        functools.partial(_combine_kernel, final=final),
        grid=(BATCH, TPB),
        in_specs=[
            pl.BlockSpec((None, 1, ROW_TILE), lambda b, t: (b * TPB + t, 0, 0), memory_space=pltpu.SMEM),
            pl.BlockSpec((None, 1, ROW_TILE), lambda b, t: (jnp.minimum(b * TPB + t + 1, last), 0, 0),
                         memory_space=pltpu.SMEM),
            pl.BlockSpec(memory_space=pl.ANY),
            row, _mod_spec(depth), _mod_spec(next_depth),
            g_next[1],
        ],
        out_specs=out_specs,
        out_shape=out_shape,
        scratch_shapes=[pltpu.VMEM((2, ROW_TILE, D), F32), pltpu.SemaphoreType.DMA((2,))],
        compiler_params=_cp(2),
        name="moe_combine",
    )(dest, dest, yb, xs, mods, mods, g_next[0])


def _class_experts():
    pairs = [(a, b) for a in range(EPG) for b in range(a + 1, EPG)]
    lo = [g * EPG + a for g in range(N_GROUPS) for a, _ in pairs]
    hi = [g * EPG + b for g in range(N_GROUPS) for _, b in pairs]
    return jnp.asarray(lo, jnp.int32), jnp.asarray(hi, jnp.int32)


def _moe_plan(cls, rank, counts):
    padded = (counts + MOE_BM - 1) // MOE_BM * MOE_BM
    pend = jnp.cumsum(padded)
    pstart = pend - padded
    cid = jnp.arange(N_CLS, dtype=jnp.int32)
    dest = rank + jnp.sum(jnp.where(cls[None, :] == cid[:, None], pstart[:, None], 0), axis=0)
    dest = dest.astype(jnp.int32).reshape(NTOK // ROW_TILE, 1, ROW_TILE)
    blk = jnp.arange(MOE_NB, dtype=jnp.int32)
    block_c = jnp.minimum(jnp.sum(pend[None, :] <= (blk * MOE_BM)[:, None], axis=1), N_CLS - 1)
    n_used = (pend[-1] // MOE_BM).astype(jnp.int32)
    used = blk < n_used
    lo, hi = _class_experts()
    onehot_c = block_c[:, None] == cid[None, :]
    e_lo = jnp.sum(jnp.where(onehot_c, lo[None, :], 0), axis=1)
    e_hi = jnp.sum(jnp.where(onehot_c, hi[None, :], 0), axis=1)

    def hold_last(e):
        return jnp.where(used, e, jnp.sum(jnp.where(blk == n_used - 1, e, 0))).astype(jnp.int32)

    key = jnp.where(used, e_hi, N_EXPERTS)
    earlier = (key[None, :] < key[:, None]) | ((key[None, :] == key[:, None]) & (blk[None, :] < blk[:, None]))
    pos = jnp.sum(earlier, axis=1)
    at = pos[None, :] == blk[:, None]
    order1 = jnp.sum(jnp.where(at, blk[None, :], 0), axis=1).astype(jnp.int32)
    e1 = jnp.sum(jnp.where(at, e_hi[None, :], 0), axis=1)
    class_end = jnp.sum(jnp.where(onehot_c, (pstart + counts)[None, :], 0), axis=1)
    nv0 = jnp.where(used, jnp.clip(class_end - blk * MOE_BM, 0, MOE_BM), 0).astype(jnp.int32)
    nv1 = jnp.sum(jnp.where(at, nv0[None, :], 0), axis=1).astype(jnp.int32)
    tail = blk[MOE_NB - MOE_TAIL:] * MOE_BM
    zrow = jnp.concatenate([jnp.where(padded > 0, pend - MOE_BM, -1), jnp.where(tail >= pend[-1], tail, -1)])
    return dest, (blk, hold_last(e_lo), nv0), (order1, hold_last(e1), nv1), zrow.astype(jnp.int32)


def _moe(h, routing, xs, mods, depth, g_next, w_gate, w_up, w_down, final):
    cr, cnt = routing
    dest, visit0, visit1, zrow = _moe_plan(cr[0], cr[1], cnt[:N_CLS, 0].astype(jnp.int32))
    xsorted = _dispatch(h.reshape(NTOK, DX), dest, zrow)
    y0 = _experts(xsorted, None, 0, *visit0, w_gate, w_up, w_down, depth)
    yb = _experts(xsorted, y0, 1, *visit1, w_gate, w_up, w_down, depth)
    return _combine(yb, dest, xs, mods, depth, g_next, final)


def _rope_tables():
    nf = RET_DK // 4
    t = np.arange(SEQ)
    inv = ROPE_BASE ** (-np.arange(nf, dtype=np.float32) / nf)
    cos = np.ones((TT, 256), np.float32)
    sin = np.zeros((TT, 256), np.float32)
    for seg, pos in enumerate(((t // GRID_W).astype(np.float32), (t % GRID_W).astype(np.float32))):
        ang = (pos[:, None] * inv[None, :]).astype(np.float32)
        c, s = np.cos(ang), np.sin(ang)
        cos[:SEQ, seg * 128:(seg + 1) * 128] = np.concatenate([c, c], axis=1)
        sin[:SEQ, seg * 128:(seg + 1) * 128] = np.concatenate([-s, s], axis=1)
    ks = np.float32(RET_DK ** -0.5)
    return jnp.asarray(np.stack([cos, cos * ks])), jnp.asarray(np.stack([sin, sin * ks]))


def _decay_tables(decay_logit):
    C = RET_CHUNK
    log_g = jax.nn.log_sigmoid(decay_logit.astype(F32))
    pos = jnp.arange(C, dtype=F32)
    diff = pos[:, None] - pos[None, :]
    lg = log_g[:, :, None, None]
    dm_f = jnp.where(diff >= 0, jnp.exp(jnp.maximum(diff, 0.0)[None, None] * lg), 0.0)
    dm_b = jnp.where(diff <= 0, jnp.exp(jnp.maximum(-diff, 0.0)[None, None] * lg), 0.0)
    dmat = jnp.stack([dm_f[0], dm_b[1]])
    lgc = log_g[:, :, None]
    qdec = jnp.stack([jnp.exp((pos + 1.0)[None, :] * lgc[0]), jnp.exp((C - pos)[None, :] * lgc[1])])
    kdec = jnp.stack([jnp.exp((C - 1.0 - pos)[None, :] * lgc[0]), jnp.exp(pos[None, :] * lgc[1])])
    cdec = jnp.exp(C * log_g).reshape(-1)
    return dmat, qdec[..., None], kdec[..., None], cdec


def kernel(x, c, ctx, c_ctx, w_mod, b_mod, norm_mix, norm_ffn, norm_final, ret_w_in, ret_decay, ret_gn, ret_w_out,
           fn_w_out, na_w_qkv, na_rpb, na_w_out, router_w, router_bias, moe_w_gate, moe_w_up, moe_w_down):
    cvec = jnp.concatenate([c_ctx[None, :], c, jnp.zeros((8 - 1 - BATCH, D), F32)], axis=0)
    mod_all = _modulation(cvec, w_mod, b_mod).reshape(DEPTH, 8, N_MOD, D)
    pad = jnp.zeros((DEPTH, BATCH, 2, 8 - N_MOD, D), F32)
    mods = jnp.stack([jnp.broadcast_to(mod_all[:, 0:1], (DEPTH, BATCH, N_MOD, D)), mod_all[:, 1:1 + BATCH]], axis=2)
    mods = jnp.concatenate([mods, pad], axis=3)

    rope = _rope_tables()
    cs, dft_ctx, dft_f1, dft_f2, dft_tw = _dft_tables()
    route = (router_w.T.astype(BF16), router_bias.reshape(N_EXPERTS, 1).astype(F32))

    norm_mix3 = norm_mix.reshape(DEPTH, 1, D)
    norm_ffn3 = norm_ffn.reshape(DEPTH, 1, D)
    ret_gn3 = ret_gn.reshape(-1, 1, 2 * D)

    xs, h = _embed(x, ctx, mods, norm_mix3)
    out = None
    for i in range(DEPTH):
        kind, j = i % 3, i // 3
        if kind == 0:
            qkvg = _project(h.reshape(NTOK, D), ret_w_in, j, rope).reshape(BATCH, TT, 6 * D)
            dmat, qdec, kdec, cdec = _decay_tables(ret_decay[j])
            o_f, o_b = _retention(qkvg, dmat, qdec, kdec, cdec)
            xs, hf, *routing = _mixer_out("ret", (o_f, o_b, qkvg, ret_gn3), ret_w_out, j, xs, mods, norm_ffn3, i, route)
        elif kind == 1:
            ab = _fn_channel(h, cs)
            f_lat = _fn_cols(_fn_rows(ab, dft_f1, *dft_tw), dft_f2)
            xs, hf, *routing = _mixer_out(
                "split", (f_lat, _fn_ctx(ab, dft_ctx)), fn_w_out, j, xs, mods, norm_ffn3, i, route)
        else:
            qkv = _project(h.reshape(NTOK, D), na_w_qkv, j).reshape(BATCH, TT, 3 * D)
            o = _na_attention(qkv, _na_bias_table(na_rpb[j]))
            xs, hf, *routing = _mixer_out("plain", (o,), na_w_out, j, xs, mods, norm_ffn3, i, route)
        final = i == DEPTH - 1
        if final:
            g_next = (norm_final[None, :], pl.BlockSpec((1, D), lambda b, t: (0, 0)))
        else:
            g_next = (norm_mix3, _layer_spec((1, D), i + 1))
        res = _moe(hf, routing, xs, mods, i, g_next, moe_w_gate, moe_w_up, moe_w_down, final)
        if final:
            out = res
        else:
            xs, h = res
    return out
```

```python
import functools

import numpy as np
import jax
import jax.numpy as jnp
from jax import lax
from jax.experimental import pallas as pl
from jax.experimental.pallas import tpu as pltpu

F32 = jnp.float32
BF16 = jnp.bfloat16
HIGHEST = lax.Precision.HIGHEST

D = 1024
BATCH = 4
SEQ = 4096
CTX = 256
TT = SEQ + CTX
NTOK = BATCH * TT
DEPTH = 4
GRID_W = 64
EPS = 1e-6
NEG_INF = -1e30
N_MOD = 6

RET_HEADS = 4
RET_DK = D // RET_HEADS
RET_DV = 2 * RET_DK
RET_CHUNK = 256
ROPE_BASE = 10000.0

FN_GROUPS = 4
FN_GC = D // FN_GROUPS

NA_HEADS = 16
NA_HD = D // NA_HEADS
NA_KH = 8
NA_KW = 16

N_EXPERTS = 16
N_GROUPS = 4
EPG = N_EXPERTS // N_GROUPS

ROW_TILE = 256
TPB = TT // ROW_TILE
LAT_TILES = SEQ // ROW_TILE
MM_TM = TT // 4
MOE_BM = 256
PAIRS = EPG * (EPG - 1) // 2
N_CLS = N_GROUPS * PAIRS
CLS_PAD = 32
MOE_NB = (NTOK + N_CLS * (MOE_BM - 1) + MOE_BM - 1) // MOE_BM
MOE_P = MOE_NB * MOE_BM
MOE_TAIL = MOE_NB - NTOK // MOE_BM
MOE_NZ = N_CLS + MOE_TAIL
DISPATCH_TILE = 512
DX = D + 128
NA_QT = GRID_W
VMEM_LIMIT = 56 * 1024 * 1024


def _cp(n_axes, vmem=VMEM_LIMIT):
    return pltpu.CompilerParams(dimension_semantics=("arbitrary",) * n_axes, vmem_limit_bytes=vmem)


def _silu(x):
    return x * jax.nn.sigmoid(x)


def _normmod(x, g, shift, scale):
    y = x * lax.rsqrt(jnp.mean(x * x, axis=-1, keepdims=True) + EPS)
    return (y * g) * (1.0 + scale) + shift


def _mod_idx(t):
    return jnp.where(t < LAT_TILES, 1, 0)


def _layer_spec(shape, layer):
    return pl.BlockSpec((None,) + shape, lambda *_: (layer,) + (0,) * len(shape))


def _mod_spec(depth):
    return pl.BlockSpec((None, None, None, 8, D), lambda b, t: (depth, b, _mod_idx(t), 0, 0))


def _mod_kernel(c_ref, w_ref, b_ref, o_ref):
    a = _silu(c_ref[...])
    o_ref[...] = jnp.dot(a.astype(BF16), w_ref[...].astype(BF16), preferred_element_type=F32) + b_ref[...]


def _modulation(cvec, w_mod, b_mod):
    return pl.pallas_call(
        _mod_kernel,
        grid=(DEPTH, N_MOD),
        in_specs=[
            pl.BlockSpec((8, D), lambda i, j: (0, 0)),
            pl.BlockSpec((None, D, D), lambda i, j: (i, 0, j)),
            pl.BlockSpec((None, 1, D), lambda i, j: (i, 0, j)),
        ],
        out_specs=pl.BlockSpec((None, 8, D), lambda i, j: (i, 0, j)),
        out_shape=jax.ShapeDtypeStruct((DEPTH, 8, N_MOD * D), F32),
        compiler_params=_cp(2),
        name="modulation",
    )(cvec, w_mod, b_mod.reshape(DEPTH, 1, N_MOD * D))


def _embed_kernel(x_ref, ctx_ref, mod_ref, g_ref, xs_ref, h_ref):
    t = pl.program_id(1)

    def emit(src):
        xs_ref[...] = src
        h_ref[...] = _normmod(src, g_ref[...], mod_ref[0:1, :], mod_ref[1:2, :]).astype(BF16)

    @pl.when(t < LAT_TILES)
    def _():
        emit(x_ref[...])

    @pl.when(t >= LAT_TILES)
    def _():
        emit(ctx_ref[...])


def _embed(x, ctx, mods, g):
    return pl.pallas_call(
        _embed_kernel,
        grid=(BATCH, TPB),
        in_specs=[
            pl.BlockSpec((None, ROW_TILE, D), lambda b, t: (b, jnp.minimum(t, LAT_TILES - 1), 0)),
            pl.BlockSpec((None, CTX, D), lambda b, t: (b, 0, 0)),
            _mod_spec(0),
            _layer_spec((1, D), 0),
        ],
        out_specs=[
            pl.BlockSpec((None, ROW_TILE, D), lambda b, t: (b, t, 0)),
            pl.BlockSpec((None, ROW_TILE, D), lambda b, t: (b, t, 0)),
        ],
        out_shape=[
            jax.ShapeDtypeStruct((BATCH, TT, D), F32),
            jax.ShapeDtypeStruct((BATCH, TT, D), BF16),
        ],
        compiler_params=_cp(2),
        name="embed",
    )(x, ctx, mods, g)


def _rope_store(acc, cos_ref, sin_ref, o_ref):
    for c in range(acc.shape[1] // 128):
        xc = acc[:, c * 128:(c + 1) * 128]
        tsl = slice((c % 2) * 128, (c % 2 + 1) * 128)
        rot = xc * cos_ref[:, tsl] + pltpu.roll(xc, 64, 1) * sin_ref[:, tsl]
        o_ref[:, c * 128:(c + 1) * 128] = rot.astype(o_ref.dtype)


def _proj_kernel(a_ref, w_ref, *rest, n_rope):
    if n_rope:
        cos_ref, sin_ref, o_ref, wb_ref = rest
    else:
        o_ref, wb_ref = rest
    j = pl.program_id(0)

    @pl.when(pl.program_id(1) == 0)
    def _():
        wb_ref[...] = w_ref[...].astype(BF16)

    acc = jnp.dot(a_ref[...], wb_ref[...], preferred_element_type=F32)
    if n_rope:
        @pl.when(j < n_rope)
        def _():
            _rope_store(acc, cos_ref, sin_ref, o_ref)

        @pl.when(j >= n_rope)
        def _():
            o_ref[...] = acc.astype(o_ref.dtype)
    else:
        o_ref[...] = acc.astype(o_ref.dtype)


def _project(a, w, layer, rope=None):
    _, k, n = w.shape
    tn = D
    n_rope = 0 if rope is None else rope[0].shape[0]
    in_specs = [
        pl.BlockSpec((MM_TM, k), lambda j, i: (i, 0)),
        pl.BlockSpec((None, k, tn), lambda j, i: (layer, 0, j)),
    ]
    args = [a, w]
    if n_rope:
        tspec = pl.BlockSpec((None, MM_TM, 256), lambda j, i: (jnp.minimum(j, n_rope - 1), i % (TT // MM_TM), 0))
        in_specs += [tspec, tspec]
        args += list(rope)
    return pl.pallas_call(
        functools.partial(_proj_kernel, n_rope=n_rope),
        grid=(n // tn, NTOK // MM_TM),
        in_specs=in_specs,
        out_specs=pl.BlockSpec((MM_TM, tn), lambda j, i: (i, j)),
        out_shape=jax.ShapeDtypeStruct((NTOK, n), BF16),
        scratch_shapes=[pltpu.VMEM((k, tn), BF16)],
        compiler_params=_cp(2),
        name="project",
    )(*args)


N_CHUNKS = TT // RET_CHUNK
LAT_CHUNKS = SEQ // RET_CHUNK
CTX_CHUNKS = CTX // RET_CHUNK


def _fwd_chunk(n):
    return jnp.where(n < CTX_CHUNKS, LAT_CHUNKS + n, n - CTX_CHUNKS)


def _bwd_chunk(n):
    return N_CHUNKS - 1 - n


def _ret_kernel(cdec_ref, qf, kf, vf, qb, kb, vb, dmat_ref, qdec_ref, kdec_ref, of_ref, ob_ref, s_ref):
    @pl.when(pl.program_id(1) == 0)
    def _():
        s_ref[...] = jnp.zeros_like(s_ref)

    for d, (q_r, k_r, v_r, o_r) in enumerate(((qf, kf, vf, of_ref), (qb, kb, vb, ob_ref))):
        for h in range(RET_HEADS):
            q = q_r[:, h * RET_DK:(h + 1) * RET_DK]
            k = k_r[:, h * RET_DK:(h + 1) * RET_DK]
            v = v_r[:, h * RET_DV:(h + 1) * RET_DV]
            att = lax.dot_general(q, k, (((1,), (1,)), ((), ())), preferred_element_type=F32) * dmat_ref[d, h]
            s = s_ref[d, h]
            qd = (q.astype(F32) * qdec_ref[d, h]).astype(BF16)
            o = (jnp.dot(att.astype(BF16), v, preferred_element_type=F32)
                 + jnp.dot(qd, s.astype(BF16), preferred_element_type=F32))
            kd = (k.astype(F32) * kdec_ref[d, h]).astype(BF16)
            s_ref[d, h] = s * cdec_ref[d * RET_HEADS + h] + lax.dot_general(
                kd, v, (((0,), (0,)), ((), ())), preferred_element_type=F32)
            o_r[:, h * RET_DV:(h + 1) * RET_DV] = o.astype(o_r.dtype)


def _retention(qkvg, dmat, qdec, kdec, cdec):
    C = RET_CHUNK

    def spec(width, col, chunk_fn):
        return pl.BlockSpec((None, C, width), lambda b, n, cd: (b, chunk_fn(n), col))

    ospec_f = pl.BlockSpec((None, C, 2 * D), lambda b, n, cd: (b, _fwd_chunk(n), 0))
    ospec_b = pl.BlockSpec((None, C, 2 * D), lambda b, n, cd: (b, _bwd_chunk(n), 0))
    full = lambda shape: pl.BlockSpec(shape, lambda b, n, cd: (0,) * len(shape))
    grid_spec = pltpu.PrefetchScalarGridSpec(
        num_scalar_prefetch=1,
        grid=(BATCH, N_CHUNKS),
        in_specs=[
            spec(D, 0, _fwd_chunk), spec(D, 1, _fwd_chunk), spec(2 * D, 1, _fwd_chunk),
            spec(D, 0, _bwd_chunk), spec(D, 1, _bwd_chunk), spec(2 * D, 1, _bwd_chunk),
            full((2, RET_HEADS, C, C)), full((2, RET_HEADS, C, 1)), full((2, RET_HEADS, C, 1)),
        ],
        out_specs=[ospec_f, ospec_b],
        scratch_shapes=[pltpu.VMEM((2, RET_HEADS, RET_DK, RET_DV), F32)],
    )
    return pl.pallas_call(
        _ret_kernel,
        grid_spec=grid_spec,
        out_shape=[jax.ShapeDtypeStruct((BATCH, TT, 2 * D), BF16)] * 2,
        compiler_params=_cp(2),
        name="retention",
    )(cdec, qkvg, qkvg, qkvg, qkvg, qkvg, qkvg, dmat, qdec, kdec)


def _mixer_out_kernel(*refs, kind):
    n_src = {"ret": 4, "split": 2, "plain": 1}[kind]
    srcs, (w_ref, x_ref, mod_ref, nf_ref), route_in = refs[:n_src], refs[n_src:n_src + 4], refs[n_src + 4:n_src + 7]
    xo_ref, h_ref, cr_ref, cnt_ref, wb_ref = refs[n_src + 7:]
    first = (pl.program_id(0) == 0) & (pl.program_id(1) == 0)

    @pl.when(first)
    def _():
        wb_ref[...] = w_ref[...].astype(BF16)

    if kind == "ret":
        of_ref, ob_ref, g_ref, gn_ref = srcs
        o = of_ref[...].astype(F32) + ob_ref[...].astype(F32)
        parts = []
        for h in range(RET_HEADS):
            oh = o[:, h * RET_DV:(h + 1) * RET_DV]
            mu = jnp.mean(oh, axis=-1, keepdims=True)
            ctr = oh - mu
            var = jnp.mean(ctr * ctr, axis=-1, keepdims=True)
            parts.append(ctr * lax.rsqrt(var + EPS))
        on = jnp.concatenate(parts, axis=-1) * gn_ref[...]
        a = (on * _silu(g_ref[...].astype(F32))).astype(BF16)
    elif kind == "split":
        a = jnp.where(pl.program_id(1) < LAT_TILES, srcs[0][...], srcs[1][...])
    else:
        a = srcs[0][...]
    y = jnp.dot(a, wb_ref[...], preferred_element_type=F32)
    xn = x_ref[...] + mod_ref[2:3, :] * y
    xo_ref[...] = xn
    h = _normmod(xn, nf_ref[...], mod_ref[3:4, :], mod_ref[4:5, :])
    wa, wb = _route(h, first, *route_in, cr_ref, cnt_ref)
    sub = lax.broadcasted_iota(jnp.int32, (DX - D, ROW_TILE), 0)
    h_ref[:, :D] = h
    h_ref[:, D:] = jnp.where(sub == 0, wa, jnp.where(sub == 1, wb, 0.0)).T


def _mixer_out(kind, srcs, w, layer, xs, mods, norm_ffn, depth, route):
    k = w.shape[1]
    row = lambda width, col=0: pl.BlockSpec((None, ROW_TILE, width), lambda b, t: (b, t, col))
    const = lambda shape: pl.BlockSpec(shape, lambda b, t: (0,) * len(shape))
    per_tok = pl.BlockSpec((2, ROW_TILE), lambda b, t: (0, b * TPB + t))
    tri = jnp.asarray(np.triu(np.ones((ROW_TILE, ROW_TILE), np.float32), 1), BF16)
    if kind == "ret":
        src_specs = [row(2 * D), row(2 * D), row(2 * D, 2), _layer_spec((1, 2 * D), layer)]
    elif kind == "split":
        src_specs = [pl.BlockSpec((None, ROW_TILE, k), lambda b, t: (b, jnp.minimum(t, LAT_TILES - 1), 0)),
                     pl.BlockSpec((None, CTX, k), lambda b, t: (b, 0, 0))]
    else:
        src_specs = [row(k)]
    return pl.pallas_call(
        functools.partial(_mixer_out_kernel, kind=kind),
        grid=(BATCH, TPB),
        in_specs=src_specs + [_layer_spec((k, D), layer), row(D), _mod_spec(depth), _layer_spec((1, D), depth),
                              const((N_EXPERTS, D)), const((N_EXPERTS, 1)), const((ROW_TILE, ROW_TILE))],
        out_specs=[row(D), row(DX), per_tok, const((CLS_PAD, 128))],
        out_shape=[jax.ShapeDtypeStruct((BATCH, TT, D), F32), jax.ShapeDtypeStruct((BATCH, TT, DX), F32),
                   jax.ShapeDtypeStruct((2, NTOK), jnp.int32), jax.ShapeDtypeStruct((CLS_PAD, 128), F32)],
        scratch_shapes=[pltpu.VMEM((k, D), BF16)],
        compiler_params=_cp(2),
        name="mixer_out_" + kind,
    )(*srcs, w, xs, mods, norm_ffn, *route, tri)


def _fn_channel_kernel(h_ref, cs_ref, o_ref):
    for g in range(FN_GROUPS):
        r = jnp.dot(h_ref[:, g * FN_GC:(g + 1) * FN_GC], cs_ref[...], preferred_element_type=F32)
        o_ref[0, :, g * FN_GC:(g + 1) * FN_GC] = r[:, :FN_GC].astype(BF16)
        o_ref[1, :, g * FN_GC:(g + 1) * FN_GC] = r[:, FN_GC:].astype(BF16)


def _fn_channel(h, cs):
    return pl.pallas_call(
        _fn_channel_kernel,
        grid=(BATCH, TPB),
        in_specs=[
            pl.BlockSpec((None, ROW_TILE, D), lambda b, t: (b, t, 0)),
            pl.BlockSpec((FN_GC, 2 * FN_GC), lambda b, t: (0, 0)),
        ],
        out_specs=pl.BlockSpec((None, 2, ROW_TILE, D), lambda b, t: (b, 0, t, 0)),
        out_shape=jax.ShapeDtypeStruct((BATCH, 2, TT, D), BF16),
        compiler_params=_cp(2),
        name="fourier_channels",
    )(h, cs)


FN_R = 64
FN_CB = 16


def _fn_rows_kernel(ab_ref, f1_ref, y_ref, u32):
    u32[...] = ab_ref[...].astype(F32)
    for j in range(FN_CB):
        st = jnp.concatenate([u32[0, :, j, :], u32[1, :, j, :]], axis=0).astype(BF16)
        y = jnp.dot(f1_ref[j], st, preferred_element_type=F32)
        y_ref[0, j] = y[:FN_R].astype(BF16)
        y_ref[1, j] = y[FN_R:].astype(BF16)


def _fn_rows(ab, f1):
    ab5 = ab.reshape(BATCH, 2, TT // FN_R, FN_R, D)
    return pl.pallas_call(
        _fn_rows_kernel,
        grid=(BATCH, FN_R // FN_CB),
        in_specs=[
            pl.BlockSpec((None, 2, FN_R, FN_CB, D), lambda b, cb: (b, 0, 0, cb, 0)),
            pl.BlockSpec((FN_CB, 2 * FN_R, 2 * FN_R), lambda b, cb: (cb, 0, 0)),
        ],
        out_specs=pl.BlockSpec((None, 2, FN_CB, FN_R, D), lambda b, cb: (b, 0, cb, 0, 0)),
        out_shape=jax.ShapeDtypeStruct((BATCH, 2, FN_R, FN_R, D), BF16),
        scratch_shapes=[pltpu.VMEM((2, FN_R, FN_CB, D), F32)],
        compiler_params=_cp(2),
        name="fourier_rows",
    )(ab5, f1)


def _fn_cols_kernel(y_ref, f2_ref, o_ref, y32, z32):
    y32[...] = y_ref[...].astype(F32)
    for j in range(FN_CB):
        st = jnp.concatenate([y32[0, :, j, :], y32[1, :, j, :]], axis=0).astype(BF16)
        z32[:, j, :] = jnp.dot(f2_ref[...], st, preferred_element_type=F32)
    o_ref[...] = z32[...].astype(BF16)


def _fn_cols(y, f2):
    out = pl.pallas_call(
        _fn_cols_kernel,
        grid=(BATCH, FN_R // FN_CB),
        in_specs=[
            pl.BlockSpec((None, 2, FN_R, FN_CB, D), lambda b, kb: (b, 0, 0, kb, 0)),
            pl.BlockSpec((FN_R, 2 * FN_R), lambda b, kb: (0, 0)),
        ],
        out_specs=pl.BlockSpec((None, FN_R, FN_CB, D), lambda b, kb: (b, 0, kb, 0)),
        out_shape=jax.ShapeDtypeStruct((BATCH, FN_R, FN_R, D), BF16),
        scratch_shapes=[pltpu.VMEM((2, FN_R, FN_CB, D), F32), pltpu.VMEM((FN_R, FN_CB, D), F32)],
        compiler_params=_cp(2),
        name="fourier_cols",
    )(y, f2)
    return out.reshape(BATCH, SEQ, D)


def _fn_ctx_kernel(ab_ref, m_ref, o_ref):
    st = jnp.concatenate([ab_ref[0], ab_ref[1]], axis=0)
    o_ref[...] = jnp.dot(m_ref[...], st, preferred_element_type=F32).astype(BF16)


def _fn_ctx(ab, m_ctx):
    return pl.pallas_call(
        _fn_ctx_kernel,
        grid=(BATCH,),
        in_specs=[
            pl.BlockSpec((None, 2, CTX, D), lambda b: (b, 0, SEQ // CTX, 0)),
            pl.BlockSpec((CTX, 2 * CTX), lambda b: (0, 0)),
        ],
        out_specs=pl.BlockSpec((None, CTX, D), lambda b: (b, 0, 0)),
        out_shape=jax.ShapeDtypeStruct((BATCH, CTX, D), BF16),
        compiler_params=_cp(1),
        name="fourier_ctx",
    )(ab, m_ctx)


def _dft_tables():
    def cs_pair(n, scale):
        i = np.arange(n)
        ang = 2.0 * np.pi * ((i[:, None] * i[None, :]) % n) / n
        return np.cos(ang) * scale, np.sin(ang) * scale

    assert FN_GC == CTX
    c, s = cs_pair(FN_GC, FN_GC ** -0.5)
    cs = np.concatenate([c, s], axis=1)
    m_ctx = np.concatenate([c, -s], axis=1)
    c, s = cs_pair(FN_R, FN_R ** -0.5)
    f1 = np.block([[c, -s], [s, c]])
    f2 = np.concatenate([c, -s], axis=1)
    i = np.arange(FN_R)
    phi = 2.0 * np.pi * (i[:, None] * i[None, :]) / SEQ
    f1_tw = np.stack([np.block([[np.diag(np.cos(p)), -np.diag(np.sin(p))],
                                [np.diag(np.sin(p)), np.diag(np.cos(p))]]) @ f1 for p in phi])
    return jnp.asarray(cs, BF16), jnp.asarray(m_ctx, BF16), jnp.asarray(f1_tw, BF16), jnp.asarray(f2, BF16)


NA_ROWS = SEQ // GRID_W
NA_STEPS = TT // NA_QT
NA_NK = NA_KH * GRID_W
NA_DR = 2 * NA_KH - 1


def _na_row_start(step):
    r = jnp.minimum(step, NA_ROWS - 1)
    return r, jnp.clip(r - NA_KH // 2, 0, NA_ROWS - NA_KH)


def _na_bias_idx(step):
    r, rs = _na_row_start(step)
    return rs - r + (NA_KH - 1)


def _na_kernel(q_ref, k_ref, v_ref, bias_ref, o_ref):
    step = pl.program_id(1)
    lane = lax.broadcasted_iota(jnp.int32, (NA_QT, 128), 1)
    nt = (((1,), (1,)), ((), ()))
    scale = NA_HD ** -0.5

    def attend(start, d0=None):
        pairs = [slice(p * 128, (p + 1) * 128) for p in range(NA_HEADS // 2)]
        scores = []
        for p, sl in enumerate(pairs):
            qp = q_ref[:, sl] * scale
            zero = jnp.zeros_like(qp)
            q2 = jnp.concatenate([jnp.where(lane < NA_HD, qp, zero), jnp.where(lane >= NA_HD, qp, zero)], axis=0)
            s_c = lax.dot_general(q2, k_ref[SEQ:TT, sl], nt, preferred_element_type=F32)
            s_w = None
            if start is not None:
                bias = jnp.concatenate(
                    [jnp.concatenate([bias_ref[2 * p + hh, d0 + 2 * j2] for j2 in range(NA_KH // 2)], axis=1)
                     for hh in range(2)], axis=0)
                s_w = lax.dot_general(q2, k_ref[pl.ds(start, NA_NK), sl], nt, preferred_element_type=F32) + bias
            scores.append((s_c, s_w))
        probs = []
        for s_c, s_w in scores:
            m = jnp.max(s_c, axis=-1, keepdims=True)
            if s_w is not None:
                m = jnp.maximum(m, jnp.max(s_w, axis=-1, keepdims=True))
            p_c = jnp.exp(s_c - m)
            l = jnp.sum(p_c, axis=-1, keepdims=True)
            p_w = None
            if s_w is not None:
                p_w = jnp.exp(s_w - m)
                l = l + jnp.sum(p_w, axis=-1, keepdims=True)
                p_w = p_w.astype(BF16)
            probs.append((p_c.astype(BF16), p_w, l))
        for sl, (p_c, p_w, l) in zip(pairs, probs):
            acc = jnp.dot(p_c, v_ref[SEQ:TT, sl], preferred_element_type=F32)
            if p_w is not None:
                acc = acc + jnp.dot(p_w, v_ref[pl.ds(start, NA_NK), sl], preferred_element_type=F32)
            o2 = acc / l
            o_ref[:, sl] = jnp.where(lane < NA_HD, o2[:NA_QT], o2[NA_QT:]).astype(BF16)

    @pl.when(step < NA_ROWS)
    def _():
        _, rs = _na_row_start(step)
        attend(pl.multiple_of(rs * GRID_W, GRID_W), _na_bias_idx(step))

    @pl.when(step >= NA_ROWS)
    def _():
        attend(None)


def _na_attention(qkv, bias):
    return pl.pallas_call(
        _na_kernel,
        grid=(BATCH, NA_STEPS),
        in_specs=[
            pl.BlockSpec((None, NA_QT, D), lambda b, s: (b, s, 0)),
            pl.BlockSpec((None, TT, D), lambda b, s: (b, 0, 1), pipeline_mode=pl.Buffered(1)),
            pl.BlockSpec((None, TT, D), lambda b, s: (b, 0, 2), pipeline_mode=pl.Buffered(1)),
            pl.BlockSpec((NA_HEADS, NA_DR - 1, GRID_W, 2 * GRID_W), lambda b, s: (0, 0, 0, 0),
                         pipeline_mode=pl.Buffered(1)),
        ],
        out_specs=pl.BlockSpec((None, NA_QT, D), lambda b, s: (b, s, 0)),
        out_shape=jax.ShapeDtypeStruct((BATCH, TT, D), BF16),
        compiler_params=_cp(2),
        name="na_attention",
    )(qkv, qkv, qkv, bias)


def _na_bias_table(rpb):
    qc = np.arange(GRID_W)
    kcol = np.arange(GRID_W)
    cstart = np.clip(qc - NA_KW // 2, 0, GRID_W - NA_KW)
    valid = (kcol[None, :] >= cstart[:, None]) & (kcol[None, :] < cstart[:, None] + NA_KW)
    dc = kcol[None, :] - qc[:, None] + (NA_KW - 1)
    pick = ((dc[None] == np.arange(2 * NA_KW - 1)[:, None, None]) & valid[None]).astype(np.float32)
    tab = jnp.einsum("hdj,jqk->hdqk", rpb.astype(F32), jnp.asarray(pick), precision=HIGHEST)
    tab = tab + jnp.asarray(np.where(valid, 0.0, NEG_INF).astype(np.float32))
    return jnp.concatenate([tab[:, :NA_DR - 1], tab[:, 1:]], axis=-1)


def _route(h, first, rw_ref, rb_ref, tri_ref, cr_ref, cnt_ref):
    logits = lax.dot_general(rw_ref[...], h.astype(BF16), (((1,), (1,)), ((), ())),
                             preferred_element_type=F32)
    sc = jax.nn.sigmoid(logits)
    sel = sc + rb_ref[...]
    sc_r = [sc[e:e + 1, :] for e in range(N_EXPERTS)]
    sel_r = [sel[e:e + 1, :] for e in range(N_EXPERTS)]
    zero_i = jnp.zeros_like(sel_r[0]).astype(jnp.int32)

    best = gi = None
    picks = []
    for g in range(N_GROUPS):
        a = sel_r[g * EPG:(g + 1) * EPG]
        u = sc_r[g * EPG:(g + 1) * EPG]
        gs = None
        for i in range(EPG):
            for j in range(i + 1, EPG):
                pair = a[i] + a[j]
                gs = pair if gs is None else jnp.maximum(gs, pair)
        m1, i1, s1 = a[0], zero_i, u[0]
        for j in range(1, EPG):
            upd = a[j] > m1
            m1 = jnp.where(upd, a[j], m1)
            i1 = jnp.where(upd, j, i1)
            s1 = jnp.where(upd, u[j], s1)
        m2 = i2 = s2 = None
        for j in range(EPG):
            cand = jnp.where(i1 == j, -jnp.inf, a[j])
            if m2 is None:
                m2, i2, s2 = cand, zero_i, u[0]
            else:
                upd = cand > m2
                m2 = jnp.where(upd, cand, m2)
                i2 = jnp.where(upd, j, i2)
                s2 = jnp.where(upd, u[j], s2)
        picks.append((i1 + g * EPG, i2 + g * EPG, s1, s2))
        if best is None:
            best, gi = gs, zero_i
        else:
            upd = gs > best
            best = jnp.where(upd, gs, best)
            gi = jnp.where(upd, g, gi)

    e1, e2, s1, s2 = picks[0]
    for g in range(1, N_GROUPS):
        on = gi == g
        e1 = jnp.where(on, picks[g][0], e1)
        e2 = jnp.where(on, picks[g][1], e2)
        s1 = jnp.where(on, picks[g][2], s1)
        s2 = jnp.where(on, picks[g][3], s2)
    tot = s1 + s2
    w1, w2 = s1 / tot, s2 / tot

    swap = e2 < e1
    la = jnp.where(swap, e2, e1) - gi * EPG
    lb = jnp.where(swap, e1, e2) - gi * EPG
    cls = gi * PAIRS + jnp.where(la == 0, 0, jnp.where(la == 1, 3, 5)) + lb - la - 1
    wa, wb = jnp.where(swap, w2, w1), jnp.where(swap, w1, w2)

    @pl.when(first)
    def _():
        cnt_ref[...] = jnp.zeros_like(cnt_ref)

    oh = (lax.broadcasted_iota(jnp.int32, (cnt_ref.shape[0], cls.shape[1]), 0) == cls).astype(F32)
    before = jnp.dot(oh.astype(BF16), tri_ref[...], preferred_element_type=F32)
    base = cnt_ref[:, 0:1]
    cr_ref[0:1, :] = cls
    cr_ref[1:2, :] = jnp.sum(oh * (base + before), axis=0, keepdims=True).astype(jnp.int32)
    cnt_ref[...] = cnt_ref[...] + jnp.sum(oh, axis=1, keepdims=True)
    return wa, wb


DMA_UNROLL = 8


def _row_copy(src_ref, src_row, dst_ref, dst_row, sem):
    return pltpu.make_async_copy(src_ref.at[pl.ds(src_row, 1)], dst_ref.at[pl.ds(dst_row, 1)], sem)


def _for_each_row(fn, rows=ROW_TILE):
    def group(gidx, carry):
        for u in range(DMA_UNROLL):
            fn(gidx * DMA_UNROLL + u)
        return carry

    lax.fori_loop(0, rows // DMA_UNROLL, group, 0)


def _dispatch_kernel(zrow_ref, dest_ref, h_ref, xs_ref, zbuf, sem, zsem):
    @pl.when(pl.program_id(0) == 0)
    def _():
        zbuf[...] = jnp.zeros_like(zbuf)

        def zero_copy(e):
            start = pl.multiple_of(jnp.maximum(zrow_ref[e], 0), MOE_BM)
            return pltpu.make_async_copy(zbuf, xs_ref.at[pl.ds(start, MOE_BM)], zsem)

        for e in range(MOE_NZ):
            pl.when(zrow_ref[e] >= 0)(lambda e=e: zero_copy(e).start())
        for e in range(MOE_NZ):
            pl.when(zrow_ref[e] >= 0)(lambda e=e: zero_copy(e).wait())

    _for_each_row(lambda r: _row_copy(h_ref, r, xs_ref, dest_ref[0, r], sem).start(), DISPATCH_TILE)
    _for_each_row(lambda r: _row_copy(h_ref, r, xs_ref, dest_ref[0, r], sem).wait(), DISPATCH_TILE)


def _dispatch(h, dest, zrow):
    grid_spec = pltpu.PrefetchScalarGridSpec(
        num_scalar_prefetch=1,
        grid=(NTOK // DISPATCH_TILE,),
        in_specs=[
            pl.BlockSpec((None, 1, DISPATCH_TILE), lambda i, z: (i, 0, 0), memory_space=pltpu.SMEM),
            pl.BlockSpec((DISPATCH_TILE, DX), lambda i, z: (i, 0)),
        ],
        out_specs=pl.BlockSpec(memory_space=pl.ANY),
        scratch_shapes=[pltpu.VMEM((MOE_BM, DX), F32), pltpu.SemaphoreType.DMA(()), pltpu.SemaphoreType.DMA(())],
    )
    return pl.pallas_call(
        _dispatch_kernel,
        grid_spec=grid_spec,
        out_shape=jax.ShapeDtypeStruct((MOE_P, DX), F32),
        compiler_params=pltpu.CompilerParams(dimension_semantics=("arbitrary",), vmem_limit_bytes=VMEM_LIMIT,
                                             has_side_effects=True),
        name="moe_dispatch",
    )(zrow, dest.reshape(NTOK // DISPATCH_TILE, 1, DISPATCH_TILE), h)


def _experts_kernel(blk_ref, ex_ref, nu_ref, x_ref, *rest, which):
    if which:
        prev_ref, wg_ref, wu_ref, wd_ref, y_ref, wgb, wub, wdb = rest
    else:
        wg_ref, wu_ref, wd_ref, y_ref, wgb, wub, wdb = rest
    del blk_ref
    v = pl.program_id(0)
    active = v < nu_ref[0]
    changed = (v == 0) | (ex_ref[v] != ex_ref[jnp.maximum(v - 1, 0)])

    @pl.when(active & changed)
    def _():
        wgb[...] = wg_ref[...].astype(BF16)
        wub[...] = wu_ref[...].astype(BF16)
        wdb[...] = wd_ref[...].astype(BF16)

    @pl.when(active)
    def _():
        x = x_ref[:, :D].astype(BF16)
        g = jnp.dot(x, wgb[...], preferred_element_type=F32)
        u = jnp.dot(x, wub[...], preferred_element_type=F32)
        hh = (_silu(g) * u).astype(BF16)
        y = jnp.dot(hh, wdb[...], preferred_element_type=F32) * x_ref[:, D + which:D + which + 1]
        y_ref[...] = prev_ref[...] + y if which else y

    @pl.when(jnp.logical_not(active))
    def _():
        y_ref[...] = jnp.zeros_like(y_ref)


def _experts(xs, prev, which, blocks, experts, n_used, w_gate, w_up, w_down, depth):
    wspec = pl.BlockSpec((None, None, D, D), lambda v, bl, ex, nu: (depth, ex[v], 0, 0))
    yspec = pl.BlockSpec((MOE_BM, D), lambda v, bl, ex, nu: (bl[v], 0))
    grid_spec = pltpu.PrefetchScalarGridSpec(
        num_scalar_prefetch=3,
        grid=(MOE_NB,),
        in_specs=[pl.BlockSpec((MOE_BM, DX), lambda v, bl, ex, nu: (bl[v], 0))] + ([yspec] if which else [])
        + [wspec, wspec, wspec],
        out_specs=yspec,
        scratch_shapes=[pltpu.VMEM((D, D), BF16)] * 3,
    )
    return pl.pallas_call(
        functools.partial(_experts_kernel, which=which),
        grid_spec=grid_spec,
        out_shape=jax.ShapeDtypeStruct((MOE_P, D), F32),
        compiler_params=_cp(1),
        name="moe_experts_%d" % which,
    )(blocks, experts, n_used, xs, *([prev] if which else []), w_gate, w_up, w_down)


def _combine_kernel(dest_ref, next_ref, yb_ref, x_ref, mod_ref, modn_ref, gn_ref, *rest, final):
    if final:
        out_ref, ybuf, sem = rest
    else:
        xo_ref, h_ref, ybuf, sem = rest
    t = pl.program_id(1)
    step = pl.program_id(0) * TPB + t
    slot = lax.rem(step, 2)

    def gather(slots_ref, buf):
        return lambda r: _row_copy(yb_ref, slots_ref[0, r], ybuf.at[buf], r, sem.at[buf])

    @pl.when(step == 0)
    def _():
        _for_each_row(lambda r: gather(dest_ref, slot)(r).start())

    @pl.when(step + 1 < BATCH * TPB)
    def _():
        _for_each_row(lambda r: gather(next_ref, 1 - slot)(r).start())

    _for_each_row(lambda r: gather(dest_ref, slot)(r).wait())

    def body():
        xn = x_ref[...] + mod_ref[5:6, :] * ybuf[slot]
        if final:
            out_ref[...] = _normmod(xn, gn_ref[...], 0.0, 0.0)
        else:
            xo_ref[...] = xn
            h_ref[...] = _normmod(xn, gn_ref[...], modn_ref[0:1, :], modn_ref[1:2, :]).astype(BF16)

    if final:
        pl.when(t < LAT_TILES)(body)
    else:
        body()


def _combine(yb, dest, xs, mods, depth, g_next, final):
    row = pl.BlockSpec((None, ROW_TILE, D), lambda b, t: (b, t, 0))
    next_depth = depth if final else depth + 1
    if final:
        out_specs = pl.BlockSpec((None, ROW_TILE, D), lambda b, t: (b, jnp.minimum(t, LAT_TILES - 1), 0))
        out_shape = jax.ShapeDtypeStruct((BATCH, SEQ, D), F32)
    else:
        out_specs = [row, row]
        out_shape = [jax.ShapeDtypeStruct((BATCH, TT, D), F32), jax.ShapeDtypeStruct((BATCH, TT, D), BF16)]
    last = BATCH * TPB - 1
    return pl.pallas_call(
        functools.partial(_combine_kernel, final=final),
        grid=(BATCH, TPB),
        in_specs=[
            pl.BlockSpec((None, 1, ROW_TILE), lambda b, t: (b * TPB + t, 0, 0), memory_space=pltpu.SMEM),
            pl.BlockSpec((None, 1, ROW_TILE), lambda b, t: (jnp.minimum(b * TPB + t + 1, last), 0, 0),
                         memory_space=pltpu.SMEM),
            pl.BlockSpec(memory_space=pl.ANY),
            row, _mod_spec(depth), _mod_spec(next_depth),
            g_next[1],
        ],
        out_specs=out_specs,
        out_shape=out_shape,
        scratch_shapes=[pltpu.VMEM((2, ROW_TILE, D), F32), pltpu.SemaphoreType.DMA((2,))],
        compiler_params=_cp(2),
        name="moe_combine",
    )(dest, dest, yb, xs, mods, mods, g_next[0])


def _class_experts():
    pairs = [(a, b) for a in range(EPG) for b in range(a + 1, EPG)]
    lo = [g * EPG + a for g in range(N_GROUPS) for a, _ in pairs]
    hi = [g * EPG + b for g in range(N_GROUPS) for _, b in pairs]
    return jnp.asarray(lo, jnp.int32), jnp.asarray(hi, jnp.int32)


def _moe_plan(cls, rank, counts):
    padded = (counts + MOE_BM - 1) // MOE_BM * MOE_BM
    pend = jnp.cumsum(padded)
    pstart = pend - padded
    cid = jnp.arange(N_CLS, dtype=jnp.int32)
    dest = rank + jnp.sum(jnp.where(cls[None, :] == cid[:, None], pstart[:, None], 0), axis=0)
    dest = dest.astype(jnp.int32).reshape(NTOK // ROW_TILE, 1, ROW_TILE)
    blk = jnp.arange(MOE_NB, dtype=jnp.int32)
    block_c = jnp.minimum(jnp.sum(pend[None, :] <= (blk * MOE_BM)[:, None], axis=1), N_CLS - 1)
    n_used = (pend[-1] // MOE_BM).astype(jnp.int32)
    used = blk < n_used
    lo, hi = _class_experts()
    onehot_c = block_c[:, None] == cid[None, :]
    e_lo = jnp.sum(jnp.where(onehot_c, lo[None, :], 0), axis=1)
    e_hi = jnp.sum(jnp.where(onehot_c, hi[None, :], 0), axis=1)

    def hold_last(e):
        return jnp.where(used, e, jnp.sum(jnp.where(blk == n_used - 1, e, 0))).astype(jnp.int32)

    key = jnp.where(used, e_hi, N_EXPERTS)
    earlier = (key[None, :] < key[:, None]) | ((key[None, :] == key[:, None]) & (blk[None, :] < blk[:, None]))
    pos = jnp.sum(earlier, axis=1)
    at = pos[None, :] == blk[:, None]
    order1 = jnp.sum(jnp.where(at, blk[None, :], 0), axis=1).astype(jnp.int32)
    e1 = jnp.sum(jnp.where(at, e_hi[None, :], 0), axis=1)
    tail = blk[MOE_NB - MOE_TAIL:] * MOE_BM
    zrow = jnp.concatenate([jnp.where(padded > 0, pend - MOE_BM, -1), jnp.where(tail >= pend[-1], tail, -1)])
    return dest, (blk, hold_last(e_lo)), (order1, hold_last(e1)), n_used[None], zrow.astype(jnp.int32)


def _moe(h, routing, xs, mods, depth, g_next, w_gate, w_up, w_down, final):
    cr, cnt = routing
    dest, visit0, visit1, n_used, zrow = _moe_plan(cr[0], cr[1], cnt[:N_CLS, 0].astype(jnp.int32))
    xsorted = _dispatch(h.reshape(NTOK, DX), dest, zrow)
    y0 = _experts(xsorted, None, 0, *visit0, n_used, w_gate, w_up, w_down, depth)
    yb = _experts(xsorted, y0, 1, *visit1, n_used, w_gate, w_up, w_down, depth)
    return _combine(yb, dest, xs, mods, depth, g_next, final)


def _rope_tables():
    nf = RET_DK // 4
    t = np.arange(SEQ)
    inv = ROPE_BASE ** (-np.arange(nf, dtype=np.float32) / nf)
    cos = np.ones((TT, 256), np.float32)
    sin = np.zeros((TT, 256), np.float32)
    for seg, pos in enumerate(((t // GRID_W).astype(np.float32), (t % GRID_W).astype(np.float32))):
        ang = (pos[:, None] * inv[None, :]).astype(np.float32)
        c, s = np.cos(ang), np.sin(ang)
        cos[:SEQ, seg * 128:(seg + 1) * 128] = np.concatenate([c, c], axis=1)
        sin[:SEQ, seg * 128:(seg + 1) * 128] = np.concatenate([-s, s], axis=1)
    ks = np.float32(RET_DK ** -0.5)
    return jnp.asarray(np.stack([cos, cos * ks])), jnp.asarray(np.stack([sin, sin * ks]))


def _decay_tables(decay_logit):
    C = RET_CHUNK
    log_g = jax.nn.log_sigmoid(decay_logit.astype(F32))
    pos = jnp.arange(C, dtype=F32)
    diff = pos[:, None] - pos[None, :]
    lg = log_g[:, :, None, None]
    dm_f = jnp.where(diff >= 0, jnp.exp(jnp.maximum(diff, 0.0)[None, None] * lg), 0.0)
    dm_b = jnp.where(diff <= 0, jnp.exp(jnp.maximum(-diff, 0.0)[None, None] * lg), 0.0)
    dmat = jnp.stack([dm_f[0], dm_b[1]])
    lgc = log_g[:, :, None]
    qdec = jnp.stack([jnp.exp((pos + 1.0)[None, :] * lgc[0]), jnp.exp((C - pos)[None, :] * lgc[1])])
    kdec = jnp.stack([jnp.exp((C - 1.0 - pos)[None, :] * lgc[0]), jnp.exp(pos[None, :] * lgc[1])])
    cdec = jnp.exp(C * log_g).reshape(-1)
    return dmat, qdec[..., None], kdec[..., None], cdec


def kernel(x, c, ctx, c_ctx, w_mod, b_mod, norm_mix, norm_ffn, norm_final, ret_w_in, ret_decay, ret_gn, ret_w_out,
           fn_w_out, na_w_qkv, na_rpb, na_w_out, router_w, router_bias, moe_w_gate, moe_w_up, moe_w_down):
    cvec = jnp.concatenate([c_ctx[None, :], c, jnp.zeros((8 - 1 - BATCH, D), F32)], axis=0)
    mod_all = _modulation(cvec, w_mod, b_mod).reshape(DEPTH, 8, N_MOD, D)
    pad = jnp.zeros((DEPTH, BATCH, 2, 8 - N_MOD, D), F32)
    mods = jnp.stack([jnp.broadcast_to(mod_all[:, 0:1], (DEPTH, BATCH, N_MOD, D)), mod_all[:, 1:1 + BATCH]], axis=2)
    mods = jnp.concatenate([mods, pad], axis=3)

    rope = _rope_tables()
    cs, dft_ctx, dft_f1, dft_f2 = _dft_tables()
    route = (router_w.T.astype(BF16), router_bias.reshape(N_EXPERTS, 1).astype(F32))

    norm_mix3 = norm_mix.reshape(DEPTH, 1, D)
    norm_ffn3 = norm_ffn.reshape(DEPTH, 1, D)
    ret_gn3 = ret_gn.reshape(-1, 1, 2 * D)

    xs, h = _embed(x, ctx, mods, norm_mix3)
    out = None
    for i in range(DEPTH):
        kind, j = i % 3, i // 3
        if kind == 0:
            qkvg = _project(h.reshape(NTOK, D), ret_w_in, j, rope).reshape(BATCH, TT, 6 * D)
            dmat, qdec, kdec, cdec = _decay_tables(ret_decay[j])
            o_f, o_b = _retention(qkvg, dmat, qdec, kdec, cdec)
            xs, hf, *routing = _mixer_out("ret", (o_f, o_b, qkvg, ret_gn3), ret_w_out, j, xs, mods, norm_ffn3, i, route)
        elif kind == 1:
            ab = _fn_channel(h, cs)
            f_lat = _fn_cols(_fn_rows(ab, dft_f1), dft_f2)
            xs, hf, *routing = _mixer_out(
                "split", (f_lat, _fn_ctx(ab, dft_ctx)), fn_w_out, j, xs, mods, norm_ffn3, i, route)
        else:
            qkv = _project(h.reshape(NTOK, D), na_w_qkv, j).reshape(BATCH, TT, 3 * D)
            o = _na_attention(qkv, _na_bias_table(na_rpb[j]))
            xs, hf, *routing = _mixer_out("plain", (o,), na_w_out, j, xs, mods, norm_ffn3, i, route)
        final = i == DEPTH - 1
        if final:
            g_next = (norm_final[None, :], pl.BlockSpec((1, D), lambda b, t: (0, 0)))
        else:
            g_next = (norm_mix3, _layer_spec((1, D), i + 1))
        res = _moe(hf, routing, xs, mods, i, g_next, moe_w_gate, moe_w_up, moe_w_down, final)
        if final:
            out = res
        else:
            xs, h = res
    return out
```

```python
import functools

import numpy as np
import jax
import jax.numpy as jnp
from jax import lax
from jax.experimental import pallas as pl
from jax.experimental.pallas import tpu as pltpu

F32 = jnp.float32
BF16 = jnp.bfloat16
HIGHEST = lax.Precision.HIGHEST

D = 1024
BATCH = 4
SEQ = 4096
CTX = 256
TT = SEQ + CTX
NTOK = BATCH * TT
DEPTH = 4
GRID_W = 64
EPS = 1e-6
NEG_INF = -1e30
N_MOD = 6

RET_HEADS = 4
RET_DK = D // RET_HEADS
RET_DV = 2 * RET_DK
RET_CHUNK = 256
ROPE_BASE = 10000.0

FN_GROUPS = 4
FN_GC = D // FN_GROUPS

NA_HEADS = 16
NA_HD = D // NA_HEADS
NA_KH = 8
NA_KW = 16

N_EXPERTS = 16
N_GROUPS = 4
EPG = N_EXPERTS // N_GROUPS

ROW_TILE = 256
TPB = TT // ROW_TILE
LAT_TILES = SEQ // ROW_TILE
MM_TM = TT // 4
MOE_BM = 256
PAIRS = EPG * (EPG - 1) // 2
N_CLS = N_GROUPS * PAIRS
CLS_PAD = 32
MOE_NB = (NTOK + N_CLS * (MOE_BM - 1) + MOE_BM - 1) // MOE_BM
MOE_P = MOE_NB * MOE_BM
MOE_TAIL = MOE_NB - NTOK // MOE_BM
MOE_NZ = N_CLS + MOE_TAIL
DISPATCH_TILE = 512
DX = D + 128
NA_QT = 2 * GRID_W
VMEM_LIMIT = 56 * 1024 * 1024


def _cp(n_axes, vmem=VMEM_LIMIT):
    return pltpu.CompilerParams(dimension_semantics=("arbitrary",) * n_axes, vmem_limit_bytes=vmem)


def _silu(x):
    return x * jax.nn.sigmoid(x)


def _normmod(x, g, shift, scale):
    y = x * lax.rsqrt(jnp.mean(x * x, axis=-1, keepdims=True) + EPS)
    return (y * g) * (1.0 + scale) + shift


def _mod_idx(t):
    return jnp.where(t < LAT_TILES, 1, 0)


def _layer_spec(shape, layer):
    return pl.BlockSpec((None,) + shape, lambda *_: (layer,) + (0,) * len(shape))


def _mod_spec(depth):
    return pl.BlockSpec((None, None, None, 8, D), lambda b, t: (depth, b, _mod_idx(t), 0, 0))


def _mod_kernel(c_ref, w_ref, b_ref, o_ref):
    a = _silu(c_ref[...])
    o_ref[...] = jnp.dot(a.astype(BF16), w_ref[...].astype(BF16), preferred_element_type=F32) + b_ref[...]


def _modulation(cvec, w_mod, b_mod):
    return pl.pallas_call(
        _mod_kernel,
        grid=(DEPTH, N_MOD),
        in_specs=[
            pl.BlockSpec((8, D), lambda i, j: (0, 0)),
            pl.BlockSpec((None, D, D), lambda i, j: (i, 0, j)),
            pl.BlockSpec((None, 1, D), lambda i, j: (i, 0, j)),
        ],
        out_specs=pl.BlockSpec((None, 8, D), lambda i, j: (i, 0, j)),
        out_shape=jax.ShapeDtypeStruct((DEPTH, 8, N_MOD * D), F32),
        compiler_params=_cp(2),
        name="modulation",
    )(cvec, w_mod, b_mod.reshape(DEPTH, 1, N_MOD * D))


def _embed_kernel(x_ref, ctx_ref, mod_ref, g_ref, xs_ref, h_ref):
    t = pl.program_id(1)

    def emit(src):
        xs_ref[...] = src
        h_ref[...] = _normmod(src, g_ref[...], mod_ref[0:1, :], mod_ref[1:2, :]).astype(BF16)

    @pl.when(t < LAT_TILES)
    def _():
        emit(x_ref[...])

    @pl.when(t >= LAT_TILES)
    def _():
        emit(ctx_ref[...])


def _embed(x, ctx, mods, g):
    return pl.pallas_call(
        _embed_kernel,
        grid=(BATCH, TPB),
        in_specs=[
            pl.BlockSpec((None, ROW_TILE, D), lambda b, t: (b, jnp.minimum(t, LAT_TILES - 1), 0)),
            pl.BlockSpec((None, CTX, D), lambda b, t: (b, 0, 0)),
            _mod_spec(0),
            _layer_spec((1, D), 0),
        ],
        out_specs=[
            pl.BlockSpec((None, ROW_TILE, D), lambda b, t: (b, t, 0)),
            pl.BlockSpec((None, ROW_TILE, D), lambda b, t: (b, t, 0)),
        ],
        out_shape=[
            jax.ShapeDtypeStruct((BATCH, TT, D), F32),
            jax.ShapeDtypeStruct((BATCH, TT, D), BF16),
        ],
        compiler_params=_cp(2),
        name="embed",
    )(x, ctx, mods, g)


def _rope_store(acc, cos_ref, sin_ref, o_ref):
    for c in range(acc.shape[1] // 128):
        xc = acc[:, c * 128:(c + 1) * 128]
        tsl = slice((c % 2) * 128, (c % 2 + 1) * 128)
        rot = xc * cos_ref[:, tsl] + pltpu.roll(xc, 64, 1) * sin_ref[:, tsl]
        o_ref[:, c * 128:(c + 1) * 128] = rot.astype(o_ref.dtype)


def _proj_kernel(a_ref, w_ref, *rest, n_rope):
    if n_rope:
        cos_ref, sin_ref, o_ref, wb_ref = rest
    else:
        o_ref, wb_ref = rest
    j = pl.program_id(0)

    @pl.when(pl.program_id(1) == 0)
    def _():
        wb_ref[...] = w_ref[...].astype(BF16)

    acc = jnp.dot(a_ref[...], wb_ref[...], preferred_element_type=F32)
    if n_rope:
        @pl.when(j < n_rope)
        def _():
            _rope_store(acc, cos_ref, sin_ref, o_ref)

        @pl.when(j >= n_rope)
        def _():
            o_ref[...] = acc.astype(o_ref.dtype)
    else:
        o_ref[...] = acc.astype(o_ref.dtype)


def _project(a, w, layer, rope=None):
    _, k, n = w.shape
    tn = D
    n_rope = 0 if rope is None else rope[0].shape[0]
    in_specs = [
        pl.BlockSpec((MM_TM, k), lambda j, i: (i, 0)),
        pl.BlockSpec((None, k, tn), lambda j, i: (layer, 0, j)),
    ]
    args = [a, w]
    if n_rope:
        tspec = pl.BlockSpec((None, MM_TM, 256), lambda j, i: (jnp.minimum(j, n_rope - 1), i % (TT // MM_TM), 0))
        in_specs += [tspec, tspec]
        args += list(rope)
    return pl.pallas_call(
        functools.partial(_proj_kernel, n_rope=n_rope),
        grid=(n // tn, NTOK // MM_TM),
        in_specs=in_specs,
        out_specs=pl.BlockSpec((MM_TM, tn), lambda j, i: (i, j)),
        out_shape=jax.ShapeDtypeStruct((NTOK, n), BF16),
        scratch_shapes=[pltpu.VMEM((k, tn), BF16)],
        compiler_params=_cp(2),
        name="project",
    )(*args)


N_CHUNKS = TT // RET_CHUNK
LAT_CHUNKS = SEQ // RET_CHUNK
CTX_CHUNKS = CTX // RET_CHUNK


def _fwd_chunk(n):
    return jnp.where(n < CTX_CHUNKS, LAT_CHUNKS + n, n - CTX_CHUNKS)


def _bwd_chunk(n):
    return N_CHUNKS - 1 - n


def _ret_kernel(cdec_ref, qf, kf, vf, qb, kb, vb, dmat_ref, qdec_ref, kdec_ref, of_ref, ob_ref, s_ref):
    @pl.when(pl.program_id(1) == 0)
    def _():
        s_ref[...] = jnp.zeros_like(s_ref)

    for d, (q_r, k_r, v_r, o_r) in enumerate(((qf, kf, vf, of_ref), (qb, kb, vb, ob_ref))):
        for h in range(RET_HEADS):
            q = q_r[:, h * RET_DK:(h + 1) * RET_DK]
            k = k_r[:, h * RET_DK:(h + 1) * RET_DK]
            v = v_r[:, h * RET_DV:(h + 1) * RET_DV]
            att = lax.dot_general(q, k, (((1,), (1,)), ((), ())), preferred_element_type=F32) * dmat_ref[d, h]
            s = s_ref[d, h]
            qd = (q.astype(F32) * qdec_ref[d, h]).astype(BF16)
            o = (jnp.dot(att.astype(BF16), v, preferred_element_type=F32)
                 + jnp.dot(qd, s.astype(BF16), preferred_element_type=F32))
            kd = (k.astype(F32) * kdec_ref[d, h]).astype(BF16)
            s_ref[d, h] = s * cdec_ref[d * RET_HEADS + h] + lax.dot_general(
                kd, v, (((0,), (0,)), ((), ())), preferred_element_type=F32)
            o_r[:, h * RET_DV:(h + 1) * RET_DV] = o.astype(o_r.dtype)


def _retention(qkvg, dmat, qdec, kdec, cdec):
    C = RET_CHUNK

    def spec(width, col, chunk_fn):
        return pl.BlockSpec((None, C, width), lambda b, n, cd: (b, chunk_fn(n), col))

    ospec_f = pl.BlockSpec((None, C, 2 * D), lambda b, n, cd: (b, _fwd_chunk(n), 0))
    ospec_b = pl.BlockSpec((None, C, 2 * D), lambda b, n, cd: (b, _bwd_chunk(n), 0))
    full = lambda shape: pl.BlockSpec(shape, lambda b, n, cd: (0,) * len(shape))
    grid_spec = pltpu.PrefetchScalarGridSpec(
        num_scalar_prefetch=1,
        grid=(BATCH, N_CHUNKS),
        in_specs=[
            spec(D, 0, _fwd_chunk), spec(D, 1, _fwd_chunk), spec(2 * D, 1, _fwd_chunk),
            spec(D, 0, _bwd_chunk), spec(D, 1, _bwd_chunk), spec(2 * D, 1, _bwd_chunk),
            full((2, RET_HEADS, C, C)), full((2, RET_HEADS, C, 1)), full((2, RET_HEADS, C, 1)),
        ],
        out_specs=[ospec_f, ospec_b],
        scratch_shapes=[pltpu.VMEM((2, RET_HEADS, RET_DK, RET_DV), F32)],
    )
    return pl.pallas_call(
        _ret_kernel,
        grid_spec=grid_spec,
        out_shape=[jax.ShapeDtypeStruct((BATCH, TT, 2 * D), BF16)] * 2,
        compiler_params=_cp(2),
        name="retention",
    )(cdec, qkvg, qkvg, qkvg, qkvg, qkvg, qkvg, dmat, qdec, kdec)


def _mixer_out_kernel(*refs, kind):
    n_src = {"ret": 4, "split": 2, "plain": 1}[kind]
    srcs, (w_ref, x_ref, mod_ref, nf_ref), route_in = refs[:n_src], refs[n_src:n_src + 4], refs[n_src + 4:n_src + 7]
    xo_ref, h_ref, cr_ref, cnt_ref, wb_ref = refs[n_src + 7:]
    first = (pl.program_id(0) == 0) & (pl.program_id(1) == 0)

    @pl.when(first)
    def _():
        wb_ref[...] = w_ref[...].astype(BF16)

    if kind == "ret":
        of_ref, ob_ref, g_ref, gn_ref = srcs
        o = of_ref[...].astype(F32) + ob_ref[...].astype(F32)
        parts = []
        for h in range(RET_HEADS):
            oh = o[:, h * RET_DV:(h + 1) * RET_DV]
            mu = jnp.mean(oh, axis=-1, keepdims=True)
            ctr = oh - mu
            var = jnp.mean(ctr * ctr, axis=-1, keepdims=True)
            parts.append(ctr * lax.rsqrt(var + EPS))
        on = jnp.concatenate(parts, axis=-1) * gn_ref[...]
        a = (on * _silu(g_ref[...].astype(F32))).astype(BF16)
    elif kind == "split":
        a = jnp.where(pl.program_id(1) < LAT_TILES, srcs[0][...], srcs[1][...])
    else:
        a = srcs[0][...]
    y = jnp.dot(a, wb_ref[...], preferred_element_type=F32)
    xn = x_ref[...] + mod_ref[2:3, :] * y
    xo_ref[...] = xn
    h = _normmod(xn, nf_ref[...], mod_ref[3:4, :], mod_ref[4:5, :])
    wa, wb = _route(h, first, *route_in, cr_ref, cnt_ref)
    sub = lax.broadcasted_iota(jnp.int32, (DX - D, ROW_TILE), 0)
    h_ref[:, :D] = h
    h_ref[:, D:] = jnp.where(sub == 0, wa, jnp.where(sub == 1, wb, 0.0)).T


def _mixer_out(kind, srcs, w, layer, xs, mods, norm_ffn, depth, route):
    k = w.shape[1]
    row = lambda width, col=0: pl.BlockSpec((None, ROW_TILE, width), lambda b, t: (b, t, col))
    const = lambda shape: pl.BlockSpec(shape, lambda b, t: (0,) * len(shape))
    per_tok = pl.BlockSpec((2, ROW_TILE), lambda b, t: (0, b * TPB + t))
    tri = jnp.asarray(np.triu(np.ones((ROW_TILE, ROW_TILE), np.float32), 1), BF16)
    if kind == "ret":
        src_specs = [row(2 * D), row(2 * D), row(2 * D, 2), _layer_spec((1, 2 * D), layer)]
    elif kind == "split":
        src_specs = [pl.BlockSpec((None, ROW_TILE, k), lambda b, t: (b, jnp.minimum(t, LAT_TILES - 1), 0)),
                     pl.BlockSpec((None, CTX, k), lambda b, t: (b, 0, 0))]
    else:
        src_specs = [row(k)]
    return pl.pallas_call(
        functools.partial(_mixer_out_kernel, kind=kind),
        grid=(BATCH, TPB),
        in_specs=src_specs + [_layer_spec((k, D), layer), row(D), _mod_spec(depth), _layer_spec((1, D), depth),
                              const((N_EXPERTS, D)), const((N_EXPERTS, 1)), const((ROW_TILE, ROW_TILE))],
        out_specs=[row(D), row(DX), per_tok, const((CLS_PAD, 128))],
        out_shape=[jax.ShapeDtypeStruct((BATCH, TT, D), F32), jax.ShapeDtypeStruct((BATCH, TT, DX), F32),
                   jax.ShapeDtypeStruct((2, NTOK), jnp.int32), jax.ShapeDtypeStruct((CLS_PAD, 128), F32)],
        scratch_shapes=[pltpu.VMEM((k, D), BF16)],
        compiler_params=_cp(2),
        name="mixer_out_" + kind,
    )(*srcs, w, xs, mods, norm_ffn, *route, tri)


def _fn_channel_kernel(h_ref, cs_ref, o_ref):
    for g in range(FN_GROUPS):
        r = jnp.dot(h_ref[:, g * FN_GC:(g + 1) * FN_GC], cs_ref[...], preferred_element_type=F32)
        o_ref[0, :, g * FN_GC:(g + 1) * FN_GC] = r[:, :FN_GC].astype(BF16)
        o_ref[1, :, g * FN_GC:(g + 1) * FN_GC] = r[:, FN_GC:].astype(BF16)


def _fn_channel(h, cs):
    return pl.pallas_call(
        _fn_channel_kernel,
        grid=(BATCH, TPB),
        in_specs=[
            pl.BlockSpec((None, ROW_TILE, D), lambda b, t: (b, t, 0)),
            pl.BlockSpec((FN_GC, 2 * FN_GC), lambda b, t: (0, 0)),
        ],
        out_specs=pl.BlockSpec((None, 2, ROW_TILE, D), lambda b, t: (b, 0, t, 0)),
        out_shape=jax.ShapeDtypeStruct((BATCH, 2, TT, D), BF16),
        compiler_params=_cp(2),
        name="fourier_channels",
    )(h, cs)


FN_R = 64
FN_CB = 16


def _fn_rows_kernel(ab_ref, f1_ref, y_ref, u32):
    u32[...] = ab_ref[...].astype(F32)
    for j in range(FN_CB):
        st = jnp.concatenate([u32[0, :, j, :], u32[1, :, j, :]], axis=0).astype(BF16)
        y = jnp.dot(f1_ref[j], st, preferred_element_type=F32)
        y_ref[0, j] = y[:FN_R].astype(BF16)
        y_ref[1, j] = y[FN_R:].astype(BF16)


def _fn_rows(ab, f1):
    ab5 = ab.reshape(BATCH, 2, TT // FN_R, FN_R, D)
    return pl.pallas_call(
        _fn_rows_kernel,
        grid=(BATCH, FN_R // FN_CB),
        in_specs=[
            pl.BlockSpec((None, 2, FN_R, FN_CB, D), lambda b, cb: (b, 0, 0, cb, 0)),
            pl.BlockSpec((FN_CB, 2 * FN_R, 2 * FN_R), lambda b, cb: (cb, 0, 0)),
        ],
        out_specs=pl.BlockSpec((None, 2, FN_CB, FN_R, D), lambda b, cb: (b, 0, cb, 0, 0)),
        out_shape=jax.ShapeDtypeStruct((BATCH, 2, FN_R, FN_R, D), BF16),
        scratch_shapes=[pltpu.VMEM((2, FN_R, FN_CB, D), F32)],
        compiler_params=_cp(2),
        name="fourier_rows",
    )(ab5, f1)


def _fn_cols_kernel(y_ref, f2_ref, o_ref, y32, z32):
    y32[...] = y_ref[...].astype(F32)
    for j in range(FN_CB):
        st = jnp.concatenate([y32[0, :, j, :], y32[1, :, j, :]], axis=0).astype(BF16)
        z32[:, j, :] = jnp.dot(f2_ref[...], st, preferred_element_type=F32)
    o_ref[...] = z32[...].astype(BF16)


def _fn_cols(y, f2):
    out = pl.pallas_call(
        _fn_cols_kernel,
        grid=(BATCH, FN_R // FN_CB),
        in_specs=[
            pl.BlockSpec((None, 2, FN_R, FN_CB, D), lambda b, kb: (b, 0, 0, kb, 0)),
            pl.BlockSpec((FN_R, 2 * FN_R), lambda b, kb: (0, 0)),
        ],
        out_specs=pl.BlockSpec((None, FN_R, FN_CB, D), lambda b, kb: (b, 0, kb, 0)),
        out_shape=jax.ShapeDtypeStruct((BATCH, FN_R, FN_R, D), BF16),
        scratch_shapes=[pltpu.VMEM((2, FN_R, FN_CB, D), F32), pltpu.VMEM((FN_R, FN_CB, D), F32)],
        compiler_params=_cp(2),
        name="fourier_cols",
    )(y, f2)
    return out.reshape(BATCH, SEQ, D)


def _fn_ctx_kernel(ab_ref, m_ref, o_ref):
    st = jnp.concatenate([ab_ref[0], ab_ref[1]], axis=0)
    o_ref[...] = jnp.dot(m_ref[...], st, preferred_element_type=F32).astype(BF16)


def _fn_ctx(ab, m_ctx):
    return pl.pallas_call(
        _fn_ctx_kernel,
        grid=(BATCH,),
        in_specs=[
            pl.BlockSpec((None, 2, CTX, D), lambda b: (b, 0, SEQ // CTX, 0)),
            pl.BlockSpec((CTX, 2 * CTX), lambda b: (0, 0)),
        ],
        out_specs=pl.BlockSpec((None, CTX, D), lambda b: (b, 0, 0)),
        out_shape=jax.ShapeDtypeStruct((BATCH, CTX, D), BF16),
        compiler_params=_cp(1),
        name="fourier_ctx",
    )(ab, m_ctx)


def _dft_tables():
    def cs_pair(n, scale):
        i = np.arange(n)
        ang = 2.0 * np.pi * ((i[:, None] * i[None, :]) % n) / n
        return np.cos(ang) * scale, np.sin(ang) * scale

    assert FN_GC == CTX
    c, s = cs_pair(FN_GC, FN_GC ** -0.5)
    cs = np.concatenate([c, s], axis=1)
    m_ctx = np.concatenate([c, -s], axis=1)
    c, s = cs_pair(FN_R, FN_R ** -0.5)
    f1 = np.block([[c, -s], [s, c]])
    f2 = np.concatenate([c, -s], axis=1)
    i = np.arange(FN_R)
    phi = 2.0 * np.pi * (i[:, None] * i[None, :]) / SEQ
    f1_tw = np.stack([np.block([[np.diag(np.cos(p)), -np.diag(np.sin(p))],
                                [np.diag(np.sin(p)), np.diag(np.cos(p))]]) @ f1 for p in phi])
    return jnp.asarray(cs, BF16), jnp.asarray(m_ctx, BF16), jnp.asarray(f1_tw, BF16), jnp.asarray(f2, BF16)


NA_ROWS = SEQ // GRID_W
NA_STEPS = TT // NA_QT
NA_NK = NA_KH * GRID_W
NA_DR = 2 * NA_KH - 1


def _na_kernel(q_ref, k_ref, v_ref, bias_ref, o_ref):
    step = pl.program_id(1)
    lane = lax.broadcasted_iota(jnp.int32, (GRID_W, 128), 1)
    nt = (((1,), (1,)), ((), ()))
    scale = NA_HD ** -0.5

    def attend(windows):
        chains = [(slice(i * GRID_W, (i + 1) * GRID_W), win, p, slice(p * 128, (p + 1) * 128))
                  for i, win in enumerate(windows) for p in range(NA_HEADS // 2)]
        scores = []
        for rows, win, p, sl in chains:
            qp = q_ref[rows, sl] * scale
            zero = jnp.zeros_like(qp)
            q2 = jnp.concatenate([jnp.where(lane < NA_HD, qp, zero), jnp.where(lane >= NA_HD, qp, zero)], axis=0)
            s_c = lax.dot_general(q2, k_ref[SEQ:TT, sl], nt, preferred_element_type=F32)
            s_w = None
            if win is not None:
                start, d0 = win
                bias = jnp.concatenate(
                    [jnp.concatenate([bias_ref[2 * p + hh, d0 + 2 * j2] for j2 in range(NA_KH // 2)], axis=1)
                     for hh in range(2)], axis=0)
                s_w = lax.dot_general(q2, k_ref[pl.ds(start, NA_NK), sl], nt, preferred_element_type=F32) + bias
            scores.append((s_c, s_w))
        probs = []
        for s_c, s_w in scores:
            m = jnp.max(s_c, axis=-1, keepdims=True)
            if s_w is not None:
                m = jnp.maximum(m, jnp.max(s_w, axis=-1, keepdims=True))
            p_c = jnp.exp(s_c - m)
            l = jnp.sum(p_c, axis=-1, keepdims=True)
            p_w = None
            if s_w is not None:
                p_w = jnp.exp(s_w - m)
                l = l + jnp.sum(p_w, axis=-1, keepdims=True)
                p_w = p_w.astype(BF16)
            probs.append((p_c.astype(BF16), p_w, l))
        for (rows, win, p, sl), (p_c, p_w, l) in zip(chains, probs):
            acc = jnp.dot(p_c, v_ref[SEQ:TT, sl], preferred_element_type=F32)
            if p_w is not None:
                acc = acc + jnp.dot(p_w, v_ref[pl.ds(win[0], NA_NK), sl], preferred_element_type=F32)
            o2 = acc / l
            o_ref[rows, sl] = jnp.where(lane < NA_HD, o2[:GRID_W], o2[GRID_W:]).astype(BF16)

    n_blocks = NA_QT // GRID_W

    @pl.when(step < NA_ROWS // n_blocks)
    def _():
        windows = []
        for i in range(n_blocks):
            r = step * n_blocks + i
            rs = jnp.clip(r - NA_KH // 2, 0, NA_ROWS - NA_KH)
            windows.append((pl.multiple_of(rs * GRID_W, GRID_W), rs - r + (NA_KH - 1)))
        attend(windows)

    @pl.when(step >= NA_ROWS // n_blocks)
    def _():
        attend([None] * n_blocks)


def _na_attention(qkv, bias):
    return pl.pallas_call(
        _na_kernel,
        grid=(BATCH, NA_STEPS),
        in_specs=[
            pl.BlockSpec((None, NA_QT, D), lambda b, s: (b, s, 0)),
            pl.BlockSpec((None, TT, D), lambda b, s: (b, 0, 1), pipeline_mode=pl.Buffered(1)),
            pl.BlockSpec((None, TT, D), lambda b, s: (b, 0, 2), pipeline_mode=pl.Buffered(1)),
            pl.BlockSpec((NA_HEADS, NA_DR - 1, GRID_W, 2 * GRID_W), lambda b, s: (0, 0, 0, 0),
                         pipeline_mode=pl.Buffered(1)),
        ],
        out_specs=pl.BlockSpec((None, NA_QT, D), lambda b, s: (b, s, 0)),
        out_shape=jax.ShapeDtypeStruct((BATCH, TT, D), BF16),
        compiler_params=_cp(2),
        name="na_attention",
    )(qkv, qkv, qkv, bias)


def _na_bias_table(rpb):
    qc = np.arange(GRID_W)
    kcol = np.arange(GRID_W)
    cstart = np.clip(qc - NA_KW // 2, 0, GRID_W - NA_KW)
    valid = (kcol[None, :] >= cstart[:, None]) & (kcol[None, :] < cstart[:, None] + NA_KW)
    dc = kcol[None, :] - qc[:, None] + (NA_KW - 1)
    pick = ((dc[None] == np.arange(2 * NA_KW - 1)[:, None, None]) & valid[None]).astype(np.float32)
    tab = jnp.einsum("hdj,jqk->hdqk", rpb.astype(F32), jnp.asarray(pick), precision=HIGHEST)
    tab = tab + jnp.asarray(np.where(valid, 0.0, NEG_INF).astype(np.float32))
    return jnp.concatenate([tab[:, :NA_DR - 1], tab[:, 1:]], axis=-1)


def _route(h, first, rw_ref, rb_ref, tri_ref, cr_ref, cnt_ref):
    logits = lax.dot_general(rw_ref[...], h.astype(BF16), (((1,), (1,)), ((), ())),
                             preferred_element_type=F32)
    sc = jax.nn.sigmoid(logits)
    sel = sc + rb_ref[...]
    sc_r = [sc[e:e + 1, :] for e in range(N_EXPERTS)]
    sel_r = [sel[e:e + 1, :] for e in range(N_EXPERTS)]
    zero_i = jnp.zeros_like(sel_r[0]).astype(jnp.int32)

    best = gi = None
    picks = []
    for g in range(N_GROUPS):
        a = sel_r[g * EPG:(g + 1) * EPG]
        u = sc_r[g * EPG:(g + 1) * EPG]
        gs = None
        for i in range(EPG):
            for j in range(i + 1, EPG):
                pair = a[i] + a[j]
                gs = pair if gs is None else jnp.maximum(gs, pair)
        m1, i1, s1 = a[0], zero_i, u[0]
        for j in range(1, EPG):
            upd = a[j] > m1
            m1 = jnp.where(upd, a[j], m1)
            i1 = jnp.where(upd, j, i1)
            s1 = jnp.where(upd, u[j], s1)
        m2 = i2 = s2 = None
        for j in range(EPG):
            cand = jnp.where(i1 == j, -jnp.inf, a[j])
            if m2 is None:
                m2, i2, s2 = cand, zero_i, u[0]
            else:
                upd = cand > m2
                m2 = jnp.where(upd, cand, m2)
                i2 = jnp.where(upd, j, i2)
                s2 = jnp.where(upd, u[j], s2)
        picks.append((i1 + g * EPG, i2 + g * EPG, s1, s2))
        if best is None:
            best, gi = gs, zero_i
        else:
            upd = gs > best
            best = jnp.where(upd, gs, best)
            gi = jnp.where(upd, g, gi)

    e1, e2, s1, s2 = picks[0]
    for g in range(1, N_GROUPS):
        on = gi == g
        e1 = jnp.where(on, picks[g][0], e1)
        e2 = jnp.where(on, picks[g][1], e2)
        s1 = jnp.where(on, picks[g][2], s1)
        s2 = jnp.where(on, picks[g][3], s2)
    tot = s1 + s2
    w1, w2 = s1 / tot, s2 / tot

    swap = e2 < e1
    la = jnp.where(swap, e2, e1) - gi * EPG
    lb = jnp.where(swap, e1, e2) - gi * EPG
    cls = gi * PAIRS + jnp.where(la == 0, 0, jnp.where(la == 1, 3, 5)) + lb - la - 1
    wa, wb = jnp.where(swap, w2, w1), jnp.where(swap, w1, w2)

    @pl.when(first)
    def _():
        cnt_ref[...] = jnp.zeros_like(cnt_ref)

    oh = (lax.broadcasted_iota(jnp.int32, (cnt_ref.shape[0], cls.shape[1]), 0) == cls).astype(F32)
    before = jnp.dot(oh.astype(BF16), tri_ref[...], preferred_element_type=F32)
    base = cnt_ref[:, 0:1]
    cr_ref[0:1, :] = cls
    cr_ref[1:2, :] = jnp.sum(oh * (base + before), axis=0, keepdims=True).astype(jnp.int32)
    cnt_ref[...] = cnt_ref[...] + jnp.sum(oh, axis=1, keepdims=True)
    return wa, wb


DMA_UNROLL = 8


def _row_copy(src_ref, src_row, dst_ref, dst_row, sem):
    return pltpu.make_async_copy(src_ref.at[pl.ds(src_row, 1)], dst_ref.at[pl.ds(dst_row, 1)], sem)


def _for_each_row(fn, rows=ROW_TILE):
    def group(gidx, carry):
        for u in range(DMA_UNROLL):
            fn(gidx * DMA_UNROLL + u)
        return carry

    lax.fori_loop(0, rows // DMA_UNROLL, group, 0)


def _dispatch_kernel(zrow_ref, dest_ref, h_ref, xs_ref, zbuf, sem, zsem):
    @pl.when(pl.program_id(0) == 0)
    def _():
        zbuf[...] = jnp.zeros_like(zbuf)

        def zero_copy(e):
            start = pl.multiple_of(jnp.maximum(zrow_ref[e], 0), MOE_BM)
            return pltpu.make_async_copy(zbuf, xs_ref.at[pl.ds(start, MOE_BM)], zsem)

        for e in range(MOE_NZ):
            pl.when(zrow_ref[e] >= 0)(lambda e=e: zero_copy(e).start())
        for e in range(MOE_NZ):
            pl.when(zrow_ref[e] >= 0)(lambda e=e: zero_copy(e).wait())

    _for_each_row(lambda r: _row_copy(h_ref, r, xs_ref, dest_ref[0, r], sem).start(), DISPATCH_TILE)
    _for_each_row(lambda r: _row_copy(h_ref, r, xs_ref, dest_ref[0, r], sem).wait(), DISPATCH_TILE)


def _dispatch(h, dest, zrow):
    grid_spec = pltpu.PrefetchScalarGridSpec(
        num_scalar_prefetch=1,
        grid=(NTOK // DISPATCH_TILE,),
        in_specs=[
            pl.BlockSpec((None, 1, DISPATCH_TILE), lambda i, z: (i, 0, 0), memory_space=pltpu.SMEM),
            pl.BlockSpec((DISPATCH_TILE, DX), lambda i, z: (i, 0)),
        ],
        out_specs=pl.BlockSpec(memory_space=pl.ANY),
        scratch_shapes=[pltpu.VMEM((MOE_BM, DX), F32), pltpu.SemaphoreType.DMA(()), pltpu.SemaphoreType.DMA(())],
    )
    return pl.pallas_call(
        _dispatch_kernel,
        grid_spec=grid_spec,
        out_shape=jax.ShapeDtypeStruct((MOE_P, DX), F32),
        compiler_params=pltpu.CompilerParams(dimension_semantics=("arbitrary",), vmem_limit_bytes=VMEM_LIMIT,
                                             has_side_effects=True),
        name="moe_dispatch",
    )(zrow, dest.reshape(NTOK // DISPATCH_TILE, 1, DISPATCH_TILE), h)


def _experts_kernel(blk_ref, ex_ref, nu_ref, x_ref, *rest, which):
    if which:
        prev_ref, wg_ref, wu_ref, wd_ref, y_ref, wgb, wub, wdb = rest
    else:
        wg_ref, wu_ref, wd_ref, y_ref, wgb, wub, wdb = rest
    del blk_ref
    v = pl.program_id(0)
    active = v < nu_ref[0]
    changed = (v == 0) | (ex_ref[v] != ex_ref[jnp.maximum(v - 1, 0)])

    @pl.when(active & changed)
    def _():
        wgb[...] = wg_ref[...].astype(BF16)
        wub[...] = wu_ref[...].astype(BF16)
        wdb[...] = wd_ref[...].astype(BF16)

    @pl.when(active)
    def _():
        x = x_ref[:, :D].astype(BF16)
        g = jnp.dot(x, wgb[...], preferred_element_type=F32)
        u = jnp.dot(x, wub[...], preferred_element_type=F32)
        hh = (_silu(g) * u).astype(BF16)
        y = jnp.dot(hh, wdb[...], preferred_element_type=F32) * x_ref[:, D + which:D + which + 1]
        y_ref[...] = prev_ref[...] + y if which else y

    @pl.when(jnp.logical_not(active))
    def _():
        y_ref[...] = jnp.zeros_like(y_ref)


def _experts(xs, prev, which, blocks, experts, n_used, w_gate, w_up, w_down, depth):
    wspec = pl.BlockSpec((None, None, D, D), lambda v, bl, ex, nu: (depth, ex[v], 0, 0))
    yspec = pl.BlockSpec((MOE_BM, D), lambda v, bl, ex, nu: (bl[v], 0))
    grid_spec = pltpu.PrefetchScalarGridSpec(
        num_scalar_prefetch=3,
        grid=(MOE_NB,),
        in_specs=[pl.BlockSpec((MOE_BM, DX), lambda v, bl, ex, nu: (bl[v], 0))] + ([yspec] if which else [])
        + [wspec, wspec, wspec],
        out_specs=yspec,
        scratch_shapes=[pltpu.VMEM((D, D), BF16)] * 3,
    )
    return pl.pallas_call(
        functools.partial(_experts_kernel, which=which),
        grid_spec=grid_spec,
        out_shape=jax.ShapeDtypeStruct((MOE_P, D), F32),
        compiler_params=_cp(1),
        name="moe_experts_%d" % which,
    )(blocks, experts, n_used, xs, *([prev] if which else []), w_gate, w_up, w_down)


def _combine_kernel(dest_ref, next_ref, yb_ref, x_ref, mod_ref, modn_ref, gn_ref, *rest, final):
    if final:
        out_ref, ybuf, sem = rest
    else:
        xo_ref, h_ref, ybuf, sem = rest
    t = pl.program_id(1)
    step = pl.program_id(0) * TPB + t
    slot = lax.rem(step, 2)

    def gather(slots_ref, buf):
        return lambda r: _row_copy(yb_ref, slots_ref[0, r], ybuf.at[buf], r, sem.at[buf])

    @pl.when(step == 0)
    def _():
        _for_each_row(lambda r: gather(dest_ref, slot)(r).start())

    @pl.when(step + 1 < BATCH * TPB)
    def _():
        _for_each_row(lambda r: gather(next_ref, 1 - slot)(r).start())

    _for_each_row(lambda r: gather(dest_ref, slot)(r).wait())

    def body():
        xn = x_ref[...] + mod_ref[5:6, :] * ybuf[slot]
        if final:
            out_ref[...] = _normmod(xn, gn_ref[...], 0.0, 0.0)
        else:
            xo_ref[...] = xn
            h_ref[...] = _normmod(xn, gn_ref[...], modn_ref[0:1, :], modn_ref[1:2, :]).astype(BF16)

    if final:
        pl.when(t < LAT_TILES)(body)
    else:
        body()


def _combine(yb, dest, xs, mods, depth, g_next, final):
    row = pl.BlockSpec((None, ROW_TILE, D), lambda b, t: (b, t, 0))
    next_depth = depth if final else depth + 1
    if final:
        out_specs = pl.BlockSpec((None, ROW_TILE, D), lambda b, t: (b, jnp.minimum(t, LAT_TILES - 1), 0))
        out_shape = jax.ShapeDtypeStruct((BATCH, SEQ, D), F32)
    else:
        out_specs = [row, row]
        out_shape = [jax.ShapeDtypeStruct((BATCH, TT, D), F32), jax.ShapeDtypeStruct((BATCH, TT, D), BF16)]
    last = BATCH * TPB - 1
    return pl.pallas_call(
        functools.partial(_combine_kernel, final=final),
        grid=(BATCH, TPB),
        in_specs=[
            pl.BlockSpec((None, 1, ROW_TILE), lambda b, t: (b * TPB + t, 0, 0), memory_space=pltpu.SMEM),
            pl.BlockSpec((None, 1, ROW_TILE), lambda b, t: (jnp.minimum(b * TPB + t + 1, last), 0, 0),
                         memory_space=pltpu.SMEM),
            pl.BlockSpec(memory_space=pl.ANY),
            row, _mod_spec(depth), _mod_spec(next_depth),
            g_next[1],
        ],
        out_specs=out_specs,
        out_shape=out_shape,
        scratch_shapes=[pltpu.VMEM((2, ROW_TILE, D), F32), pltpu.SemaphoreType.DMA((2,))],
        compiler_params=_cp(2),
        name="moe_combine",
    )(dest, dest, yb, xs, mods, mods, g_next[0])


def _class_experts():
    pairs = [(a, b) for a in range(EPG) for b in range(a + 1, EPG)]
    lo = [g * EPG + a for g in range(N_GROUPS) for a, _ in pairs]
    hi = [g * EPG + b for g in range(N_GROUPS) for _, b in pairs]
    return jnp.asarray(lo, jnp.int32), jnp.asarray(hi, jnp.int32)


def _moe_plan(cls, rank, counts):
    padded = (counts + MOE_BM - 1) // MOE_BM * MOE_BM
    pend = jnp.cumsum(padded)
    pstart = pend - padded
    cid = jnp.arange(N_CLS, dtype=jnp.int32)
    dest = rank + jnp.sum(jnp.where(cls[None, :] == cid[:, None], pstart[:, None], 0), axis=0)
    dest = dest.astype(jnp.int32).reshape(NTOK // ROW_TILE, 1, ROW_TILE)
    blk = jnp.arange(MOE_NB, dtype=jnp.int32)
    block_c = jnp.minimum(jnp.sum(pend[None, :] <= (blk * MOE_BM)[:, None], axis=1), N_CLS - 1)
    n_used = (pend[-1] // MOE_BM).astype(jnp.int32)
    used = blk < n_used
    lo, hi = _class_experts()
    onehot_c = block_c[:, None] == cid[None, :]
    e_lo = jnp.sum(jnp.where(onehot_c, lo[None, :], 0), axis=1)
    e_hi = jnp.sum(jnp.where(onehot_c, hi[None, :], 0), axis=1)

    def hold_last(e):
        return jnp.where(used, e, jnp.sum(jnp.where(blk == n_used - 1, e, 0))).astype(jnp.int32)

    key = jnp.where(used, e_hi, N_EXPERTS)
    earlier = (key[None, :] < key[:, None]) | ((key[None, :] == key[:, None]) & (blk[None, :] < blk[:, None]))
    pos = jnp.sum(earlier, axis=1)
    at = pos[None, :] == blk[:, None]
    order1 = jnp.sum(jnp.where(at, blk[None, :], 0), axis=1).astype(jnp.int32)
    e1 = jnp.sum(jnp.where(at, e_hi[None, :], 0), axis=1)
    tail = blk[MOE_NB - MOE_TAIL:] * MOE_BM
    zrow = jnp.concatenate([jnp.where(padded > 0, pend - MOE_BM, -1), jnp.where(tail >= pend[-1], tail, -1)])
    return dest, (blk, hold_last(e_lo)), (order1, hold_last(e1)), n_used[None], zrow.astype(jnp.int32)


def _moe(h, routing, xs, mods, depth, g_next, w_gate, w_up, w_down, final):
    cr, cnt = routing
    dest, visit0, visit1, n_used, zrow = _moe_plan(cr[0], cr[1], cnt[:N_CLS, 0].astype(jnp.int32))
    xsorted = _dispatch(h.reshape(NTOK, DX), dest, zrow)
    y0 = _experts(xsorted, None, 0, *visit0, n_used, w_gate, w_up, w_down, depth)
    yb = _experts(xsorted, y0, 1, *visit1, n_used, w_gate, w_up, w_down, depth)
    return _combine(yb, dest, xs, mods, depth, g_next, final)


def _rope_tables():
    nf = RET_DK // 4
    t = np.arange(SEQ)
    inv = ROPE_BASE ** (-np.arange(nf, dtype=np.float32) / nf)
    cos = np.ones((TT, 256), np.float32)
    sin = np.zeros((TT, 256), np.float32)
    for seg, pos in enumerate(((t // GRID_W).astype(np.float32), (t % GRID_W).astype(np.float32))):
        ang = (pos[:, None] * inv[None, :]).astype(np.float32)
        c, s = np.cos(ang), np.sin(ang)
        cos[:SEQ, seg * 128:(seg + 1) * 128] = np.concatenate([c, c], axis=1)
        sin[:SEQ, seg * 128:(seg + 1) * 128] = np.concatenate([-s, s], axis=1)
    ks = np.float32(RET_DK ** -0.5)
    return jnp.asarray(np.stack([cos, cos * ks])), jnp.asarray(np.stack([sin, sin * ks]))


def _decay_tables(decay_logit):
    C = RET_CHUNK
    log_g = jax.nn.log_sigmoid(decay_logit.astype(F32))
    pos = jnp.arange(C, dtype=F32)
    diff = pos[:, None] - pos[None, :]
    lg = log_g[:, :, None, None]
    dm_f = jnp.where(diff >= 0, jnp.exp(jnp.maximum(diff, 0.0)[None, None] * lg), 0.0)
    dm_b = jnp.where(diff <= 0, jnp.exp(jnp.maximum(-diff, 0.0)[None, None] * lg), 0.0)
    dmat = jnp.stack([dm_f[0], dm_b[1]])
    lgc = log_g[:, :, None]
    qdec = jnp.stack([jnp.exp((pos + 1.0)[None, :] * lgc[0]), jnp.exp((C - pos)[None, :] * lgc[1])])
    kdec = jnp.stack([jnp.exp((C - 1.0 - pos)[None, :] * lgc[0]), jnp.exp(pos[None, :] * lgc[1])])
    cdec = jnp.exp(C * log_g).reshape(-1)
    return dmat, qdec[..., None], kdec[..., None], cdec


def kernel(x, c, ctx, c_ctx, w_mod, b_mod, norm_mix, norm_ffn, norm_final, ret_w_in, ret_decay, ret_gn, ret_w_out,
           fn_w_out, na_w_qkv, na_rpb, na_w_out, router_w, router_bias, moe_w_gate, moe_w_up, moe_w_down):
    cvec = jnp.concatenate([c_ctx[None, :], c, jnp.zeros((8 - 1 - BATCH, D), F32)], axis=0)
    mod_all = _modulation(cvec, w_mod, b_mod).reshape(DEPTH, 8, N_MOD, D)
    pad = jnp.zeros((DEPTH, BATCH, 2, 8 - N_MOD, D), F32)
    mods = jnp.stack([jnp.broadcast_to(mod_all[:, 0:1], (DEPTH, BATCH, N_MOD, D)), mod_all[:, 1:1 + BATCH]], axis=2)
    mods = jnp.concatenate([mods, pad], axis=3)

    rope = _rope_tables()
    cs, dft_ctx, dft_f1, dft_f2 = _dft_tables()
    route = (router_w.T.astype(BF16), router_bias.reshape(N_EXPERTS, 1).astype(F32))

    norm_mix3 = norm_mix.reshape(DEPTH, 1, D)
    norm_ffn3 = norm_ffn.reshape(DEPTH, 1, D)
    ret_gn3 = ret_gn.reshape(-1, 1, 2 * D)

    xs, h = _embed(x, ctx, mods, norm_mix3)
    out = None
    for i in range(DEPTH):
        kind, j = i % 3, i // 3
        if kind == 0:
            qkvg = _project(h.reshape(NTOK, D), ret_w_in, j, rope).reshape(BATCH, TT, 6 * D)
            dmat, qdec, kdec, cdec = _decay_tables(ret_decay[j])
            o_f, o_b = _retention(qkvg, dmat, qdec, kdec, cdec)
            xs, hf, *routing = _mixer_out("ret", (o_f, o_b, qkvg, ret_gn3), ret_w_out, j, xs, mods, norm_ffn3, i, route)
        elif kind == 1:
            ab = _fn_channel(h, cs)
            f_lat = _fn_cols(_fn_rows(ab, dft_f1), dft_f2)
            xs, hf, *routing = _mixer_out(
                "split", (f_lat, _fn_ctx(ab, dft_ctx)), fn_w_out, j, xs, mods, norm_ffn3, i, route)
        else:
            qkv = _project(h.reshape(NTOK, D), na_w_qkv, j).reshape(BATCH, TT, 3 * D)
            o = _na_attention(qkv, _na_bias_table(na_rpb[j]))
            xs, hf, *routing = _mixer_out("plain", (o,), na_w_out, j, xs, mods, norm_ffn3, i, route)
        final = i == DEPTH - 1
        if final:
            g_next = (norm_final[None, :], pl.BlockSpec((1, D), lambda b, t: (0, 0)))
        else:
            g_next = (norm_mix3, _layer_spec((1, D), i + 1))
        res = _moe(hf, routing, xs, mods, i, g_next, moe_w_gate, moe_w_up, moe_w_down, final)
        if final:
            out = res
        else:
            xs, h = res
    return out
```

```python
import functools

import numpy as np
import jax
import jax.numpy as jnp
from jax import lax
from jax.experimental import pallas as pl
from jax.experimental.pallas import tpu as pltpu

F32 = jnp.float32
BF16 = jnp.bfloat16
HIGHEST = lax.Precision.HIGHEST

D = 1024
BATCH = 4
SEQ = 4096
CTX = 256
TT = SEQ + CTX
NTOK = BATCH * TT
DEPTH = 4
GRID_W = 64
EPS = 1e-6
NEG_INF = -1e30
N_MOD = 6

RET_HEADS = 4
RET_DK = D // RET_HEADS
RET_DV = 2 * RET_DK
RET_CHUNK = 256
ROPE_BASE = 10000.0

FN_GROUPS = 4
FN_GC = D // FN_GROUPS

NA_HEADS = 16
NA_HD = D // NA_HEADS
NA_KH = 8
NA_KW = 16

N_EXPERTS = 16
N_GROUPS = 4
EPG = N_EXPERTS // N_GROUPS

ROW_TILE = 256
TPB = TT // ROW_TILE
LAT_TILES = SEQ // ROW_TILE
MM_TM = TT // 4
MOE_BM = 256
PAIRS = EPG * (EPG - 1) // 2
N_CLS = N_GROUPS * PAIRS
CLS_PAD = 32
MOE_NB = (NTOK + N_CLS * (MOE_BM - 1) + MOE_BM - 1) // MOE_BM
MOE_P = MOE_NB * MOE_BM
MOE_TAIL = MOE_NB - NTOK // MOE_BM
MOE_NZ = N_CLS + MOE_TAIL
DISPATCH_TILE = 512
DX = D + 128
NA_QT = 4 * GRID_W
VMEM_LIMIT = 56 * 1024 * 1024


def _cp(n_axes, vmem=VMEM_LIMIT):
    return pltpu.CompilerParams(dimension_semantics=("arbitrary",) * n_axes, vmem_limit_bytes=vmem)


def _silu(x):
    return x * jax.nn.sigmoid(x)


def _normmod(x, g, shift, scale):
    y = x * lax.rsqrt(jnp.mean(x * x, axis=-1, keepdims=True) + EPS)
    return (y * g) * (1.0 + scale) + shift


def _mod_idx(t):
    return jnp.where(t < LAT_TILES, 1, 0)


def _layer_spec(shape, layer):
    return pl.BlockSpec((None,) + shape, lambda *_: (layer,) + (0,) * len(shape))


def _mod_spec(depth):
    return pl.BlockSpec((None, None, None, 8, D), lambda b, t: (depth, b, _mod_idx(t), 0, 0))


def _mod_kernel(c_ref, w_ref, b_ref, o_ref):
    a = _silu(c_ref[...])
    o_ref[...] = jnp.dot(a.astype(BF16), w_ref[...].astype(BF16), preferred_element_type=F32) + b_ref[...]


def _modulation(cvec, w_mod, b_mod):
    return pl.pallas_call(
        _mod_kernel,
        grid=(DEPTH, N_MOD),
        in_specs=[
            pl.BlockSpec((8, D), lambda i, j: (0, 0)),
            pl.BlockSpec((None, D, D), lambda i, j: (i, 0, j)),
            pl.BlockSpec((None, 1, D), lambda i, j: (i, 0, j)),
        ],
        out_specs=pl.BlockSpec((None, 8, D), lambda i, j: (i, 0, j)),
        out_shape=jax.ShapeDtypeStruct((DEPTH, 8, N_MOD * D), F32),
        compiler_params=_cp(2),
        name="modulation",
    )(cvec, w_mod, b_mod.reshape(DEPTH, 1, N_MOD * D))


def _embed_kernel(x_ref, ctx_ref, mod_ref, g_ref, xs_ref, h_ref):
    t = pl.program_id(1)

    def emit(src):
        xs_ref[...] = src
        h_ref[...] = _normmod(src, g_ref[...], mod_ref[0:1, :], mod_ref[1:2, :]).astype(BF16)

    @pl.when(t < LAT_TILES)
    def _():
        emit(x_ref[...])

    @pl.when(t >= LAT_TILES)
    def _():
        emit(ctx_ref[...])


def _embed(x, ctx, mods, g):
    return pl.pallas_call(
        _embed_kernel,
        grid=(BATCH, TPB),
        in_specs=[
            pl.BlockSpec((None, ROW_TILE, D), lambda b, t: (b, jnp.minimum(t, LAT_TILES - 1), 0)),
            pl.BlockSpec((None, CTX, D), lambda b, t: (b, 0, 0)),
            _mod_spec(0),
            _layer_spec((1, D), 0),
        ],
        out_specs=[
            pl.BlockSpec((None, ROW_TILE, D), lambda b, t: (b, t, 0)),
            pl.BlockSpec((None, ROW_TILE, D), lambda b, t: (b, t, 0)),
        ],
        out_shape=[
            jax.ShapeDtypeStruct((BATCH, TT, D), F32),
            jax.ShapeDtypeStruct((BATCH, TT, D), BF16),
        ],
        compiler_params=_cp(2),
        name="embed",
    )(x, ctx, mods, g)


def _rope_store(acc, cos_ref, sin_ref, o_ref):
    for c in range(acc.shape[1] // 128):
        xc = acc[:, c * 128:(c + 1) * 128]
        tsl = slice((c % 2) * 128, (c % 2 + 1) * 128)
        rot = xc * cos_ref[:, tsl] + pltpu.roll(xc, 64, 1) * sin_ref[:, tsl]
        o_ref[:, c * 128:(c + 1) * 128] = rot.astype(o_ref.dtype)


def _proj_kernel(a_ref, w_ref, *rest, n_rope):
    if n_rope:
        cos_ref, sin_ref, o_ref, wb_ref = rest
    else:
        o_ref, wb_ref = rest
    j = pl.program_id(0)

    @pl.when(pl.program_id(1) == 0)
    def _():
        wb_ref[...] = w_ref[...].astype(BF16)

    acc = jnp.dot(a_ref[...], wb_ref[...], preferred_element_type=F32)
    if n_rope:
        @pl.when(j < n_rope)
        def _():
            _rope_store(acc, cos_ref, sin_ref, o_ref)

        @pl.when(j >= n_rope)
        def _():
            o_ref[...] = acc.astype(o_ref.dtype)
    else:
        o_ref[...] = acc.astype(o_ref.dtype)


def _project(a, w, layer, rope=None):
    _, k, n = w.shape
    tn = D
    n_rope = 0 if rope is None else rope[0].shape[0]
    in_specs = [
        pl.BlockSpec((MM_TM, k), lambda j, i: (i, 0)),
        pl.BlockSpec((None, k, tn), lambda j, i: (layer, 0, j)),
    ]
    args = [a, w]
    if n_rope:
        tspec = pl.BlockSpec((None, MM_TM, 256), lambda j, i: (jnp.minimum(j, n_rope - 1), i % (TT // MM_TM), 0))
        in_specs += [tspec, tspec]
        args += list(rope)
    return pl.pallas_call(
        functools.partial(_proj_kernel, n_rope=n_rope),
        grid=(n // tn, NTOK // MM_TM),
        in_specs=in_specs,
        out_specs=pl.BlockSpec((MM_TM, tn), lambda j, i: (i, j)),
        out_shape=jax.ShapeDtypeStruct((NTOK, n), BF16),
        scratch_shapes=[pltpu.VMEM((k, tn), BF16)],
        compiler_params=_cp(2),
        name="project",
    )(*args)


N_CHUNKS = TT // RET_CHUNK
LAT_CHUNKS = SEQ // RET_CHUNK
CTX_CHUNKS = CTX // RET_CHUNK


def _fwd_chunk(n):
    return jnp.where(n < CTX_CHUNKS, LAT_CHUNKS + n, n - CTX_CHUNKS)


def _bwd_chunk(n):
    return N_CHUNKS - 1 - n


def _ret_kernel(cdec_ref, qf, kf, vf, qb, kb, vb, dmat_ref, qdec_ref, kdec_ref, of_ref, ob_ref, s_ref):
    @pl.when(pl.program_id(1) == 0)
    def _():
        s_ref[...] = jnp.zeros_like(s_ref)

    for d, (q_r, k_r, v_r, o_r) in enumerate(((qf, kf, vf, of_ref), (qb, kb, vb, ob_ref))):
        for h in range(RET_HEADS):
            q = q_r[:, h * RET_DK:(h + 1) * RET_DK]
            k = k_r[:, h * RET_DK:(h + 1) * RET_DK]
            v = v_r[:, h * RET_DV:(h + 1) * RET_DV]
            att = lax.dot_general(q, k, (((1,), (1,)), ((), ())), preferred_element_type=F32) * dmat_ref[d, h]
            s = s_ref[d, h]
            qd = (q.astype(F32) * qdec_ref[d, h]).astype(BF16)
            o = (jnp.dot(att.astype(BF16), v, preferred_element_type=F32)
                 + jnp.dot(qd, s.astype(BF16), preferred_element_type=F32))
            kd = (k.astype(F32) * kdec_ref[d, h]).astype(BF16)
            s_ref[d, h] = s * cdec_ref[d * RET_HEADS + h] + lax.dot_general(
                kd, v, (((0,), (0,)), ((), ())), preferred_element_type=F32)
            o_r[:, h * RET_DV:(h + 1) * RET_DV] = o.astype(o_r.dtype)


def _retention(qkvg, dmat, qdec, kdec, cdec):
    C = RET_CHUNK

    def spec(width, col, chunk_fn):
        return pl.BlockSpec((None, C, width), lambda b, n, cd: (b, chunk_fn(n), col))

    ospec_f = pl.BlockSpec((None, C, 2 * D), lambda b, n, cd: (b, _fwd_chunk(n), 0))
    ospec_b = pl.BlockSpec((None, C, 2 * D), lambda b, n, cd: (b, _bwd_chunk(n), 0))
    full = lambda shape: pl.BlockSpec(shape, lambda b, n, cd: (0,) * len(shape))
    grid_spec = pltpu.PrefetchScalarGridSpec(
        num_scalar_prefetch=1,
        grid=(BATCH, N_CHUNKS),
        in_specs=[
            spec(D, 0, _fwd_chunk), spec(D, 1, _fwd_chunk), spec(2 * D, 1, _fwd_chunk),
            spec(D, 0, _bwd_chunk), spec(D, 1, _bwd_chunk), spec(2 * D, 1, _bwd_chunk),
            full((2, RET_HEADS, C, C)), full((2, RET_HEADS, C, 1)), full((2, RET_HEADS, C, 1)),
        ],
        out_specs=[ospec_f, ospec_b],
        scratch_shapes=[pltpu.VMEM((2, RET_HEADS, RET_DK, RET_DV), F32)],
    )
    return pl.pallas_call(
        _ret_kernel,
        grid_spec=grid_spec,
        out_shape=[jax.ShapeDtypeStruct((BATCH, TT, 2 * D), BF16)] * 2,
        compiler_params=_cp(2),
        name="retention",
    )(cdec, qkvg, qkvg, qkvg, qkvg, qkvg, qkvg, dmat, qdec, kdec)


def _mixer_out_kernel(*refs, kind):
    n_src = {"ret": 4, "split": 2, "plain": 1}[kind]
    srcs, (w_ref, x_ref, mod_ref, nf_ref), route_in = refs[:n_src], refs[n_src:n_src + 4], refs[n_src + 4:n_src + 7]
    xo_ref, h_ref, cr_ref, cnt_ref, wb_ref = refs[n_src + 7:]
    first = (pl.program_id(0) == 0) & (pl.program_id(1) == 0)

    @pl.when(first)
    def _():
        wb_ref[...] = w_ref[...].astype(BF16)

    if kind == "ret":
        of_ref, ob_ref, g_ref, gn_ref = srcs
        o = of_ref[...].astype(F32) + ob_ref[...].astype(F32)
        parts = []
        for h in range(RET_HEADS):
            oh = o[:, h * RET_DV:(h + 1) * RET_DV]
            mu = jnp.mean(oh, axis=-1, keepdims=True)
            ctr = oh - mu
            var = jnp.mean(ctr * ctr, axis=-1, keepdims=True)
            parts.append(ctr * lax.rsqrt(var + EPS))
        on = jnp.concatenate(parts, axis=-1) * gn_ref[...]
        a = (on * _silu(g_ref[...].astype(F32))).astype(BF16)
    elif kind == "split":
        a = jnp.where(pl.program_id(1) < LAT_TILES, srcs[0][...], srcs[1][...])
    else:
        a = srcs[0][...]
    y = jnp.dot(a, wb_ref[...], preferred_element_type=F32)
    xn = x_ref[...] + mod_ref[2:3, :] * y
    xo_ref[...] = xn
    h = _normmod(xn, nf_ref[...], mod_ref[3:4, :], mod_ref[4:5, :])
    wa, wb = _route(h, first, *route_in, cr_ref, cnt_ref)
    sub = lax.broadcasted_iota(jnp.int32, (DX - D, ROW_TILE), 0)
    h_ref[:, :D] = h
    h_ref[:, D:] = jnp.where(sub == 0, wa, jnp.where(sub == 1, wb, 0.0)).T


def _mixer_out(kind, srcs, w, layer, xs, mods, norm_ffn, depth, route):
    k = w.shape[1]
    row = lambda width, col=0: pl.BlockSpec((None, ROW_TILE, width), lambda b, t: (b, t, col))
    const = lambda shape: pl.BlockSpec(shape, lambda b, t: (0,) * len(shape))
    per_tok = pl.BlockSpec((2, ROW_TILE), lambda b, t: (0, b * TPB + t))
    tri = jnp.asarray(np.triu(np.ones((ROW_TILE, ROW_TILE), np.float32), 1), BF16)
    if kind == "ret":
        src_specs = [row(2 * D), row(2 * D), row(2 * D, 2), _layer_spec((1, 2 * D), layer)]
    elif kind == "split":
        src_specs = [pl.BlockSpec((None, ROW_TILE, k), lambda b, t: (b, jnp.minimum(t, LAT_TILES - 1), 0)),
                     pl.BlockSpec((None, CTX, k), lambda b, t: (b, 0, 0))]
    else:
        src_specs = [row(k)]
    return pl.pallas_call(
        functools.partial(_mixer_out_kernel, kind=kind),
        grid=(BATCH, TPB),
        in_specs=src_specs + [_layer_spec((k, D), layer), row(D), _mod_spec(depth), _layer_spec((1, D), depth),
                              const((N_EXPERTS, D)), const((N_EXPERTS, 1)), const((ROW_TILE, ROW_TILE))],
        out_specs=[row(D), row(DX), per_tok, const((CLS_PAD, 128))],
        out_shape=[jax.ShapeDtypeStruct((BATCH, TT, D), F32), jax.ShapeDtypeStruct((BATCH, TT, DX), F32),
                   jax.ShapeDtypeStruct((2, NTOK), jnp.int32), jax.ShapeDtypeStruct((CLS_PAD, 128), F32)],
        scratch_shapes=[pltpu.VMEM((k, D), BF16)],
        compiler_params=_cp(2),
        name="mixer_out_" + kind,
    )(*srcs, w, xs, mods, norm_ffn, *route, tri)


def _fn_channel_kernel(h_ref, cs_ref, o_ref):
    for g in range(FN_GROUPS):
        r = jnp.dot(h_ref[:, g * FN_GC:(g + 1) * FN_GC], cs_ref[...], preferred_element_type=F32)
        o_ref[0, :, g * FN_GC:(g + 1) * FN_GC] = r[:, :FN_GC].astype(BF16)
        o_ref[1, :, g * FN_GC:(g + 1) * FN_GC] = r[:, FN_GC:].astype(BF16)


def _fn_channel(h, cs):
    return pl.pallas_call(
        _fn_channel_kernel,
        grid=(BATCH, TPB),
        in_specs=[
            pl.BlockSpec((None, ROW_TILE, D), lambda b, t: (b, t, 0)),
            pl.BlockSpec((FN_GC, 2 * FN_GC), lambda b, t: (0, 0)),
        ],
        out_specs=pl.BlockSpec((None, 2, ROW_TILE, D), lambda b, t: (b, 0, t, 0)),
        out_shape=jax.ShapeDtypeStruct((BATCH, 2, TT, D), BF16),
        compiler_params=_cp(2),
        name="fourier_channels",
    )(h, cs)


FN_R = 64
FN_CB = 16


def _fn_rows_kernel(ab_ref, f1_ref, y_ref, u32):
    u32[...] = ab_ref[...].astype(F32)
    for j in range(FN_CB):
        st = jnp.concatenate([u32[0, :, j, :], u32[1, :, j, :]], axis=0).astype(BF16)
        y = jnp.dot(f1_ref[j], st, preferred_element_type=F32)
        y_ref[0, j] = y[:FN_R].astype(BF16)
        y_ref[1, j] = y[FN_R:].astype(BF16)


def _fn_rows(ab, f1):
    ab5 = ab.reshape(BATCH, 2, TT // FN_R, FN_R, D)
    return pl.pallas_call(
        _fn_rows_kernel,
        grid=(BATCH, FN_R // FN_CB),
        in_specs=[
            pl.BlockSpec((None, 2, FN_R, FN_CB, D), lambda b, cb: (b, 0, 0, cb, 0)),
            pl.BlockSpec((FN_CB, 2 * FN_R, 2 * FN_R), lambda b, cb: (cb, 0, 0)),
        ],
        out_specs=pl.BlockSpec((None, 2, FN_CB, FN_R, D), lambda b, cb: (b, 0, cb, 0, 0)),
        out_shape=jax.ShapeDtypeStruct((BATCH, 2, FN_R, FN_R, D), BF16),
        scratch_shapes=[pltpu.VMEM((2, FN_R, FN_CB, D), F32)],
        compiler_params=_cp(2),
        name="fourier_rows",
    )(ab5, f1)


def _fn_cols_kernel(y_ref, f2_ref, o_ref, y32, z32):
    y32[...] = y_ref[...].astype(F32)
    for j in range(FN_CB):
        st = jnp.concatenate([y32[0, :, j, :], y32[1, :, j, :]], axis=0).astype(BF16)
        z32[:, j, :] = jnp.dot(f2_ref[...], st, preferred_element_type=F32)
    o_ref[...] = z32[...].astype(BF16)


def _fn_cols(y, f2):
    out = pl.pallas_call(
        _fn_cols_kernel,
        grid=(BATCH, FN_R // FN_CB),
        in_specs=[
            pl.BlockSpec((None, 2, FN_R, FN_CB, D), lambda b, kb: (b, 0, 0, kb, 0)),
            pl.BlockSpec((FN_R, 2 * FN_R), lambda b, kb: (0, 0)),
        ],
        out_specs=pl.BlockSpec((None, FN_R, FN_CB, D), lambda b, kb: (b, 0, kb, 0)),
        out_shape=jax.ShapeDtypeStruct((BATCH, FN_R, FN_R, D), BF16),
        scratch_shapes=[pltpu.VMEM((2, FN_R, FN_CB, D), F32), pltpu.VMEM((FN_R, FN_CB, D), F32)],
        compiler_params=_cp(2),
        name="fourier_cols",
    )(y, f2)
    return out.reshape(BATCH, SEQ, D)


def _fn_ctx_kernel(ab_ref, m_ref, o_ref):
    st = jnp.concatenate([ab_ref[0], ab_ref[1]], axis=0)
    o_ref[...] = jnp.dot(m_ref[...], st, preferred_element_type=F32).astype(BF16)


def _fn_ctx(ab, m_ctx):
    return pl.pallas_call(
        _fn_ctx_kernel,
        grid=(BATCH,),
        in_specs=[
            pl.BlockSpec((None, 2, CTX, D), lambda b: (b, 0, SEQ // CTX, 0)),
            pl.BlockSpec((CTX, 2 * CTX), lambda b: (0, 0)),
        ],
        out_specs=pl.BlockSpec((None, CTX, D), lambda b: (b, 0, 0)),
        out_shape=jax.ShapeDtypeStruct((BATCH, CTX, D), BF16),
        compiler_params=_cp(1),
        name="fourier_ctx",
    )(ab, m_ctx)


def _dft_tables():
    def cs_pair(n, scale):
        i = np.arange(n)
        ang = 2.0 * np.pi * ((i[:, None] * i[None, :]) % n) / n
        return np.cos(ang) * scale, np.sin(ang) * scale

    assert FN_GC == CTX
    c, s = cs_pair(FN_GC, FN_GC ** -0.5)
    cs = np.concatenate([c, s], axis=1)
    m_ctx = np.concatenate([c, -s], axis=1)
    c, s = cs_pair(FN_R, FN_R ** -0.5)
    f1 = np.block([[c, -s], [s, c]])
    f2 = np.concatenate([c, -s], axis=1)
    i = np.arange(FN_R)
    phi = 2.0 * np.pi * (i[:, None] * i[None, :]) / SEQ
    f1_tw = np.stack([np.block([[np.diag(np.cos(p)), -np.diag(np.sin(p))],
                                [np.diag(np.sin(p)), np.diag(np.cos(p))]]) @ f1 for p in phi])
    return jnp.asarray(cs, BF16), jnp.asarray(m_ctx, BF16), jnp.asarray(f1_tw, BF16), jnp.asarray(f2, BF16)


NA_ROWS = SEQ // GRID_W
NA_STEPS = TT // NA_QT
NA_NK = NA_KH * GRID_W
NA_DR = 2 * NA_KH - 1


def _na_kernel(q_ref, k_ref, v_ref, bias_ref, o_ref):
    step = pl.program_id(1)
    lane = lax.broadcasted_iota(jnp.int32, (GRID_W, 128), 1)
    nt = (((1,), (1,)), ((), ()))
    scale = NA_HD ** -0.5

    def attend(windows):
        chains = [(slice(i * GRID_W, (i + 1) * GRID_W), win, p, slice(p * 128, (p + 1) * 128))
                  for i, win in enumerate(windows) for p in range(NA_HEADS // 2)]
        scores = []
        for rows, win, p, sl in chains:
            qp = q_ref[rows, sl] * scale
            zero = jnp.zeros_like(qp)
            q2 = jnp.concatenate([jnp.where(lane < NA_HD, qp, zero), jnp.where(lane >= NA_HD, qp, zero)], axis=0)
            s_c = lax.dot_general(q2, k_ref[SEQ:TT, sl], nt, preferred_element_type=F32)
            s_w = None
            if win is not None:
                start, d0 = win
                bias = jnp.concatenate(
                    [jnp.concatenate([bias_ref[2 * p + hh, d0 + 2 * j2] for j2 in range(NA_KH // 2)], axis=1)
                     for hh in range(2)], axis=0)
                s_w = lax.dot_general(q2, k_ref[pl.ds(start, NA_NK), sl], nt, preferred_element_type=F32) + bias
            scores.append((s_c, s_w))
        probs = []
        for s_c, s_w in scores:
            m = jnp.max(s_c, axis=-1, keepdims=True)
            if s_w is not None:
                m = jnp.maximum(m, jnp.max(s_w, axis=-1, keepdims=True))
            p_c = jnp.exp(s_c - m)
            l = jnp.sum(p_c, axis=-1, keepdims=True)
            p_w = None
            if s_w is not None:
                p_w = jnp.exp(s_w - m)
                l = l + jnp.sum(p_w, axis=-1, keepdims=True)
                p_w = p_w.astype(BF16)
            probs.append((p_c.astype(BF16), p_w, l))
        for (rows, win, p, sl), (p_c, p_w, l) in zip(chains, probs):
            acc = jnp.dot(p_c, v_ref[SEQ:TT, sl], preferred_element_type=F32)
            if p_w is not None:
                acc = acc + jnp.dot(p_w, v_ref[pl.ds(win[0], NA_NK), sl], preferred_element_type=F32)
            o2 = acc / l
            o_ref[rows, sl] = jnp.where(lane < NA_HD, o2[:GRID_W], o2[GRID_W:]).astype(BF16)

    n_blocks = NA_QT // GRID_W

    @pl.when(step < NA_ROWS // n_blocks)
    def _():
        windows = []
        for i in range(n_blocks):
            r = step * n_blocks + i
            rs = jnp.clip(r - NA_KH // 2, 0, NA_ROWS - NA_KH)
            windows.append((pl.multiple_of(rs * GRID_W, GRID_W), rs - r + (NA_KH - 1)))
        attend(windows)

    @pl.when(step >= NA_ROWS // n_blocks)
    def _():
        attend([None] * n_blocks)


def _na_attention(qkv, bias):
    return pl.pallas_call(
        _na_kernel,
        grid=(BATCH, NA_STEPS),
        in_specs=[
            pl.BlockSpec((None, NA_QT, D), lambda b, s: (b, s, 0)),
            pl.BlockSpec((None, TT, D), lambda b, s: (b, 0, 1), pipeline_mode=pl.Buffered(1)),
            pl.BlockSpec((None, TT, D), lambda b, s: (b, 0, 2), pipeline_mode=pl.Buffered(1)),
            pl.BlockSpec((NA_HEADS, NA_DR - 1, GRID_W, 2 * GRID_W), lambda b, s: (0, 0, 0, 0),
                         pipeline_mode=pl.Buffered(1)),
        ],
        out_specs=pl.BlockSpec((None, NA_QT, D), lambda b, s: (b, s, 0)),
        out_shape=jax.ShapeDtypeStruct((BATCH, TT, D), BF16),
        compiler_params=_cp(2),
        name="na_attention",
    )(qkv, qkv, qkv, bias)


def _na_bias_table(rpb):
    qc = np.arange(GRID_W)
    kcol = np.arange(GRID_W)
    cstart = np.clip(qc - NA_KW // 2, 0, GRID_W - NA_KW)
    valid = (kcol[None, :] >= cstart[:, None]) & (kcol[None, :] < cstart[:, None] + NA_KW)
    dc = kcol[None, :] - qc[:, None] + (NA_KW - 1)
    pick = ((dc[None] == np.arange(2 * NA_KW - 1)[:, None, None]) & valid[None]).astype(np.float32)
    tab = jnp.einsum("hdj,jqk->hdqk", rpb.astype(F32), jnp.asarray(pick), precision=HIGHEST)
    tab = tab + jnp.asarray(np.where(valid, 0.0, NEG_INF).astype(np.float32))
    return jnp.concatenate([tab[:, :NA_DR - 1], tab[:, 1:]], axis=-1)


def _route(h, first, rw_ref, rb_ref, tri_ref, cr_ref, cnt_ref):
    logits = lax.dot_general(rw_ref[...], h.astype(BF16), (((1,), (1,)), ((), ())),
                             preferred_element_type=F32)
    sc = jax.nn.sigmoid(logits)
    sel = sc + rb_ref[...]
    sc_r = [sc[e:e + 1, :] for e in range(N_EXPERTS)]
    sel_r = [sel[e:e + 1, :] for e in range(N_EXPERTS)]
    zero_i = jnp.zeros_like(sel_r[0]).astype(jnp.int32)

    best = gi = None
    picks = []
    for g in range(N_GROUPS):
        a = sel_r[g * EPG:(g + 1) * EPG]
        u = sc_r[g * EPG:(g + 1) * EPG]
        gs = None
        for i in range(EPG):
            for j in range(i + 1, EPG):
                pair = a[i] + a[j]
                gs = pair if gs is None else jnp.maximum(gs, pair)
        m1, i1, s1 = a[0], zero_i, u[0]
        for j in range(1, EPG):
            upd = a[j] > m1
            m1 = jnp.where(upd, a[j], m1)
            i1 = jnp.where(upd, j, i1)
            s1 = jnp.where(upd, u[j], s1)
        m2 = i2 = s2 = None
        for j in range(EPG):
            cand = jnp.where(i1 == j, -jnp.inf, a[j])
            if m2 is None:
                m2, i2, s2 = cand, zero_i, u[0]
            else:
                upd = cand > m2
                m2 = jnp.where(upd, cand, m2)
                i2 = jnp.where(upd, j, i2)
                s2 = jnp.where(upd, u[j], s2)
        picks.append((i1 + g * EPG, i2 + g * EPG, s1, s2))
        if best is None:
            best, gi = gs, zero_i
        else:
            upd = gs > best
            best = jnp.where(upd, gs, best)
            gi = jnp.where(upd, g, gi)

    e1, e2, s1, s2 = picks[0]
    for g in range(1, N_GROUPS):
        on = gi == g
        e1 = jnp.where(on, picks[g][0], e1)
        e2 = jnp.where(on, picks[g][1], e2)
        s1 = jnp.where(on, picks[g][2], s1)
        s2 = jnp.where(on, picks[g][3], s2)
    tot = s1 + s2
    w1, w2 = s1 / tot, s2 / tot

    swap = e2 < e1
    la = jnp.where(swap, e2, e1) - gi * EPG
    lb = jnp.where(swap, e1, e2) - gi * EPG
    cls = gi * PAIRS + jnp.where(la == 0, 0, jnp.where(la == 1, 3, 5)) + lb - la - 1
    wa, wb = jnp.where(swap, w2, w1), jnp.where(swap, w1, w2)

    @pl.when(first)
    def _():
        cnt_ref[...] = jnp.zeros_like(cnt_ref)

    oh = (lax.broadcasted_iota(jnp.int32, (cnt_ref.shape[0], cls.shape[1]), 0) == cls).astype(F32)
    before = jnp.dot(oh.astype(BF16), tri_ref[...], preferred_element_type=F32)
    base = cnt_ref[:, 0:1]
    cr_ref[0:1, :] = cls
    cr_ref[1:2, :] = jnp.sum(oh * (base + before), axis=0, keepdims=True).astype(jnp.int32)
    cnt_ref[...] = cnt_ref[...] + jnp.sum(oh, axis=1, keepdims=True)
    return wa, wb


DMA_UNROLL = 8


def _row_copy(src_ref, src_row, dst_ref, dst_row, sem):
    return pltpu.make_async_copy(src_ref.at[pl.ds(src_row, 1)], dst_ref.at[pl.ds(dst_row, 1)], sem)


def _for_each_row(fn, rows=ROW_TILE):
    def group(gidx, carry):
        for u in range(DMA_UNROLL):
            fn(gidx * DMA_UNROLL + u)
        return carry

    lax.fori_loop(0, rows // DMA_UNROLL, group, 0)


def _dispatch_kernel(zrow_ref, dest_ref, h_ref, xs_ref, zbuf, sem, zsem):
    @pl.when(pl.program_id(0) == 0)
    def _():
        zbuf[...] = jnp.zeros_like(zbuf)

        def zero_copy(e):
            start = pl.multiple_of(jnp.maximum(zrow_ref[e], 0), MOE_BM)
            return pltpu.make_async_copy(zbuf, xs_ref.at[pl.ds(start, MOE_BM)], zsem)

        for e in range(MOE_NZ):
            pl.when(zrow_ref[e] >= 0)(lambda e=e: zero_copy(e).start())
        for e in range(MOE_NZ):
            pl.when(zrow_ref[e] >= 0)(lambda e=e: zero_copy(e).wait())

    _for_each_row(lambda r: _row_copy(h_ref, r, xs_ref, dest_ref[0, r], sem).start(), DISPATCH_TILE)
    _for_each_row(lambda r: _row_copy(h_ref, r, xs_ref, dest_ref[0, r], sem).wait(), DISPATCH_TILE)


def _dispatch(h, dest, zrow):
    grid_spec = pltpu.PrefetchScalarGridSpec(
        num_scalar_prefetch=1,
        grid=(NTOK // DISPATCH_TILE,),
        in_specs=[
            pl.BlockSpec((None, 1, DISPATCH_TILE), lambda i, z: (i, 0, 0), memory_space=pltpu.SMEM),
            pl.BlockSpec((DISPATCH_TILE, DX), lambda i, z: (i, 0)),
        ],
        out_specs=pl.BlockSpec(memory_space=pl.ANY),
        scratch_shapes=[pltpu.VMEM((MOE_BM, DX), F32), pltpu.SemaphoreType.DMA(()), pltpu.SemaphoreType.DMA(())],
    )
    return pl.pallas_call(
        _dispatch_kernel,
        grid_spec=grid_spec,
        out_shape=jax.ShapeDtypeStruct((MOE_P, DX), F32),
        compiler_params=pltpu.CompilerParams(dimension_semantics=("arbitrary",), vmem_limit_bytes=VMEM_LIMIT,
                                             has_side_effects=True),
        name="moe_dispatch",
    )(zrow, dest.reshape(NTOK // DISPATCH_TILE, 1, DISPATCH_TILE), h)


def _experts_kernel(blk_ref, ex_ref, nu_ref, x_ref, *rest, which):
    if which:
        prev_ref, wg_ref, wu_ref, wd_ref, y_ref, wgb, wub, wdb = rest
    else:
        wg_ref, wu_ref, wd_ref, y_ref, wgb, wub, wdb = rest
    del blk_ref
    v = pl.program_id(0)
    active = v < nu_ref[0]
    changed = (v == 0) | (ex_ref[v] != ex_ref[jnp.maximum(v - 1, 0)])

    @pl.when(active & changed)
    def _():
        wgb[...] = wg_ref[...].astype(BF16)
        wub[...] = wu_ref[...].astype(BF16)
        wdb[...] = wd_ref[...].astype(BF16)

    @pl.when(active)
    def _():
        x = x_ref[:, :D].astype(BF16)
        g = jnp.dot(x, wgb[...], preferred_element_type=F32)
        u = jnp.dot(x, wub[...], preferred_element_type=F32)
        hh = (_silu(g) * u).astype(BF16)
        y = jnp.dot(hh, wdb[...], preferred_element_type=F32) * x_ref[:, D + which:D + which + 1]
        y_ref[...] = prev_ref[...] + y if which else y

    @pl.when(jnp.logical_not(active))
    def _():
        y_ref[...] = jnp.zeros_like(y_ref)


def _experts(xs, prev, which, blocks, experts, n_used, w_gate, w_up, w_down, depth):
    wspec = pl.BlockSpec((None, None, D, D), lambda v, bl, ex, nu: (depth, ex[v], 0, 0))
    yspec = pl.BlockSpec((MOE_BM, D), lambda v, bl, ex, nu: (bl[v], 0))
    grid_spec = pltpu.PrefetchScalarGridSpec(
        num_scalar_prefetch=3,
        grid=(MOE_NB,),
        in_specs=[pl.BlockSpec((MOE_BM, DX), lambda v, bl, ex, nu: (bl[v], 0))] + ([yspec] if which else [])
        + [wspec, wspec, wspec],
        out_specs=yspec,
        scratch_shapes=[pltpu.VMEM((D, D), BF16)] * 3,
    )
    return pl.pallas_call(
        functools.partial(_experts_kernel, which=which),
        grid_spec=grid_spec,
        out_shape=jax.ShapeDtypeStruct((MOE_P, D), F32),
        compiler_params=_cp(1),
        name="moe_experts_%d" % which,
    )(blocks, experts, n_used, xs, *([prev] if which else []), w_gate, w_up, w_down)


def _combine_kernel(dest_ref, next_ref, yb_ref, x_ref, mod_ref, modn_ref, gn_ref, *rest, final):
    if final:
        out_ref, ybuf, sem = rest
    else:
        xo_ref, h_ref, ybuf, sem = rest
    t = pl.program_id(1)
    step = pl.program_id(0) * TPB + t
    slot = lax.rem(step, 2)

    def gather(slots_ref, buf):
        return lambda r: _row_copy(yb_ref, slots_ref[0, r], ybuf.at[buf], r, sem.at[buf])

    @pl.when(step == 0)
    def _():
        _for_each_row(lambda r: gather(dest_ref, slot)(r).start())

    @pl.when(step + 1 < BATCH * TPB)
    def _():
        _for_each_row(lambda r: gather(next_ref, 1 - slot)(r).start())

    _for_each_row(lambda r: gather(dest_ref, slot)(r).wait())

    def body():
        xn = x_ref[...] + mod_ref[5:6, :] * ybuf[slot]
        if final:
            out_ref[...] = _normmod(xn, gn_ref[...], 0.0, 0.0)
        else:
            xo_ref[...] = xn
            h_ref[...] = _normmod(xn, gn_ref[...], modn_ref[0:1, :], modn_ref[1:2, :]).astype(BF16)

    if final:
        pl.when(t < LAT_TILES)(body)
    else:
        body()


def _combine(yb, dest, xs, mods, depth, g_next, final):
    row = pl.BlockSpec((None, ROW_TILE, D), lambda b, t: (b, t, 0))
    next_depth = depth if final else depth + 1
    if final:
        out_specs = pl.BlockSpec((None, ROW_TILE, D), lambda b, t: (b, jnp.minimum(t, LAT_TILES - 1), 0))
        out_shape = jax.ShapeDtypeStruct((BATCH, SEQ, D), F32)
    else:
        out_specs = [row, row]
        out_shape = [jax.ShapeDtypeStruct((BATCH, TT, D), F32), jax.ShapeDtypeStruct((BATCH, TT, D), BF16)]
    last = BATCH * TPB - 1
    return pl.pallas_call(
        functools.partial(_combine_kernel, final=final),
        grid=(BATCH, TPB),
        in_specs=[
            pl.BlockSpec((None, 1, ROW_TILE), lambda b, t: (b * TPB + t, 0, 0), memory_space=pltpu.SMEM),
            pl.BlockSpec((None, 1, ROW_TILE), lambda b, t: (jnp.minimum(b * TPB + t + 1, last), 0, 0),
                         memory_space=pltpu.SMEM),
            pl.BlockSpec(memory_space=pl.ANY),
            row, _mod_spec(depth), _mod_spec(next_depth),
            g_next[1],
        ],
        out_specs=out_specs,
        out_shape=out_shape,
        scratch_shapes=[pltpu.VMEM((2, ROW_TILE, D), F32), pltpu.SemaphoreType.DMA((2,))],
        compiler_params=_cp(2),
        name="moe_combine",
    )(dest, dest, yb, xs, mods, mods, g_next[0])


def _class_experts():
    pairs = [(a, b) for a in range(EPG) for b in range(a + 1, EPG)]
    lo = [g * EPG + a for g in range(N_GROUPS) for a, _ in pairs]
    hi = [g * EPG + b for g in range(N_GROUPS) for _, b in pairs]
    return jnp.asarray(lo, jnp.int32), jnp.asarray(hi, jnp.int32)


def _moe_plan(cls, rank, counts):
    padded = (counts + MOE_BM - 1) // MOE_BM * MOE_BM
    pend = jnp.cumsum(padded)
    pstart = pend - padded
    cid = jnp.arange(N_CLS, dtype=jnp.int32)
    dest = rank + jnp.sum(jnp.where(cls[None, :] == cid[:, None], pstart[:, None], 0), axis=0)
    dest = dest.astype(jnp.int32).reshape(NTOK // ROW_TILE, 1, ROW_TILE)
    blk = jnp.arange(MOE_NB, dtype=jnp.int32)
    block_c = jnp.minimum(jnp.sum(pend[None, :] <= (blk * MOE_BM)[:, None], axis=1), N_CLS - 1)
    n_used = (pend[-1] // MOE_BM).astype(jnp.int32)
    used = blk < n_used
    lo, hi = _class_experts()
    onehot_c = block_c[:, None] == cid[None, :]
    e_lo = jnp.sum(jnp.where(onehot_c, lo[None, :], 0), axis=1)
    e_hi = jnp.sum(jnp.where(onehot_c, hi[None, :], 0), axis=1)

    def hold_last(e):
        return jnp.where(used, e, jnp.sum(jnp.where(blk == n_used - 1, e, 0))).astype(jnp.int32)

    key = jnp.where(used, e_hi, N_EXPERTS)
    earlier = (key[None, :] < key[:, None]) | ((key[None, :] == key[:, None]) & (blk[None, :] < blk[:, None]))
    pos = jnp.sum(earlier, axis=1)
    at = pos[None, :] == blk[:, None]
    order1 = jnp.sum(jnp.where(at, blk[None, :], 0), axis=1).astype(jnp.int32)
    e1 = jnp.sum(jnp.where(at, e_hi[None, :], 0), axis=1)
    tail = blk[MOE_NB - MOE_TAIL:] * MOE_BM
    zrow = jnp.concatenate([jnp.where(padded > 0, pend - MOE_BM, -1), jnp.where(tail >= pend[-1], tail, -1)])
    return dest, (blk, hold_last(e_lo)), (order1, hold_last(e1)), n_used[None], zrow.astype(jnp.int32)


def _moe(h, routing, xs, mods, depth, g_next, w_gate, w_up, w_down, final):
    cr, cnt = routing
    dest, visit0, visit1, n_used, zrow = _moe_plan(cr[0], cr[1], cnt[:N_CLS, 0].astype(jnp.int32))
    xsorted = _dispatch(h.reshape(NTOK, DX), dest, zrow)
    y0 = _experts(xsorted, None, 0, *visit0, n_used, w_gate, w_up, w_down, depth)
    yb = _experts(xsorted, y0, 1, *visit1, n_used, w_gate, w_up, w_down, depth)
    return _combine(yb, dest, xs, mods, depth, g_next, final)


def _rope_tables():
    nf = RET_DK // 4
    t = np.arange(SEQ)
    inv = ROPE_BASE ** (-np.arange(nf, dtype=np.float32) / nf)
    cos = np.ones((TT, 256), np.float32)
    sin = np.zeros((TT, 256), np.float32)
    for seg, pos in enumerate(((t // GRID_W).astype(np.float32), (t % GRID_W).astype(np.float32))):
        ang = (pos[:, None] * inv[None, :]).astype(np.float32)
        c, s = np.cos(ang), np.sin(ang)
        cos[:SEQ, seg * 128:(seg + 1) * 128] = np.concatenate([c, c], axis=1)
        sin[:SEQ, seg * 128:(seg + 1) * 128] = np.concatenate([-s, s], axis=1)
    ks = np.float32(RET_DK ** -0.5)
    return jnp.asarray(np.stack([cos, cos * ks])), jnp.asarray(np.stack([sin, sin * ks]))


def _decay_tables(decay_logit):
    C = RET_CHUNK
    log_g = jax.nn.log_sigmoid(decay_logit.astype(F32))
    pos = jnp.arange(C, dtype=F32)
    diff = pos[:, None] - pos[None, :]
    lg = log_g[:, :, None, None]
    dm_f = jnp.where(diff >= 0, jnp.exp(jnp.maximum(diff, 0.0)[None, None] * lg), 0.0)
    dm_b = jnp.where(diff <= 0, jnp.exp(jnp.maximum(-diff, 0.0)[None, None] * lg), 0.0)
    dmat = jnp.stack([dm_f[0], dm_b[1]])
    lgc = log_g[:, :, None]
    qdec = jnp.stack([jnp.exp((pos + 1.0)[None, :] * lgc[0]), jnp.exp((C - pos)[None, :] * lgc[1])])
    kdec = jnp.stack([jnp.exp((C - 1.0 - pos)[None, :] * lgc[0]), jnp.exp(pos[None, :] * lgc[1])])
    cdec = jnp.exp(C * log_g).reshape(-1)
    return dmat, qdec[..., None], kdec[..., None], cdec


def kernel(x, c, ctx, c_ctx, w_mod, b_mod, norm_mix, norm_ffn, norm_final, ret_w_in, ret_decay, ret_gn, ret_w_out,
           fn_w_out, na_w_qkv, na_rpb, na_w_out, router_w, router_bias, moe_w_gate, moe_w_up, moe_w_down):
    cvec = jnp.concatenate([c_ctx[None, :], c, jnp.zeros((8 - 1 - BATCH, D), F32)], axis=0)
    mod_all = _modulation(cvec, w_mod, b_mod).reshape(DEPTH, 8, N_MOD, D)
    pad = jnp.zeros((DEPTH, BATCH, 2, 8 - N_MOD, D), F32)
    mods = jnp.stack([jnp.broadcast_to(mod_all[:, 0:1], (DEPTH, BATCH, N_MOD, D)), mod_all[:, 1:1 + BATCH]], axis=2)
    mods = jnp.concatenate([mods, pad], axis=3)

    rope = _rope_tables()
    cs, dft_ctx, dft_f1, dft_f2 = _dft_tables()
    route = (router_w.T.astype(BF16), router_bias.reshape(N_EXPERTS, 1).astype(F32))

    norm_mix3 = norm_mix.reshape(DEPTH, 1, D)
    norm_ffn3 = norm_ffn.reshape(DEPTH, 1, D)
    ret_gn3 = ret_gn.reshape(-1, 1, 2 * D)

    xs, h = _embed(x, ctx, mods, norm_mix3)
    out = None
    for i in range(DEPTH):
        kind, j = i % 3, i // 3
        if kind == 0:
            qkvg = _project(h.reshape(NTOK, D), ret_w_in, j, rope).reshape(BATCH, TT, 6 * D)
            dmat, qdec, kdec, cdec = _decay_tables(ret_decay[j])
            o_f, o_b = _retention(qkvg, dmat, qdec, kdec, cdec)
            xs, hf, *routing = _mixer_out("ret", (o_f, o_b, qkvg, ret_gn3), ret_w_out, j, xs, mods, norm_ffn3, i, route)
        elif kind == 1:
            ab = _fn_channel(h, cs)
            f_lat = _fn_cols(_fn_rows(ab, dft_f1), dft_f2)
            xs, hf, *routing = _mixer_out(
                "split", (f_lat, _fn_ctx(ab, dft_ctx)), fn_w_out, j, xs, mods, norm_ffn3, i, route)
        else:
            qkv = _project(h.reshape(NTOK, D), na_w_qkv, j).reshape(BATCH, TT, 3 * D)
            o = _na_attention(qkv, _na_bias_table(na_rpb[j]))
            xs, hf, *routing = _mixer_out("plain", (o,), na_w_out, j, xs, mods, norm_ffn3, i, route)
        final = i == DEPTH - 1
        if final:
            g_next = (norm_final[None, :], pl.BlockSpec((1, D), lambda b, t: (0, 0)))
        else:
            g_next = (norm_mix3, _layer_spec((1, D), i + 1))
        res = _moe(hf, routing, xs, mods, i, g_next, moe_w_gate, moe_w_up, moe_w_down, final)
        if final:
            out = res
        else:
            xs, h = res
    return out
```

```python
import functools

import numpy as np
import jax
import jax.numpy as jnp
from jax import lax
from jax.experimental import pallas as pl
from jax.experimental.pallas import tpu as pltpu

F32 = jnp.float32
BF16 = jnp.bfloat16
HIGHEST = lax.Precision.HIGHEST

D = 1024
BATCH = 4
SEQ = 4096
CTX = 256
TT = SEQ + CTX
NTOK = BATCH * TT
DEPTH = 4
GRID_W = 64
EPS = 1e-6
NEG_INF = -1e30
N_MOD = 6

RET_HEADS = 4
RET_DK = D // RET_HEADS
RET_DV = 2 * RET_DK
RET_CHUNK = 256
ROPE_BASE = 10000.0

FN_GROUPS = 4
FN_GC = D // FN_GROUPS

NA_HEADS = 16
NA_HD = D // NA_HEADS
NA_KH = 8
NA_KW = 16

N_EXPERTS = 16
N_GROUPS = 4
EPG = N_EXPERTS // N_GROUPS

ROW_TILE = 256
TPB = TT // ROW_TILE
LAT_TILES = SEQ // ROW_TILE
MM_TM = TT // 4
MOE_BM = 256
PAIRS = EPG * (EPG - 1) // 2
N_CLS = N_GROUPS * PAIRS
CLS_PAD = 32
MOE_NB = (NTOK + N_CLS * (MOE_BM - 1) + MOE_BM - 1) // MOE_BM
MOE_P = MOE_NB * MOE_BM
MOE_TAIL = MOE_NB - NTOK // MOE_BM
MOE_NZ = N_CLS + MOE_TAIL
DISPATCH_TILE = 512
DX = D + 128
NA_QT = 2 * GRID_W
VMEM_LIMIT = 56 * 1024 * 1024


def _cp(n_axes, vmem=VMEM_LIMIT):
    return pltpu.CompilerParams(dimension_semantics=("arbitrary",) * n_axes, vmem_limit_bytes=vmem)


def _silu(x):
    return x * jax.nn.sigmoid(x)


def _normmod(x, g, shift, scale):
    y = x * lax.rsqrt(jnp.mean(x * x, axis=-1, keepdims=True) + EPS)
    return (y * g) * (1.0 + scale) + shift


def _mod_idx(t):
    return jnp.where(t < LAT_TILES, 1, 0)


def _layer_spec(shape, layer):
    return pl.BlockSpec((None,) + shape, lambda *_: (layer,) + (0,) * len(shape))


def _mod_spec(depth):
    return pl.BlockSpec((None, None, None, 8, D), lambda b, t: (depth, b, _mod_idx(t), 0, 0))


def _mod_kernel(c_ref, w_ref, b_ref, o_ref):
    a = _silu(c_ref[...])
    o_ref[...] = jnp.dot(a.astype(BF16), w_ref[...].astype(BF16), preferred_element_type=F32) + b_ref[...]


def _modulation(cvec, w_mod, b_mod):
    return pl.pallas_call(
        _mod_kernel,
        grid=(DEPTH, N_MOD),
        in_specs=[
            pl.BlockSpec((8, D), lambda i, j: (0, 0)),
            pl.BlockSpec((None, D, D), lambda i, j: (i, 0, j)),
            pl.BlockSpec((None, 1, D), lambda i, j: (i, 0, j)),
        ],
        out_specs=pl.BlockSpec((None, 8, D), lambda i, j: (i, 0, j)),
        out_shape=jax.ShapeDtypeStruct((DEPTH, 8, N_MOD * D), F32),
        compiler_params=_cp(2),
        name="modulation",
    )(cvec, w_mod, b_mod.reshape(DEPTH, 1, N_MOD * D))


def _embed_kernel(x_ref, ctx_ref, mod_ref, g_ref, xs_ref, h_ref):
    t = pl.program_id(1)

    def emit(src):
        xs_ref[...] = src
        h_ref[...] = _normmod(src, g_ref[...], mod_ref[0:1, :], mod_ref[1:2, :]).astype(BF16)

    @pl.when(t < LAT_TILES)
    def _():
        emit(x_ref[...])

    @pl.when(t >= LAT_TILES)
    def _():
        emit(ctx_ref[...])


def _embed(x, ctx, mods, g):
    return pl.pallas_call(
        _embed_kernel,
        grid=(BATCH, TPB),
        in_specs=[
            pl.BlockSpec((None, ROW_TILE, D), lambda b, t: (b, jnp.minimum(t, LAT_TILES - 1), 0)),
            pl.BlockSpec((None, CTX, D), lambda b, t: (b, 0, 0)),
            _mod_spec(0),
            _layer_spec((1, D), 0),
        ],
        out_specs=[
            pl.BlockSpec((None, ROW_TILE, D), lambda b, t: (b, t, 0)),
            pl.BlockSpec((None, ROW_TILE, D), lambda b, t: (b, t, 0)),
        ],
        out_shape=[
            jax.ShapeDtypeStruct((BATCH, TT, D), F32),
            jax.ShapeDtypeStruct((BATCH, TT, D), BF16),
        ],
        compiler_params=_cp(2),
        name="embed",
    )(x, ctx, mods, g)


def _rope_store(acc, cos_ref, sin_ref, o_ref):
    for c in range(acc.shape[1] // 128):
        xc = acc[:, c * 128:(c + 1) * 128]
        tsl = slice((c % 2) * 128, (c % 2 + 1) * 128)
        rot = xc * cos_ref[:, tsl] + pltpu.roll(xc, 64, 1) * sin_ref[:, tsl]
        o_ref[:, c * 128:(c + 1) * 128] = rot.astype(o_ref.dtype)


def _proj_kernel(a_ref, w_ref, *rest, n_rope):
    if n_rope:
        cos_ref, sin_ref, o_ref, wb_ref = rest
    else:
        o_ref, wb_ref = rest
    j = pl.program_id(0)

    @pl.when(pl.program_id(1) == 0)
    def _():
        wb_ref[...] = w_ref[...].astype(BF16)

    acc = jnp.dot(a_ref[...], wb_ref[...], preferred_element_type=F32)
    if n_rope:
        @pl.when(j < n_rope)
        def _():
            _rope_store(acc, cos_ref, sin_ref, o_ref)

        @pl.when(j >= n_rope)
        def _():
            o_ref[...] = acc.astype(o_ref.dtype)
    else:
        o_ref[...] = acc.astype(o_ref.dtype)


def _project(a, w, layer, rope=None):
    _, k, n = w.shape
    tn = D
    n_rope = 0 if rope is None else rope[0].shape[0]
    in_specs = [
        pl.BlockSpec((MM_TM, k), lambda j, i: (i, 0)),
        pl.BlockSpec((None, k, tn), lambda j, i: (layer, 0, j)),
    ]
    args = [a, w]
    if n_rope:
        tspec = pl.BlockSpec((None, MM_TM, 256), lambda j, i: (jnp.minimum(j, n_rope - 1), i % (TT // MM_TM), 0))
        in_specs += [tspec, tspec]
        args += list(rope)
    return pl.pallas_call(
        functools.partial(_proj_kernel, n_rope=n_rope),
        grid=(n // tn, NTOK // MM_TM),
        in_specs=in_specs,
        out_specs=pl.BlockSpec((MM_TM, tn), lambda j, i: (i, j)),
        out_shape=jax.ShapeDtypeStruct((NTOK, n), BF16),
        scratch_shapes=[pltpu.VMEM((k, tn), BF16)],
        compiler_params=_cp(2),
        name="project",
    )(*args)


N_CHUNKS = TT // RET_CHUNK
LAT_CHUNKS = SEQ // RET_CHUNK
CTX_CHUNKS = CTX // RET_CHUNK


def _fwd_chunk(n):
    return jnp.where(n < CTX_CHUNKS, LAT_CHUNKS + n, n - CTX_CHUNKS)


def _bwd_chunk(n):
    return N_CHUNKS - 1 - n


def _ret_kernel(cdec_ref, qf, kf, vf, qb, kb, vb, dmat_ref, qdec_ref, kdec_ref, of_ref, ob_ref, s_ref):
    @pl.when(pl.program_id(1) == 0)
    def _():
        s_ref[...] = jnp.zeros_like(s_ref)

    for d, (q_r, k_r, v_r, o_r) in enumerate(((qf, kf, vf, of_ref), (qb, kb, vb, ob_ref))):
        for h in range(RET_HEADS):
            q = q_r[:, h * RET_DK:(h + 1) * RET_DK]
            k = k_r[:, h * RET_DK:(h + 1) * RET_DK]
            v = v_r[:, h * RET_DV:(h + 1) * RET_DV]
            att = lax.dot_general(q, k, (((1,), (1,)), ((), ())), preferred_element_type=F32) * dmat_ref[d, h]
            s = s_ref[d, h]
            qd = (q.astype(F32) * qdec_ref[d, h]).astype(BF16)
            o = (jnp.dot(att.astype(BF16), v, preferred_element_type=F32)
                 + jnp.dot(qd, s.astype(BF16), preferred_element_type=F32))
            kd = (k.astype(F32) * kdec_ref[d, h]).astype(BF16)
            s_ref[d, h] = s * cdec_ref[d * RET_HEADS + h] + lax.dot_general(
                kd, v, (((0,), (0,)), ((), ())), preferred_element_type=F32)
            o_r[:, h * RET_DV:(h + 1) * RET_DV] = o.astype(o_r.dtype)


def _retention(qkvg, dmat, qdec, kdec, cdec):
    C = RET_CHUNK

    def spec(width, col, chunk_fn):
        return pl.BlockSpec((None, C, width), lambda b, n, cd: (b, chunk_fn(n), col))

    ospec_f = pl.BlockSpec((None, C, 2 * D), lambda b, n, cd: (b, _fwd_chunk(n), 0))
    ospec_b = pl.BlockSpec((None, C, 2 * D), lambda b, n, cd: (b, _bwd_chunk(n), 0))
    full = lambda shape: pl.BlockSpec(shape, lambda b, n, cd: (0,) * len(shape))
    grid_spec = pltpu.PrefetchScalarGridSpec(
        num_scalar_prefetch=1,
        grid=(BATCH, N_CHUNKS),
        in_specs=[
            spec(D, 0, _fwd_chunk), spec(D, 1, _fwd_chunk), spec(2 * D, 1, _fwd_chunk),
            spec(D, 0, _bwd_chunk), spec(D, 1, _bwd_chunk), spec(2 * D, 1, _bwd_chunk),
            full((2, RET_HEADS, C, C)), full((2, RET_HEADS, C, 1)), full((2, RET_HEADS, C, 1)),
        ],
        out_specs=[ospec_f, ospec_b],
        scratch_shapes=[pltpu.VMEM((2, RET_HEADS, RET_DK, RET_DV), F32)],
    )
    return pl.pallas_call(
        _ret_kernel,
        grid_spec=grid_spec,
        out_shape=[jax.ShapeDtypeStruct((BATCH, TT, 2 * D), BF16)] * 2,
        compiler_params=_cp(2),
        name="retention",
    )(cdec, qkvg, qkvg, qkvg, qkvg, qkvg, qkvg, dmat, qdec, kdec)


def _mixer_out_kernel(*refs, kind):
    n_src = {"ret": 4, "split": 2, "plain": 1}[kind]
    srcs, (w_ref, x_ref, mod_ref, nf_ref), route_in = refs[:n_src], refs[n_src:n_src + 4], refs[n_src + 4:n_src + 7]
    xo_ref, h_ref, cr_ref, cnt_ref, wb_ref = refs[n_src + 7:]
    first = (pl.program_id(0) == 0) & (pl.program_id(1) == 0)

    @pl.when(first)
    def _():
        wb_ref[...] = w_ref[...].astype(BF16)

    if kind == "ret":
        of_ref, ob_ref, g_ref, gn_ref = srcs
        o = of_ref[...].astype(F32) + ob_ref[...].astype(F32)
        parts = []
        for h in range(RET_HEADS):
            oh = o[:, h * RET_DV:(h + 1) * RET_DV]
            mu = jnp.mean(oh, axis=-1, keepdims=True)
            ctr = oh - mu
            var = jnp.mean(ctr * ctr, axis=-1, keepdims=True)
            parts.append(ctr * lax.rsqrt(var + EPS))
        on = jnp.concatenate(parts, axis=-1) * gn_ref[...]
        a = (on * _silu(g_ref[...].astype(F32))).astype(BF16)
    elif kind == "split":
        a = jnp.where(pl.program_id(1) < LAT_TILES, srcs[0][...], srcs[1][...])
    else:
        a = srcs[0][...]
    y = jnp.dot(a, wb_ref[...], preferred_element_type=F32)
    xn = x_ref[...] + mod_ref[2:3, :] * y
    xo_ref[...] = xn
    h = _normmod(xn, nf_ref[...], mod_ref[3:4, :], mod_ref[4:5, :])
    wa, wb = _route(h, first, *route_in, cr_ref, cnt_ref)
    sub = lax.broadcasted_iota(jnp.int32, (DX - D, ROW_TILE), 0)
    h_ref[:, :D] = h
    h_ref[:, D:] = jnp.where(sub == 0, wa, jnp.where(sub == 1, wb, 0.0)).T


def _mixer_out(kind, srcs, w, layer, xs, mods, norm_ffn, depth, route):
    k = w.shape[1]
    row = lambda width, col=0: pl.BlockSpec((None, ROW_TILE, width), lambda b, t: (b, t, col))
    const = lambda shape: pl.BlockSpec(shape, lambda b, t: (0,) * len(shape))
    per_tok = pl.BlockSpec((2, ROW_TILE), lambda b, t: (0, b * TPB + t))
    tri = jnp.asarray(np.triu(np.ones((ROW_TILE, ROW_TILE), np.float32), 1), BF16)
    if kind == "ret":
        src_specs = [row(2 * D), row(2 * D), row(2 * D, 2), _layer_spec((1, 2 * D), layer)]
    elif kind == "split":
        src_specs = [pl.BlockSpec((None, ROW_TILE, k), lambda b, t: (b, jnp.minimum(t, LAT_TILES - 1), 0)),
                     pl.BlockSpec((None, CTX, k), lambda b, t: (b, 0, 0))]
    else:
        src_specs = [row(k)]
    return pl.pallas_call(
        functools.partial(_mixer_out_kernel, kind=kind),
        grid=(BATCH, TPB),
        in_specs=src_specs + [_layer_spec((k, D), layer), row(D), _mod_spec(depth), _layer_spec((1, D), depth),
                              const((N_EXPERTS, D)), const((N_EXPERTS, 1)), const((ROW_TILE, ROW_TILE))],
        out_specs=[row(D), row(DX), per_tok, const((CLS_PAD, 128))],
        out_shape=[jax.ShapeDtypeStruct((BATCH, TT, D), F32), jax.ShapeDtypeStruct((BATCH, TT, DX), F32),
                   jax.ShapeDtypeStruct((2, NTOK), jnp.int32), jax.ShapeDtypeStruct((CLS_PAD, 128), F32)],
        scratch_shapes=[pltpu.VMEM((k, D), BF16)],
        compiler_params=_cp(2),
        name="mixer_out_" + kind,
    )(*srcs, w, xs, mods, norm_ffn, *route, tri)


def _fn_channel_kernel(h_ref, cs_ref, o_ref):
    for g in range(FN_GROUPS):
        r = jnp.dot(h_ref[:, g * FN_GC:(g + 1) * FN_GC], cs_ref[...], preferred_element_type=F32)
        o_ref[0, :, g * FN_GC:(g + 1) * FN_GC] = r[:, :FN_GC].astype(BF16)
        o_ref[1, :, g * FN_GC:(g + 1) * FN_GC] = r[:, FN_GC:].astype(BF16)


def _fn_channel(h, cs):
    return pl.pallas_call(
        _fn_channel_kernel,
        grid=(BATCH, TPB),
        in_specs=[
            pl.BlockSpec((None, ROW_TILE, D), lambda b, t: (b, t, 0)),
            pl.BlockSpec((FN_GC, 2 * FN_GC), lambda b, t: (0, 0)),
        ],
        out_specs=pl.BlockSpec((None, 2, ROW_TILE, D), lambda b, t: (b, 0, t, 0)),
        out_shape=jax.ShapeDtypeStruct((BATCH, 2, TT, D), BF16),
        compiler_params=_cp(2),
        name="fourier_channels",
    )(h, cs)


FN_R = 64
FN_CB = 16


def _fn_rows_kernel(ab_ref, f1_ref, y_ref, u32):
    u32[...] = ab_ref[...].astype(F32)
    for j in range(FN_CB):
        st = jnp.concatenate([u32[0, :, j, :], u32[1, :, j, :]], axis=0).astype(BF16)
        y = jnp.dot(f1_ref[j], st, preferred_element_type=F32)
        y_ref[0, j] = y[:FN_R].astype(BF16)
        y_ref[1, j] = y[FN_R:].astype(BF16)


def _fn_rows(ab, f1):
    ab5 = ab.reshape(BATCH, 2, TT // FN_R, FN_R, D)
    return pl.pallas_call(
        _fn_rows_kernel,
        grid=(BATCH, FN_R // FN_CB),
        in_specs=[
            pl.BlockSpec((None, 2, FN_R, FN_CB, D), lambda b, cb: (b, 0, 0, cb, 0)),
            pl.BlockSpec((FN_CB, 2 * FN_R, 2 * FN_R), lambda b, cb: (cb, 0, 0)),
        ],
        out_specs=pl.BlockSpec((None, 2, FN_CB, FN_R, D), lambda b, cb: (b, 0, cb, 0, 0)),
        out_shape=jax.ShapeDtypeStruct((BATCH, 2, FN_R, FN_R, D), BF16),
        scratch_shapes=[pltpu.VMEM((2, FN_R, FN_CB, D), F32)],
        compiler_params=_cp(2),
        name="fourier_rows",
    )(ab5, f1)


def _fn_cols_kernel(y_ref, f2_ref, o_ref, y32, z32):
    y32[...] = y_ref[...].astype(F32)
    for j in range(FN_CB):
        st = jnp.concatenate([y32[0, :, j, :], y32[1, :, j, :]], axis=0).astype(BF16)
        z32[:, j, :] = jnp.dot(f2_ref[...], st, preferred_element_type=F32)
    o_ref[...] = z32[...].astype(BF16)


def _fn_cols(y, f2):
    out = pl.pallas_call(
        _fn_cols_kernel,
        grid=(BATCH, FN_R // FN_CB),
        in_specs=[
            pl.BlockSpec((None, 2, FN_R, FN_CB, D), lambda b, kb: (b, 0, 0, kb, 0)),
            pl.BlockSpec((FN_R, 2 * FN_R), lambda b, kb: (0, 0)),
        ],
        out_specs=pl.BlockSpec((None, FN_R, FN_CB, D), lambda b, kb: (b, 0, kb, 0)),
        out_shape=jax.ShapeDtypeStruct((BATCH, FN_R, FN_R, D), BF16),
        scratch_shapes=[pltpu.VMEM((2, FN_R, FN_CB, D), F32), pltpu.VMEM((FN_R, FN_CB, D), F32)],
        compiler_params=_cp(2),
        name="fourier_cols",
    )(y, f2)
    return out.reshape(BATCH, SEQ, D)


def _fn_ctx_kernel(ab_ref, m_ref, o_ref):
    st = jnp.concatenate([ab_ref[0], ab_ref[1]], axis=0)
    o_ref[...] = jnp.dot(m_ref[...], st, preferred_element_type=F32).astype(BF16)


def _fn_ctx(ab, m_ctx):
    return pl.pallas_call(
        _fn_ctx_kernel,
        grid=(BATCH,),
        in_specs=[
            pl.BlockSpec((None, 2, CTX, D), lambda b: (b, 0, SEQ // CTX, 0)),
            pl.BlockSpec((CTX, 2 * CTX), lambda b: (0, 0)),
        ],
        out_specs=pl.BlockSpec((None, CTX, D), lambda b: (b, 0, 0)),
        out_shape=jax.ShapeDtypeStruct((BATCH, CTX, D), BF16),
        compiler_params=_cp(1),
        name="fourier_ctx",
    )(ab, m_ctx)


def _dft_tables():
    def cs_pair(n, scale):
        i = np.arange(n)
        ang = 2.0 * np.pi * ((i[:, None] * i[None, :]) % n) / n
        return np.cos(ang) * scale, np.sin(ang) * scale

    assert FN_GC == CTX
    c, s = cs_pair(FN_GC, FN_GC ** -0.5)
    cs = np.concatenate([c, s], axis=1)
    m_ctx = np.concatenate([c, -s], axis=1)
    c, s = cs_pair(FN_R, FN_R ** -0.5)
    f1 = np.block([[c, -s], [s, c]])
    f2 = np.concatenate([c, -s], axis=1)
    i = np.arange(FN_R)
    phi = 2.0 * np.pi * (i[:, None] * i[None, :]) / SEQ
    f1_tw = np.stack([np.block([[np.diag(np.cos(p)), -np.diag(np.sin(p))],
                                [np.diag(np.sin(p)), np.diag(np.cos(p))]]) @ f1 for p in phi])
    return jnp.asarray(cs, BF16), jnp.asarray(m_ctx, BF16), jnp.asarray(f1_tw, BF16), jnp.asarray(f2, BF16)


NA_ROWS = SEQ // GRID_W
NA_STEPS = TT // NA_QT
NA_NK = NA_KH * GRID_W
NA_DR = 2 * NA_KH - 1


def _na_kernel(q_ref, k_ref, v_ref, bias_ref, o_ref):
    step = pl.program_id(1)
    lane = lax.broadcasted_iota(jnp.int32, (GRID_W, 128), 1)
    nt = (((1,), (1,)), ((), ()))
    scale = NA_HD ** -0.5

    def attend(windows):
        chains = [(slice(i * GRID_W, (i + 1) * GRID_W), win, p, slice(p * 128, (p + 1) * 128))
                  for i, win in enumerate(windows) for p in range(NA_HEADS // 2)]
        scores = []
        for rows, win, p, sl in chains:
            qp = q_ref[rows, sl] * scale
            zero = jnp.zeros_like(qp)
            q2 = jnp.concatenate([jnp.where(lane < NA_HD, qp, zero), jnp.where(lane >= NA_HD, qp, zero)], axis=0)
            s_c = lax.dot_general(q2, k_ref[SEQ:TT, sl], nt, preferred_element_type=F32)
            s_w = None
            if win is not None:
                start, d0 = win
                bias = jnp.concatenate(
                    [jnp.concatenate([bias_ref[2 * p + hh, d0 + 2 * j2] for j2 in range(NA_KH // 2)], axis=1)
                     for hh in range(2)], axis=0)
                s_w = lax.dot_general(q2, k_ref[pl.ds(start, NA_NK), sl], nt, preferred_element_type=F32) + bias
            scores.append((s_c, s_w))
        probs = []
        for s_c, s_w in scores:
            m = jnp.max(s_c, axis=-1, keepdims=True)
            if s_w is not None:
                m = jnp.maximum(m, jnp.max(s_w, axis=-1, keepdims=True))
            p_c = jnp.exp(s_c - m)
            l = jnp.sum(p_c, axis=-1, keepdims=True)
            p_w = None
            if s_w is not None:
                p_w = jnp.exp(s_w - m)
                l = l + jnp.sum(p_w, axis=-1, keepdims=True)
                p_w = p_w.astype(BF16)
            probs.append((p_c.astype(BF16), p_w, l))
        for (rows, win, p, sl), (p_c, p_w, l) in zip(chains, probs):
            acc = jnp.dot(p_c, v_ref[SEQ:TT, sl], preferred_element_type=F32)
            if p_w is not None:
                acc = acc + jnp.dot(p_w, v_ref[pl.ds(win[0], NA_NK), sl], preferred_element_type=F32)
            o2 = acc / l
            o_ref[rows, sl] = jnp.where(lane < NA_HD, o2[:GRID_W], o2[GRID_W:]).astype(BF16)

    n_blocks = NA_QT // GRID_W

    @pl.when(step < NA_ROWS // n_blocks)
    def _():
        windows = []
        for i in range(n_blocks):
            r = step * n_blocks + i
            rs = jnp.clip(r - NA_KH // 2, 0, NA_ROWS - NA_KH)
            windows.append((pl.multiple_of(rs * GRID_W, GRID_W), rs - r + (NA_KH - 1)))
        attend(windows)

    @pl.when(step >= NA_ROWS // n_blocks)
    def _():
        attend([None] * n_blocks)


def _na_attention(qkv, bias):
    return pl.pallas_call(
        _na_kernel,
        grid=(BATCH, NA_STEPS),
        in_specs=[
            pl.BlockSpec((None, NA_QT, D), lambda b, s: (b, s, 0)),
            pl.BlockSpec((None, TT, D), lambda b, s: (b, 0, 1), pipeline_mode=pl.Buffered(1)),
            pl.BlockSpec((None, TT, D), lambda b, s: (b, 0, 2), pipeline_mode=pl.Buffered(1)),
            pl.BlockSpec((NA_HEADS, NA_DR - 1, GRID_W, 2 * GRID_W), lambda b, s: (0, 0, 0, 0),
                         pipeline_mode=pl.Buffered(1)),
        ],
        out_specs=pl.BlockSpec((None, NA_QT, D), lambda b, s: (b, s, 0)),
        out_shape=jax.ShapeDtypeStruct((BATCH, TT, D), BF16),
        compiler_params=_cp(2),
        name="na_attention",
    )(qkv, qkv, qkv, bias)


def _na_bias_table(rpb):
    qc = np.arange(GRID_W)
    kcol = np.arange(GRID_W)
    cstart = np.clip(qc - NA_KW // 2, 0, GRID_W - NA_KW)
    valid = (kcol[None, :] >= cstart[:, None]) & (kcol[None, :] < cstart[:, None] + NA_KW)
    dc = kcol[None, :] - qc[:, None] + (NA_KW - 1)
    pick = ((dc[None] == np.arange(2 * NA_KW - 1)[:, None, None]) & valid[None]).astype(np.float32)
    tab = jnp.einsum("hdj,jqk->hdqk", rpb.astype(F32), jnp.asarray(pick), precision=HIGHEST)
    tab = tab + jnp.asarray(np.where(valid, 0.0, NEG_INF).astype(np.float32))
    return jnp.concatenate([tab[:, :NA_DR - 1], tab[:, 1:]], axis=-1)


def _route(h, first, rw_ref, rb_ref, tri_ref, cr_ref, cnt_ref):
    logits = lax.dot_general(rw_ref[...], h.astype(BF16), (((1,), (1,)), ((), ())),
                             preferred_element_type=F32)
    sc = jax.nn.sigmoid(logits)
    sel = sc + rb_ref[...]
    sc_r = [sc[e:e + 1, :] for e in range(N_EXPERTS)]
    sel_r = [sel[e:e + 1, :] for e in range(N_EXPERTS)]
    zero_i = jnp.zeros_like(sel_r[0]).astype(jnp.int32)

    best = gi = None
    picks = []
    for g in range(N_GROUPS):
        a = sel_r[g * EPG:(g + 1) * EPG]
        u = sc_r[g * EPG:(g + 1) * EPG]
        gs = None
        for i in range(EPG):
            for j in range(i + 1, EPG):
                pair = a[i] + a[j]
                gs = pair if gs is None else jnp.maximum(gs, pair)
        m1, i1, s1 = a[0], zero_i, u[0]
        for j in range(1, EPG):
            upd = a[j] > m1
            m1 = jnp.where(upd, a[j], m1)
            i1 = jnp.where(upd, j, i1)
            s1 = jnp.where(upd, u[j], s1)
        m2 = i2 = s2 = None
        for j in range(EPG):
            cand = jnp.where(i1 == j, -jnp.inf, a[j])
            if m2 is None:
                m2, i2, s2 = cand, zero_i, u[0]
            else:
                upd = cand > m2
                m2 = jnp.where(upd, cand, m2)
                i2 = jnp.where(upd, j, i2)
                s2 = jnp.where(upd, u[j], s2)
        picks.append((i1 + g * EPG, i2 + g * EPG, s1, s2))
        if best is None:
            best, gi = gs, zero_i
        else:
            upd = gs > best
            best = jnp.where(upd, gs, best)
            gi = jnp.where(upd, g, gi)

    e1, e2, s1, s2 = picks[0]
    for g in range(1, N_GROUPS):
        on = gi == g
        e1 = jnp.where(on, picks[g][0], e1)
        e2 = jnp.where(on, picks[g][1], e2)
        s1 = jnp.where(on, picks[g][2], s1)
        s2 = jnp.where(on, picks[g][3], s2)
    tot = s1 + s2
    w1, w2 = s1 / tot, s2 / tot

    swap = e2 < e1
    la = jnp.where(swap, e2, e1) - gi * EPG
    lb = jnp.where(swap, e1, e2) - gi * EPG
    cls = gi * PAIRS + jnp.where(la == 0, 0, jnp.where(la == 1, 3, 5)) + lb - la - 1
    wa, wb = jnp.where(swap, w2, w1), jnp.where(swap, w1, w2)

    @pl.when(first)
    def _():
        cnt_ref[...] = jnp.zeros_like(cnt_ref)

    oh = (lax.broadcasted_iota(jnp.int32, (cnt_ref.shape[0], cls.shape[1]), 0) == cls).astype(F32)
    before = jnp.dot(oh.astype(BF16), tri_ref[...], preferred_element_type=F32)
    base = cnt_ref[:, 0:1]
    cr_ref[0:1, :] = cls
    cr_ref[1:2, :] = jnp.sum(oh * (base + before), axis=0, keepdims=True).astype(jnp.int32)
    cnt_ref[...] = cnt_ref[...] + jnp.sum(oh, axis=1, keepdims=True)
    return wa, wb


DMA_UNROLL = 16


def _row_copy(src_ref, src_row, dst_ref, dst_row, sem):
    return pltpu.make_async_copy(src_ref.at[pl.ds(src_row, 1)], dst_ref.at[pl.ds(dst_row, 1)], sem)


def _for_each_row(fn, rows=ROW_TILE):
    def group(gidx, carry):
        for u in range(DMA_UNROLL):
            fn(gidx * DMA_UNROLL + u)
        return carry

    lax.fori_loop(0, rows // DMA_UNROLL, group, 0)


def _dispatch_kernel(zrow_ref, dest_ref, h_ref, xs_ref, zbuf, sem, zsem):
    @pl.when(pl.program_id(0) == 0)
    def _():
        zbuf[...] = jnp.zeros_like(zbuf)

        def zero_copy(e):
            start = pl.multiple_of(jnp.maximum(zrow_ref[e], 0), MOE_BM)
            return pltpu.make_async_copy(zbuf, xs_ref.at[pl.ds(start, MOE_BM)], zsem)

        for e in range(MOE_NZ):
            pl.when(zrow_ref[e] >= 0)(lambda e=e: zero_copy(e).start())
        for e in range(MOE_NZ):
            pl.when(zrow_ref[e] >= 0)(lambda e=e: zero_copy(e).wait())

    _for_each_row(lambda r: _row_copy(h_ref, r, xs_ref, dest_ref[0, r], sem).start(), DISPATCH_TILE)
    _for_each_row(lambda r: _row_copy(h_ref, r, xs_ref, dest_ref[0, r], sem).wait(), DISPATCH_TILE)


def _dispatch(h, dest, zrow):
    grid_spec = pltpu.PrefetchScalarGridSpec(
        num_scalar_prefetch=1,
        grid=(NTOK // DISPATCH_TILE,),
        in_specs=[
            pl.BlockSpec((None, 1, DISPATCH_TILE), lambda i, z: (i, 0, 0), memory_space=pltpu.SMEM),
            pl.BlockSpec((DISPATCH_TILE, DX), lambda i, z: (i, 0)),
        ],
        out_specs=pl.BlockSpec(memory_space=pl.ANY),
        scratch_shapes=[pltpu.VMEM((MOE_BM, DX), F32), pltpu.SemaphoreType.DMA(()), pltpu.SemaphoreType.DMA(())],
    )
    return pl.pallas_call(
        _dispatch_kernel,
        grid_spec=grid_spec,
        out_shape=jax.ShapeDtypeStruct((MOE_P, DX), F32),
        compiler_params=pltpu.CompilerParams(dimension_semantics=("arbitrary",), vmem_limit_bytes=VMEM_LIMIT,
                                             has_side_effects=True),
        name="moe_dispatch",
    )(zrow, dest.reshape(NTOK // DISPATCH_TILE, 1, DISPATCH_TILE), h)


def _experts_kernel(blk_ref, ex_ref, nu_ref, x_ref, *rest, which):
    if which:
        prev_ref, wg_ref, wu_ref, wd_ref, y_ref, wgb, wub, wdb = rest
    else:
        wg_ref, wu_ref, wd_ref, y_ref, wgb, wub, wdb = rest
    del blk_ref
    v = pl.program_id(0)
    active = v < nu_ref[0]
    changed = (v == 0) | (ex_ref[v] != ex_ref[jnp.maximum(v - 1, 0)])

    @pl.when(active & changed)
    def _():
        wgb[...] = wg_ref[...].astype(BF16)
        wub[...] = wu_ref[...].astype(BF16)
        wdb[...] = wd_ref[...].astype(BF16)

    @pl.when(active)
    def _():
        x = x_ref[:, :D].astype(BF16)
        g = jnp.dot(x, wgb[...], preferred_element_type=F32)
        u = jnp.dot(x, wub[...], preferred_element_type=F32)
        hh = (_silu(g) * u).astype(BF16)
        y = jnp.dot(hh, wdb[...], preferred_element_type=F32) * x_ref[:, D + which:D + which + 1]
        y_ref[...] = prev_ref[...] + y if which else y

    @pl.when(jnp.logical_not(active))
    def _():
        y_ref[...] = jnp.zeros_like(y_ref)


def _experts(xs, prev, which, blocks, experts, n_used, w_gate, w_up, w_down, depth):
    wspec = pl.BlockSpec((None, None, D, D), lambda v, bl, ex, nu: (depth, ex[v], 0, 0))
    yspec = pl.BlockSpec((MOE_BM, D), lambda v, bl, ex, nu: (bl[v], 0))
    grid_spec = pltpu.PrefetchScalarGridSpec(
        num_scalar_prefetch=3,
        grid=(MOE_NB,),
        in_specs=[pl.BlockSpec((MOE_BM, DX), lambda v, bl, ex, nu: (bl[v], 0))] + ([yspec] if which else [])
        + [wspec, wspec, wspec],
        out_specs=yspec,
        scratch_shapes=[pltpu.VMEM((D, D), BF16)] * 3,
    )
    return pl.pallas_call(
        functools.partial(_experts_kernel, which=which),
        grid_spec=grid_spec,
        out_shape=jax.ShapeDtypeStruct((MOE_P, D), F32),
        compiler_params=_cp(1),
        name="moe_experts_%d" % which,
    )(blocks, experts, n_used, xs, *([prev] if which else []), w_gate, w_up, w_down)


def _combine_kernel(dest_ref, next_ref, yb_ref, x_ref, mod_ref, modn_ref, gn_ref, *rest, final):
    if final:
        out_ref, ybuf, sem = rest
    else:
        xo_ref, h_ref, ybuf, sem = rest
    t = pl.program_id(1)
    step = pl.program_id(0) * TPB + t
    slot = lax.rem(step, 2)

    def gather(slots_ref, buf):
        return lambda r: _row_copy(yb_ref, slots_ref[0, r], ybuf.at[buf], r, sem.at[buf])

    @pl.when(step == 0)
    def _():
        _for_each_row(lambda r: gather(dest_ref, slot)(r).start())

    @pl.when(step + 1 < BATCH * TPB)
    def _():
        _for_each_row(lambda r: gather(next_ref, 1 - slot)(r).start())

    _for_each_row(lambda r: gather(dest_ref, slot)(r).wait())

    def body():
        xn = x_ref[...] + mod_ref[5:6, :] * ybuf[slot]
        if final:
            out_ref[...] = _normmod(xn, gn_ref[...], 0.0, 0.0)
        else:
            xo_ref[...] = xn
            h_ref[...] = _normmod(xn, gn_ref[...], modn_ref[0:1, :], modn_ref[1:2, :]).astype(BF16)

    if final:
        pl.when(t < LAT_TILES)(body)
    else:
        body()


def _combine(yb, dest, xs, mods, depth, g_next, final):
    row = pl.BlockSpec((None, ROW_TILE, D), lambda b, t: (b, t, 0))
    next_depth = depth if final else depth + 1
    if final:
        out_specs = pl.BlockSpec((None, ROW_TILE, D), lambda b, t: (b, jnp.minimum(t, LAT_TILES - 1), 0))
        out_shape = jax.ShapeDtypeStruct((BATCH, SEQ, D), F32)
    else:
        out_specs = [row, row]
        out_shape = [jax.ShapeDtypeStruct((BATCH, TT, D), F32), jax.ShapeDtypeStruct((BATCH, TT, D), BF16)]
    last = BATCH * TPB - 1
    return pl.pallas_call(
        functools.partial(_combine_kernel, final=final),
        grid=(BATCH, TPB),
        in_specs=[
            pl.BlockSpec((None, 1, ROW_TILE), lambda b, t: (b * TPB + t, 0, 0), memory_space=pltpu.SMEM),
            pl.BlockSpec((None, 1, ROW_TILE), lambda b, t: (jnp.minimum(b * TPB + t + 1, last), 0, 0),
                         memory_space=pltpu.SMEM),
            pl.BlockSpec(memory_space=pl.ANY),
            row, _mod_spec(depth), _mod_spec(next_depth),
            g_next[1],
        ],
        out_specs=out_specs,
        out_shape=out_shape,
        scratch_shapes=[pltpu.VMEM((2, ROW_TILE, D), F32), pltpu.SemaphoreType.DMA((2,))],
        compiler_params=_cp(2),
        name="moe_combine",
    )(dest, dest, yb, xs, mods, mods, g_next[0])


def _class_experts():
    pairs = [(a, b) for a in range(EPG) for b in range(a + 1, EPG)]
    lo = [g * EPG + a for g in range(N_GROUPS) for a, _ in pairs]
    hi = [g * EPG + b for g in range(N_GROUPS) for _, b in pairs]
    return jnp.asarray(lo, jnp.int32), jnp.asarray(hi, jnp.int32)


def _moe_plan(cls, rank, counts):
    padded = (counts + MOE_BM - 1) // MOE_BM * MOE_BM
    pend = jnp.cumsum(padded)
    pstart = pend - padded
    cid = jnp.arange(N_CLS, dtype=jnp.int32)
    dest = rank + jnp.sum(jnp.where(cls[None, :] == cid[:, None], pstart[:, None], 0), axis=0)
    dest = dest.astype(jnp.int32).reshape(NTOK // ROW_TILE, 1, ROW_TILE)
    blk = jnp.arange(MOE_NB, dtype=jnp.int32)
    block_c = jnp.minimum(jnp.sum(pend[None, :] <= (blk * MOE_BM)[:, None], axis=1), N_CLS - 1)
    n_used = (pend[-1] // MOE_BM).astype(jnp.int32)
    used = blk < n_used
    lo, hi = _class_experts()
    onehot_c = block_c[:, None] == cid[None, :]
    e_lo = jnp.sum(jnp.where(onehot_c, lo[None, :], 0), axis=1)
    e_hi = jnp.sum(jnp.where(onehot_c, hi[None, :], 0), axis=1)

    def hold_last(e):
        return jnp.where(used, e, jnp.sum(jnp.where(blk == n_used - 1, e, 0))).astype(jnp.int32)

    key = jnp.where(used, e_hi, N_EXPERTS)
    earlier = (key[None, :] < key[:, None]) | ((key[None, :] == key[:, None]) & (blk[None, :] < blk[:, None]))
    pos = jnp.sum(earlier, axis=1)
    at = pos[None, :] == blk[:, None]
    order1 = jnp.sum(jnp.where(at, blk[None, :], 0), axis=1).astype(jnp.int32)
    e1 = jnp.sum(jnp.where(at, e_hi[None, :], 0), axis=1)
    tail = blk[MOE_NB - MOE_TAIL:] * MOE_BM
    zrow = jnp.concatenate([jnp.where(padded > 0, pend - MOE_BM, -1), jnp.where(tail >= pend[-1], tail, -1)])
    return dest, (blk, hold_last(e_lo)), (order1, hold_last(e1)), n_used[None], zrow.astype(jnp.int32)


def _moe(h, routing, xs, mods, depth, g_next, w_gate, w_up, w_down, final):
    cr, cnt = routing
    dest, visit0, visit1, n_used, zrow = _moe_plan(cr[0], cr[1], cnt[:N_CLS, 0].astype(jnp.int32))
    xsorted = _dispatch(h.reshape(NTOK, DX), dest, zrow)
    y0 = _experts(xsorted, None, 0, *visit0, n_used, w_gate, w_up, w_down, depth)
    yb = _experts(xsorted, y0, 1, *visit1, n_used, w_gate, w_up, w_down, depth)
    return _combine(yb, dest, xs, mods, depth, g_next, final)


def _rope_tables():
    nf = RET_DK // 4
    t = np.arange(SEQ)
    inv = ROPE_BASE ** (-np.arange(nf, dtype=np.float32) / nf)
    cos = np.ones((TT, 256), np.float32)
    sin = np.zeros((TT, 256), np.float32)
    for seg, pos in enumerate(((t // GRID_W).astype(np.float32), (t % GRID_W).astype(np.float32))):
        ang = (pos[:, None] * inv[None, :]).astype(np.float32)
        c, s = np.cos(ang), np.sin(ang)
        cos[:SEQ, seg * 128:(seg + 1) * 128] = np.concatenate([c, c], axis=1)
        sin[:SEQ, seg * 128:(seg + 1) * 128] = np.concatenate([-s, s], axis=1)
    ks = np.float32(RET_DK ** -0.5)
    return jnp.asarray(np.stack([cos, cos * ks])), jnp.asarray(np.stack([sin, sin * ks]))


def _decay_tables(decay_logit):
    C = RET_CHUNK
    log_g = jax.nn.log_sigmoid(decay_logit.astype(F32))
    pos = jnp.arange(C, dtype=F32)
    diff = pos[:, None] - pos[None, :]
    lg = log_g[:, :, None, None]
    dm_f = jnp.where(diff >= 0, jnp.exp(jnp.maximum(diff, 0.0)[None, None] * lg), 0.0)
    dm_b = jnp.where(diff <= 0, jnp.exp(jnp.maximum(-diff, 0.0)[None, None] * lg), 0.0)
    dmat = jnp.stack([dm_f[0], dm_b[1]])
    lgc = log_g[:, :, None]
    qdec = jnp.stack([jnp.exp((pos + 1.0)[None, :] * lgc[0]), jnp.exp((C - pos)[None, :] * lgc[1])])
    kdec = jnp.stack([jnp.exp((C - 1.0 - pos)[None, :] * lgc[0]), jnp.exp(pos[None, :] * lgc[1])])
    cdec = jnp.exp(C * log_g).reshape(-1)
    return dmat, qdec[..., None], kdec[..., None], cdec


def kernel(x, c, ctx, c_ctx, w_mod, b_mod, norm_mix, norm_ffn, norm_final, ret_w_in, ret_decay, ret_gn, ret_w_out,
           fn_w_out, na_w_qkv, na_rpb, na_w_out, router_w, router_bias, moe_w_gate, moe_w_up, moe_w_down):
    cvec = jnp.concatenate([c_ctx[None, :], c, jnp.zeros((8 - 1 - BATCH, D), F32)], axis=0)
    mod_all = _modulation(cvec, w_mod, b_mod).reshape(DEPTH, 8, N_MOD, D)
    pad = jnp.zeros((DEPTH, BATCH, 2, 8 - N_MOD, D), F32)
    mods = jnp.stack([jnp.broadcast_to(mod_all[:, 0:1], (DEPTH, BATCH, N_MOD, D)), mod_all[:, 1:1 + BATCH]], axis=2)
    mods = jnp.concatenate([mods, pad], axis=3)

    rope = _rope_tables()
    cs, dft_ctx, dft_f1, dft_f2 = _dft_tables()
    route = (router_w.T.astype(BF16), router_bias.reshape(N_EXPERTS, 1).astype(F32))

    norm_mix3 = norm_mix.reshape(DEPTH, 1, D)
    norm_ffn3 = norm_ffn.reshape(DEPTH, 1, D)
    ret_gn3 = ret_gn.reshape(-1, 1, 2 * D)

    xs, h = _embed(x, ctx, mods, norm_mix3)
    out = None
    for i in range(DEPTH):
        kind, j = i % 3, i // 3
        if kind == 0:
            qkvg = _project(h.reshape(NTOK, D), ret_w_in, j, rope).reshape(BATCH, TT, 6 * D)
            dmat, qdec, kdec, cdec = _decay_tables(ret_decay[j])
            o_f, o_b = _retention(qkvg, dmat, qdec, kdec, cdec)
            xs, hf, *routing = _mixer_out("ret", (o_f, o_b, qkvg, ret_gn3), ret_w_out, j, xs, mods, norm_ffn3, i, route)
        elif kind == 1:
            ab = _fn_channel(h, cs)
            f_lat = _fn_cols(_fn_rows(ab, dft_f1), dft_f2)
            xs, hf, *routing = _mixer_out(
                "split", (f_lat, _fn_ctx(ab, dft_ctx)), fn_w_out, j, xs, mods, norm_ffn3, i, route)
        else:
            qkv = _project(h.reshape(NTOK, D), na_w_qkv, j).reshape(BATCH, TT, 3 * D)
            o = _na_attention(qkv, _na_bias_table(na_rpb[j]))
            xs, hf, *routing = _mixer_out("plain", (o,), na_w_out, j, xs, mods, norm_ffn3, i, route)
        final = i == DEPTH - 1
        if final:
            g_next = (norm_final[None, :], pl.BlockSpec((1, D), lambda b, t: (0, 0)))
        else:
            g_next = (norm_mix3, _layer_spec((1, D), i + 1))
        res = _moe(hf, routing, xs, mods, i, g_next, moe_w_gate, moe_w_up, moe_w_down, final)
        if final:
            out = res
        else:
            xs, h = res
    return out
```

```python
import functools

import numpy as np
import jax
import jax.numpy as jnp
from jax import lax
from jax.experimental import pallas as pl
from jax.experimental.pallas import tpu as pltpu

F32 = jnp.float32
BF16 = jnp.bfloat16
HIGHEST = lax.Precision.HIGHEST

D = 1024
BATCH = 4
SEQ = 4096
CTX = 256
TT = SEQ + CTX
NTOK = BATCH * TT
DEPTH = 4
GRID_W = 64
EPS = 1e-6
NEG_INF = -1e30
N_MOD = 6

RET_HEADS = 4
RET_DK = D // RET_HEADS
RET_DV = 2 * RET_DK
RET_CHUNK = 256
ROPE_BASE = 10000.0

FN_GROUPS = 4
FN_GC = D // FN_GROUPS

NA_HEADS = 16
NA_HD = D // NA_HEADS
NA_KH = 8
NA_KW = 16

N_EXPERTS = 16
N_GROUPS = 4
EPG = N_EXPERTS // N_GROUPS

ROW_TILE = 256
TPB = TT // ROW_TILE
LAT_TILES = SEQ // ROW_TILE
MM_TM = TT // 4
MOE_BM = 256
PAIRS = EPG * (EPG - 1) // 2
N_CLS = N_GROUPS * PAIRS
CLS_PAD = 32
MOE_NB = (NTOK + N_CLS * (MOE_BM - 1) + MOE_BM - 1) // MOE_BM
MOE_P = MOE_NB * MOE_BM
MOE_TAIL = MOE_NB - NTOK // MOE_BM
MOE_NZ = N_CLS + MOE_TAIL
DISPATCH_TILE = 512
DX = D + 128
NA_QT = 2 * GRID_W
VMEM_LIMIT = 56 * 1024 * 1024


def _cp(n_axes, vmem=VMEM_LIMIT):
    return pltpu.CompilerParams(dimension_semantics=("arbitrary",) * n_axes, vmem_limit_bytes=vmem)


def _silu(x):
    return x * jax.nn.sigmoid(x)


def _normmod(x, g, shift, scale):
    y = x * lax.rsqrt(jnp.mean(x * x, axis=-1, keepdims=True) + EPS)
    return (y * g) * (1.0 + scale) + shift


def _mod_idx(t):
    return jnp.where(t < LAT_TILES, 1, 0)


def _layer_spec(shape, layer):
    return pl.BlockSpec((None,) + shape, lambda *_: (layer,) + (0,) * len(shape))


def _mod_spec(depth):
    return pl.BlockSpec((None, None, None, 8, D), lambda b, t: (depth, b, _mod_idx(t), 0, 0))


def _mod_kernel(c_ref, w_ref, b_ref, o_ref):
    a = _silu(c_ref[...])
    o_ref[...] = jnp.dot(a.astype(BF16), w_ref[...].astype(BF16), preferred_element_type=F32) + b_ref[...]


def _modulation(cvec, w_mod, b_mod):
    return pl.pallas_call(
        _mod_kernel,
        grid=(DEPTH, N_MOD),
        in_specs=[
            pl.BlockSpec((8, D), lambda i, j: (0, 0)),
            pl.BlockSpec((None, D, D), lambda i, j: (i, 0, j)),
            pl.BlockSpec((None, 1, D), lambda i, j: (i, 0, j)),
        ],
        out_specs=pl.BlockSpec((None, 8, D), lambda i, j: (i, 0, j)),
        out_shape=jax.ShapeDtypeStruct((DEPTH, 8, N_MOD * D), F32),
        compiler_params=_cp(2),
        name="modulation",
    )(cvec, w_mod, b_mod.reshape(DEPTH, 1, N_MOD * D))


def _embed_kernel(x_ref, ctx_ref, mod_ref, g_ref, xs_ref, h_ref):
    t = pl.program_id(1)

    def emit(src):
        xs_ref[...] = src
        h_ref[...] = _normmod(src, g_ref[...], mod_ref[0:1, :], mod_ref[1:2, :]).astype(BF16)

    @pl.when(t < LAT_TILES)
    def _():
        emit(x_ref[...])

    @pl.when(t >= LAT_TILES)
    def _():
        emit(ctx_ref[...])


def _embed(x, ctx, mods, g):
    return pl.pallas_call(
        _embed_kernel,
        grid=(BATCH, TPB),
        in_specs=[
            pl.BlockSpec((None, ROW_TILE, D), lambda b, t: (b, jnp.minimum(t, LAT_TILES - 1), 0)),
            pl.BlockSpec((None, CTX, D), lambda b, t: (b, 0, 0)),
            _mod_spec(0),
            _layer_spec((1, D), 0),
        ],
        out_specs=[
            pl.BlockSpec((None, ROW_TILE, D), lambda b, t: (b, t, 0)),
            pl.BlockSpec((None, ROW_TILE, D), lambda b, t: (b, t, 0)),
        ],
        out_shape=[
            jax.ShapeDtypeStruct((BATCH, TT, D), F32),
            jax.ShapeDtypeStruct((BATCH, TT, D), BF16),
        ],
        compiler_params=_cp(2),
        name="embed",
    )(x, ctx, mods, g)


def _rope_store(acc, cos_ref, sin_ref, o_ref):
    for c in range(acc.shape[1] // 128):
        xc = acc[:, c * 128:(c + 1) * 128]
        tsl = slice((c % 2) * 128, (c % 2 + 1) * 128)
        rot = xc * cos_ref[:, tsl] + pltpu.roll(xc, 64, 1) * sin_ref[:, tsl]
        o_ref[:, c * 128:(c + 1) * 128] = rot.astype(o_ref.dtype)


def _proj_kernel(a_ref, w_ref, *rest, n_rope):
    if n_rope:
        cos_ref, sin_ref, o_ref, wb_ref = rest
    else:
        o_ref, wb_ref = rest
    j = pl.program_id(0)

    @pl.when(pl.program_id(1) == 0)
    def _():
        wb_ref[...] = w_ref[...].astype(BF16)

    acc = jnp.dot(a_ref[...], wb_ref[...], preferred_element_type=F32)
    if n_rope:
        @pl.when(j < n_rope)
        def _():
            _rope_store(acc, cos_ref, sin_ref, o_ref)

        @pl.when(j >= n_rope)
        def _():
            o_ref[...] = acc.astype(o_ref.dtype)
    else:
        o_ref[...] = acc.astype(o_ref.dtype)


def _project(a, w, layer, rope=None):
    _, k, n = w.shape
    tn = D
    n_rope = 0 if rope is None else rope[0].shape[0]
    in_specs = [
        pl.BlockSpec((MM_TM, k), lambda j, i: (i, 0)),
        pl.BlockSpec((None, k, tn), lambda j, i: (layer, 0, j)),
    ]
    args = [a, w]
    if n_rope:
        tspec = pl.BlockSpec((None, MM_TM, 256), lambda j, i: (jnp.minimum(j, n_rope - 1), i % (TT // MM_TM), 0))
        in_specs += [tspec, tspec]
        args += list(rope)
    return pl.pallas_call(
        functools.partial(_proj_kernel, n_rope=n_rope),
        grid=(n // tn, NTOK // MM_TM),
        in_specs=in_specs,
        out_specs=pl.BlockSpec((MM_TM, tn), lambda j, i: (i, j)),
        out_shape=jax.ShapeDtypeStruct((NTOK, n), BF16),
        scratch_shapes=[pltpu.VMEM((k, tn), BF16)],
        compiler_params=_cp(2),
        name="project",
    )(*args)


N_CHUNKS = TT // RET_CHUNK
LAT_CHUNKS = SEQ // RET_CHUNK
CTX_CHUNKS = CTX // RET_CHUNK


def _fwd_chunk(n):
    return jnp.where(n < CTX_CHUNKS, LAT_CHUNKS + n, n - CTX_CHUNKS)


def _bwd_chunk(n):
    return N_CHUNKS - 1 - n


def _ret_kernel(cdec_ref, qf, kf, vf, qb, kb, vb, dmat_ref, qdec_ref, kdec_ref, of_ref, ob_ref, s_ref):
    @pl.when(pl.program_id(1) == 0)
    def _():
        s_ref[...] = jnp.zeros_like(s_ref)

    for d, (q_r, k_r, v_r, o_r) in enumerate(((qf, kf, vf, of_ref), (qb, kb, vb, ob_ref))):
        for h in range(RET_HEADS):
            q = q_r[:, h * RET_DK:(h + 1) * RET_DK]
            k = k_r[:, h * RET_DK:(h + 1) * RET_DK]
            v = v_r[:, h * RET_DV:(h + 1) * RET_DV]
            att = lax.dot_general(q, k, (((1,), (1,)), ((), ())), preferred_element_type=F32) * dmat_ref[d, h]
            s = s_ref[d, h]
            qd = (q.astype(F32) * qdec_ref[d, h]).astype(BF16)
            o = (jnp.dot(att.astype(BF16), v, preferred_element_type=F32)
                 + jnp.dot(qd, s.astype(BF16), preferred_element_type=F32))
            kd = (k.astype(F32) * kdec_ref[d, h]).astype(BF16)
            s_ref[d, h] = s * cdec_ref[d * RET_HEADS + h] + lax.dot_general(
                kd, v, (((0,), (0,)), ((), ())), preferred_element_type=F32)
            o_r[:, h * RET_DV:(h + 1) * RET_DV] = o.astype(o_r.dtype)


def _retention(qkvg, dmat, qdec, kdec, cdec):
    C = RET_CHUNK

    def spec(width, col, chunk_fn):
        return pl.BlockSpec((None, C, width), lambda b, n, cd: (b, chunk_fn(n), col))

    ospec_f = pl.BlockSpec((None, C, 2 * D), lambda b, n, cd: (b, _fwd_chunk(n), 0))
    ospec_b = pl.BlockSpec((None, C, 2 * D), lambda b, n, cd: (b, _bwd_chunk(n), 0))
    full = lambda shape: pl.BlockSpec(shape, lambda b, n, cd: (0,) * len(shape))
    grid_spec = pltpu.PrefetchScalarGridSpec(
        num_scalar_prefetch=1,
        grid=(BATCH, N_CHUNKS),
        in_specs=[
            spec(D, 0, _fwd_chunk), spec(D, 1, _fwd_chunk), spec(2 * D, 1, _fwd_chunk),
            spec(D, 0, _bwd_chunk), spec(D, 1, _bwd_chunk), spec(2 * D, 1, _bwd_chunk),
            full((2, RET_HEADS, C, C)), full((2, RET_HEADS, C, 1)), full((2, RET_HEADS, C, 1)),
        ],
        out_specs=[ospec_f, ospec_b],
        scratch_shapes=[pltpu.VMEM((2, RET_HEADS, RET_DK, RET_DV), F32)],
    )
    return pl.pallas_call(
        _ret_kernel,
        grid_spec=grid_spec,
        out_shape=[jax.ShapeDtypeStruct((BATCH, TT, 2 * D), BF16)] * 2,
        compiler_params=_cp(2),
        name="retention",
    )(cdec, qkvg, qkvg, qkvg, qkvg, qkvg, qkvg, dmat, qdec, kdec)


def _mixer_out_kernel(*refs, kind):
    n_src = {"ret": 4, "split": 2, "plain": 1}[kind]
    srcs, (w_ref, x_ref, mod_ref, nf_ref), route_in = refs[:n_src], refs[n_src:n_src + 4], refs[n_src + 4:n_src + 7]
    xo_ref, h_ref, cr_ref, cnt_ref, wb_ref = refs[n_src + 7:]
    first = (pl.program_id(0) == 0) & (pl.program_id(1) == 0)

    @pl.when(first)
    def _():
        wb_ref[...] = w_ref[...].astype(BF16)

    if kind == "ret":
        of_ref, ob_ref, g_ref, gn_ref = srcs
        o = of_ref[...].astype(F32) + ob_ref[...].astype(F32)
        parts = []
        for h in range(RET_HEADS):
            oh = o[:, h * RET_DV:(h + 1) * RET_DV]
            mu = jnp.mean(oh, axis=-1, keepdims=True)
            ctr = oh - mu
            var = jnp.mean(ctr * ctr, axis=-1, keepdims=True)
            parts.append(ctr * lax.rsqrt(var + EPS))
        on = jnp.concatenate(parts, axis=-1) * gn_ref[...]
        a = (on * _silu(g_ref[...].astype(F32))).astype(BF16)
    elif kind == "split":
        a = jnp.where(pl.program_id(1) < LAT_TILES, srcs[0][...], srcs[1][...])
    else:
        a = srcs[0][...]
    y = jnp.dot(a, wb_ref[...], preferred_element_type=F32)
    xn = x_ref[...] + mod_ref[2:3, :] * y
    xo_ref[...] = xn
    h = _normmod(xn, nf_ref[...], mod_ref[3:4, :], mod_ref[4:5, :])
    wa, wb = _route(h, first, *route_in, cr_ref, cnt_ref)
    sub = lax.broadcasted_iota(jnp.int32, (DX - D, ROW_TILE), 0)
    h_ref[:, :D] = h
    h_ref[:, D:] = jnp.where(sub == 0, wa, jnp.where(sub == 1, wb, 0.0)).T


def _mixer_out(kind, srcs, w, layer, xs, mods, norm_ffn, depth, route):
    k = w.shape[1]
    row = lambda width, col=0: pl.BlockSpec((None, ROW_TILE, width), lambda b, t: (b, t, col))
    const = lambda shape: pl.BlockSpec(shape, lambda b, t: (0,) * len(shape))
    per_tok = pl.BlockSpec((2, ROW_TILE), lambda b, t: (0, b * TPB + t))
    tri = jnp.asarray(np.triu(np.ones((ROW_TILE, ROW_TILE), np.float32), 1), BF16)
    if kind == "ret":
        src_specs = [row(2 * D), row(2 * D), row(2 * D, 2), _layer_spec((1, 2 * D), layer)]
    elif kind == "split":
        src_specs = [pl.BlockSpec((None, ROW_TILE, k), lambda b, t: (b, jnp.minimum(t, LAT_TILES - 1), 0)),
                     pl.BlockSpec((None, CTX, k), lambda b, t: (b, 0, 0))]
    else:
        src_specs = [row(k)]
    return pl.pallas_call(
        functools.partial(_mixer_out_kernel, kind=kind),
        grid=(BATCH, TPB),
        in_specs=src_specs + [_layer_spec((k, D), layer), row(D), _mod_spec(depth), _layer_spec((1, D), depth),
                              const((N_EXPERTS, D)), const((N_EXPERTS, 1)), const((ROW_TILE, ROW_TILE))],
        out_specs=[row(D), row(DX), per_tok, const((CLS_PAD, 128))],
        out_shape=[jax.ShapeDtypeStruct((BATCH, TT, D), F32), jax.ShapeDtypeStruct((BATCH, TT, DX), F32),
                   jax.ShapeDtypeStruct((2, NTOK), jnp.int32), jax.ShapeDtypeStruct((CLS_PAD, 128), F32)],
        scratch_shapes=[pltpu.VMEM((k, D), BF16)],
        compiler_params=_cp(2),
        name="mixer_out_" + kind,
    )(*srcs, w, xs, mods, norm_ffn, *route, tri)


def _fn_channel_kernel(h_ref, cs_ref, o_ref):
    for g in range(FN_GROUPS):
        r = jnp.dot(h_ref[:, g * FN_GC:(g + 1) * FN_GC], cs_ref[...], preferred_element_type=F32)
        o_ref[0, :, g * FN_GC:(g + 1) * FN_GC] = r[:, :FN_GC].astype(BF16)
        o_ref[1, :, g * FN_GC:(g + 1) * FN_GC] = r[:, FN_GC:].astype(BF16)


def _fn_channel(h, cs):
    return pl.pallas_call(
        _fn_channel_kernel,
        grid=(BATCH, TPB),
        in_specs=[
            pl.BlockSpec((None, ROW_TILE, D), lambda b, t: (b, t, 0)),
            pl.BlockSpec((FN_GC, 2 * FN_GC), lambda b, t: (0, 0)),
        ],
        out_specs=pl.BlockSpec((None, 2, ROW_TILE, D), lambda b, t: (b, 0, t, 0)),
        out_shape=jax.ShapeDtypeStruct((BATCH, 2, TT, D), BF16),
        compiler_params=_cp(2),
        name="fourier_channels",
    )(h, cs)


FN_R = 64
FN_CB = 16


def _fn_rows_kernel(ab_ref, f1_ref, y_ref, u32):
    u32[...] = ab_ref[...].astype(F32)
    for j in range(FN_CB):
        st = jnp.concatenate([u32[0, :, j, :], u32[1, :, j, :]], axis=0).astype(BF16)
        y = jnp.dot(f1_ref[j], st, preferred_element_type=F32)
        y_ref[0, j] = y[:FN_R].astype(BF16)
        y_ref[1, j] = y[FN_R:].astype(BF16)


def _fn_rows(ab, f1):
    ab5 = ab.reshape(BATCH, 2, TT // FN_R, FN_R, D)
    return pl.pallas_call(
        _fn_rows_kernel,
        grid=(BATCH, FN_R // FN_CB),
        in_specs=[
            pl.BlockSpec((None, 2, FN_R, FN_CB, D), lambda b, cb: (b, 0, 0, cb, 0)),
            pl.BlockSpec((FN_CB, 2 * FN_R, 2 * FN_R), lambda b, cb: (cb, 0, 0)),
        ],
        out_specs=pl.BlockSpec((None, 2, FN_CB, FN_R, D), lambda b, cb: (b, 0, cb, 0, 0)),
        out_shape=jax.ShapeDtypeStruct((BATCH, 2, FN_R, FN_R, D), BF16),
        scratch_shapes=[pltpu.VMEM((2, FN_R, FN_CB, D), F32)],
        compiler_params=_cp(2),
        name="fourier_rows",
    )(ab5, f1)


def _fn_cols_kernel(y_ref, f2_ref, o_ref, y32, z32):
    y32[...] = y_ref[...].astype(F32)
    for j in range(FN_CB):
        st = jnp.concatenate([y32[0, :, j, :], y32[1, :, j, :]], axis=0).astype(BF16)
        z32[:, j, :] = jnp.dot(f2_ref[...], st, preferred_element_type=F32)
    o_ref[...] = z32[...].astype(BF16)


def _fn_cols(y, f2):
    out = pl.pallas_call(
        _fn_cols_kernel,
        grid=(BATCH, FN_R // FN_CB),
        in_specs=[
            pl.BlockSpec((None, 2, FN_R, FN_CB, D), lambda b, kb: (b, 0, 0, kb, 0)),
            pl.BlockSpec((FN_R, 2 * FN_R), lambda b, kb: (0, 0)),
        ],
        out_specs=pl.BlockSpec((None, FN_R, FN_CB, D), lambda b, kb: (b, 0, kb, 0)),
        out_shape=jax.ShapeDtypeStruct((BATCH, FN_R, FN_R, D), BF16),
        scratch_shapes=[pltpu.VMEM((2, FN_R, FN_CB, D), F32), pltpu.VMEM((FN_R, FN_CB, D), F32)],
        compiler_params=_cp(2),
        name="fourier_cols",
    )(y, f2)
    return out.reshape(BATCH, SEQ, D)


def _fn_ctx_kernel(ab_ref, m_ref, o_ref):
    st = jnp.concatenate([ab_ref[0], ab_ref[1]], axis=0)
    o_ref[...] = jnp.dot(m_ref[...], st, preferred_element_type=F32).astype(BF16)


def _fn_ctx(ab, m_ctx):
    return pl.pallas_call(
        _fn_ctx_kernel,
        grid=(BATCH,),
        in_specs=[
            pl.BlockSpec((None, 2, CTX, D), lambda b: (b, 0, SEQ // CTX, 0)),
            pl.BlockSpec((CTX, 2 * CTX), lambda b: (0, 0)),
        ],
        out_specs=pl.BlockSpec((None, CTX, D), lambda b: (b, 0, 0)),
        out_shape=jax.ShapeDtypeStruct((BATCH, CTX, D), BF16),
        compiler_params=_cp(1),
        name="fourier_ctx",
    )(ab, m_ctx)


def _dft_tables():
    def cs_pair(n, scale):
        i = np.arange(n)
        ang = 2.0 * np.pi * ((i[:, None] * i[None, :]) % n) / n
        return np.cos(ang) * scale, np.sin(ang) * scale

    assert FN_GC == CTX
    c, s = cs_pair(FN_GC, FN_GC ** -0.5)
    cs = np.concatenate([c, s], axis=1)
    m_ctx = np.concatenate([c, -s], axis=1)
    c, s = cs_pair(FN_R, FN_R ** -0.5)
    f1 = np.block([[c, -s], [s, c]])
    f2 = np.concatenate([c, -s], axis=1)
    i = np.arange(FN_R)
    phi = 2.0 * np.pi * (i[:, None] * i[None, :]) / SEQ
    f1_tw = np.stack([np.block([[np.diag(np.cos(p)), -np.diag(np.sin(p))],
                                [np.diag(np.sin(p)), np.diag(np.cos(p))]]) @ f1 for p in phi])
    return jnp.asarray(cs, BF16), jnp.asarray(m_ctx, BF16), jnp.asarray(f1_tw, BF16), jnp.asarray(f2, BF16)


NA_ROWS = SEQ // GRID_W
NA_STEPS = TT // NA_QT
NA_NK = NA_KH * GRID_W
NA_DR = 2 * NA_KH - 1


def _na_kernel(q_ref, k_ref, v_ref, bias_ref, o_ref):
    step = pl.program_id(1)
    lane = lax.broadcasted_iota(jnp.int32, (GRID_W, 128), 1)
    nt = (((1,), (1,)), ((), ()))
    scale = NA_HD ** -0.5

    def attend(windows):
        chains = [(slice(i * GRID_W, (i + 1) * GRID_W), win, p, slice(p * 128, (p + 1) * 128))
                  for i, win in enumerate(windows) for p in range(NA_HEADS // 2)]
        scores = []
        for rows, win, p, sl in chains:
            qp = q_ref[rows, sl] * scale
            zero = jnp.zeros_like(qp)
            q2 = jnp.concatenate([jnp.where(lane < NA_HD, qp, zero), jnp.where(lane >= NA_HD, qp, zero)], axis=0)
            s_c = lax.dot_general(q2, k_ref[SEQ:TT, sl], nt, preferred_element_type=F32)
            s_w = None
            if win is not None:
                start, d0 = win
                bias = jnp.concatenate(
                    [jnp.concatenate([bias_ref[2 * p + hh, d0 + 2 * j2] for j2 in range(NA_KH // 2)], axis=1)
                     for hh in range(2)], axis=0)
                s_w = lax.dot_general(q2, k_ref[pl.ds(start, NA_NK), sl], nt, preferred_element_type=F32) + bias
            scores.append((s_c, s_w))
        probs = []
        for s_c, s_w in scores:
            m = jnp.max(s_c, axis=-1, keepdims=True)
            if s_w is not None:
                m = jnp.maximum(m, jnp.max(s_w, axis=-1, keepdims=True))
            p_c = jnp.exp(s_c - m)
            l = jnp.sum(p_c, axis=-1, keepdims=True)
            p_w = None
            if s_w is not None:
                p_w = jnp.exp(s_w - m)
                l = l + jnp.sum(p_w, axis=-1, keepdims=True)
                p_w = p_w.astype(BF16)
            probs.append((p_c.astype(BF16), p_w, l))
        for (rows, win, p, sl), (p_c, p_w, l) in zip(chains, probs):
            acc = jnp.dot(p_c, v_ref[SEQ:TT, sl], preferred_element_type=F32)
            if p_w is not None:
                acc = acc + jnp.dot(p_w, v_ref[pl.ds(win[0], NA_NK), sl], preferred_element_type=F32)
            o2 = acc / l
            o_ref[rows, sl] = jnp.where(lane < NA_HD, o2[:GRID_W], o2[GRID_W:]).astype(BF16)

    n_blocks = NA_QT // GRID_W

    @pl.when(step < NA_ROWS // n_blocks)
    def _():
        windows = []
        for i in range(n_blocks):
            r = step * n_blocks + i
            rs = jnp.clip(r - NA_KH // 2, 0, NA_ROWS - NA_KH)
            windows.append((pl.multiple_of(rs * GRID_W, GRID_W), rs - r + (NA_KH - 1)))
        attend(windows)

    @pl.when(step >= NA_ROWS // n_blocks)
    def _():
        attend([None] * n_blocks)


def _na_attention(qkv, bias):
    return pl.pallas_call(
        _na_kernel,
        grid=(BATCH, NA_STEPS),
        in_specs=[
            pl.BlockSpec((None, NA_QT, D), lambda b, s: (b, s, 0)),
            pl.BlockSpec((None, TT, D), lambda b, s: (b, 0, 1), pipeline_mode=pl.Buffered(1)),
            pl.BlockSpec((None, TT, D), lambda b, s: (b, 0, 2), pipeline_mode=pl.Buffered(1)),
            pl.BlockSpec((NA_HEADS, NA_DR - 1, GRID_W, 2 * GRID_W), lambda b, s: (0, 0, 0, 0),
                         pipeline_mode=pl.Buffered(1)),
        ],
        out_specs=pl.BlockSpec((None, NA_QT, D), lambda b, s: (b, s, 0)),
        out_shape=jax.ShapeDtypeStruct((BATCH, TT, D), BF16),
        compiler_params=_cp(2),
        name="na_attention",
    )(qkv, qkv, qkv, bias)


def _na_bias_table(rpb):
    qc = np.arange(GRID_W)
    kcol = np.arange(GRID_W)
    cstart = np.clip(qc - NA_KW // 2, 0, GRID_W - NA_KW)
    valid = (kcol[None, :] >= cstart[:, None]) & (kcol[None, :] < cstart[:, None] + NA_KW)
    dc = kcol[None, :] - qc[:, None] + (NA_KW - 1)
    pick = ((dc[None] == np.arange(2 * NA_KW - 1)[:, None, None]) & valid[None]).astype(np.float32)
    tab = jnp.einsum("hdj,jqk->hdqk", rpb.astype(F32), jnp.asarray(pick), precision=HIGHEST)
    tab = tab + jnp.asarray(np.where(valid, 0.0, NEG_INF).astype(np.float32))
    return jnp.concatenate([tab[:, :NA_DR - 1], tab[:, 1:]], axis=-1)


def _route(h, first, rw_ref, rb_ref, tri_ref, cr_ref, cnt_ref):
    logits = lax.dot_general(rw_ref[...], h.astype(BF16), (((1,), (1,)), ((), ())),
                             preferred_element_type=F32)
    sc = jax.nn.sigmoid(logits)
    sel = sc + rb_ref[...]
    sc_r = [sc[e:e + 1, :] for e in range(N_EXPERTS)]
    sel_r = [sel[e:e + 1, :] for e in range(N_EXPERTS)]
    zero_i = jnp.zeros_like(sel_r[0]).astype(jnp.int32)

    best = gi = None
    picks = []
    for g in range(N_GROUPS):
        a = sel_r[g * EPG:(g + 1) * EPG]
        u = sc_r[g * EPG:(g + 1) * EPG]
        gs = None
        for i in range(EPG):
            for j in range(i + 1, EPG):
                pair = a[i] + a[j]
                gs = pair if gs is None else jnp.maximum(gs, pair)
        m1, i1, s1 = a[0], zero_i, u[0]
        for j in range(1, EPG):
            upd = a[j] > m1
            m1 = jnp.where(upd, a[j], m1)
            i1 = jnp.where(upd, j, i1)
            s1 = jnp.where(upd, u[j], s1)
        m2 = i2 = s2 = None
        for j in range(EPG):
            cand = jnp.where(i1 == j, -jnp.inf, a[j])
            if m2 is None:
                m2, i2, s2 = cand, zero_i, u[0]
            else:
                upd = cand > m2
                m2 = jnp.where(upd, cand, m2)
                i2 = jnp.where(upd, j, i2)
                s2 = jnp.where(upd, u[j], s2)
        picks.append((i1 + g * EPG, i2 + g * EPG, s1, s2))
        if best is None:
            best, gi = gs, zero_i
        else:
            upd = gs > best
            best = jnp.where(upd, gs, best)
            gi = jnp.where(upd, g, gi)

    e1, e2, s1, s2 = picks[0]
    for g in range(1, N_GROUPS):
        on = gi == g
        e1 = jnp.where(on, picks[g][0], e1)
        e2 = jnp.where(on, picks[g][1], e2)
        s1 = jnp.where(on, picks[g][2], s1)
        s2 = jnp.where(on, picks[g][3], s2)
    tot = s1 + s2
    w1, w2 = s1 / tot, s2 / tot

    swap = e2 < e1
    la = jnp.where(swap, e2, e1) - gi * EPG
    lb = jnp.where(swap, e1, e2) - gi * EPG
    cls = gi * PAIRS + jnp.where(la == 0, 0, jnp.where(la == 1, 3, 5)) + lb - la - 1
    wa, wb = jnp.where(swap, w2, w1), jnp.where(swap, w1, w2)

    @pl.when(first)
    def _():
        cnt_ref[...] = jnp.zeros_like(cnt_ref)

    oh = (lax.broadcasted_iota(jnp.int32, (cnt_ref.shape[0], cls.shape[1]), 0) == cls).astype(F32)
    before = jnp.dot(oh.astype(BF16), tri_ref[...], preferred_element_type=F32)
    base = cnt_ref[:, 0:1]
    cr_ref[0:1, :] = cls
    cr_ref[1:2, :] = jnp.sum(oh * (base + before), axis=0, keepdims=True).astype(jnp.int32)
    cnt_ref[...] = cnt_ref[...] + jnp.sum(oh, axis=1, keepdims=True)
    return wa, wb


DMA_UNROLL = 16


def _row_copy(src_ref, src_row, dst_ref, dst_row, sem):
    return pltpu.make_async_copy(src_ref.at[pl.ds(src_row, 1)], dst_ref.at[pl.ds(dst_row, 1)], sem)


def _for_each_row(fn, rows=ROW_TILE):
    def group(gidx, carry):
        for u in range(DMA_UNROLL):
            fn(gidx * DMA_UNROLL + u, u % 2)
        return carry

    lax.fori_loop(0, rows // DMA_UNROLL, group, 0)


def _dispatch_kernel(zrow_ref, dest_ref, h_ref, xs_ref, zbuf, sem, zsem):
    @pl.when(pl.program_id(0) == 0)
    def _():
        zbuf[...] = jnp.zeros_like(zbuf)

        def zero_copy(e):
            start = pl.multiple_of(jnp.maximum(zrow_ref[e], 0), MOE_BM)
            return pltpu.make_async_copy(zbuf, xs_ref.at[pl.ds(start, MOE_BM)], zsem)

        for e in range(MOE_NZ):
            pl.when(zrow_ref[e] >= 0)(lambda e=e: zero_copy(e).start())
        for e in range(MOE_NZ):
            pl.when(zrow_ref[e] >= 0)(lambda e=e: zero_copy(e).wait())

    _for_each_row(lambda r, pr: _row_copy(h_ref, r, xs_ref, dest_ref[0, r], sem).start(priority=pr), DISPATCH_TILE)
    _for_each_row(lambda r, pr: _row_copy(h_ref, r, xs_ref, dest_ref[0, r], sem).wait(), DISPATCH_TILE)


def _dispatch(h, dest, zrow):
    grid_spec = pltpu.PrefetchScalarGridSpec(
        num_scalar_prefetch=1,
        grid=(NTOK // DISPATCH_TILE,),
        in_specs=[
            pl.BlockSpec((None, 1, DISPATCH_TILE), lambda i, z: (i, 0, 0), memory_space=pltpu.SMEM),
            pl.BlockSpec((DISPATCH_TILE, DX), lambda i, z: (i, 0)),
        ],
        out_specs=pl.BlockSpec(memory_space=pl.ANY),
        scratch_shapes=[pltpu.VMEM((MOE_BM, DX), F32), pltpu.SemaphoreType.DMA(()), pltpu.SemaphoreType.DMA(())],
    )
    return pl.pallas_call(
        _dispatch_kernel,
        grid_spec=grid_spec,
        out_shape=jax.ShapeDtypeStruct((MOE_P, DX), F32),
        compiler_params=pltpu.CompilerParams(dimension_semantics=("arbitrary",), vmem_limit_bytes=VMEM_LIMIT,
                                             has_side_effects=True),
        name="moe_dispatch",
    )(zrow, dest.reshape(NTOK // DISPATCH_TILE, 1, DISPATCH_TILE), h)


def _experts_kernel(blk_ref, ex_ref, nu_ref, x_ref, *rest, which):
    if which:
        prev_ref, wg_ref, wu_ref, wd_ref, y_ref, wgb, wub, wdb = rest
    else:
        wg_ref, wu_ref, wd_ref, y_ref, wgb, wub, wdb = rest
    del blk_ref
    v = pl.program_id(0)
    active = v < nu_ref[0]
    changed = (v == 0) | (ex_ref[v] != ex_ref[jnp.maximum(v - 1, 0)])

    @pl.when(active & changed)
    def _():
        wgb[...] = wg_ref[...].astype(BF16)
        wub[...] = wu_ref[...].astype(BF16)
        wdb[...] = wd_ref[...].astype(BF16)

    @pl.when(active)
    def _():
        x = x_ref[:, :D].astype(BF16)
        g = jnp.dot(x, wgb[...], preferred_element_type=F32)
        u = jnp.dot(x, wub[...], preferred_element_type=F32)
        hh = (_silu(g) * u).astype(BF16)
        y = jnp.dot(hh, wdb[...], preferred_element_type=F32) * x_ref[:, D + which:D + which + 1]
        y_ref[...] = prev_ref[...] + y if which else y

    @pl.when(jnp.logical_not(active))
    def _():
        y_ref[...] = jnp.zeros_like(y_ref)


def _experts(xs, prev, which, blocks, experts, n_used, w_gate, w_up, w_down, depth):
    wspec = pl.BlockSpec((None, None, D, D), lambda v, bl, ex, nu: (depth, ex[v], 0, 0))
    yspec = pl.BlockSpec((MOE_BM, D), lambda v, bl, ex, nu: (bl[v], 0))
    grid_spec = pltpu.PrefetchScalarGridSpec(
        num_scalar_prefetch=3,
        grid=(MOE_NB,),
        in_specs=[pl.BlockSpec((MOE_BM, DX), lambda v, bl, ex, nu: (bl[v], 0))] + ([yspec] if which else [])
        + [wspec, wspec, wspec],
        out_specs=yspec,
        scratch_shapes=[pltpu.VMEM((D, D), BF16)] * 3,
    )
    return pl.pallas_call(
        functools.partial(_experts_kernel, which=which),
        grid_spec=grid_spec,
        out_shape=jax.ShapeDtypeStruct((MOE_P, D), F32),
        compiler_params=_cp(1),
        name="moe_experts_%d" % which,
    )(blocks, experts, n_used, xs, *([prev] if which else []), w_gate, w_up, w_down)


def _combine_kernel(dest_ref, next_ref, yb_ref, x_ref, mod_ref, modn_ref, gn_ref, *rest, final):
    if final:
        out_ref, ybuf, sem = rest
    else:
        xo_ref, h_ref, ybuf, sem = rest
    t = pl.program_id(1)
    step = pl.program_id(0) * TPB + t
    slot = lax.rem(step, 2)

    def gather(slots_ref, buf):
        return lambda r: _row_copy(yb_ref, slots_ref[0, r], ybuf.at[buf], r, sem.at[buf])

    @pl.when(step == 0)
    def _():
        _for_each_row(lambda r, pr: gather(dest_ref, slot)(r).start(priority=pr))

    @pl.when(step + 1 < BATCH * TPB)
    def _():
        _for_each_row(lambda r, pr: gather(next_ref, 1 - slot)(r).start(priority=pr))

    _for_each_row(lambda r, pr: gather(dest_ref, slot)(r).wait())

    def body():
        xn = x_ref[...] + mod_ref[5:6, :] * ybuf[slot]
        if final:
            out_ref[...] = _normmod(xn, gn_ref[...], 0.0, 0.0)
        else:
            xo_ref[...] = xn
            h_ref[...] = _normmod(xn, gn_ref[...], modn_ref[0:1, :], modn_ref[1:2, :]).astype(BF16)

    if final:
        pl.when(t < LAT_TILES)(body)
    else:
        body()


def _combine(yb, dest, xs, mods, depth, g_next, final):
    row = pl.BlockSpec((None, ROW_TILE, D), lambda b, t: (b, t, 0))
    next_depth = depth if final else depth + 1
    if final:
        out_specs = pl.BlockSpec((None, ROW_TILE, D), lambda b, t: (b, jnp.minimum(t, LAT_TILES - 1), 0))
        out_shape = jax.ShapeDtypeStruct((BATCH, SEQ, D), F32)
    else:
        out_specs = [row, row]
        out_shape = [jax.ShapeDtypeStruct((BATCH, TT, D), F32), jax.ShapeDtypeStruct((BATCH, TT, D), BF16)]
    last = BATCH * TPB - 1
    return pl.pallas_call(
        functools.partial(_combine_kernel, final=final),
        grid=(BATCH, TPB),
        in_specs=[
            pl.BlockSpec((None, 1, ROW_TILE), lambda b, t: (b * TPB + t, 0, 0), memory_space=pltpu.SMEM),
            pl.BlockSpec((None, 1, ROW_TILE), lambda b, t: (jnp.minimum(b * TPB + t + 1, last), 0, 0),
                         memory_space=pltpu.SMEM),
            pl.BlockSpec(memory_space=pl.ANY),
            row, _mod_spec(depth), _mod_spec(next_depth),
            g_next[1],
        ],
        out_specs=out_specs,
        out_shape=out_shape,
        scratch_shapes=[pltpu.VMEM((2, ROW_TILE, D), F32), pltpu.SemaphoreType.DMA((2,))],
        compiler_params=_cp(2),
        name="moe_combine",
    )(dest, dest, yb, xs, mods, mods, g_next[0])


def _class_experts():
    pairs = [(a, b) for a in range(EPG) for b in range(a + 1, EPG)]
    lo = [g * EPG + a for g in range(N_GROUPS) for a, _ in pairs]
    hi = [g * EPG + b for g in range(N_GROUPS) for _, b in pairs]
    return jnp.asarray(lo, jnp.int32), jnp.asarray(hi, jnp.int32)


def _moe_plan(cls, rank, counts):
    padded = (counts + MOE_BM - 1) // MOE_BM * MOE_BM
    pend = jnp.cumsum(padded)
    pstart = pend - padded
    cid = jnp.arange(N_CLS, dtype=jnp.int32)
    dest = rank + jnp.sum(jnp.where(cls[None, :] == cid[:, None], pstart[:, None], 0), axis=0)
    dest = dest.astype(jnp.int32).reshape(NTOK // ROW_TILE, 1, ROW_TILE)
    blk = jnp.arange(MOE_NB, dtype=jnp.int32)
    block_c = jnp.minimum(jnp.sum(pend[None, :] <= (blk * MOE_BM)[:, None], axis=1), N_CLS - 1)
    n_used = (pend[-1] // MOE_BM).astype(jnp.int32)
    used = blk < n_used
    lo, hi = _class_experts()
    onehot_c = block_c[:, None] == cid[None, :]
    e_lo = jnp.sum(jnp.where(onehot_c, lo[None, :], 0), axis=1)
    e_hi = jnp.sum(jnp.where(onehot_c, hi[None, :], 0), axis=1)

    def hold_last(e):
        return jnp.where(used, e, jnp.sum(jnp.where(blk == n_used - 1, e, 0))).astype(jnp.int32)

    key = jnp.where(used, e_hi, N_EXPERTS)
    earlier = (key[None, :] < key[:, None]) | ((key[None, :] == key[:, None]) & (blk[None, :] < blk[:, None]))
    pos = jnp.sum(earlier, axis=1)
    at = pos[None, :] == blk[:, None]
    order1 = jnp.sum(jnp.where(at, blk[None, :], 0), axis=1).astype(jnp.int32)
    e1 = jnp.sum(jnp.where(at, e_hi[None, :], 0), axis=1)
    tail = blk[MOE_NB - MOE_TAIL:] * MOE_BM
    zrow = jnp.concatenate([jnp.where(padded > 0, pend - MOE_BM, -1), jnp.where(tail >= pend[-1], tail, -1)])
    return dest, (blk, hold_last(e_lo)), (order1, hold_last(e1)), n_used[None], zrow.astype(jnp.int32)


def _moe(h, routing, xs, mods, depth, g_next, w_gate, w_up, w_down, final):
    cr, cnt = routing
    dest, visit0, visit1, n_used, zrow = _moe_plan(cr[0], cr[1], cnt[:N_CLS, 0].astype(jnp.int32))
    xsorted = _dispatch(h.reshape(NTOK, DX), dest, zrow)
    y0 = _experts(xsorted, None, 0, *visit0, n_used, w_gate, w_up, w_down, depth)
    yb = _experts(xsorted, y0, 1, *visit1, n_used, w_gate, w_up, w_down, depth)
    return _combine(yb, dest, xs, mods, depth, g_next, final)


def _rope_tables():
    nf = RET_DK // 4
    t = np.arange(SEQ)
    inv = ROPE_BASE ** (-np.arange(nf, dtype=np.float32) / nf)
    cos = np.ones((TT, 256), np.float32)
    sin = np.zeros((TT, 256), np.float32)
    for seg, pos in enumerate(((t // GRID_W).astype(np.float32), (t % GRID_W).astype(np.float32))):
        ang = (pos[:, None] * inv[None, :]).astype(np.float32)
        c, s = np.cos(ang), np.sin(ang)
        cos[:SEQ, seg * 128:(seg + 1) * 128] = np.concatenate([c, c], axis=1)
        sin[:SEQ, seg * 128:(seg + 1) * 128] = np.concatenate([-s, s], axis=1)
    ks = np.float32(RET_DK ** -0.5)
    return jnp.asarray(np.stack([cos, cos * ks])), jnp.asarray(np.stack([sin, sin * ks]))


def _decay_tables(decay_logit):
    C = RET_CHUNK
    log_g = jax.nn.log_sigmoid(decay_logit.astype(F32))
    pos = jnp.arange(C, dtype=F32)
    diff = pos[:, None] - pos[None, :]
    lg = log_g[:, :, None, None]
    dm_f = jnp.where(diff >= 0, jnp.exp(jnp.maximum(diff, 0.0)[None, None] * lg), 0.0)
    dm_b = jnp.where(diff <= 0, jnp.exp(jnp.maximum(-diff, 0.0)[None, None] * lg), 0.0)
    dmat = jnp.stack([dm_f[0], dm_b[1]])
    lgc = log_g[:, :, None]
    qdec = jnp.stack([jnp.exp((pos + 1.0)[None, :] * lgc[0]), jnp.exp((C - pos)[None, :] * lgc[1])])
    kdec = jnp.stack([jnp.exp((C - 1.0 - pos)[None, :] * lgc[0]), jnp.exp(pos[None, :] * lgc[1])])
    cdec = jnp.exp(C * log_g).reshape(-1)
    return dmat, qdec[..., None], kdec[..., None], cdec


def kernel(x, c, ctx, c_ctx, w_mod, b_mod, norm_mix, norm_ffn, norm_final, ret_w_in, ret_decay, ret_gn, ret_w_out,
           fn_w_out, na_w_qkv, na_rpb, na_w_out, router_w, router_bias, moe_w_gate, moe_w_up, moe_w_down):
    cvec = jnp.concatenate([c_ctx[None, :], c, jnp.zeros((8 - 1 - BATCH, D), F32)], axis=0)
    mod_all = _modulation(cvec, w_mod, b_mod).reshape(DEPTH, 8, N_MOD, D)
    pad = jnp.zeros((DEPTH, BATCH, 2, 8 - N_MOD, D), F32)
    mods = jnp.stack([jnp.broadcast_to(mod_all[:, 0:1], (DEPTH, BATCH, N_MOD, D)), mod_all[:, 1:1 + BATCH]], axis=2)
    mods = jnp.concatenate([mods, pad], axis=3)

    rope = _rope_tables()
    cs, dft_ctx, dft_f1, dft_f2 = _dft_tables()
    route = (router_w.T.astype(BF16), router_bias.reshape(N_EXPERTS, 1).astype(F32))

    norm_mix3 = norm_mix.reshape(DEPTH, 1, D)
    norm_ffn3 = norm_ffn.reshape(DEPTH, 1, D)
    ret_gn3 = ret_gn.reshape(-1, 1, 2 * D)

    xs, h = _embed(x, ctx, mods, norm_mix3)
    out = None
    for i in range(DEPTH):
        kind, j = i % 3, i // 3
        if kind == 0:
            qkvg = _project(h.reshape(NTOK, D), ret_w_in, j, rope).reshape(BATCH, TT, 6 * D)
            dmat, qdec, kdec, cdec = _decay_tables(ret_decay[j])
            o_f, o_b = _retention(qkvg, dmat, qdec, kdec, cdec)
            xs, hf, *routing = _mixer_out("ret", (o_f, o_b, qkvg, ret_gn3), ret_w_out, j, xs, mods, norm_ffn3, i, route)
        elif kind == 1:
            ab = _fn_channel(h, cs)
            f_lat = _fn_cols(_fn_rows(ab, dft_f1), dft_f2)
            xs, hf, *routing = _mixer_out(
                "split", (f_lat, _fn_ctx(ab, dft_ctx)), fn_w_out, j, xs, mods, norm_ffn3, i, route)
        else:
            qkv = _project(h.reshape(NTOK, D), na_w_qkv, j).reshape(BATCH, TT, 3 * D)
            o = _na_attention(qkv, _na_bias_table(na_rpb[j]))
            xs, hf, *routing = _mixer_out("plain", (o,), na_w_out, j, xs, mods, norm_ffn3, i, route)
        final = i == DEPTH - 1
        if final:
            g_next = (norm_final[None, :], pl.BlockSpec((1, D), lambda b, t: (0, 0)))
        else:
            g_next = (norm_mix3, _layer_spec((1, D), i + 1))
        res = _moe(hf, routing, xs, mods, i, g_next, moe_w_gate, moe_w_up, moe_w_down, final)
        if final:
            out = res
        else:
            xs, h = res
    return out
```
